```python
import math
import jax, jax.numpy as jnp
from jax import lax
import numpy as np

D_MODEL = 1024
BATCH = 8
SEQ = 8192
DEPTH = 4

CHUNK = 64
Q_BLOCK = 128
EPS = 1e-6

FOX_HEADS = 4
FOX_DIM = 128
GDN_HEADS = 4
GDN_DK = 128
GDN_DV = 128
GDN_CONV = 4
SB_HEADS = 4
SB_DIM = 128
MEM_TOKENS = 256
MEM_HEADS = 4
MEM_DIM = 128
D_FF = 2816
FFN_CONV = 3
N_BRANCH = 3

FOX_W = FOX_HEADS * FOX_DIM
GDN_KW = GDN_HEADS * GDN_DK
GDN_VW = GDN_HEADS * GDN_DV
SB_W = SB_HEADS * SB_DIM
MEM_W = MEM_HEADS * MEM_DIM

IN_SPLITS = (
    FOX_W, FOX_W, FOX_W, FOX_HEADS,
    GDN_KW, GDN_KW, GDN_VW, GDN_HEADS, GDN_HEADS, GDN_VW,
    SB_W, SB_W, SB_W,
    N_BRANCH * D_MODEL,
)
N_IN = sum(IN_SPLITS)

kernel_name = "hybrid_fox_gdn_stickbreak_encoder"


def rmsnorm(x, g):
    xf = x.astype(jnp.float32)
    y = xf * lax.rsqrt(jnp.mean(xf * xf, axis=-1, keepdims=True) + EPS)
    return (y * g.astype(jnp.float32)).astype(x.dtype)


def l2norm(x):
    xf = x.astype(jnp.float32)
    return xf * lax.rsqrt(jnp.sum(xf * xf, axis=-1, keepdims=True) + EPS)


def heads(x, n):
    return x.reshape(x.shape[:-1] + (n, -1))


def causal_dwconv(x, w):
    width, ch = w.shape
    return lax.conv_general_dilated(
        x, w[:, None, :].astype(x.dtype), window_strides=(1,), padding=[(width - 1, 0)],
        dimension_numbers=("NWC", "WIO", "NWC"), feature_group_count=ch)


def split_projection(p):
    points = [int(s) for s in np.cumsum(IN_SPLITS)[:-1]]
    return jnp.split(p, points, axis=-1)


def fox_attention(q, k, v, logf):
    B, S, H, dh = q.shape
    c = jnp.cumsum(logf, axis=1).transpose(0, 2, 1)
    qh = q.transpose(0, 2, 1, 3)
    kh = k.transpose(0, 2, 1, 3)
    vh = v.transpose(0, 2, 1, 3)
    scale = dh ** -0.5
    outs = []
    for i in range(S // Q_BLOCK):
        lo, hi = i * Q_BLOCK, (i + 1) * Q_BLOCK
        s = (jnp.einsum("bhqd,bhkd->bhqk", qh[:, :, lo:hi], kh[:, :, :hi]).astype(jnp.float32) * scale
             + (c[:, :, lo:hi, None] - c[:, :, None, :hi]))
        mask = jnp.arange(hi)[None, :] <= jnp.arange(lo, hi)[:, None]
        p = jax.nn.softmax(jnp.where(mask, s, -jnp.inf), axis=-1).astype(vh.dtype)
        outs.append(jnp.einsum("bhqk,bhkd->bhqd", p, vh[:, :, :hi]))
    o = jnp.concatenate(outs, axis=2)
    return o.transpose(0, 2, 1, 3).reshape(B, S, H * dh)


def stick_breaking_attention(q, k, v):
    B, S, H, dh = q.shape
    qh = q.transpose(0, 2, 1, 3)
    kh = k.transpose(0, 2, 1, 3)
    vh = v.transpose(0, 2, 1, 3)
    scale = dh ** -0.5
    idx = jnp.arange(Q_BLOCK)
    tri_in = (idx[:, None] >= idx[None, :]).astype(jnp.float32)
    outs = []
    for i in range(S // Q_BLOCK):
        lo, hi = i * Q_BLOCK, (i + 1) * Q_BLOCK
        nk = i + 1
        z = jnp.einsum("bhqd,bhkd->bhqk", qh[:, :, lo:hi], kh[:, :, :hi]).astype(jnp.float32) * scale
        mask = jnp.arange(hi)[None, :] < jnp.arange(lo, hi)[:, None]
        log_keep = jnp.where(mask, -jax.nn.softplus(z), 0.0)
        lk = log_keep.reshape(B, H, Q_BLOCK, nk, Q_BLOCK)
        within = jnp.einsum("bhqnj,jk->bhqnk", lk, tri_in)
        tot = jnp.sum(lk, axis=-1)
        blk = jnp.arange(nk)
        tri_blk = (blk[:, None] > blk[None, :]).astype(jnp.float32)
        after = jnp.einsum("bhqm,mn->bhqn", tot, tri_blk)
        rc = (within + after[..., None]).reshape(B, H, Q_BLOCK, hi)
        a = jnp.exp(jnp.where(mask, z + rc, -jnp.inf)).astype(vh.dtype)
        outs.append(jnp.einsum("bhqk,bhkd->bhqd", a, vh[:, :, :hi]))
    o = jnp.concatenate(outs, axis=2)
    return o.transpose(0, 2, 1, 3).reshape(B, S, H * dh)


def gated_delta_rule(q, k, v, g, beta):
    B, T, H, dk = q.shape
    dv = v.shape[-1]
    N = T // CHUNK
    f32 = jnp.float32

    def chunks(a):
        a = a.astype(f32).reshape((B, N, CHUNK, H) + a.shape[3:])
        return jnp.moveaxis(a, (1, 3), (0, 2))

    qc = chunks(q) * dk ** -0.5
    kc = chunks(k)
    vc = chunks(v)
    bc = chunks(beta)
    gc = jnp.cumsum(chunks(g), axis=-1)
    idx = jnp.arange(CHUNK)
    causal = idx[:, None] >= idx[None, :]
    strict = idx[:, None] > idx[None, :]
    decay = jnp.exp(jnp.where(causal, gc[..., :, None] - gc[..., None, :], -jnp.inf))
    kk = jnp.einsum("nbhcd,nbhed->nbhce", kc, kc)
    a_mat = jnp.where(strict, bc[..., :, None] * kk * decay, 0.0) + jnp.eye(CHUNK, dtype=f32)
    rhs = jnp.concatenate([vc * bc[..., None], kc * (bc * jnp.exp(gc))[..., None]], axis=-1)
    sol = lax.linalg.triangular_solve(a_mat, rhs, left_side=True, lower=True)
    u, w = sol[..., :dv], sol[..., dv:]
    attn = jnp.where(causal, jnp.einsum("nbhcd,nbhed->nbhce", qc, kc) * decay, 0.0)
    g_last = gc[..., -1]
    k_tail = kc * jnp.exp(g_last[..., None] - gc)[..., None]

    def step(state, xs):
        q_n, u_n, w_n, g_n, attn_n, kt_n, gl_n = xs
        v_new = u_n - jnp.einsum("bhck,bhkv->bhcv", w_n, state)
        o = (jnp.einsum("bhck,bhkv->bhcv", q_n * jnp.exp(g_n)[..., None], state)
             + jnp.einsum("bhce,bhev->bhcv", attn_n, v_new))
        state = state * jnp.exp(gl_n)[..., None, None] + jnp.einsum("bhck,bhcv->bhkv", kt_n, v_new)
        return state, o

    s0 = jnp.zeros((B, H, dk, dv), f32)
    _, o = lax.scan(step, s0, (qc, u, w, gc, attn, k_tail, g_last))
    return jnp.moveaxis(o, (0, 2), (1, 3)).reshape(B, T, H, dv)


def memory_cross_attention(h, m, w_q, w_kv, q_g, k_g, w_o):
    B, S, _ = h.shape
    q = rmsnorm(heads(h @ w_q, MEM_HEADS), q_g)
    k, v = jnp.split(m @ w_kv, 2, axis=-1)
    k = rmsnorm(heads(k, MEM_HEADS), k_g)
    v = heads(v, MEM_HEADS)
    s = jnp.einsum("bqhd,bkhd->bhqk", q, k).astype(jnp.float32) * MEM_DIM ** -0.5
    p = jax.nn.softmax(s, axis=-1).astype(v.dtype)
    o = jnp.einsum("bhqk,bkhd->bqhd", p, v).reshape(B, S, MEM_W)
    return o @ w_o


def conv_ffn(h, w_up, conv_w, conv_b, w_down):
    u = causal_dwconv(h @ w_up, conv_w) + conv_b
    a, b = jnp.split(u, 2, axis=-1)
    return (jax.nn.silu(a) * b) @ w_down


def _fwd_setup_inputs(seed: int = 0) -> dict:
    key = jax.random.key(seed)
    ks = iter(jax.random.split(key, 40))
    L, D = DEPTH, D_MODEL
    f32 = jnp.float32

    def nrm(shape, scale):
        return jax.random.normal(next(ks), shape, f32) * scale

    def gain(shape):
        return 1.0 + 0.02 * jax.random.normal(next(ks), shape, f32)

    x = nrm((BATCH, SEQ, D), 1.0)
    mem = nrm((BATCH, MEM_TOKENS, D), 1.0)
    norm_mix = gain((L, D))
    w_in = nrm((L, D, N_IN), D ** -0.5)
    fox_fbias = jax.random.uniform(next(ks), (L, FOX_HEADS), f32, minval=1.0, maxval=5.0)
    fox_qnorm = gain((L, FOX_DIM))
    fox_knorm = gain((L, FOX_DIM))
    gdn_conv = nrm((L, GDN_CONV, 2 * GDN_KW + GDN_VW), GDN_CONV ** -0.5)
    gdn_a_log = jnp.log(jax.random.uniform(next(ks), (L, GDN_HEADS), f32, minval=1.0, maxval=16.0))
    dt = jnp.exp(jax.random.uniform(next(ks), (L, GDN_HEADS), f32,
                                    minval=math.log(1e-3), maxval=math.log(1e-1)))
    gdn_dt_bias = dt + jnp.log(-jnp.expm1(-dt))
    gdn_onorm = gain((L, GDN_DV))
    gate_bias = nrm((L, N_BRANCH * D), 0.01)
    w_oa = nrm((L, FOX_W, D), FOX_W ** -0.5)
    w_ob = nrm((L, GDN_VW, D), GDN_VW ** -0.5)
    w_oc = nrm((L, SB_W, D), SB_W ** -0.5)
    w_out = nrm((L, D, D), D ** -0.5)
    norm_xq = gain((L, D))
    norm_mem = gain((L, D))
    w_mq = nrm((L, D, MEM_W), D ** -0.5)
    w_mkv = nrm((L, D, 2 * MEM_W), D ** -0.5)
    mq_norm = gain((L, MEM_DIM))
    mk_norm = gain((L, MEM_DIM))
    w_mo = nrm((L, MEM_W, D), MEM_W ** -0.5)
    norm_ffn = gain((L, D))
    w_up = nrm((L, D, 2 * D_FF), D ** -0.5)
    ffn_conv = nrm((L, FFN_CONV, 2 * D_FF), FFN_CONV ** -0.5)
    ffn_conv_b = nrm((L, 2 * D_FF), 0.01)
    w_down = nrm((L, D_FF, D), D_FF ** -0.5)
    return {
        "x": x, "mem": mem, "norm_mix": norm_mix, "w_in": w_in,
        "fox_fbias": fox_fbias, "fox_qnorm": fox_qnorm, "fox_knorm": fox_knorm,
        "gdn_conv": gdn_conv, "gdn_a_log": gdn_a_log, "gdn_dt_bias": gdn_dt_bias,
        "gdn_onorm": gdn_onorm, "gate_bias": gate_bias,
        "w_oa": w_oa, "w_ob": w_ob, "w_oc": w_oc, "w_out": w_out,
        "norm_xq": norm_xq, "norm_mem": norm_mem, "w_mq": w_mq, "w_mkv": w_mkv,
        "mq_norm": mq_norm, "mk_norm": mk_norm, "w_mo": w_mo,
        "norm_ffn": norm_ffn, "w_up": w_up, "ffn_conv": ffn_conv,
        "ffn_conv_b": ffn_conv_b, "w_down": w_down,
    }


def _fwd_reference(x, mem, norm_mix, w_in, fox_fbias, fox_qnorm, fox_knorm, gdn_conv, gdn_a_log,
              gdn_dt_bias, gdn_onorm, gate_bias, w_oa, w_ob, w_oc, w_out, norm_xq, norm_mem,
              w_mq, w_mkv, mq_norm, mk_norm, w_mo, norm_ffn, w_up, ffn_conv, ffn_conv_b, w_down):
    B, S, D = x.shape
    f32 = jnp.float32
    for l in range(DEPTH):
        h = rmsnorm(x, norm_mix[l])
        (fq, fk, fv, ff, gq, gk, gv, gb, ga, gz, sq, sk, sv, gates) = split_projection(h @ w_in[l])

        fq = rmsnorm(heads(fq, FOX_HEADS), fox_qnorm[l])
        fk = rmsnorm(heads(fk, FOX_HEADS), fox_knorm[l])
        logf = jax.nn.log_sigmoid((ff + fox_fbias[l]).astype(f32))
        ya = fox_attention(fq, fk, heads(fv, FOX_HEADS), logf)

        qkv = jax.nn.silu(causal_dwconv(jnp.concatenate([gq, gk, gv], axis=-1), gdn_conv[l]))
        cq, ck, cv = jnp.split(qkv, [GDN_KW, 2 * GDN_KW], axis=-1)
        beta = jax.nn.sigmoid(gb.astype(f32))
        g_log = -jnp.exp(gdn_a_log[l].astype(f32)) * jax.nn.softplus((ga + gdn_dt_bias[l]).astype(f32))
        o = gated_delta_rule(l2norm(heads(cq, GDN_HEADS)), l2norm(heads(ck, GDN_HEADS)),
                             heads(cv, GDN_HEADS), g_log, beta)
        yb = (rmsnorm(o, gdn_onorm[l]) * jax.nn.silu(heads(gz, GDN_HEADS).astype(f32)))
        yb = yb.astype(x.dtype).reshape(B, S, GDN_VW)

        yc = stick_breaking_attention(heads(sq, SB_HEADS), heads(sk, SB_HEADS), heads(sv, SB_HEADS))

        g = jax.nn.sigmoid((gates + gate_bias[l]).astype(f32)).astype(x.dtype).reshape(B, S, N_BRANCH, D)
        mixed = (g[..., 0, :] * (ya @ w_oa[l]) + g[..., 1, :] * (yb @ w_ob[l])
                 + g[..., 2, :] * (yc @ w_oc[l]))
        x = x + mixed @ w_out[l]

        x = x + memory_cross_attention(rmsnorm(x, norm_xq[l]), rmsnorm(mem, norm_mem[l]),
                                       w_mq[l], w_mkv[l], mq_norm[l], mk_norm[l], w_mo[l])

        x = x + conv_ffn(rmsnorm(x, norm_ffn[l]), w_up[l], ffn_conv[l], ffn_conv_b[l], w_down[l])
    return x


import jax as _jax
import jax.numpy as _jnp

TWIN_FORMAT = 'train_step'
FWD_PARAMS = ['x', 'mem', 'norm_mix', 'w_in', 'fox_fbias', 'fox_qnorm', 'fox_knorm', 'gdn_conv', 'gdn_a_log', 'gdn_dt_bias', 'gdn_onorm', 'gate_bias', 'w_oa', 'w_ob', 'w_oc', 'w_out', 'norm_xq', 'norm_mem', 'w_mq', 'w_mkv', 'mq_norm', 'mk_norm', 'w_mo', 'norm_ffn', 'w_up', 'ffn_conv', 'ffn_conv_b', 'w_down']
TWIN_WEIGHTS = ['norm_mix', 'w_in', 'fox_fbias', 'fox_qnorm', 'fox_knorm', 'gdn_conv', 'gdn_a_log', 'gdn_dt_bias', 'gdn_onorm', 'gate_bias', 'w_oa', 'w_ob', 'w_oc', 'w_out', 'norm_xq', 'norm_mem', 'w_mq', 'w_mkv', 'mq_norm', 'mk_norm', 'w_mo', 'norm_ffn', 'w_up', 'ffn_conv', 'ffn_conv_b', 'w_down']
TWIN_DIFF_INPUT = 'x'
TWIN_INPUTS = ['x', 'mem', 'norm_mix', 'w_in', 'fox_fbias', 'fox_qnorm', 'fox_knorm', 'gdn_conv', 'gdn_a_log', 'gdn_dt_bias', 'gdn_onorm', 'gate_bias', 'w_oa', 'w_ob', 'w_oc', 'w_out', 'norm_xq', 'norm_mem', 'w_mq', 'w_mkv', 'mq_norm', 'mk_norm', 'w_mo', 'norm_ffn', 'w_up', 'ffn_conv', 'ffn_conv_b', 'w_down', 'loss_target', 'm_norm_mix', 'm_w_in', 'm_fox_fbias', 'm_fox_qnorm', 'm_fox_knorm', 'm_gdn_conv', 'm_gdn_a_log', 'm_gdn_dt_bias', 'm_gdn_onorm', 'm_gate_bias', 'm_w_oa', 'm_w_ob', 'm_w_oc', 'm_w_out', 'm_norm_xq', 'm_norm_mem', 'm_w_mq', 'm_w_mkv', 'm_mq_norm', 'm_mk_norm', 'm_w_mo', 'm_norm_ffn', 'm_w_up', 'm_ffn_conv', 'm_ffn_conv_b', 'm_w_down', 'v_norm_mix', 'v_w_in', 'v_fox_fbias', 'v_fox_qnorm', 'v_fox_knorm', 'v_gdn_conv', 'v_gdn_a_log', 'v_gdn_dt_bias', 'v_gdn_onorm', 'v_gate_bias', 'v_w_oa', 'v_w_ob', 'v_w_oc', 'v_w_out', 'v_norm_xq', 'v_norm_mem', 'v_w_mq', 'v_w_mkv', 'v_mq_norm', 'v_mk_norm', 'v_w_mo', 'v_norm_ffn', 'v_w_up', 'v_ffn_conv', 'v_ffn_conv_b', 'v_w_down']
TWIN_OUTPUTS = ['loss', 'grad_x', 'grad_norm_mix', 'grad_w_in', 'grad_fox_fbias', 'grad_fox_qnorm', 'grad_fox_knorm', 'grad_gdn_conv', 'grad_gdn_a_log', 'grad_gdn_dt_bias', 'grad_gdn_onorm', 'grad_gate_bias', 'grad_w_oa', 'grad_w_ob', 'grad_w_oc', 'grad_w_out', 'grad_norm_xq', 'grad_norm_mem', 'grad_w_mq', 'grad_w_mkv', 'grad_mq_norm', 'grad_mk_norm', 'grad_w_mo', 'grad_norm_ffn', 'grad_w_up', 'grad_ffn_conv', 'grad_ffn_conv_b', 'grad_w_down', 'delta_norm_mix', 'delta_w_in', 'delta_fox_fbias', 'delta_fox_qnorm', 'delta_fox_knorm', 'delta_gdn_conv', 'delta_gdn_a_log', 'delta_gdn_dt_bias', 'delta_gdn_onorm', 'delta_gate_bias', 'delta_w_oa', 'delta_w_ob', 'delta_w_oc', 'delta_w_out', 'delta_norm_xq', 'delta_norm_mem', 'delta_w_mq', 'delta_w_mkv', 'delta_mq_norm', 'delta_mk_norm', 'delta_w_mo', 'delta_norm_ffn', 'delta_w_up', 'delta_ffn_conv', 'delta_ffn_conv_b', 'delta_w_down', 'new_m_norm_mix', 'new_m_w_in', 'new_m_fox_fbias', 'new_m_fox_qnorm', 'new_m_fox_knorm', 'new_m_gdn_conv', 'new_m_gdn_a_log', 'new_m_gdn_dt_bias', 'new_m_gdn_onorm', 'new_m_gate_bias', 'new_m_w_oa', 'new_m_w_ob', 'new_m_w_oc', 'new_m_w_out', 'new_m_norm_xq', 'new_m_norm_mem', 'new_m_w_mq', 'new_m_w_mkv', 'new_m_mq_norm', 'new_m_mk_norm', 'new_m_w_mo', 'new_m_norm_ffn', 'new_m_w_up', 'new_m_ffn_conv', 'new_m_ffn_conv_b', 'new_m_w_down', 'new_v_norm_mix', 'new_v_w_in', 'new_v_fox_fbias', 'new_v_fox_qnorm', 'new_v_fox_knorm', 'new_v_gdn_conv', 'new_v_gdn_a_log', 'new_v_gdn_dt_bias', 'new_v_gdn_onorm', 'new_v_gate_bias', 'new_v_w_oa', 'new_v_w_ob', 'new_v_w_oc', 'new_v_w_out', 'new_v_norm_xq', 'new_v_norm_mem', 'new_v_w_mq', 'new_v_w_mkv', 'new_v_mq_norm', 'new_v_mk_norm', 'new_v_w_mo', 'new_v_norm_ffn', 'new_v_w_up', 'new_v_ffn_conv', 'new_v_ffn_conv_b', 'new_v_w_down']
TWIN_LEAF_KINDS = {'loss': 'loss', 'grad_x': 'grad_x', 'grad_norm_mix': 'grad_w', 'grad_w_in': 'grad_w', 'grad_fox_fbias': 'grad_w', 'grad_fox_qnorm': 'grad_w', 'grad_fox_knorm': 'grad_w', 'grad_gdn_conv': 'grad_w', 'grad_gdn_a_log': 'grad_w', 'grad_gdn_dt_bias': 'grad_w', 'grad_gdn_onorm': 'grad_w', 'grad_gate_bias': 'grad_w', 'grad_w_oa': 'grad_w', 'grad_w_ob': 'grad_w', 'grad_w_oc': 'grad_w', 'grad_w_out': 'grad_w', 'grad_norm_xq': 'grad_w', 'grad_norm_mem': 'grad_w', 'grad_w_mq': 'grad_w', 'grad_w_mkv': 'grad_w', 'grad_mq_norm': 'grad_w', 'grad_mk_norm': 'grad_w', 'grad_w_mo': 'grad_w', 'grad_norm_ffn': 'grad_w', 'grad_w_up': 'grad_w', 'grad_ffn_conv': 'grad_w', 'grad_ffn_conv_b': 'grad_w', 'grad_w_down': 'grad_w', 'delta_norm_mix': 'delta_w', 'delta_w_in': 'delta_w', 'delta_fox_fbias': 'delta_w', 'delta_fox_qnorm': 'delta_w', 'delta_fox_knorm': 'delta_w', 'delta_gdn_conv': 'delta_w', 'delta_gdn_a_log': 'delta_w', 'delta_gdn_dt_bias': 'delta_w', 'delta_gdn_onorm': 'delta_w', 'delta_gate_bias': 'delta_w', 'delta_w_oa': 'delta_w', 'delta_w_ob': 'delta_w', 'delta_w_oc': 'delta_w', 'delta_w_out': 'delta_w', 'delta_norm_xq': 'delta_w', 'delta_norm_mem': 'delta_w', 'delta_w_mq': 'delta_w', 'delta_w_mkv': 'delta_w', 'delta_mq_norm': 'delta_w', 'delta_mk_norm': 'delta_w', 'delta_w_mo': 'delta_w', 'delta_norm_ffn': 'delta_w', 'delta_w_up': 'delta_w', 'delta_ffn_conv': 'delta_w', 'delta_ffn_conv_b': 'delta_w', 'delta_w_down': 'delta_w', 'new_m_norm_mix': 'new_m', 'new_m_w_in': 'new_m', 'new_m_fox_fbias': 'new_m', 'new_m_fox_qnorm': 'new_m', 'new_m_fox_knorm': 'new_m', 'new_m_gdn_conv': 'new_m', 'new_m_gdn_a_log': 'new_m', 'new_m_gdn_dt_bias': 'new_m', 'new_m_gdn_onorm': 'new_m', 'new_m_gate_bias': 'new_m', 'new_m_w_oa': 'new_m', 'new_m_w_ob': 'new_m', 'new_m_w_oc': 'new_m', 'new_m_w_out': 'new_m', 'new_m_norm_xq': 'new_m', 'new_m_norm_mem': 'new_m', 'new_m_w_mq': 'new_m', 'new_m_w_mkv': 'new_m', 'new_m_mq_norm': 'new_m', 'new_m_mk_norm': 'new_m', 'new_m_w_mo': 'new_m', 'new_m_norm_ffn': 'new_m', 'new_m_w_up': 'new_m', 'new_m_ffn_conv': 'new_m', 'new_m_ffn_conv_b': 'new_m', 'new_m_w_down': 'new_m', 'new_v_norm_mix': 'new_v', 'new_v_w_in': 'new_v', 'new_v_fox_fbias': 'new_v', 'new_v_fox_qnorm': 'new_v', 'new_v_fox_knorm': 'new_v', 'new_v_gdn_conv': 'new_v', 'new_v_gdn_a_log': 'new_v', 'new_v_gdn_dt_bias': 'new_v', 'new_v_gdn_onorm': 'new_v', 'new_v_gate_bias': 'new_v', 'new_v_w_oa': 'new_v', 'new_v_w_ob': 'new_v', 'new_v_w_oc': 'new_v', 'new_v_w_out': 'new_v', 'new_v_norm_xq': 'new_v', 'new_v_norm_mem': 'new_v', 'new_v_w_mq': 'new_v', 'new_v_w_mkv': 'new_v', 'new_v_mq_norm': 'new_v', 'new_v_mk_norm': 'new_v', 'new_v_w_mo': 'new_v', 'new_v_norm_ffn': 'new_v', 'new_v_w_up': 'new_v', 'new_v_ffn_conv': 'new_v', 'new_v_ffn_conv_b': 'new_v', 'new_v_w_down': 'new_v'}


def _forward(args):
    return _fwd_reference(*[args[k] for k in FWD_PARAMS])


def _output_shape():
    def fwd():
        inp = _fwd_setup_inputs(0)
        return _fwd_reference(*[inp[k] for k in FWD_PARAMS])
    out = _jax.eval_shape(fwd)
    return out.shape, out.dtype

N_MICROBATCH = 1
ADAM_LR = 0.001
ADAM_B1 = 0.9
ADAM_B2 = 0.999
ADAM_EPS = 1e-08
ADAM_WD = 0.01
ADAM_STEP = 10
PER_EXAMPLE_BATCH_AXIS = {'x': 0, 'mem': 0, 'loss_target': 0}
SHARED_INPUTS = []
_WEIGHT_DTYPES = {'norm_mix': _jnp.float32, 'w_in': _jnp.float32, 'fox_fbias': _jnp.float32, 'fox_qnorm': _jnp.float32, 'fox_knorm': _jnp.float32, 'gdn_conv': _jnp.float32, 'gdn_a_log': _jnp.float32, 'gdn_dt_bias': _jnp.float32, 'gdn_onorm': _jnp.float32, 'gate_bias': _jnp.float32, 'w_oa': _jnp.float32, 'w_ob': _jnp.float32, 'w_oc': _jnp.float32, 'w_out': _jnp.float32, 'norm_xq': _jnp.float32, 'norm_mem': _jnp.float32, 'w_mq': _jnp.float32, 'w_mkv': _jnp.float32, 'mq_norm': _jnp.float32, 'mk_norm': _jnp.float32, 'w_mo': _jnp.float32, 'norm_ffn': _jnp.float32, 'w_up': _jnp.float32, 'ffn_conv': _jnp.float32, 'ffn_conv_b': _jnp.float32, 'w_down': _jnp.float32}
MOMENT_SCALE = {'norm_mix': 2.319828e+01, 'w_in': 5.455289e-01, 'fox_fbias': 2.102735e+02, 'fox_qnorm': 6.146483e+00, 'fox_knorm': 6.153320e+00, 'gdn_conv': 1.063212e+00, 'gdn_a_log': 4.485862e+01, 'gdn_dt_bias': 4.208143e+01, 'gdn_onorm': 5.749609e+01, 'gate_bias': 2.221410e+00, 'w_oa': 5.819773e-01, 'w_ob': 1.802178e+00, 'w_oc': 7.976582e-01, 'w_out': 1.928772e+00, 'norm_xq': 1.711144e-01, 'norm_mem': 9.321881e-01, 'w_mq': 2.268337e-01, 'w_mkv': 6.434362e-01, 'mq_norm': 4.941174e+00, 'mk_norm': 4.938504e+00, 'w_mo': 6.550262e-01, 'norm_ffn': 5.165099e+01, 'w_up': 7.357481e-01, 'ffn_conv': 7.168902e+00, 'ffn_conv_b': 7.001575e+00, 'w_down': 9.198543e-01}


def _to_microbatches(a, axis):
    t = _jnp.moveaxis(a, axis, 0)
    t = t.reshape((N_MICROBATCH, t.shape[0] // N_MICROBATCH) + t.shape[1:])
    return _jnp.moveaxis(t, 1, axis + 1)


def setup_inputs(seed: int = 0) -> dict:
    inp = _fwd_setup_inputs(seed)
    key = _jax.random.fold_in(_jax.random.key(seed), 7919)
    shape, _ = _output_shape()
    out = dict(inp)
    out["loss_target"] = _jax.random.normal(_jax.random.fold_in(key, 0), shape, _jnp.float32)
    for i, name in enumerate(TWIN_WEIGHTS):
        w = inp[name].astype(_jnp.float32)
        if MOMENT_SCALE is None:
            s = _jnp.sqrt(_jnp.mean(_jnp.square(w)) + 1e-30)
        else:
            s = MOMENT_SCALE[name]
        km, kv = _jax.random.split(_jax.random.fold_in(key, i + 1))
        out[name] = w
        out["m_" + name] = s * _jax.random.normal(km, w.shape, _jnp.float32)
        out["v_" + name] = (s * s) * _jax.random.uniform(kv, w.shape, _jnp.float32, 0.5, 1.5)
    if N_MICROBATCH > 1:
        for name, axis in PER_EXAMPLE_BATCH_AXIS.items():
            out[name] = _to_microbatches(out[name], axis)
    return {'x': out['x'], 'mem': out['mem'], 'norm_mix': out['norm_mix'], 'w_in': out['w_in'], 'fox_fbias': out['fox_fbias'], 'fox_qnorm': out['fox_qnorm'], 'fox_knorm': out['fox_knorm'], 'gdn_conv': out['gdn_conv'], 'gdn_a_log': out['gdn_a_log'], 'gdn_dt_bias': out['gdn_dt_bias'], 'gdn_onorm': out['gdn_onorm'], 'gate_bias': out['gate_bias'], 'w_oa': out['w_oa'], 'w_ob': out['w_ob'], 'w_oc': out['w_oc'], 'w_out': out['w_out'], 'norm_xq': out['norm_xq'], 'norm_mem': out['norm_mem'], 'w_mq': out['w_mq'], 'w_mkv': out['w_mkv'], 'mq_norm': out['mq_norm'], 'mk_norm': out['mk_norm'], 'w_mo': out['w_mo'], 'norm_ffn': out['norm_ffn'], 'w_up': out['w_up'], 'ffn_conv': out['ffn_conv'], 'ffn_conv_b': out['ffn_conv_b'], 'w_down': out['w_down'], 'loss_target': out['loss_target'], 'm_norm_mix': out['m_norm_mix'], 'm_w_in': out['m_w_in'], 'm_fox_fbias': out['m_fox_fbias'], 'm_fox_qnorm': out['m_fox_qnorm'], 'm_fox_knorm': out['m_fox_knorm'], 'm_gdn_conv': out['m_gdn_conv'], 'm_gdn_a_log': out['m_gdn_a_log'], 'm_gdn_dt_bias': out['m_gdn_dt_bias'], 'm_gdn_onorm': out['m_gdn_onorm'], 'm_gate_bias': out['m_gate_bias'], 'm_w_oa': out['m_w_oa'], 'm_w_ob': out['m_w_ob'], 'm_w_oc': out['m_w_oc'], 'm_w_out': out['m_w_out'], 'm_norm_xq': out['m_norm_xq'], 'm_norm_mem': out['m_norm_mem'], 'm_w_mq': out['m_w_mq'], 'm_w_mkv': out['m_w_mkv'], 'm_mq_norm': out['m_mq_norm'], 'm_mk_norm': out['m_mk_norm'], 'm_w_mo': out['m_w_mo'], 'm_norm_ffn': out['m_norm_ffn'], 'm_w_up': out['m_w_up'], 'm_ffn_conv': out['m_ffn_conv'], 'm_ffn_conv_b': out['m_ffn_conv_b'], 'm_w_down': out['m_w_down'], 'v_norm_mix': out['v_norm_mix'], 'v_w_in': out['v_w_in'], 'v_fox_fbias': out['v_fox_fbias'], 'v_fox_qnorm': out['v_fox_qnorm'], 'v_fox_knorm': out['v_fox_knorm'], 'v_gdn_conv': out['v_gdn_conv'], 'v_gdn_a_log': out['v_gdn_a_log'], 'v_gdn_dt_bias': out['v_gdn_dt_bias'], 'v_gdn_onorm': out['v_gdn_onorm'], 'v_gate_bias': out['v_gate_bias'], 'v_w_oa': out['v_w_oa'], 'v_w_ob': out['v_w_ob'], 'v_w_oc': out['v_w_oc'], 'v_w_out': out['v_w_out'], 'v_norm_xq': out['v_norm_xq'], 'v_norm_mem': out['v_norm_mem'], 'v_w_mq': out['v_w_mq'], 'v_w_mkv': out['v_w_mkv'], 'v_mq_norm': out['v_mq_norm'], 'v_mk_norm': out['v_mk_norm'], 'v_w_mo': out['v_w_mo'], 'v_norm_ffn': out['v_norm_ffn'], 'v_w_up': out['v_w_up'], 'v_ffn_conv': out['v_ffn_conv'], 'v_ffn_conv_b': out['v_ffn_conv_b'], 'v_w_down': out['v_w_down']}


def _loss(weights, diff, rest, loss_target):
    with _jax.named_scope("forward"):
        args = {**rest, TWIN_DIFF_INPUT: diff, **{k: w.astype(_WEIGHT_DTYPES[k]) for k, w in weights.items()}}
        y = _forward(args)
    with _jax.named_scope("loss_head"):
        err = _jnp.square(y.astype(_jnp.float32) - loss_target)
        return 0.5 * _jnp.sum(_jnp.mean(err, axis=-1)) if err.ndim else 0.5 * err


def _adamw(w, g, m, v):
    m = ADAM_B1 * m + (1.0 - ADAM_B1) * g
    v = ADAM_B2 * v + (1.0 - ADAM_B2) * _jnp.square(g)
    m_hat = m / (1.0 - ADAM_B1 ** ADAM_STEP)
    v_hat = v / (1.0 - ADAM_B2 ** ADAM_STEP)
    delta = -ADAM_LR * (m_hat / (_jnp.sqrt(v_hat) + ADAM_EPS) + ADAM_WD * w)
    return delta, m, v


def reference(x, mem, norm_mix, w_in, fox_fbias, fox_qnorm, fox_knorm, gdn_conv, gdn_a_log, gdn_dt_bias, gdn_onorm, gate_bias, w_oa, w_ob, w_oc, w_out, norm_xq, norm_mem, w_mq, w_mkv, mq_norm, mk_norm, w_mo, norm_ffn, w_up, ffn_conv, ffn_conv_b, w_down, loss_target, m_norm_mix, m_w_in, m_fox_fbias, m_fox_qnorm, m_fox_knorm, m_gdn_conv, m_gdn_a_log, m_gdn_dt_bias, m_gdn_onorm, m_gate_bias, m_w_oa, m_w_ob, m_w_oc, m_w_out, m_norm_xq, m_norm_mem, m_w_mq, m_w_mkv, m_mq_norm, m_mk_norm, m_w_mo, m_norm_ffn, m_w_up, m_ffn_conv, m_ffn_conv_b, m_w_down, v_norm_mix, v_w_in, v_fox_fbias, v_fox_qnorm, v_fox_knorm, v_gdn_conv, v_gdn_a_log, v_gdn_dt_bias, v_gdn_onorm, v_gate_bias, v_w_oa, v_w_ob, v_w_oc, v_w_out, v_norm_xq, v_norm_mem, v_w_mq, v_w_mkv, v_mq_norm, v_mk_norm, v_w_mo, v_norm_ffn, v_w_up, v_ffn_conv, v_ffn_conv_b, v_w_down):
    given = dict(x=x, mem=mem, norm_mix=norm_mix, w_in=w_in, fox_fbias=fox_fbias, fox_qnorm=fox_qnorm, fox_knorm=fox_knorm, gdn_conv=gdn_conv, gdn_a_log=gdn_a_log, gdn_dt_bias=gdn_dt_bias, gdn_onorm=gdn_onorm, gate_bias=gate_bias, w_oa=w_oa, w_ob=w_ob, w_oc=w_oc, w_out=w_out, norm_xq=norm_xq, norm_mem=norm_mem, w_mq=w_mq, w_mkv=w_mkv, mq_norm=mq_norm, mk_norm=mk_norm, w_mo=w_mo, norm_ffn=norm_ffn, w_up=w_up, ffn_conv=ffn_conv, ffn_conv_b=ffn_conv_b, w_down=w_down, loss_target=loss_target, m_norm_mix=m_norm_mix, m_w_in=m_w_in, m_fox_fbias=m_fox_fbias, m_fox_qnorm=m_fox_qnorm, m_fox_knorm=m_fox_knorm, m_gdn_conv=m_gdn_conv, m_gdn_a_log=m_gdn_a_log, m_gdn_dt_bias=m_gdn_dt_bias, m_gdn_onorm=m_gdn_onorm, m_gate_bias=m_gate_bias, m_w_oa=m_w_oa, m_w_ob=m_w_ob, m_w_oc=m_w_oc, m_w_out=m_w_out, m_norm_xq=m_norm_xq, m_norm_mem=m_norm_mem, m_w_mq=m_w_mq, m_w_mkv=m_w_mkv, m_mq_norm=m_mq_norm, m_mk_norm=m_mk_norm, m_w_mo=m_w_mo, m_norm_ffn=m_norm_ffn, m_w_up=m_w_up, m_ffn_conv=m_ffn_conv, m_ffn_conv_b=m_ffn_conv_b, m_w_down=m_w_down, v_norm_mix=v_norm_mix, v_w_in=v_w_in, v_fox_fbias=v_fox_fbias, v_fox_qnorm=v_fox_qnorm, v_fox_knorm=v_fox_knorm, v_gdn_conv=v_gdn_conv, v_gdn_a_log=v_gdn_a_log, v_gdn_dt_bias=v_gdn_dt_bias, v_gdn_onorm=v_gdn_onorm, v_gate_bias=v_gate_bias, v_w_oa=v_w_oa, v_w_ob=v_w_ob, v_w_oc=v_w_oc, v_w_out=v_w_out, v_norm_xq=v_norm_xq, v_norm_mem=v_norm_mem, v_w_mq=v_w_mq, v_w_mkv=v_w_mkv, v_mq_norm=v_mq_norm, v_mk_norm=v_mk_norm, v_w_mo=v_w_mo, v_norm_ffn=v_norm_ffn, v_w_up=v_w_up, v_ffn_conv=v_ffn_conv, v_ffn_conv_b=v_ffn_conv_b, v_w_down=v_w_down)
    weights = {n: given[n] for n in TWIN_WEIGHTS}
    shared = {n: given[n] for n in SHARED_INPUTS}
    per_example = {n: given[n] for n in ['x', 'mem']}
    grad_fn = _jax.value_and_grad(_loss, argnums=(0, 1))

    def one_microbatch(ex, loss_target):
        ex = dict(ex)
        diff = ex.pop(TWIN_DIFF_INPUT)
        return grad_fn(weights, diff, {**shared, **ex}, loss_target)

    if N_MICROBATCH == 1:
        loss, (grad_w, grad_x) = one_microbatch(per_example, given["loss_target"])
    else:
        def body(carry, xs):
            loss_sum, grad_sum = carry
            l_k, (gw_k, gx_k) = one_microbatch(xs[0], xs[1])
            with _jax.named_scope("update"):
                return (loss_sum + l_k, _jax.tree.map(_jnp.add, grad_sum, gw_k)), gx_k

        init = (_jnp.zeros((), _jnp.float32), _jax.tree.map(_jnp.zeros_like, weights))
        (loss, grad_w), grad_x = _jax.lax.scan(body, init, (per_example, given["loss_target"]))
    with _jax.named_scope("update"):
        delta_w, new_m, new_v = {}, {}, {}
        for n in TWIN_WEIGHTS:
            delta_w[n], new_m[n], new_v[n] = _adamw(weights[n], grad_w[n], given["m_" + n], given["v_" + n])
    return (loss, grad_x, *[grad_w[n] for n in TWIN_WEIGHTS], *[delta_w[n] for n in TWIN_WEIGHTS],
            *[new_m[n] for n in TWIN_WEIGHTS], *[new_v[n] for n in TWIN_WEIGHTS])
```

```python
import functools

import jax
import jax.numpy as jnp
import numpy as np
from jax import lax
from jax.experimental import pallas as pl
from jax.experimental.pallas import tpu as pltpu

F32 = jnp.float32
BF16 = jnp.bfloat16
MESH = pl.DeviceIdType.MESH
ANY = pl.BlockSpec(memory_space=pl.ANY)
VMEM = pl.BlockSpec(memory_space=pltpu.VMEM)

D_MODEL = 1024
DEPTH = 4
CHUNK = 64
Q_BLOCK = 128
EPS = 1e-6
HEADS = 4
HEAD_DIM = 128
GDN_CONV = 4
MEM_DIM = 128
D_FF = 2816
N_BRANCH = 3
N_IN = 8204
N_IN_PAD = 8320

ADAM_LR = 0.001
ADAM_B1 = 0.9
ADAM_B2 = 0.999
ADAM_EPS = 1e-08
ADAM_WD = 0.01
ADAM_STEP = 10

N_CHIPS = 4
LANES = 128
FLAT_TILE_ROWS = 2048
VMEM_LIMIT = 48 * 1024 * 1024

IN_NAMES = ['x', 'mem', 'norm_mix', 'w_in', 'fox_fbias', 'fox_qnorm', 'fox_knorm', 'gdn_conv', 'gdn_a_log',
            'gdn_dt_bias', 'gdn_onorm', 'gate_bias', 'w_oa', 'w_ob', 'w_oc', 'w_out', 'norm_xq', 'norm_mem',
            'w_mq', 'w_mkv', 'mq_norm', 'mk_norm', 'w_mo', 'norm_ffn', 'w_up', 'ffn_conv', 'ffn_conv_b', 'w_down']
WEIGHTS = IN_NAMES[2:]
SHARDED = {'w_in': 2, 'gdn_conv': 2, 'w_oa': 2, 'w_ob': 2, 'w_oc': 2, 'w_out': 1, 'w_mq': 1, 'w_mkv': 1,
           'w_mo': 2, 'w_up': 2, 'ffn_conv': 2, 'w_down': 1}
SMALL = [n for n in WEIGHTS if n not in SHARDED]


def _pick(n, cands):
    for c in cands:
        if n % c == 0:
            return c
    return n


_DOT_DIMS = {
    'nn': (((1,), (0,)), ((), ())),
    'nt': (((1,), (1,)), ((), ())),
    'tn': (((0,), (0,)), ((), ())),
}


def _mm(a, b, mode):
    if mode == 'nn':
        (m, c), (_, n) = a.shape, b.shape
    elif mode == 'nt':
        (m, c), (n, _) = a.shape, b.shape
    else:
        (c, m), (_, n) = a.shape, b.shape
    tm = _pick(m, (512, 256, 128))
    tn = _pick(n, (1664, 1408, 1024, 512, 256, 128))
    tc = _pick(c, (1024, 1408, 640, 512, 256, 128))
    if mode == 'tn':
        a_spec = pl.BlockSpec((tc, tm), lambda i, j, k: (k, i))
    else:
        a_spec = pl.BlockSpec((tm, tc), lambda i, j, k: (i, k))
    if mode == 'nt':
        b_spec = pl.BlockSpec((tn, tc), lambda i, j, k: (j, k))
    else:
        b_spec = pl.BlockSpec((tc, tn), lambda i, j, k: (k, j))
    dims = _DOT_DIMS[mode]

    def body(a_ref, b_ref, o_ref):
        @pl.when(pl.program_id(2) == 0)
        def _():
            o_ref[...] = jnp.zeros_like(o_ref)

        o_ref[...] += lax.dot_general(a_ref[...].astype(BF16), b_ref[...].astype(BF16), dims,
                                      preferred_element_type=F32)

    return pl.pallas_call(
        body,
        grid=(m // tm, n // tn, c // tc),
        in_specs=[a_spec, b_spec],
        out_specs=pl.BlockSpec((tm, tn), lambda i, j, k: (i, j)),
        out_shape=jax.ShapeDtypeStruct((m, n), F32),
        compiler_params=pltpu.CompilerParams(
            dimension_semantics=("parallel", "parallel", "arbitrary"), vmem_limit_bytes=VMEM_LIMIT),
        name=f"mm_{mode}_{m}x{c}x{n}",
    )(a, b)


@jax.custom_vjp
def matmul(a, w):
    return _mm(a, w, 'nn')


def _matmul_fwd(a, w):
    return _mm(a, w, 'nn'), (a, w)


def _matmul_bwd(res, dy):
    a, w = res
    return _mm(dy, w, 'nt'), _mm(a, dy, 'tn')


matmul.defvjp(_matmul_fwd, _matmul_bwd)


ATT_TILE = 256
_NT = (((1,), (1,)), ((), ()))
_TN = (((0,), (0,)), ((), ()))
_NN = (((1,), (0,)), ((), ()))


def _dot(a, b, dims):
    return lax.dot_general(a, b, dims, preferred_element_type=F32)


def _att_specs(s, t):
    tile = pl.BlockSpec((t, HEAD_DIM), lambda h, i: (i, h))
    whole = pl.BlockSpec((s, HEAD_DIM), lambda h, i: (0, h))
    col = pl.BlockSpec((None, t, 1), lambda h, i: (h, i, 0))
    row = pl.BlockSpec((None, 1, s), lambda h, i: (h, 0, 0))
    return tile, whole, col, row


def _tile_mask(t, strict):
    r = lax.broadcasted_iota(jnp.int32, (t, t), 0)
    c = lax.broadcasted_iota(jnp.int32, (t, t), 1)
    return c < r if strict else c <= r


def _fox_fwd_call(q, k, v, c_col, c_row):
    s, w = q.shape
    t = min(ATT_TILE, s)
    scale = HEAD_DIM ** -0.5

    def body(q_ref, k_ref, v_ref, cc_ref, cr_ref, o_ref, lse_ref):
        i = pl.program_id(1)
        qb = q_ref[...]
        cq = cc_ref[...]
        mask = _tile_mask(t, False)

        def block(j, carry, diag):
            m, l, acc = carry
            sl = pl.ds(pl.multiple_of(j * t, t), t)
            sc = _dot(qb, k_ref[sl, :], _NT) * scale + (cq - cr_ref[:, sl])
            if diag:
                sc = jnp.where(mask, sc, -jnp.inf)
            m_new = jnp.maximum(m, jnp.max(sc, axis=-1, keepdims=True))
            p = jnp.exp(sc - m_new)
            alpha = jnp.exp(m - m_new)
            l = alpha * l + jnp.sum(p, axis=-1, keepdims=True)
            acc = alpha * acc + _dot(p.astype(BF16), v_ref[sl, :], _NN)
            return m_new, l, acc

        init = (jnp.full((t, 1), -jnp.inf, F32), jnp.zeros((t, 1), F32), jnp.zeros((t, HEAD_DIM), F32))
        carry = lax.fori_loop(0, i, lambda j, cr: block(j, cr, False), init)
        m, l, acc = block(i, carry, True)
        o_ref[...] = acc / l
        lse_ref[...] = m + jnp.log(l)

    tile, whole, col, row = _att_specs(s, t)
    return pl.pallas_call(
        body,
        grid=(w // HEAD_DIM, s // t),
        in_specs=[tile, whole, whole, col, row],
        out_specs=[tile, col],
        out_shape=[jax.ShapeDtypeStruct((s, w), F32), jax.ShapeDtypeStruct((w // HEAD_DIM, s, 1), F32)],
        compiler_params=pltpu.CompilerParams(dimension_semantics=("parallel", "parallel"),
                                             vmem_limit_bytes=VMEM_LIMIT),
        name="fox_fwd",
    )(q, k, v, c_col, c_row)


def _fox_bwd_call(q, k, v, c_col, c_row, lse, do):
    s, w = q.shape
    t = min(ATT_TILE, s)
    scale = HEAD_DIM ** -0.5

    def body(q_ref, k_ref, v_ref, cc_ref, cr_ref, lse_ref, do_ref, dq_ref, dk_ref, dv_ref, dcr_ref):
        i = pl.program_id(1)

        @pl.when(i == 0)
        def _():
            dk_ref[...] = jnp.zeros_like(dk_ref)
            dv_ref[...] = jnp.zeros_like(dv_ref)
            dcr_ref[...] = jnp.zeros_like(dcr_ref)

        qb = q_ref[...]
        do16 = do_ref[...].astype(BF16)
        lse_q = lse_ref[...]
        cq = cc_ref[...]
        mask = _tile_mask(t, False)

        def probs(j, diag):
            sl = pl.ds(pl.multiple_of(j * t, t), t)
            ks = k_ref[sl, :]
            sc = _dot(qb, ks, _NT) * scale + (cq - cr_ref[:, sl])
            p = jnp.exp(sc - lse_q)
            if diag:
                p = jnp.where(mask, p, 0.0)
            return sl, ks, p, _dot(do16, v_ref[sl, :], _NT)

        def row_dot(j, acc, diag):
            _, _, p, dp = probs(j, diag)
            return acc + jnp.sum(p * dp, axis=-1, keepdims=True)

        delta = lax.fori_loop(0, i, lambda j, a: row_dot(j, a, False), jnp.zeros((t, 1), F32))
        delta = row_dot(i, delta, True)

        def block(j, dq, diag):
            sl, ks, p, dp = probs(j, diag)
            ds = p * (dp - delta)
            ds16 = ds.astype(BF16)
            dv_ref[sl, :] += _dot(p.astype(BF16), do16, _TN)
            dk_ref[sl, :] += _dot(ds16, qb, _TN) * scale
            dcr_ref[:, sl] += -jnp.sum(ds, axis=0, keepdims=True)
            return dq + _dot(ds16, ks, _NN) * scale

        dq = lax.fori_loop(0, i, lambda j, a: block(j, a, False), jnp.zeros((t, HEAD_DIM), F32))
        dq_ref[...] = block(i, dq, True)

    tile, whole, col, row = _att_specs(s, t)
    full = jax.ShapeDtypeStruct((s, w), F32)
    return pl.pallas_call(
        body,
        grid=(w // HEAD_DIM, s // t),
        in_specs=[tile, whole, whole, col, row, col, tile],
        out_specs=[tile, whole, whole, row],
        out_shape=[full, full, full, jax.ShapeDtypeStruct((w // HEAD_DIM, 1, s), F32)],
        compiler_params=pltpu.CompilerParams(dimension_semantics=("parallel", "arbitrary"),
                                             vmem_limit_bytes=VMEM_LIMIT),
        name="fox_bwd",
    )(q, k, v, c_col, c_row, lse, do)


@jax.custom_vjp
def fox_core(q, k, v, c):
    return _fox_fwd(q, k, v, c)[0]


def _fox_fwd(q, k, v, c):
    q16, k16, v16 = q.astype(BF16), k.astype(BF16), v.astype(BF16)
    c_col, c_row = c.T[:, :, None], c.T[:, None, :]
    o, lse = _fox_fwd_call(q16, k16, v16, c_col, c_row)
    return o, (q16, k16, v16, c_col, c_row, lse)


def _fox_bwd(res, do):
    dq, dk, dv, dcr = _fox_bwd_call(*res, do)
    return dq, dk, dv, dcr[:, 0, :].T


fox_core.defvjp(_fox_fwd, _fox_bwd)


def _neg_softplus(z):
    e = jnp.exp(-jnp.abs(z))
    return -(jnp.maximum(z, 0.0) + jnp.log(1.0 + e)), e


def _split_dot(x, tri):
    hi = x.astype(BF16)
    lo = (x - hi.astype(F32)).astype(BF16)
    return _dot(hi, tri, _NN) + _dot(lo, tri, _NN)


def _sb_fwd_call(q, k, v):
    s, w = q.shape
    t = min(ATT_TILE, s)
    scale = HEAD_DIM ** -0.5

    def body(q_ref, k_ref, v_ref, o_ref, tot_ref):
        i = pl.program_id(1)
        qb = q_ref[...]
        mask = _tile_mask(t, True)
        r = lax.broadcasted_iota(jnp.int32, (t, t), 0)
        c = lax.broadcasted_iota(jnp.int32, (t, t), 1)
        tri = (r >= c).astype(BF16)

        def block(j, carry, diag):
            later, acc = carry
            sl = pl.ds(pl.multiple_of(j * t, t), t)
            z = _dot(qb, k_ref[sl, :], _NT) * scale
            lk, _ = _neg_softplus(z)
            if diag:
                lk = jnp.where(mask, lk, 0.0)
            a = jnp.exp(z + _split_dot(lk, tri) + later)
            if diag:
                a = jnp.where(mask, a, 0.0)
            acc = acc + _dot(a.astype(BF16), v_ref[sl, :], _NN)
            return later + jnp.sum(lk, axis=-1, keepdims=True), acc

        carry = block(i, (jnp.zeros((t, 1), F32), jnp.zeros((t, HEAD_DIM), F32)), True)
        later, acc = lax.fori_loop(0, i, lambda jj, cr: block(i - 1 - jj, cr, False), carry)
        o_ref[...] = acc
        tot_ref[...] = later

    tile, whole, col, _ = _att_specs(s, t)
    return pl.pallas_call(
        body,
        grid=(w // HEAD_DIM, s // t),
        in_specs=[tile, whole, whole],
        out_specs=[tile, col],
        out_shape=[jax.ShapeDtypeStruct((s, w), F32), jax.ShapeDtypeStruct((w // HEAD_DIM, s, 1), F32)],
        compiler_params=pltpu.CompilerParams(dimension_semantics=("parallel", "parallel"),
                                             vmem_limit_bytes=VMEM_LIMIT),
        name="sb_fwd",
    )(q, k, v)


def _sb_bwd_call(q, k, v, tot, do):
    s, w = q.shape
    t = min(ATT_TILE, s)
    scale = HEAD_DIM ** -0.5

    def body(q_ref, k_ref, v_ref, tot_ref, do_ref, dq_ref, dk_ref, dv_ref):
        i = pl.program_id(1)

        @pl.when(i == 0)
        def _():
            dk_ref[...] = jnp.zeros_like(dk_ref)
            dv_ref[...] = jnp.zeros_like(dv_ref)

        qb = q_ref[...]
        do16 = do_ref[...].astype(BF16)
        tot_q = tot_ref[...]
        mask = _tile_mask(t, True)
        r = lax.broadcasted_iota(jnp.int32, (t, t), 0)
        c = lax.broadcasted_iota(jnp.int32, (t, t), 1)
        tri_before = (r < c).astype(BF16)
        tri_upto = (r <= c).astype(BF16)

        def block(j, carry, diag):
            before, dl_before, dq = carry
            sl = pl.ds(pl.multiple_of(j * t, t), t)
            ks = k_ref[sl, :]
            z = _dot(qb, ks, _NT) * scale
            lk, e = _neg_softplus(z)
            if diag:
                lk = jnp.where(mask, lk, 0.0)
            sig = jnp.where(z >= 0, 1.0, e) / (1.0 + e)
            a = jnp.exp(z + (tot_q - (_split_dot(lk, tri_before) + before)))
            if diag:
                a = jnp.where(mask, a, 0.0)
            dl = a * _dot(do16, v_ref[sl, :], _NT)
            dz = dl - sig * (_dot(dl.astype(BF16), tri_upto, _NN) + dl_before)
            if diag:
                dz = jnp.where(mask, dz, 0.0)
            dz16 = dz.astype(BF16)
            dv_ref[sl, :] += _dot(a.astype(BF16), do16, _TN)
            dk_ref[sl, :] += _dot(dz16, qb, _TN) * scale
            return (before + jnp.sum(lk, axis=-1, keepdims=True), dl_before + jnp.sum(dl, axis=-1, keepdims=True),
                    dq + _dot(dz16, ks, _NN) * scale)

        init = (jnp.zeros((t, 1), F32), jnp.zeros((t, 1), F32), jnp.zeros((t, HEAD_DIM), F32))
        carry = lax.fori_loop(0, i, lambda j, cr: block(j, cr, False), init)
        dq_ref[...] = block(i, carry, True)[2]

    tile, whole, col, _ = _att_specs(s, t)
    full = jax.ShapeDtypeStruct((s, w), F32)
    return pl.pallas_call(
        body,
        grid=(w // HEAD_DIM, s // t),
        in_specs=[tile, whole, whole, col, tile],
        out_specs=[tile, whole, whole],
        out_shape=[full, full, full],
        compiler_params=pltpu.CompilerParams(dimension_semantics=("parallel", "arbitrary"),
                                             vmem_limit_bytes=VMEM_LIMIT),
        name="sb_bwd",
    )(q, k, v, tot, do)


@jax.custom_vjp
def sb_core(q, k, v):
    return _sb_fwd(q, k, v)[0]


def _sb_fwd(q, k, v):
    q16, k16, v16 = q.astype(BF16), k.astype(BF16), v.astype(BF16)
    o, tot = _sb_fwd_call(q16, k16, v16)
    return o, (q16, k16, v16, tot)


def _sb_bwd(res, do):
    return tuple(_sb_bwd_call(*res, do))


sb_core.defvjp(_sb_fwd, _sb_bwd)


def _gdn_specs(h, c, d):
    vec = pl.BlockSpec((None, h, c, d), lambda n: (n, 0, 0, 0))
    sq = pl.BlockSpec((None, h, c, c), lambda n: (n, 0, 0, 0))
    dec = pl.BlockSpec((None, h, 1, d), lambda n: (n, 0, 0, 0))
    st = pl.BlockSpec((None, h, d, d), lambda n: (n, 0, 0, 0))
    return vec, sq, dec, st


def _b16(x):
    return x.astype(BF16)


def _gdn_scan_fwd_call(qg, u, w, attn, kt, egl):
    n, h, c, d = qg.shape

    def body(qg_ref, u_ref, w_ref, attn_ref, kt_ref, egl_ref, o_ref, st_ref, state):
        @pl.when(pl.program_id(0) == 0)
        def _():
            state[...] = jnp.zeros_like(state)

        for hh in range(h):
            s0 = state[hh]
            st_ref[hh] = s0
            s16 = _b16(s0)
            vn = u_ref[hh] - _dot(_b16(w_ref[hh]), s16, _NN)
            vn16 = _b16(vn)
            o_ref[hh] = _dot(_b16(qg_ref[hh]), s16, _NN) + _dot(_b16(attn_ref[hh]), vn16, _NN)
            state[hh] = s0 * egl_ref[hh] + _dot(_b16(kt_ref[hh]), vn16, _TN)

    vec, sq, dec, st = _gdn_specs(h, c, d)
    return pl.pallas_call(
        body,
        grid=(n,),
        in_specs=[vec, vec, vec, sq, vec, dec],
        out_specs=[vec, st],
        out_shape=[jax.ShapeDtypeStruct((n, h, c, d), F32), jax.ShapeDtypeStruct((n, h, d, d), F32)],
        scratch_shapes=[pltpu.VMEM((h, d, d), F32)],
        compiler_params=pltpu.CompilerParams(dimension_semantics=("arbitrary",)),
        name="gdn_scan_fwd",
    )(qg, u, w, attn, kt, egl)


def _gdn_scan_bwd_call(qg, u, w, attn, kt, egl, states, do):
    n, h, c, d = qg.shape

    def body(qg_ref, u_ref, w_ref, attn_ref, kt_ref, egl_ref, st_ref, do_ref,
             dqg_ref, du_ref, dw_ref, dattn_ref, dkt_ref, degl_ref, dstate):
        @pl.when(pl.program_id(0) == 0)
        def _():
            dstate[...] = jnp.zeros_like(dstate)

        for hh in range(h):
            s0 = st_ref[hh]
            s16 = _b16(s0)
            big_d = dstate[hh]
            d16 = _b16(big_d)
            w16, kt16, qg16, attn16 = _b16(w_ref[hh]), _b16(kt_ref[hh]), _b16(qg_ref[hh]), _b16(attn_ref[hh])
            do16 = _b16(do_ref[hh])
            vn16 = _b16(u_ref[hh] - _dot(w16, s16, _NN))
            dvn = _dot(attn16, do16, _TN) + _dot(kt16, d16, _NN)
            dvn16 = _b16(dvn)
            du_ref[hh] = dvn
            dattn_ref[hh] = _dot(do16, vn16, _NT)
            dqg_ref[hh] = _dot(do16, s16, _NT)
            dkt_ref[hh] = _dot(vn16, d16, _NT)
            dw_ref[hh] = -_dot(dvn16, s16, _NT)
            degl_ref[hh] = jnp.sum(big_d * s0, axis=0, keepdims=True)
            dstate[hh] = big_d * egl_ref[hh] + _dot(qg16, do16, _TN) - _dot(w16, dvn16, _TN)

    vec, sq, dec, st = _gdn_specs(h, c, d)
    rev = lambda spec: pl.BlockSpec(spec.block_shape, lambda i: (n - 1 - i, 0, 0, 0))
    vec, sq, dec, st = rev(vec), rev(sq), rev(dec), rev(st)
    vshape = jax.ShapeDtypeStruct((n, h, c, d), F32)
    return pl.pallas_call(
        body,
        grid=(n,),
        in_specs=[vec, vec, vec, sq, vec, dec, st, vec],
        out_specs=[vec, vec, vec, sq, vec, dec],
        out_shape=[vshape, vshape, vshape, jax.ShapeDtypeStruct((n, h, c, c), F32), vshape,
                   jax.ShapeDtypeStruct((n, h, 1, d), F32)],
        scratch_shapes=[pltpu.VMEM((h, d, d), F32)],
        compiler_params=pltpu.CompilerParams(dimension_semantics=("arbitrary",)),
        name="gdn_scan_bwd",
    )(qg, u, w, attn, kt, egl, states, do)


@jax.custom_vjp
def gdn_scan(qg, u, w, attn, kt, egl):
    return _gdn_scan_fwd_call(qg, u, w, attn, kt, egl)[0]


def _gdn_scan_fwd(qg, u, w, attn, kt, egl):
    o, states = _gdn_scan_fwd_call(qg, u, w, attn, kt, egl)
    return o, (qg, u, w, attn, kt, egl, states)


def _gdn_scan_bwd(res, do):
    return tuple(_gdn_scan_bwd_call(*res, do))


gdn_scan.defvjp(_gdn_scan_fwd, _gdn_scan_bwd)


def _flat_rows(rows):
    return _pick(rows, (FLAT_TILE_ROWS, 1024, 512, 256, 128, 64, 32, 16, 8))


def _add_own_half(g, a, core):
    _, nb, rows, _ = g.shape
    tr = _flat_rows(rows)

    def body(c_ref, g_ref, a_ref, o_ref):
        o_ref[...] = g_ref[...] + a_ref[...]

    return pl.pallas_call(
        body,
        grid_spec=pltpu.PrefetchScalarGridSpec(
            num_scalar_prefetch=1,
            grid=(nb, rows // tr),
            in_specs=[pl.BlockSpec((None, None, tr, LANES), lambda j, r, c_ref: (c_ref[0], j, r, 0)),
                      pl.BlockSpec((None, tr, LANES), lambda j, r, c_ref: (j, r, 0))],
            out_specs=pl.BlockSpec((None, tr, LANES), lambda j, r, c_ref: (j, r, 0)),
        ),
        out_shape=jax.ShapeDtypeStruct((nb, rows, LANES), F32),
        compiler_params=pltpu.CompilerParams(dimension_semantics=("parallel", "parallel")),
        name="rs_add_own_half",
    )(core, g, a)


def _sum_chips(b):
    nb, rows, _ = b.shape
    tr = _flat_rows(rows)

    def body(b_ref, o_ref):
        acc = b_ref[0] + b_ref[1]
        for j in range(2, nb):
            acc = acc + b_ref[j]
        o_ref[...] = acc

    return pl.pallas_call(
        body,
        grid=(rows // tr,),
        in_specs=[pl.BlockSpec((nb, tr, LANES), lambda r: (0, r, 0))],
        out_specs=pl.BlockSpec((tr, LANES), lambda r: (r, 0)),
        out_shape=jax.ShapeDtypeStruct((rows, LANES), F32),
        compiler_params=pltpu.CompilerParams(dimension_semantics=("parallel",)),
        name="rs_sum_chips",
    )(b)


def _adamw(w, g, m, v, tag):
    rows = w.shape[0]
    tr = _flat_rows(rows)

    def body(w_ref, g_ref, m_ref, v_ref, d_ref, nm_ref, nv_ref):
        gg = g_ref[...]
        nm = ADAM_B1 * m_ref[...] + (1.0 - ADAM_B1) * gg
        nv = ADAM_B2 * v_ref[...] + (1.0 - ADAM_B2) * jnp.square(gg)
        m_hat = nm / (1.0 - ADAM_B1 ** ADAM_STEP)
        v_hat = nv / (1.0 - ADAM_B2 ** ADAM_STEP)
        d_ref[...] = -ADAM_LR * (m_hat / (jnp.sqrt(v_hat) + ADAM_EPS) + ADAM_WD * w_ref[...])
        nm_ref[...] = nm
        nv_ref[...] = nv

    spec = pl.BlockSpec((tr, LANES), lambda r: (r, 0))
    shape = jax.ShapeDtypeStruct((rows, LANES), F32)
    return pl.pallas_call(
        body,
        grid=(rows // tr,),
        in_specs=[spec] * 4,
        out_specs=[spec] * 3,
        out_shape=[shape] * 3,
        compiler_params=pltpu.CompilerParams(dimension_semantics=("parallel",)),
        name=f"adamw_{tag}",
    )(w, g, m, v)


def _loss_head(y, t):
    s, d = y.shape
    tr = _pick(s, (512, 256, 128, 64, 32, 16, 8))

    def body(y_ref, t_ref, dy_ref, l_ref):
        @pl.when(pl.program_id(0) == 0)
        def _():
            l_ref[...] = jnp.zeros_like(l_ref)

        diff = y_ref[...] - t_ref[...]
        dy_ref[...] = diff / d
        row = jnp.mean(jnp.square(diff), axis=-1, keepdims=True)
        l_ref[...] += 0.5 * jnp.sum(row, axis=0, keepdims=True)

    return pl.pallas_call(
        body,
        grid=(s // tr,),
        in_specs=[pl.BlockSpec((tr, d), lambda r: (r, 0))] * 2,
        out_specs=[pl.BlockSpec((tr, d), lambda r: (r, 0)), pl.BlockSpec((1, LANES), lambda r: (0, 0))],
        out_shape=[jax.ShapeDtypeStruct((s, d), F32), jax.ShapeDtypeStruct((1, LANES), F32)],
        compiler_params=pltpu.CompilerParams(dimension_semantics=("arbitrary",)),
        name="loss_head",
    )(y, t)


def _place():
    x, y, c = lax.axis_index("x"), lax.axis_index("y"), lax.axis_index("c")
    chips = [(1 - x, y), (x, 1 - y), (1 - x, 1 - y)]
    return x, y, c, chips


def _all_gather_chips(flat):
    _, rows, _ = flat.shape

    def body(x_ref, o_ref, send_sems, recv_sems, local_sem):
        x, y, c, chips = _place()
        me = 2 * x + y
        sib = (x, y, 1 - c)

        def remote(k, src, dst, to):
            return pltpu.make_async_remote_copy(src_ref=src, dst_ref=dst, send_sem=send_sems.at[k],
                                                recv_sem=recv_sems.at[k], device_id=to, device_id_type=MESH)

        mine = pltpu.make_async_copy(x_ref, o_ref.at[me], local_sem)
        mine.start()
        first = [remote(k, x_ref.at[c], o_ref.at[me, c], (px, py, c)) for k, (px, py) in enumerate(chips)]
        for cp in first:
            cp.start()
        passed = []
        for k, (px, py) in enumerate(chips):
            blk = o_ref.at[2 * px + py, c]
            remote(k, x_ref.at[c], blk, (px, py, c)).wait_recv()
            cp = remote(3 + k, blk, blk, sib)
            cp.start()
            passed.append(cp)
        for k, (px, py) in enumerate(chips):
            blk = o_ref.at[2 * px + py, 1 - c]
            remote(3 + k, blk, blk, sib).wait_recv()
        for cp in first + passed:
            cp.wait_send()
        mine.wait()

    return pl.pallas_call(
        body,
        in_specs=[ANY],
        out_specs=ANY,
        out_shape=jax.ShapeDtypeStruct((N_CHIPS, 2, rows, LANES), flat.dtype),
        scratch_shapes=[pltpu.SemaphoreType.DMA((6,)), pltpu.SemaphoreType.DMA((6,)), pltpu.SemaphoreType.DMA],
        name="all_gather_chips",
    )(flat)


def _rs_sibling_exchange(g):
    _, nb, rows, _ = g.shape

    def body(g_ref, a_ref, send_sem, recv_sem):
        x, y, c, _ = _place()
        cp = pltpu.make_async_remote_copy(src_ref=g_ref.at[1 - c], dst_ref=a_ref, send_sem=send_sem,
                                          recv_sem=recv_sem, device_id=(x, y, 1 - c), device_id_type=MESH)
        cp.start()
        cp.wait()

    return pl.pallas_call(
        body,
        in_specs=[ANY],
        out_specs=ANY,
        out_shape=jax.ShapeDtypeStruct((nb, rows, LANES), F32),
        scratch_shapes=[pltpu.SemaphoreType.DMA, pltpu.SemaphoreType.DMA],
        name="rs_sibling_exchange",
    )(g)


def _rs_chip_exchange(p):
    nb, rows, _ = p.shape

    def body(p_ref, b_ref, send_sems, recv_sems, local_sem):
        x, y, c, chips = _place()
        me = 2 * x + y
        mine = pltpu.make_async_copy(p_ref.at[me], b_ref.at[me], local_sem)
        mine.start()
        copies = [pltpu.make_async_remote_copy(src_ref=p_ref.at[2 * px + py], dst_ref=b_ref.at[me],
                                               send_sem=send_sems.at[k], recv_sem=recv_sems.at[k],
                                               device_id=(px, py, c), device_id_type=MESH)
                  for k, (px, py) in enumerate(chips)]
        for cp in copies:
            cp.start()
        for cp in copies:
            cp.wait()
        mine.wait()

    return pl.pallas_call(
        body,
        in_specs=[ANY],
        out_specs=ANY,
        out_shape=jax.ShapeDtypeStruct((nb, rows, LANES), F32),
        scratch_shapes=[pltpu.SemaphoreType.DMA((3,)), pltpu.SemaphoreType.DMA((3,)), pltpu.SemaphoreType.DMA],
        name="rs_chip_exchange",
    )(p)


def _rs_sibling_gather(r):
    rows, _ = r.shape

    def body(r_ref, o_ref, send_sem, recv_sem, local_sem):
        x, y, c, _ = _place()
        mine = pltpu.make_async_copy(r_ref, o_ref.at[c], local_sem)
        mine.start()
        cp = pltpu.make_async_remote_copy(src_ref=r_ref, dst_ref=o_ref.at[c], send_sem=send_sem,
                                          recv_sem=recv_sem, device_id=(x, y, 1 - c), device_id_type=MESH)
        cp.start()
        cp.wait()
        mine.wait()

    return pl.pallas_call(
        body,
        in_specs=[ANY],
        out_specs=ANY,
        out_shape=jax.ShapeDtypeStruct((2, rows, LANES), F32),
        scratch_shapes=[pltpu.SemaphoreType.DMA, pltpu.SemaphoreType.DMA, pltpu.SemaphoreType.DMA],
        name="rs_sibling_gather",
    )(r)


def _reduce_scatter(g, core):
    a = _rs_sibling_exchange(g)
    p = _add_own_half(g, a, core)
    b = _rs_chip_exchange(p)
    r = _sum_chips(b)
    return _rs_sibling_gather(r)


def _all_reduce_small(v):
    rows, _ = v.shape
    n_dev = 8

    def body(v_ref, o_ref, gath, send_sems, recv_sems):
        x, y, c, _ = _place()
        me = 4 * x + 2 * y + c
        gath[me] = v_ref[...]
        copies = []
        for mask in range(1, n_dev):
            px = 1 - x if mask & 4 else x
            py = 1 - y if mask & 2 else y
            pc = 1 - c if mask & 1 else c
            copies.append(pltpu.make_async_remote_copy(
                src_ref=v_ref, dst_ref=gath.at[me], send_sem=send_sems.at[mask - 1],
                recv_sem=recv_sems.at[mask - 1], device_id=(px, py, pc), device_id_type=MESH))
        for cp in copies:
            cp.start()
        for cp in copies:
            cp.wait()
        acc = gath[0]
        for k in range(1, n_dev):
            acc = acc + gath[k]
        o_ref[...] = acc

    return pl.pallas_call(
        body,
        in_specs=[VMEM],
        out_specs=VMEM,
        out_shape=jax.ShapeDtypeStruct((rows, LANES), F32),
        scratch_shapes=[pltpu.VMEM((n_dev, rows, LANES), F32), pltpu.SemaphoreType.DMA((n_dev - 1,)),
                        pltpu.SemaphoreType.DMA((n_dev - 1,))],
        name="all_reduce_small",
    )(v)


def _flat_len(shapes):
    n = sum(int(np.prod(s)) for s in shapes)
    unit = 2 * FLAT_TILE_ROWS * LANES
    return -(-n // unit) * unit


def _pack(arrays, total):
    flat = jnp.concatenate([a.reshape(-1) for a in arrays])
    return jnp.pad(flat, (0, total - flat.shape[0]))


def _unpack(flat, shapes):
    out, off = [], 0
    for s in shapes:
        n = int(np.prod(s))
        out.append(flat[off:off + n].reshape(s))
        off += n
    return out


def _permute_w_in(w):
    pad = jnp.zeros(w.shape[:-1] + (N_IN_PAD - N_IN,), w.dtype)
    return jnp.concatenate([w[..., 0:1536], w[..., 1540:3076], w[..., 3084:3596], w[..., 3596:5132],
                            w[..., 5132:8204], w[..., 1536:1540], w[..., 3076:3080], w[..., 3080:3084], pad],
                           axis=-1)


def _unpermute_w_in(w):
    return jnp.concatenate([w[..., 0:1536], w[..., 8192:8196], w[..., 1536:3072], w[..., 8196:8200],
                            w[..., 8200:8204], w[..., 3072:3584], w[..., 3584:5120], w[..., 5120:8192]], axis=-1)


P_FQ, P_FK, P_FV, P_GQ, P_GK, P_GV, P_GZ, P_SQ, P_SK, P_SV, P_GATES, P_FF, P_GB, P_GA = (
    0, 512, 1024, 1536, 2048, 2560, 3072, 3584, 4096, 4608, 5120, 8192, 8196, 8200)


def rmsnorm(x, g):
    xf = x.astype(F32)
    y = xf * lax.rsqrt(jnp.mean(xf * xf, axis=-1, keepdims=True) + EPS)
    return (y * g.astype(F32)).astype(x.dtype)


def l2norm(x):
    xf = x.astype(F32)
    return xf * lax.rsqrt(jnp.sum(xf * xf, axis=-1, keepdims=True) + EPS)


def heads(x, n):
    return x.reshape(x.shape[:-1] + (n, -1))


def causal_dwconv(x, w):
    width, ch = w.shape
    return lax.conv_general_dilated(
        x, w[:, None, :].astype(x.dtype), window_strides=(1,), padding=[(width - 1, 0)],
        dimension_numbers=("NWC", "WIO", "NWC"), feature_group_count=ch)


def gated_delta_rule(q, k, v, g, beta):
    B, T, H, dk = q.shape
    dv = v.shape[-1]
    N = T // CHUNK

    def chunks(a):
        a = a.astype(F32).reshape((B, N, CHUNK, H) + a.shape[3:])
        return jnp.moveaxis(a, (1, 3), (0, 2))

    qc = chunks(q) * dk ** -0.5
    kc = chunks(k)
    vc = chunks(v)
    bc = chunks(beta)
    gc = jnp.cumsum(chunks(g), axis=-1)
    idx = jnp.arange(CHUNK)
    causal = idx[:, None] >= idx[None, :]
    strict = idx[:, None] > idx[None, :]
    decay = jnp.exp(jnp.where(causal, gc[..., :, None] - gc[..., None, :], -jnp.inf))
    kk = jnp.einsum("nbhcd,nbhed->nbhce", kc, kc)
    a_mat = jnp.where(strict, bc[..., :, None] * kk * decay, 0.0) + jnp.eye(CHUNK, dtype=F32)
    rhs = jnp.concatenate([vc * bc[..., None], kc * (bc * jnp.exp(gc))[..., None]], axis=-1)
    sol = lax.linalg.triangular_solve(a_mat, rhs, left_side=True, lower=True)
    u, w = sol[..., :dv], sol[..., dv:]
    attn = jnp.where(causal, jnp.einsum("nbhcd,nbhed->nbhce", qc, kc) * decay, 0.0)
    g_last = gc[..., -1]
    k_tail = kc * jnp.exp(g_last[..., None] - gc)[..., None]

    egl = jnp.broadcast_to(jnp.exp(g_last)[:, 0, :, None, None], (N, H, 1, dv))
    o = gdn_scan((qc * jnp.exp(gc)[..., None])[:, 0], u[:, 0], w[:, 0], attn[:, 0], k_tail[:, 0], egl)
    return o.transpose(0, 2, 1, 3).reshape(B, T, H, dv)


def _layer(x, mem, p):
    S, D = x.shape
    h = rmsnorm(x, p['norm_mix'])
    proj = matmul(h, p['w_in'])[None]

    def seg(off, width):
        return proj[..., off:off + width]

    fq = rmsnorm(heads(seg(P_FQ, 512), HEADS), p['fox_qnorm']).reshape(S, 512)
    fk = rmsnorm(heads(seg(P_FK, 512), HEADS), p['fox_knorm']).reshape(S, 512)
    logf = jax.nn.log_sigmoid((seg(P_FF, HEADS) + p['fox_fbias']).astype(F32))
    ya = fox_core(fq, fk, seg(P_FV, 512)[0], jnp.cumsum(logf[0], axis=0))

    qkv = jax.nn.silu(causal_dwconv(seg(P_GQ, 1536), p['gdn_conv']))
    cq, ck, cv = jnp.split(qkv, [512, 1024], axis=-1)
    beta = jax.nn.sigmoid(seg(P_GB, HEADS).astype(F32))
    g_log = -jnp.exp(p['gdn_a_log'].astype(F32)) * jax.nn.softplus((seg(P_GA, HEADS) + p['gdn_dt_bias']).astype(F32))
    o = gated_delta_rule(l2norm(heads(cq, HEADS)), l2norm(heads(ck, HEADS)), heads(cv, HEADS), g_log, beta)
    yb = rmsnorm(o, p['gdn_onorm']) * jax.nn.silu(heads(seg(P_GZ, 512), HEADS).astype(F32))
    yb = yb.astype(x.dtype).reshape(1, S, 512)

    yc = sb_core(seg(P_SQ, 512)[0], seg(P_SK, 512)[0], seg(P_SV, 512)[0])

    g = jax.nn.sigmoid((seg(P_GATES, N_BRANCH * D) + p['gate_bias']).astype(F32)).astype(x.dtype)
    g = g.reshape(S, N_BRANCH, D)
    mixed = (g[:, 0, :] * matmul(ya, p['w_oa']) + g[:, 1, :] * matmul(yb[0], p['w_ob'])
             + g[:, 2, :] * matmul(yc, p['w_oc']))
    x = x + matmul(mixed, p['w_out'])

    hq = rmsnorm(x, p['norm_xq'])
    hm = rmsnorm(mem, p['norm_mem'])
    q = rmsnorm(heads(matmul(hq, p['w_mq']), HEADS), p['mq_norm'])
    kv = matmul(hm, p['w_mkv'])
    k, v = jnp.split(kv, 2, axis=-1)
    k = rmsnorm(heads(k, HEADS), p['mk_norm'])
    v = heads(v, HEADS)
    s = jnp.einsum("qhd,khd->hqk", q, k).astype(F32) * MEM_DIM ** -0.5
    pr = jax.nn.softmax(s, axis=-1).astype(v.dtype)
    om = jnp.einsum("hqk,khd->qhd", pr, v).reshape(S, 512)
    x = x + matmul(om, p['w_mo'])

    hf = rmsnorm(x, p['norm_ffn'])
    u = causal_dwconv(matmul(hf, p['w_up'])[None], p['ffn_conv'])[0] + p['ffn_conv_b']
    a, b = jnp.split(u, 2, axis=-1)
    return x + matmul(jax.nn.silu(a) * b, p['w_down'])


def kernel(x, mem, norm_mix, w_in, fox_fbias, fox_qnorm, fox_knorm, gdn_conv, gdn_a_log, gdn_dt_bias, gdn_onorm, gate_bias, w_oa, w_ob, w_oc, w_out, norm_xq, norm_mem, w_mq, w_mkv, mq_norm, mk_norm, w_mo, norm_ffn, w_up, ffn_conv, ffn_conv_b, w_down, loss_target, m_norm_mix, m_w_in, m_fox_fbias, m_fox_qnorm, m_fox_knorm, m_gdn_conv, m_gdn_a_log, m_gdn_dt_bias, m_gdn_onorm, m_gate_bias, m_w_oa, m_w_ob, m_w_oc, m_w_out, m_norm_xq, m_norm_mem, m_w_mq, m_w_mkv, m_mq_norm, m_mk_norm, m_w_mo, m_norm_ffn, m_w_up, m_ffn_conv, m_ffn_conv_b, m_w_down, v_norm_mix, v_w_in, v_fox_fbias, v_fox_qnorm, v_fox_knorm, v_gdn_conv, v_gdn_a_log, v_gdn_dt_bias, v_gdn_onorm, v_gate_bias, v_w_oa, v_w_ob, v_w_oc, v_w_out, v_norm_xq, v_norm_mem, v_w_mq, v_w_mkv, v_mq_norm, v_mk_norm, v_w_mo, v_norm_ffn, v_w_up, v_ffn_conv, v_ffn_conv_b, v_w_down):
    args = (x, mem, norm_mix, w_in, fox_fbias, fox_qnorm, fox_knorm, gdn_conv, gdn_a_log, gdn_dt_bias, gdn_onorm, gate_bias, w_oa, w_ob, w_oc, w_out, norm_xq, norm_mem, w_mq, w_mkv, mq_norm, mk_norm, w_mo, norm_ffn, w_up, ffn_conv, ffn_conv_b, w_down)
    moments_m = (m_norm_mix, m_w_in, m_fox_fbias, m_fox_qnorm, m_fox_knorm, m_gdn_conv, m_gdn_a_log, m_gdn_dt_bias, m_gdn_onorm, m_gate_bias, m_w_oa, m_w_ob, m_w_oc, m_w_out, m_norm_xq, m_norm_mem, m_w_mq, m_w_mkv, m_mq_norm, m_mk_norm, m_w_mo, m_norm_ffn, m_w_up, m_ffn_conv, m_ffn_conv_b, m_w_down)
    moments_v = (v_norm_mix, v_w_in, v_fox_fbias, v_fox_qnorm, v_fox_knorm, v_gdn_conv, v_gdn_a_log, v_gdn_dt_bias, v_gdn_onorm, v_gate_bias, v_w_oa, v_w_ob, v_w_oc, v_w_out, v_norm_xq, v_norm_mem, v_w_mq, v_w_mkv, v_mq_norm, v_mk_norm, v_w_mo, v_norm_ffn, v_w_up, v_ffn_conv, v_ffn_conv_b, v_w_down)
    w = dict(zip(IN_NAMES, args))
    m = dict(zip(WEIGHTS, moments_m))
    v = dict(zip(WEIGHTS, moments_v))
    xs, mems, tgt = x[0], mem[0], loss_target[0]
    core = lax.axis_index("c").astype(jnp.int32).reshape(1)

    big = list(SHARDED)
    shard_shapes = [w[n].shape for n in big]
    total = _flat_len(shard_shapes)
    half_rows = total // (2 * LANES)
    small_shapes = [w[n].shape for n in SMALL]
    n_small = sum(int(np.prod(s)) for s in small_shapes) + 1
    small_total = -(-n_small // (8 * LANES)) * (8 * LANES)

    w_flat = _pack([w[n] for n in big], total)
    gathered = _all_gather_chips(w_flat.reshape(2, half_rows, LANES)).reshape(N_CHIPS, total)
    params = {}
    for n, blocks in zip(big, zip(*[_unpack(gathered[j], shard_shapes) for j in range(N_CHIPS)])):
        params[n] = jnp.concatenate(blocks, axis=SHARDED[n])
    params['w_in'] = _permute_w_in(params['w_in'])
    for n in SMALL:
        params[n] = w[n]

    def model(x0, pp):
        for layer in range(DEPTH):
            x0 = _layer(x0, mems, {n: a[layer] for n, a in pp.items()})
        return x0

    y, model_vjp = jax.vjp(model, xs, params)
    dy, loss_part = _loss_head(y, tgt)
    dx0, grads = model_vjp(dy)
    grads['w_in'] = _unpermute_w_in(grads['w_in'])

    def chip_blocks(g, axis):
        return jnp.stack(jnp.split(g, N_CHIPS, axis=axis)).reshape(N_CHIPS, -1)

    g_blocks = jnp.concatenate([chip_blocks(grads[n], SHARDED[n]) for n in big], axis=1)
    g_blocks = jnp.pad(g_blocks, ((0, 0), (0, total - g_blocks.shape[1])))
    g_halves = g_blocks.reshape(N_CHIPS, 2, half_rows, LANES).transpose(1, 0, 2, 3)
    g_flat = _reduce_scatter(g_halves, core).reshape(total // LANES, LANES)

    s_part = _pack([grads[n] for n in SMALL] + [loss_part[0, :1]], small_total)
    s_sum = _all_reduce_small(s_part.reshape(small_total // LANES, LANES))
    small_grads = _unpack(s_sum.reshape(-1), small_shapes + [(1,)])
    loss = small_grads.pop()[0]

    rows = total // LANES
    d_flat, nm_flat, nv_flat = _adamw(w_flat.reshape(rows, LANES), g_flat,
                                      _pack([m[n] for n in big], total).reshape(rows, LANES),
                                      _pack([v[n] for n in big], total).reshape(rows, LANES), "sharded")
    srows = small_total // LANES
    sd, snm, snv = _adamw(_pack([w[n] for n in SMALL], small_total).reshape(srows, LANES), s_sum,
                          _pack([m[n] for n in SMALL], small_total).reshape(srows, LANES),
                          _pack([v[n] for n in SMALL], small_total).reshape(srows, LANES), "replicated")

    out = {}
    for kind, flat_big, flat_small in (('grad', g_flat, s_sum), ('delta', d_flat, sd), ('new_m', nm_flat, snm),
                                       ('new_v', nv_flat, snv)):
        for n, a in zip(big, _unpack(flat_big.reshape(-1), shard_shapes)):
            out[kind, n] = a
        for n, a in zip(SMALL, _unpack(flat_small.reshape(-1), small_shapes)):
            out[kind, n] = a
    return (loss, dx0[None], *[out[kind, n] for kind in ('grad', 'delta', 'new_m', 'new_v') for n in WEIGHTS])
```

```python
import functools

import jax
import jax.numpy as jnp
import numpy as np
from jax import lax
from jax.experimental import pallas as pl
from jax.experimental.pallas import tpu as pltpu

F32 = jnp.float32
BF16 = jnp.bfloat16
MESH = pl.DeviceIdType.MESH
ANY = pl.BlockSpec(memory_space=pl.ANY)
VMEM = pl.BlockSpec(memory_space=pltpu.VMEM)

D_MODEL = 1024
DEPTH = 4
CHUNK = 64
Q_BLOCK = 128
EPS = 1e-6
HEADS = 4
HEAD_DIM = 128
GDN_CONV = 4
MEM_DIM = 128
D_FF = 2816
N_BRANCH = 3
N_IN = 8204
N_IN_PAD = 8320

ADAM_LR = 0.001
ADAM_B1 = 0.9
ADAM_B2 = 0.999
ADAM_EPS = 1e-08
ADAM_WD = 0.01
ADAM_STEP = 10

N_CHIPS = 4
LANES = 128
FLAT_TILE_ROWS = 2048
VMEM_LIMIT = 48 * 1024 * 1024

IN_NAMES = ['x', 'mem', 'norm_mix', 'w_in', 'fox_fbias', 'fox_qnorm', 'fox_knorm', 'gdn_conv', 'gdn_a_log',
            'gdn_dt_bias', 'gdn_onorm', 'gate_bias', 'w_oa', 'w_ob', 'w_oc', 'w_out', 'norm_xq', 'norm_mem',
            'w_mq', 'w_mkv', 'mq_norm', 'mk_norm', 'w_mo', 'norm_ffn', 'w_up', 'ffn_conv', 'ffn_conv_b', 'w_down']
WEIGHTS = IN_NAMES[2:]
SHARDED = {'w_in': 2, 'gdn_conv': 2, 'w_oa': 2, 'w_ob': 2, 'w_oc': 2, 'w_out': 1, 'w_mq': 1, 'w_mkv': 1,
           'w_mo': 2, 'w_up': 2, 'ffn_conv': 2, 'w_down': 1}
SMALL = [n for n in WEIGHTS if n not in SHARDED]


def _pick(n, cands):
    for c in cands:
        if n % c == 0:
            return c
    return n


_DOT_DIMS = {
    'nn': (((1,), (0,)), ((), ())),
    'nt': (((1,), (1,)), ((), ())),
    'tn': (((0,), (0,)), ((), ())),
}


def _mm(a, b, mode):
    if mode == 'nn':
        (m, c), (_, n) = a.shape, b.shape
    elif mode == 'nt':
        (m, c), (n, _) = a.shape, b.shape
    else:
        (c, m), (_, n) = a.shape, b.shape
    tm = _pick(m, (512, 256, 128))
    tn = _pick(n, (1664, 1408, 1024, 512, 256, 128))
    tc = _pick(c, (1024, 1408, 640, 512, 256, 128))
    if mode == 'tn':
        a_spec = pl.BlockSpec((tc, tm), lambda i, j, k: (k, i))
    else:
        a_spec = pl.BlockSpec((tm, tc), lambda i, j, k: (i, k))
    if mode == 'nt':
        b_spec = pl.BlockSpec((tn, tc), lambda i, j, k: (j, k))
    else:
        b_spec = pl.BlockSpec((tc, tn), lambda i, j, k: (k, j))
    dims = _DOT_DIMS[mode]

    def body(a_ref, b_ref, o_ref):
        @pl.when(pl.program_id(2) == 0)
        def _():
            o_ref[...] = jnp.zeros_like(o_ref)

        o_ref[...] += lax.dot_general(a_ref[...].astype(BF16), b_ref[...].astype(BF16), dims,
                                      preferred_element_type=F32)

    return pl.pallas_call(
        body,
        grid=(m // tm, n // tn, c // tc),
        in_specs=[a_spec, b_spec],
        out_specs=pl.BlockSpec((tm, tn), lambda i, j, k: (i, j)),
        out_shape=jax.ShapeDtypeStruct((m, n), F32),
        compiler_params=pltpu.CompilerParams(
            dimension_semantics=("parallel", "parallel", "arbitrary"), vmem_limit_bytes=VMEM_LIMIT),
        name=f"mm_{mode}_{m}x{c}x{n}",
    )(a, b)


@jax.custom_vjp
def matmul(a, w):
    return _mm(a, w, 'nn')


def _matmul_fwd(a, w):
    return _mm(a, w, 'nn'), (a, w)


def _matmul_bwd(res, dy):
    a, w = res
    return _mm(dy, w, 'nt'), _mm(a, dy, 'tn')


matmul.defvjp(_matmul_fwd, _matmul_bwd)


ATT_TILE = 256
_NT = (((1,), (1,)), ((), ()))
_TN = (((0,), (0,)), ((), ()))
_NN = (((1,), (0,)), ((), ()))


def _dot(a, b, dims):
    return lax.dot_general(a, b, dims, preferred_element_type=F32)


def _att_specs(s, t):
    tile = pl.BlockSpec((t, HEAD_DIM), lambda h, i: (i, h))
    whole = pl.BlockSpec((s, HEAD_DIM), lambda h, i: (0, h))
    col = pl.BlockSpec((None, t, 1), lambda h, i: (h, i, 0))
    row = pl.BlockSpec((None, 1, s), lambda h, i: (h, 0, 0))
    return tile, whole, col, row


def _tile_mask(t, strict):
    r = lax.broadcasted_iota(jnp.int32, (t, t), 0)
    c = lax.broadcasted_iota(jnp.int32, (t, t), 1)
    return c < r if strict else c <= r


def _fox_fwd_call(q, k, v, c_col, c_row):
    s, w = q.shape
    t = min(ATT_TILE, s)
    scale = HEAD_DIM ** -0.5

    def body(q_ref, k_ref, v_ref, cc_ref, cr_ref, o_ref, lse_ref):
        i = pl.program_id(1)
        qb = q_ref[...]
        cq = cc_ref[...]
        mask = _tile_mask(t, False)

        def block(j, carry, diag):
            m, l, acc = carry
            sl = pl.ds(pl.multiple_of(j * t, t), t)
            sc = _dot(qb, k_ref[sl, :], _NT) * scale + (cq - cr_ref[:, sl])
            if diag:
                sc = jnp.where(mask, sc, -jnp.inf)
            m_new = jnp.maximum(m, jnp.max(sc, axis=-1, keepdims=True))
            p = jnp.exp(sc - m_new)
            alpha = jnp.exp(m - m_new)
            l = alpha * l + jnp.sum(p, axis=-1, keepdims=True)
            acc = alpha * acc + _dot(p.astype(BF16), v_ref[sl, :], _NN)
            return m_new, l, acc

        init = (jnp.full((t, 1), -jnp.inf, F32), jnp.zeros((t, 1), F32), jnp.zeros((t, HEAD_DIM), F32))
        carry = lax.fori_loop(0, i, lambda j, cr: block(j, cr, False), init)
        m, l, acc = block(i, carry, True)
        o_ref[...] = acc / l
        lse_ref[...] = m + jnp.log(l)

    tile, whole, col, row = _att_specs(s, t)
    return pl.pallas_call(
        body,
        grid=(w // HEAD_DIM, s // t),
        in_specs=[tile, whole, whole, col, row],
        out_specs=[tile, col],
        out_shape=[jax.ShapeDtypeStruct((s, w), F32), jax.ShapeDtypeStruct((w // HEAD_DIM, s, 1), F32)],
        compiler_params=pltpu.CompilerParams(dimension_semantics=("parallel", "parallel"),
                                             vmem_limit_bytes=VMEM_LIMIT),
        name="fox_fwd",
    )(q, k, v, c_col, c_row)


def _fox_bwd_call(q, k, v, c_col, c_row, lse, do):
    s, w = q.shape
    t = min(ATT_TILE, s)
    scale = HEAD_DIM ** -0.5

    def body(q_ref, k_ref, v_ref, cc_ref, cr_ref, lse_ref, do_ref, dq_ref, dk_ref, dv_ref, dcr_ref):
        i = pl.program_id(1)

        @pl.when(i == 0)
        def _():
            dk_ref[...] = jnp.zeros_like(dk_ref)
            dv_ref[...] = jnp.zeros_like(dv_ref)
            dcr_ref[...] = jnp.zeros_like(dcr_ref)

        qb = q_ref[...]
        do16 = do_ref[...].astype(BF16)
        lse_q = lse_ref[...]
        cq = cc_ref[...]
        mask = _tile_mask(t, False)

        def probs(j, diag):
            sl = pl.ds(pl.multiple_of(j * t, t), t)
            ks = k_ref[sl, :]
            sc = _dot(qb, ks, _NT) * scale + (cq - cr_ref[:, sl])
            p = jnp.exp(sc - lse_q)
            if diag:
                p = jnp.where(mask, p, 0.0)
            return sl, ks, p, _dot(do16, v_ref[sl, :], _NT)

        def row_dot(j, acc, diag):
            _, _, p, dp = probs(j, diag)
            return acc + jnp.sum(p * dp, axis=-1, keepdims=True)

        delta = lax.fori_loop(0, i, lambda j, a: row_dot(j, a, False), jnp.zeros((t, 1), F32))
        delta = row_dot(i, delta, True)

        def block(j, dq, diag):
            sl, ks, p, dp = probs(j, diag)
            ds = p * (dp - delta)
            ds16 = ds.astype(BF16)
            dv_ref[sl, :] += _dot(p.astype(BF16), do16, _TN)
            dk_ref[sl, :] += _dot(ds16, qb, _TN) * scale
            dcr_ref[:, sl] += -jnp.sum(ds, axis=0, keepdims=True)
            return dq + _dot(ds16, ks, _NN) * scale

        dq = lax.fori_loop(0, i, lambda j, a: block(j, a, False), jnp.zeros((t, HEAD_DIM), F32))
        dq_ref[...] = block(i, dq, True)

    tile, whole, col, row = _att_specs(s, t)
    full = jax.ShapeDtypeStruct((s, w), F32)
    return pl.pallas_call(
        body,
        grid=(w // HEAD_DIM, s // t),
        in_specs=[tile, whole, whole, col, row, col, tile],
        out_specs=[tile, whole, whole, row],
        out_shape=[full, full, full, jax.ShapeDtypeStruct((w // HEAD_DIM, 1, s), F32)],
        compiler_params=pltpu.CompilerParams(dimension_semantics=("parallel", "arbitrary"),
                                             vmem_limit_bytes=VMEM_LIMIT),
        name="fox_bwd",
    )(q, k, v, c_col, c_row, lse, do)


@jax.custom_vjp
def fox_core(q, k, v, c):
    return _fox_fwd(q, k, v, c)[0]


def _fox_fwd(q, k, v, c):
    q16, k16, v16 = q.astype(BF16), k.astype(BF16), v.astype(BF16)
    c_col, c_row = c.T[:, :, None], c.T[:, None, :]
    o, lse = _fox_fwd_call(q16, k16, v16, c_col, c_row)
    return o, (q16, k16, v16, c_col, c_row, lse)


def _fox_bwd(res, do):
    dq, dk, dv, dcr = _fox_bwd_call(*res, do)
    return dq, dk, dv, dcr[:, 0, :].T


fox_core.defvjp(_fox_fwd, _fox_bwd)


def _neg_softplus(z):
    e = jnp.exp(-jnp.abs(z))
    return -(jnp.maximum(z, 0.0) + jnp.log(1.0 + e)), e


def _split_dot(x, tri):
    hi = x.astype(BF16)
    lo = (x - hi.astype(F32)).astype(BF16)
    return _dot(hi, tri, _NN) + _dot(lo, tri, _NN)


def _sb_fwd_call(q, k, v):
    s, w = q.shape
    t = min(ATT_TILE, s)
    scale = HEAD_DIM ** -0.5

    def body(q_ref, k_ref, v_ref, o_ref, tot_ref):
        i = pl.program_id(1)
        qb = q_ref[...]
        mask = _tile_mask(t, True)
        r = lax.broadcasted_iota(jnp.int32, (t, t), 0)
        c = lax.broadcasted_iota(jnp.int32, (t, t), 1)
        tri = (r >= c).astype(BF16)

        def block(j, carry, diag):
            later, acc = carry
            sl = pl.ds(pl.multiple_of(j * t, t), t)
            z = _dot(qb, k_ref[sl, :], _NT) * scale
            lk, _ = _neg_softplus(z)
            if diag:
                lk = jnp.where(mask, lk, 0.0)
            a = jnp.exp(z + _split_dot(lk, tri) + later)
            if diag:
                a = jnp.where(mask, a, 0.0)
            acc = acc + _dot(a.astype(BF16), v_ref[sl, :], _NN)
            return later + jnp.sum(lk, axis=-1, keepdims=True), acc

        carry = block(i, (jnp.zeros((t, 1), F32), jnp.zeros((t, HEAD_DIM), F32)), True)
        later, acc = lax.fori_loop(0, i, lambda jj, cr: block(i - 1 - jj, cr, False), carry)
        o_ref[...] = acc
        tot_ref[...] = later

    tile, whole, col, _ = _att_specs(s, t)
    return pl.pallas_call(
        body,
        grid=(w // HEAD_DIM, s // t),
        in_specs=[tile, whole, whole],
        out_specs=[tile, col],
        out_shape=[jax.ShapeDtypeStruct((s, w), F32), jax.ShapeDtypeStruct((w // HEAD_DIM, s, 1), F32)],
        compiler_params=pltpu.CompilerParams(dimension_semantics=("parallel", "parallel"),
                                             vmem_limit_bytes=VMEM_LIMIT),
        name="sb_fwd",
    )(q, k, v)


def _sb_bwd_call(q, k, v, tot, do):
    s, w = q.shape
    t = min(ATT_TILE, s)
    scale = HEAD_DIM ** -0.5

    def body(q_ref, k_ref, v_ref, tot_ref, do_ref, dq_ref, dk_ref, dv_ref):
        i = pl.program_id(1)

        @pl.when(i == 0)
        def _():
            dk_ref[...] = jnp.zeros_like(dk_ref)
            dv_ref[...] = jnp.zeros_like(dv_ref)

        qb = q_ref[...]
        do16 = do_ref[...].astype(BF16)
        tot_q = tot_ref[...]
        mask = _tile_mask(t, True)
        r = lax.broadcasted_iota(jnp.int32, (t, t), 0)
        c = lax.broadcasted_iota(jnp.int32, (t, t), 1)
        tri_before = (r < c).astype(BF16)
        tri_upto = (r <= c).astype(BF16)

        def block(j, carry, diag):
            before, dl_before, dq = carry
            sl = pl.ds(pl.multiple_of(j * t, t), t)
            ks = k_ref[sl, :]
            z = _dot(qb, ks, _NT) * scale
            lk, e = _neg_softplus(z)
            if diag:
                lk = jnp.where(mask, lk, 0.0)
            sig = jnp.where(z >= 0, 1.0, e) / (1.0 + e)
            a = jnp.exp(z + (tot_q - (_split_dot(lk, tri_before) + before)))
            if diag:
                a = jnp.where(mask, a, 0.0)
            dl = a * _dot(do16, v_ref[sl, :], _NT)
            dz = dl - sig * (_dot(dl.astype(BF16), tri_upto, _NN) + dl_before)
            if diag:
                dz = jnp.where(mask, dz, 0.0)
            dz16 = dz.astype(BF16)
            dv_ref[sl, :] += _dot(a.astype(BF16), do16, _TN)
            dk_ref[sl, :] += _dot(dz16, qb, _TN) * scale
            return (before + jnp.sum(lk, axis=-1, keepdims=True), dl_before + jnp.sum(dl, axis=-1, keepdims=True),
                    dq + _dot(dz16, ks, _NN) * scale)

        init = (jnp.zeros((t, 1), F32), jnp.zeros((t, 1), F32), jnp.zeros((t, HEAD_DIM), F32))
        carry = lax.fori_loop(0, i, lambda j, cr: block(j, cr, False), init)
        dq_ref[...] = block(i, carry, True)[2]

    tile, whole, col, _ = _att_specs(s, t)
    full = jax.ShapeDtypeStruct((s, w), F32)
    return pl.pallas_call(
        body,
        grid=(w // HEAD_DIM, s // t),
        in_specs=[tile, whole, whole, col, tile],
        out_specs=[tile, whole, whole],
        out_shape=[full, full, full],
        compiler_params=pltpu.CompilerParams(dimension_semantics=("parallel", "arbitrary"),
                                             vmem_limit_bytes=VMEM_LIMIT),
        name="sb_bwd",
    )(q, k, v, tot, do)


@jax.custom_vjp
def sb_core(q, k, v):
    return _sb_fwd(q, k, v)[0]


def _sb_fwd(q, k, v):
    q16, k16, v16 = q.astype(BF16), k.astype(BF16), v.astype(BF16)
    o, tot = _sb_fwd_call(q16, k16, v16)
    return o, (q16, k16, v16, tot)


def _sb_bwd(res, do):
    return tuple(_sb_bwd_call(*res, do))


sb_core.defvjp(_sb_fwd, _sb_bwd)


def _gdn_specs(h, c, d):
    vec = pl.BlockSpec((None, h, c, d), lambda n: (n, 0, 0, 0))
    sq = pl.BlockSpec((None, h, c, c), lambda n: (n, 0, 0, 0))
    dec = pl.BlockSpec((None, h, 1, d), lambda n: (n, 0, 0, 0))
    st = pl.BlockSpec((None, h, d, d), lambda n: (n, 0, 0, 0))
    return vec, sq, dec, st


def _b16(x):
    return x.astype(BF16)


def _gdn_scan_fwd_call(qg, u, w, attn, kt, egl):
    n, h, c, d = qg.shape

    def body(qg_ref, u_ref, w_ref, attn_ref, kt_ref, egl_ref, o_ref, st_ref, state):
        @pl.when(pl.program_id(0) == 0)
        def _():
            state[...] = jnp.zeros_like(state)

        for hh in range(h):
            s0 = state[hh]
            st_ref[hh] = s0
            s16 = _b16(s0)
            vn = u_ref[hh] - _dot(_b16(w_ref[hh]), s16, _NN)
            vn16 = _b16(vn)
            o_ref[hh] = _dot(_b16(qg_ref[hh]), s16, _NN) + _dot(_b16(attn_ref[hh]), vn16, _NN)
            state[hh] = s0 * egl_ref[hh] + _dot(_b16(kt_ref[hh]), vn16, _TN)

    vec, sq, dec, st = _gdn_specs(h, c, d)
    return pl.pallas_call(
        body,
        grid=(n,),
        in_specs=[vec, vec, vec, sq, vec, dec],
        out_specs=[vec, st],
        out_shape=[jax.ShapeDtypeStruct((n, h, c, d), F32), jax.ShapeDtypeStruct((n, h, d, d), F32)],
        scratch_shapes=[pltpu.VMEM((h, d, d), F32)],
        compiler_params=pltpu.CompilerParams(dimension_semantics=("arbitrary",)),
        name="gdn_scan_fwd",
    )(qg, u, w, attn, kt, egl)


def _gdn_scan_bwd_call(qg, u, w, attn, kt, egl, states, do):
    n, h, c, d = qg.shape

    def body(qg_ref, u_ref, w_ref, attn_ref, kt_ref, egl_ref, st_ref, do_ref,
             dqg_ref, du_ref, dw_ref, dattn_ref, dkt_ref, degl_ref, dstate):
        @pl.when(pl.program_id(0) == 0)
        def _():
            dstate[...] = jnp.zeros_like(dstate)

        for hh in range(h):
            s0 = st_ref[hh]
            s16 = _b16(s0)
            big_d = dstate[hh]
            d16 = _b16(big_d)
            w16, kt16, qg16, attn16 = _b16(w_ref[hh]), _b16(kt_ref[hh]), _b16(qg_ref[hh]), _b16(attn_ref[hh])
            do16 = _b16(do_ref[hh])
            vn16 = _b16(u_ref[hh] - _dot(w16, s16, _NN))
            dvn = _dot(attn16, do16, _TN) + _dot(kt16, d16, _NN)
            dvn16 = _b16(dvn)
            du_ref[hh] = dvn
            dattn_ref[hh] = _dot(do16, vn16, _NT)
            dqg_ref[hh] = _dot(do16, s16, _NT)
            dkt_ref[hh] = _dot(vn16, d16, _NT)
            dw_ref[hh] = -_dot(dvn16, s16, _NT)
            degl_ref[hh] = jnp.sum(big_d * s0, axis=0, keepdims=True)
            dstate[hh] = big_d * egl_ref[hh] + _dot(qg16, do16, _TN) - _dot(w16, dvn16, _TN)

    vec, sq, dec, st = _gdn_specs(h, c, d)
    rev = lambda spec: pl.BlockSpec(spec.block_shape, lambda i: (n - 1 - i, 0, 0, 0))
    vec, sq, dec, st = rev(vec), rev(sq), rev(dec), rev(st)
    vshape = jax.ShapeDtypeStruct((n, h, c, d), F32)
    return pl.pallas_call(
        body,
        grid=(n,),
        in_specs=[vec, vec, vec, sq, vec, dec, st, vec],
        out_specs=[vec, vec, vec, sq, vec, dec],
        out_shape=[vshape, vshape, vshape, jax.ShapeDtypeStruct((n, h, c, c), F32), vshape,
                   jax.ShapeDtypeStruct((n, h, 1, d), F32)],
        scratch_shapes=[pltpu.VMEM((h, d, d), F32)],
        compiler_params=pltpu.CompilerParams(dimension_semantics=("arbitrary",)),
        name="gdn_scan_bwd",
    )(qg, u, w, attn, kt, egl, states, do)


@jax.custom_vjp
def gdn_scan(qg, u, w, attn, kt, egl):
    return _gdn_scan_fwd_call(qg, u, w, attn, kt, egl)[0]


def _gdn_scan_fwd(qg, u, w, attn, kt, egl):
    o, states = _gdn_scan_fwd_call(qg, u, w, attn, kt, egl)
    return o, (qg, u, w, attn, kt, egl, states)


def _gdn_scan_bwd(res, do):
    return tuple(_gdn_scan_bwd_call(*res, do))


gdn_scan.defvjp(_gdn_scan_fwd, _gdn_scan_bwd)


CONV_ROWS = 512
HALO = 8


def _sigmoid(x):
    return 1.0 / (1.0 + jnp.exp(-x))


def _shifted(ext, k, rows):
    if k == 0:
        return ext[HALO:HALO + rows]
    return pltpu.roll(ext, k % ext.shape[0], 0)[HALO:HALO + rows]


def _conv_specs(s, ch, tr, tc, off):
    per, last = tr // HALO, s // HALO - 1
    blk = pl.BlockSpec((tr, tc), lambda j, i: (i, j + off))
    prev = pl.BlockSpec((HALO, tc), lambda j, i: (jnp.maximum(i * per - 1, 0), j + off))
    nxt = pl.BlockSpec((HALO, tc), lambda j, i: (jnp.minimum((i + 1) * per, last), j + off))
    return blk, prev, nxt


def _dwconv_fwd_call(x, w, b, gated):
    s, ch = x.shape
    taps = w.shape[0]
    out_ch = ch // 2 if gated else ch
    tr = _pick(s, (CONV_ROWS, 256, 128, 64, 32, 16, 8))
    tc = _pick(out_ch, (1408, 512, 384, 256, 128))
    n_j = out_ch // tc
    parts = (0, n_j) if gated else (0,)

    def conv(x_ref, p_ref, w_ref, first):
        xb = x_ref[...]
        ext = jnp.concatenate([jnp.where(first, 0.0, p_ref[...]), xb], axis=0)
        y = w_ref[taps - 1:taps, :] * xb
        for k in range(1, taps):
            y = y + w_ref[taps - 1 - k:taps - k, :] * _shifted(ext, k, tr)
        return y

    def body(*refs):
        first = pl.program_id(1) == 0
        if gated:
            xa, pa, wa, ba, xb_, pb, wb, bb, o_ref = refs
            a = conv(xa, pa, wa, first) + ba[...]
            g = conv(xb_, pb, wb, first) + bb[...]
            o_ref[...] = a * _sigmoid(a) * g
        else:
            xa, pa, wa, o_ref = refs
            a = conv(xa, pa, wa, first)
            o_ref[...] = a * _sigmoid(a)

    in_specs, args = [], []
    for off in parts:
        blk, prev, _ = _conv_specs(s, ch, tr, tc, off)
        in_specs += [blk, prev, pl.BlockSpec((taps, tc), lambda j, i, off=off: (0, j + off))]
        args += [x, x, w]
        if gated:
            in_specs.append(pl.BlockSpec((1, tc), lambda j, i, off=off: (0, j + off)))
            args.append(b)
    return pl.pallas_call(
        body,
        grid=(n_j, s // tr),
        in_specs=in_specs,
        out_specs=pl.BlockSpec((tr, tc), lambda j, i: (i, j)),
        out_shape=jax.ShapeDtypeStruct((s, out_ch), F32),
        compiler_params=pltpu.CompilerParams(dimension_semantics=("parallel", "parallel"),
                                             vmem_limit_bytes=VMEM_LIMIT),
        name="dwconv_gate_fwd" if gated else "dwconv_silu_fwd",
    )(*args)


def _dwconv_bwd_call(x, w, b, do, gated):
    s, ch = x.shape
    taps = w.shape[0]
    out_ch = ch // 2 if gated else ch
    tr = _pick(s, (CONV_ROWS, 256, 128, 64, 32, 16, 8))
    tc = _pick(out_ch, (1408, 512, 384, 256, 128))
    n_j, n_i = out_ch // tc, s // tr
    parts = (0, n_j) if gated else (0,)
    ext_rows = tr + 2 * HALO

    def pre_act(x_ref, p_ref, n_ref, w_ref, first, last):
        ext = jnp.concatenate([jnp.where(first, 0.0, p_ref[...]), x_ref[...], n_ref[...]], axis=0)
        y = w_ref[taps - 1:taps, :] * ext
        for k in range(1, taps):
            y = y + w_ref[taps - 1 - k:taps - k, :] * pltpu.roll(ext, k, 0)
        return ext, y

    def grads(ext, dy, w_ref, dx_ref, dw_ref):
        dx = w_ref[taps - 1:taps, :] * dy[HALO:HALO + tr]
        for k in range(1, taps):
            dx = dx + w_ref[taps - 1 - k:taps - k, :] * _shifted(dy, -k, tr)
        dx_ref[...] = dx
        dyb = dy[HALO:HALO + tr]
        for k in range(taps):
            dw_ref[taps - 1 - k:taps - k, :] += jnp.sum(dyb * _shifted(ext, k, tr), axis=0, keepdims=True)
        return dyb

    def body(*refs):
        i = pl.program_id(1)
        first, last = i == 0, i == n_i - 1
        rows = lax.broadcasted_iota(jnp.int32, (ext_rows, 1), 0)
        inside = jnp.logical_and(rows >= HALO, jnp.logical_or(rows < HALO + tr, jnp.logical_not(last)))
        if gated:
            (xa, pa, na, wa, ba, xb_, pb, nb, wb, bb, do_ref, don_ref,
             dxa_ref, dxb_ref, dwa_ref, dwb_ref, dba_ref, dbb_ref) = refs
        else:
            xa, pa, na, wa, do_ref, don_ref, dxa_ref, dwa_ref = refs

        @pl.when(first)
        def _():
            dwa_ref[...] = jnp.zeros_like(dwa_ref)
            if gated:
                dwb_ref[...] = jnp.zeros_like(dwb_ref)
                dba_ref[...] = jnp.zeros_like(dba_ref)
                dbb_ref[...] = jnp.zeros_like(dbb_ref)

        d_out = jnp.concatenate([jnp.zeros((HALO, tc), F32), do_ref[...], don_ref[...]], axis=0)
        d_out = jnp.where(inside, d_out, 0.0)
        ext_a, a = pre_act(xa, pa, na, wa, first, last)
        if gated:
            a = a + ba[...]
            ext_b, g = pre_act(xb_, pb, nb, wb, first, last)
            g = g + bb[...]
            sg = _sigmoid(a)
            silu = a * sg
            dya = jnp.where(inside, d_out * g * (sg + silu * (1.0 - sg)), 0.0)
            dyg = jnp.where(inside, d_out * silu, 0.0)
            dba_ref[...] += jnp.sum(grads(ext_a, dya, wa, dxa_ref, dwa_ref), axis=0, keepdims=True)
            dbb_ref[...] += jnp.sum(grads(ext_b, dyg, wb, dxb_ref, dwb_ref), axis=0, keepdims=True)
        else:
            sg = _sigmoid(a)
            dya = jnp.where(inside, d_out * (sg + a * sg * (1.0 - sg)), 0.0)
            grads(ext_a, dya, wa, dxa_ref, dwa_ref)

    in_specs, args = [], []
    for off in parts:
        blk, prev, nxt = _conv_specs(s, ch, tr, tc, off)
        in_specs += [blk, prev, nxt, pl.BlockSpec((taps, tc), lambda j, i, off=off: (0, j + off))]
        args += [x, x, x, w]
        if gated:
            in_specs.append(pl.BlockSpec((1, tc), lambda j, i, off=off: (0, j + off)))
            args.append(b)
    blk, _, nxt = _conv_specs(s, out_ch, tr, tc, 0)
    in_specs += [blk, nxt]
    args += [do, do]
    n_half = len(parts)
    out_specs = ([blk] * n_half + [pl.BlockSpec((taps, tc), lambda j, i: (0, j))] * n_half
                 + ([pl.BlockSpec((1, tc), lambda j, i: (0, j))] * n_half if gated else []))
    out_shape = ([jax.ShapeDtypeStruct((s, out_ch), F32)] * n_half + [jax.ShapeDtypeStruct((taps, out_ch), F32)] * n_half
                 + ([jax.ShapeDtypeStruct((1, out_ch), F32)] * n_half if gated else []))
    return pl.pallas_call(
        body,
        grid=(n_j, n_i),
        in_specs=in_specs,
        out_specs=out_specs,
        out_shape=out_shape,
        compiler_params=pltpu.CompilerParams(dimension_semantics=("parallel", "arbitrary"),
                                             vmem_limit_bytes=VMEM_LIMIT),
        name="dwconv_gate_bwd" if gated else "dwconv_silu_bwd",
    )(*args)


@jax.custom_vjp
def conv_gate(u, w, b):
    return _dwconv_fwd_call(u, w, b[None], True)


def _conv_gate_fwd(u, w, b):
    return _dwconv_fwd_call(u, w, b[None], True), (u, w, b)


def _conv_gate_bwd(res, do):
    u, w, b = res
    dxa, dxb, dwa, dwb, dba, dbb = _dwconv_bwd_call(u, w, b[None], do, True)
    return (jnp.concatenate([dxa, dxb], axis=1), jnp.concatenate([dwa, dwb], axis=1),
            jnp.concatenate([dba, dbb], axis=1)[0])


conv_gate.defvjp(_conv_gate_fwd, _conv_gate_bwd)


@jax.custom_vjp
def conv_silu(x, w):
    return _dwconv_fwd_call(x, w, None, False)


def _conv_silu_fwd(x, w):
    return _dwconv_fwd_call(x, w, None, False), (x, w)


def _conv_silu_bwd(res, do):
    x, w = res
    dx, dw = _dwconv_bwd_call(x, w, None, do, False)
    return dx, dw


conv_silu.defvjp(_conv_silu_fwd, _conv_silu_bwd)


def _flat_rows(rows):
    return _pick(rows, (FLAT_TILE_ROWS, 1024, 512, 256, 128, 64, 32, 16, 8))


def _add_own_half(g, a, core):
    _, nb, rows, _ = g.shape
    tr = _flat_rows(rows)

    def body(c_ref, g_ref, a_ref, o_ref):
        o_ref[...] = g_ref[...] + a_ref[...]

    return pl.pallas_call(
        body,
        grid_spec=pltpu.PrefetchScalarGridSpec(
            num_scalar_prefetch=1,
            grid=(nb, rows // tr),
            in_specs=[pl.BlockSpec((None, None, tr, LANES), lambda j, r, c_ref: (c_ref[0], j, r, 0)),
                      pl.BlockSpec((None, tr, LANES), lambda j, r, c_ref: (j, r, 0))],
            out_specs=pl.BlockSpec((None, tr, LANES), lambda j, r, c_ref: (j, r, 0)),
        ),
        out_shape=jax.ShapeDtypeStruct((nb, rows, LANES), F32),
        compiler_params=pltpu.CompilerParams(dimension_semantics=("parallel", "parallel")),
        name="rs_add_own_half",
    )(core, g, a)


def _sum_chips(b, core):
    nb, rows, _ = b.shape
    tr = _flat_rows(rows)

    def body(c_ref, b_ref, o_ref):
        acc = b_ref[0] + b_ref[1]
        for j in range(2, nb):
            acc = acc + b_ref[j]
        o_ref[...] = acc

    return pl.pallas_call(
        body,
        grid_spec=pltpu.PrefetchScalarGridSpec(
            num_scalar_prefetch=1,
            grid=(rows // tr,),
            in_specs=[pl.BlockSpec((nb, tr, LANES), lambda r, c_ref: (0, r, 0))],
            out_specs=pl.BlockSpec((None, tr, LANES), lambda r, c_ref: (c_ref[0], r, 0)),
        ),
        out_shape=jax.ShapeDtypeStruct((2, rows, LANES), F32),
        compiler_params=pltpu.CompilerParams(dimension_semantics=("parallel",)),
        name="rs_sum_chips",
    )(core, b)


def _adamw(w, g, m, v, tag):
    rows = w.shape[0]
    tr = _flat_rows(rows)

    def body(w_ref, g_ref, m_ref, v_ref, d_ref, nm_ref, nv_ref):
        gg = g_ref[...]
        nm = ADAM_B1 * m_ref[...] + (1.0 - ADAM_B1) * gg
        nv = ADAM_B2 * v_ref[...] + (1.0 - ADAM_B2) * jnp.square(gg)
        m_hat = nm / (1.0 - ADAM_B1 ** ADAM_STEP)
        v_hat = nv / (1.0 - ADAM_B2 ** ADAM_STEP)
        d_ref[...] = -ADAM_LR * (m_hat / (jnp.sqrt(v_hat) + ADAM_EPS) + ADAM_WD * w_ref[...])
        nm_ref[...] = nm
        nv_ref[...] = nv

    spec = pl.BlockSpec((tr, LANES), lambda r: (r, 0))
    shape = jax.ShapeDtypeStruct((rows, LANES), F32)
    return pl.pallas_call(
        body,
        grid=(rows // tr,),
        in_specs=[spec] * 4,
        out_specs=[spec] * 3,
        out_shape=[shape] * 3,
        compiler_params=pltpu.CompilerParams(dimension_semantics=("parallel",)),
        name=f"adamw_{tag}",
    )(w, g, m, v)


def _loss_head(y, t):
    s, d = y.shape
    tr = _pick(s, (512, 256, 128, 64, 32, 16, 8))

    def body(y_ref, t_ref, dy_ref, l_ref):
        @pl.when(pl.program_id(0) == 0)
        def _():
            l_ref[...] = jnp.zeros_like(l_ref)

        diff = y_ref[...] - t_ref[...]
        dy_ref[...] = diff / d
        row = jnp.mean(jnp.square(diff), axis=-1, keepdims=True)
        l_ref[...] += 0.5 * jnp.sum(row, axis=0, keepdims=True)

    return pl.pallas_call(
        body,
        grid=(s // tr,),
        in_specs=[pl.BlockSpec((tr, d), lambda r: (r, 0))] * 2,
        out_specs=[pl.BlockSpec((tr, d), lambda r: (r, 0)), pl.BlockSpec((1, LANES), lambda r: (0, 0))],
        out_shape=[jax.ShapeDtypeStruct((s, d), F32), jax.ShapeDtypeStruct((1, LANES), F32)],
        compiler_params=pltpu.CompilerParams(dimension_semantics=("arbitrary",)),
        name="loss_head",
    )(y, t)


def _place():
    x, y, c = lax.axis_index("x"), lax.axis_index("y"), lax.axis_index("c")
    chips = [(1 - x, y), (x, 1 - y), (1 - x, 1 - y)]
    return x, y, c, chips


def _all_gather_chips(flat):
    _, rows, _ = flat.shape

    def body(x_ref, o_ref, send_sems, recv_sems):
        x, y, c, chips = _place()
        me = 2 * x + y
        sib = (x, y, 1 - c)

        def remote(k, src, dst, to):
            return pltpu.make_async_remote_copy(src_ref=src, dst_ref=dst, send_sem=send_sems.at[k],
                                                recv_sem=recv_sems.at[k], device_id=to, device_id_type=MESH)

        first = [remote(k, x_ref.at[c], o_ref.at[me, c], (px, py, c)) for k, (px, py) in enumerate(chips)]
        for cp in first:
            cp.start()
        passed = []
        for k, (px, py) in enumerate(chips):
            blk = o_ref.at[2 * px + py, c]
            remote(k, x_ref.at[c], blk, (px, py, c)).wait_recv()
            cp = remote(3 + k, blk, blk, sib)
            cp.start()
            passed.append(cp)
        for k, (px, py) in enumerate(chips):
            blk = o_ref.at[2 * px + py, 1 - c]
            remote(3 + k, blk, blk, sib).wait_recv()
        for cp in first + passed:
            cp.wait_send()

    return pl.pallas_call(
        body,
        in_specs=[ANY],
        out_specs=ANY,
        out_shape=jax.ShapeDtypeStruct((N_CHIPS, 2, rows, LANES), flat.dtype),
        scratch_shapes=[pltpu.SemaphoreType.DMA((6,)), pltpu.SemaphoreType.DMA((6,))],
        name=f"all_gather_chips_{jnp.dtype(flat.dtype).name}",
    )(flat)


def _rs_sibling_exchange(g):
    _, nb, rows, _ = g.shape

    def body(g_ref, a_ref, send_sem, recv_sem):
        x, y, c, _ = _place()
        cp = pltpu.make_async_remote_copy(src_ref=g_ref.at[1 - c], dst_ref=a_ref, send_sem=send_sem,
                                          recv_sem=recv_sem, device_id=(x, y, 1 - c), device_id_type=MESH)
        cp.start()
        cp.wait()

    return pl.pallas_call(
        body,
        in_specs=[ANY],
        out_specs=ANY,
        out_shape=jax.ShapeDtypeStruct((nb, rows, LANES), F32),
        scratch_shapes=[pltpu.SemaphoreType.DMA, pltpu.SemaphoreType.DMA],
        name="rs_sibling_exchange",
    )(g)


def _rs_chip_exchange(p):
    nb, rows, _ = p.shape

    def body(p_ref, b_ref, send_sems, recv_sems, local_sem):
        x, y, c, chips = _place()
        me = 2 * x + y
        mine = pltpu.make_async_copy(p_ref.at[me], b_ref.at[me], local_sem)
        mine.start()
        copies = [pltpu.make_async_remote_copy(src_ref=p_ref.at[2 * px + py], dst_ref=b_ref.at[me],
                                               send_sem=send_sems.at[k], recv_sem=recv_sems.at[k],
                                               device_id=(px, py, c), device_id_type=MESH)
                  for k, (px, py) in enumerate(chips)]
        for cp in copies:
            cp.start()
        for cp in copies:
            cp.wait()
        mine.wait()

    return pl.pallas_call(
        body,
        in_specs=[ANY],
        out_specs=ANY,
        out_shape=jax.ShapeDtypeStruct((nb, rows, LANES), F32),
        scratch_shapes=[pltpu.SemaphoreType.DMA((3,)), pltpu.SemaphoreType.DMA((3,)), pltpu.SemaphoreType.DMA],
        name="rs_chip_exchange",
    )(p)


def _rs_sibling_gather(r):
    _, rows, _ = r.shape

    def body(r_ref, o_ref, send_sem, recv_sem):
        x, y, c, _ = _place()
        cp = pltpu.make_async_remote_copy(src_ref=o_ref.at[c], dst_ref=o_ref.at[c], send_sem=send_sem,
                                          recv_sem=recv_sem, device_id=(x, y, 1 - c), device_id_type=MESH)
        cp.start()
        cp.wait()

    return pl.pallas_call(
        body,
        in_specs=[ANY],
        out_specs=ANY,
        out_shape=jax.ShapeDtypeStruct((2, rows, LANES), F32),
        input_output_aliases={0: 0},
        scratch_shapes=[pltpu.SemaphoreType.DMA, pltpu.SemaphoreType.DMA],
        name="rs_sibling_gather",
    )(r)


def _reduce_scatter(g, core):
    a = _rs_sibling_exchange(g)
    p = _add_own_half(g, a, core)
    b = _rs_chip_exchange(p)
    return _rs_sibling_gather(_sum_chips(b, core))


def _all_reduce_small(v):
    rows, _ = v.shape
    n_dev = 8

    def body(v_ref, o_ref, gath, send_sems, recv_sems):
        x, y, c, _ = _place()
        me = 4 * x + 2 * y + c
        gath[me] = v_ref[...]
        copies = []
        for mask in range(1, n_dev):
            px = 1 - x if mask & 4 else x
            py = 1 - y if mask & 2 else y
            pc = 1 - c if mask & 1 else c
            copies.append(pltpu.make_async_remote_copy(
                src_ref=v_ref, dst_ref=gath.at[me], send_sem=send_sems.at[mask - 1],
                recv_sem=recv_sems.at[mask - 1], device_id=(px, py, pc), device_id_type=MESH))
        for cp in copies:
            cp.start()
        for cp in copies:
            cp.wait()
        acc = gath[0]
        for k in range(1, n_dev):
            acc = acc + gath[k]
        o_ref[...] = acc

    return pl.pallas_call(
        body,
        in_specs=[VMEM],
        out_specs=VMEM,
        out_shape=jax.ShapeDtypeStruct((rows, LANES), F32),
        scratch_shapes=[pltpu.VMEM((n_dev, rows, LANES), F32), pltpu.SemaphoreType.DMA((n_dev - 1,)),
                        pltpu.SemaphoreType.DMA((n_dev - 1,))],
        name="all_reduce_small",
    )(v)


def _flat_len(shapes, unit_rows=FLAT_TILE_ROWS):
    n = sum(int(np.prod(s)) for s in shapes)
    unit = 2 * unit_rows * LANES
    return -(-n // unit) * unit


def _pack(arrays, total):
    flat = jnp.concatenate([a.reshape(-1) for a in arrays])
    return jnp.pad(flat, (0, total - flat.shape[0]))


def _unpack(flat, shapes):
    out, off = [], 0
    for s in shapes:
        n = int(np.prod(s))
        out.append(flat[off:off + n].reshape(s))
        off += n
    return out


def _permute_w_in(w):
    pad = jnp.zeros(w.shape[:-1] + (N_IN_PAD - N_IN,), w.dtype)
    return jnp.concatenate([w[..., 0:1536], w[..., 1540:3076], w[..., 3084:3596], w[..., 3596:5132],
                            w[..., 5132:8204], w[..., 1536:1540], w[..., 3076:3080], w[..., 3080:3084], pad],
                           axis=-1)


def _unpermute_w_in(w):
    return jnp.concatenate([w[..., 0:1536], w[..., 8192:8196], w[..., 1536:3072], w[..., 8196:8200],
                            w[..., 8200:8204], w[..., 3072:3584], w[..., 3584:5120], w[..., 5120:8192]], axis=-1)


P_FQ, P_FK, P_FV, P_GQ, P_GK, P_GV, P_GZ, P_SQ, P_SK, P_SV, P_GATES, P_FF, P_GB, P_GA = (
    0, 512, 1024, 1536, 2048, 2560, 3072, 3584, 4096, 4608, 5120, 8192, 8196, 8200)


def rmsnorm(x, g):
    xf = x.astype(F32)
    y = xf * lax.rsqrt(jnp.mean(xf * xf, axis=-1, keepdims=True) + EPS)
    return (y * g.astype(F32)).astype(x.dtype)


def l2norm(x):
    xf = x.astype(F32)
    return xf * lax.rsqrt(jnp.sum(xf * xf, axis=-1, keepdims=True) + EPS)


def heads(x, n):
    return x.reshape(x.shape[:-1] + (n, -1))


def gated_delta_rule(q, k, v, g, beta):
    B, T, H, dk = q.shape
    dv = v.shape[-1]
    N = T // CHUNK

    def chunks(a):
        a = a.astype(F32).reshape((B, N, CHUNK, H) + a.shape[3:])
        return jnp.moveaxis(a, (1, 3), (0, 2))

    qc = chunks(q) * dk ** -0.5
    kc = chunks(k)
    vc = chunks(v)
    bc = chunks(beta)
    gc = jnp.cumsum(chunks(g), axis=-1)
    idx = jnp.arange(CHUNK)
    causal = idx[:, None] >= idx[None, :]
    strict = idx[:, None] > idx[None, :]
    decay = jnp.exp(jnp.where(causal, gc[..., :, None] - gc[..., None, :], -jnp.inf))
    kk = jnp.einsum("nbhcd,nbhed->nbhce", kc, kc)
    a_mat = jnp.where(strict, bc[..., :, None] * kk * decay, 0.0) + jnp.eye(CHUNK, dtype=F32)
    rhs = jnp.concatenate([vc * bc[..., None], kc * (bc * jnp.exp(gc))[..., None]], axis=-1)
    sol = lax.linalg.triangular_solve(a_mat, rhs, left_side=True, lower=True)
    u, w = sol[..., :dv], sol[..., dv:]
    attn = jnp.where(causal, jnp.einsum("nbhcd,nbhed->nbhce", qc, kc) * decay, 0.0)
    g_last = gc[..., -1]
    k_tail = kc * jnp.exp(g_last[..., None] - gc)[..., None]

    egl = jnp.broadcast_to(jnp.exp(g_last)[:, 0, :, None, None], (N, H, 1, dv))
    o = gdn_scan((qc * jnp.exp(gc)[..., None])[:, 0], u[:, 0], w[:, 0], attn[:, 0], k_tail[:, 0], egl)
    return o.transpose(0, 2, 1, 3).reshape(B, T, H, dv)


def _layer(x, mem, p):
    S, D = x.shape
    h = rmsnorm(x, p['norm_mix'])
    proj = matmul(h, p['w_in'])[None]

    def seg(off, width):
        return proj[..., off:off + width]

    fq = rmsnorm(heads(seg(P_FQ, 512), HEADS), p['fox_qnorm']).reshape(S, 512)
    fk = rmsnorm(heads(seg(P_FK, 512), HEADS), p['fox_knorm']).reshape(S, 512)
    logf = jax.nn.log_sigmoid((seg(P_FF, HEADS) + p['fox_fbias']).astype(F32))
    ya = fox_core(fq, fk, seg(P_FV, 512)[0], jnp.cumsum(logf[0], axis=0))

    qkv = conv_silu(seg(P_GQ, 1536)[0], p['gdn_conv'])[None]
    cq, ck, cv = jnp.split(qkv, [512, 1024], axis=-1)
    beta = jax.nn.sigmoid(seg(P_GB, HEADS).astype(F32))
    g_log = -jnp.exp(p['gdn_a_log'].astype(F32)) * jax.nn.softplus((seg(P_GA, HEADS) + p['gdn_dt_bias']).astype(F32))
    o = gated_delta_rule(l2norm(heads(cq, HEADS)), l2norm(heads(ck, HEADS)), heads(cv, HEADS), g_log, beta)
    yb = rmsnorm(o, p['gdn_onorm']) * jax.nn.silu(heads(seg(P_GZ, 512), HEADS).astype(F32))
    yb = yb.astype(x.dtype).reshape(1, S, 512)

    yc = sb_core(seg(P_SQ, 512)[0], seg(P_SK, 512)[0], seg(P_SV, 512)[0])

    g = jax.nn.sigmoid((seg(P_GATES, N_BRANCH * D) + p['gate_bias']).astype(F32)).astype(x.dtype)
    g = g.reshape(S, N_BRANCH, D)
    mixed = (g[:, 0, :] * matmul(ya, p['w_oa']) + g[:, 1, :] * matmul(yb[0], p['w_ob'])
             + g[:, 2, :] * matmul(yc, p['w_oc']))
    x = x + matmul(mixed, p['w_out'])

    hq = rmsnorm(x, p['norm_xq'])
    hm = rmsnorm(mem, p['norm_mem'])
    q = rmsnorm(heads(matmul(hq, p['w_mq']), HEADS), p['mq_norm'])
    kv = matmul(hm, p['w_mkv'])
    k, v = jnp.split(kv, 2, axis=-1)
    k = rmsnorm(heads(k, HEADS), p['mk_norm'])
    v = heads(v, HEADS)
    s = jnp.einsum("qhd,khd->hqk", q, k).astype(F32) * MEM_DIM ** -0.5
    pr = jax.nn.softmax(s, axis=-1).astype(v.dtype)
    om = jnp.einsum("hqk,khd->qhd", pr, v).reshape(S, 512)
    x = x + matmul(om, p['w_mo'])

    hf = rmsnorm(x, p['norm_ffn'])
    act = conv_gate(matmul(hf, p['w_up']), p['ffn_conv'], p['ffn_conv_b'])
    return x + matmul(act, p['w_down'])


def kernel(x, mem, norm_mix, w_in, fox_fbias, fox_qnorm, fox_knorm, gdn_conv, gdn_a_log, gdn_dt_bias, gdn_onorm, gate_bias, w_oa, w_ob, w_oc, w_out, norm_xq, norm_mem, w_mq, w_mkv, mq_norm, mk_norm, w_mo, norm_ffn, w_up, ffn_conv, ffn_conv_b, w_down, loss_target, m_norm_mix, m_w_in, m_fox_fbias, m_fox_qnorm, m_fox_knorm, m_gdn_conv, m_gdn_a_log, m_gdn_dt_bias, m_gdn_onorm, m_gate_bias, m_w_oa, m_w_ob, m_w_oc, m_w_out, m_norm_xq, m_norm_mem, m_w_mq, m_w_mkv, m_mq_norm, m_mk_norm, m_w_mo, m_norm_ffn, m_w_up, m_ffn_conv, m_ffn_conv_b, m_w_down, v_norm_mix, v_w_in, v_fox_fbias, v_fox_qnorm, v_fox_knorm, v_gdn_conv, v_gdn_a_log, v_gdn_dt_bias, v_gdn_onorm, v_gate_bias, v_w_oa, v_w_ob, v_w_oc, v_w_out, v_norm_xq, v_norm_mem, v_w_mq, v_w_mkv, v_mq_norm, v_mk_norm, v_w_mo, v_norm_ffn, v_w_up, v_ffn_conv, v_ffn_conv_b, v_w_down):
    args = (x, mem, norm_mix, w_in, fox_fbias, fox_qnorm, fox_knorm, gdn_conv, gdn_a_log, gdn_dt_bias, gdn_onorm, gate_bias, w_oa, w_ob, w_oc, w_out, norm_xq, norm_mem, w_mq, w_mkv, mq_norm, mk_norm, w_mo, norm_ffn, w_up, ffn_conv, ffn_conv_b, w_down)
    moments_m = (m_norm_mix, m_w_in, m_fox_fbias, m_fox_qnorm, m_fox_knorm, m_gdn_conv, m_gdn_a_log, m_gdn_dt_bias, m_gdn_onorm, m_gate_bias, m_w_oa, m_w_ob, m_w_oc, m_w_out, m_norm_xq, m_norm_mem, m_w_mq, m_w_mkv, m_mq_norm, m_mk_norm, m_w_mo, m_norm_ffn, m_w_up, m_ffn_conv, m_ffn_conv_b, m_w_down)
    moments_v = (v_norm_mix, v_w_in, v_fox_fbias, v_fox_qnorm, v_fox_knorm, v_gdn_conv, v_gdn_a_log, v_gdn_dt_bias, v_gdn_onorm, v_gate_bias, v_w_oa, v_w_ob, v_w_oc, v_w_out, v_norm_xq, v_norm_mem, v_w_mq, v_w_mkv, v_mq_norm, v_mk_norm, v_w_mo, v_norm_ffn, v_w_up, v_ffn_conv, v_ffn_conv_b, v_w_down)
    w = dict(zip(IN_NAMES, args))
    m = dict(zip(WEIGHTS, moments_m))
    v = dict(zip(WEIGHTS, moments_v))
    xs, mems, tgt = x[0], mem[0], loss_target[0]
    core = lax.axis_index("c").astype(jnp.int32).reshape(1)

    big = list(SHARDED)
    shard_shapes = [w[n].shape for n in big]
    total = _flat_len(shard_shapes)
    half_rows = total // (2 * LANES)
    small_shapes = [w[n].shape for n in SMALL]
    n_small = sum(int(np.prod(s)) for s in small_shapes) + 1
    small_total = -(-n_small // (8 * LANES)) * (8 * LANES)

    w_flat = _pack([w[n] for n in big], total)
    my_chip = 2 * lax.axis_index("x") + lax.axis_index("y")

    def gather(names, dtype, unit_rows):
        shapes = [w[n].shape for n in names]
        tot = _flat_len(shapes, unit_rows)
        flat = _pack([w[n].astype(dtype) for n in names], tot)
        got = _all_gather_chips(flat.reshape(2, tot // (2 * LANES), LANES)).reshape(N_CHIPS, tot)
        out = {}
        for n, blocks in zip(names, zip(*[_unpack(got[j], shapes) for j in range(N_CHIPS)])):
            axis = SHARDED[n]
            full = jnp.concatenate(blocks, axis=axis).astype(F32)
            out[n] = lax.dynamic_update_slice_in_dim(full, w[n], my_chip * w[n].shape[axis], axis)
        return out

    conv_names = ['gdn_conv', 'ffn_conv']
    params = {**gather([n for n in big if n not in conv_names], BF16, FLAT_TILE_ROWS), **gather(conv_names, F32, 8)}
    params['w_in'] = _permute_w_in(params['w_in'])
    for n in SMALL:
        params[n] = w[n]

    def model(x0, pp):
        for layer in range(DEPTH):
            x0 = _layer(x0, mems, {n: a[layer] for n, a in pp.items()})
        return x0

    y, model_vjp = jax.vjp(model, xs, params)
    dy, loss_part = _loss_head(y, tgt)
    dx0, grads = model_vjp(dy)
    grads['w_in'] = _unpermute_w_in(grads['w_in'])

    def chip_blocks(g, axis):
        return jnp.stack(jnp.split(g, N_CHIPS, axis=axis)).reshape(N_CHIPS, -1)

    g_blocks = jnp.concatenate([chip_blocks(grads[n], SHARDED[n]) for n in big], axis=1)
    g_blocks = jnp.pad(g_blocks, ((0, 0), (0, total - g_blocks.shape[1])))
    g_halves = g_blocks.reshape(N_CHIPS, 2, half_rows, LANES).transpose(1, 0, 2, 3)
    g_flat = _reduce_scatter(g_halves, core).reshape(total // LANES, LANES)

    s_part = _pack([grads[n] for n in SMALL] + [loss_part[0, :1]], small_total)
    s_sum = _all_reduce_small(s_part.reshape(small_total // LANES, LANES))
    small_grads = _unpack(s_sum.reshape(-1), small_shapes + [(1,)])
    loss = small_grads.pop()[0]

    rows = total // LANES
    d_flat, nm_flat, nv_flat = _adamw(w_flat.reshape(rows, LANES), g_flat,
                                      _pack([m[n] for n in big], total).reshape(rows, LANES),
                                      _pack([v[n] for n in big], total).reshape(rows, LANES), "sharded")
    srows = small_total // LANES
    sd, snm, snv = _adamw(_pack([w[n] for n in SMALL], small_total).reshape(srows, LANES), s_sum,
                          _pack([m[n] for n in SMALL], small_total).reshape(srows, LANES),
                          _pack([v[n] for n in SMALL], small_total).reshape(srows, LANES), "replicated")

    out = {}
    for kind, flat_big, flat_small in (('grad', g_flat, s_sum), ('delta', d_flat, sd), ('new_m', nm_flat, snm),
                                       ('new_v', nv_flat, snv)):
        for n, a in zip(big, _unpack(flat_big.reshape(-1), shard_shapes)):
            out[kind, n] = a
        for n, a in zip(SMALL, _unpack(flat_small.reshape(-1), small_shapes)):
            out[kind, n] = a
    return (loss, dx0[None], *[out[kind, n] for kind in ('grad', 'delta', 'new_m', 'new_v') for n in WEIGHTS])
```

```python
import functools

import jax
import jax.numpy as jnp
import numpy as np
from jax import lax
from jax.experimental import pallas as pl
from jax.experimental.pallas import tpu as pltpu

F32 = jnp.float32
BF16 = jnp.bfloat16
MESH = pl.DeviceIdType.MESH
ANY = pl.BlockSpec(memory_space=pl.ANY)
VMEM = pl.BlockSpec(memory_space=pltpu.VMEM)

D_MODEL = 1024
DEPTH = 4
CHUNK = 64
Q_BLOCK = 128
EPS = 1e-6
HEADS = 4
HEAD_DIM = 128
GDN_CONV = 4
MEM_DIM = 128
D_FF = 2816
N_BRANCH = 3
N_IN = 8204
N_IN_PAD = 8320

ADAM_LR = 0.001
ADAM_B1 = 0.9
ADAM_B2 = 0.999
ADAM_EPS = 1e-08
ADAM_WD = 0.01
ADAM_STEP = 10

N_CHIPS = 4
LANES = 128
FLAT_TILE_ROWS = 2048
VMEM_LIMIT = 48 * 1024 * 1024

IN_NAMES = ['x', 'mem', 'norm_mix', 'w_in', 'fox_fbias', 'fox_qnorm', 'fox_knorm', 'gdn_conv', 'gdn_a_log',
            'gdn_dt_bias', 'gdn_onorm', 'gate_bias', 'w_oa', 'w_ob', 'w_oc', 'w_out', 'norm_xq', 'norm_mem',
            'w_mq', 'w_mkv', 'mq_norm', 'mk_norm', 'w_mo', 'norm_ffn', 'w_up', 'ffn_conv', 'ffn_conv_b', 'w_down']
WEIGHTS = IN_NAMES[2:]
SHARDED = {'w_in': 2, 'gdn_conv': 2, 'w_oa': 2, 'w_ob': 2, 'w_oc': 2, 'w_out': 1, 'w_mq': 1, 'w_mkv': 1,
           'w_mo': 2, 'w_up': 2, 'ffn_conv': 2, 'w_down': 1}
SMALL = [n for n in WEIGHTS if n not in SHARDED]


def _pick(n, cands):
    for c in cands:
        if n % c == 0:
            return c
    return n


_DOT_DIMS = {
    'nn': (((1,), (0,)), ((), ())),
    'nt': (((1,), (1,)), ((), ())),
    'tn': (((0,), (0,)), ((), ())),
}


def _mm(a, b, mode):
    if mode == 'nn':
        (m, c), (_, n) = a.shape, b.shape
    elif mode == 'nt':
        (m, c), (n, _) = a.shape, b.shape
    else:
        (c, m), (_, n) = a.shape, b.shape
    tm = _pick(m, (512, 256, 128)) if mode == 'tn' else _pick(m, (1024, 512, 256, 128))
    tn = _pick(n, (1664, 1408, 1024, 512, 256, 128))
    tc = _pick(c, (1024, 1408, 640, 512, 256, 128))
    if mode == 'tn':
        a_spec = pl.BlockSpec((tc, tm), lambda i, j, k: (k, i))
    else:
        a_spec = pl.BlockSpec((tm, tc), lambda i, j, k: (i, k))
    if mode == 'nt':
        b_spec = pl.BlockSpec((tn, tc), lambda i, j, k: (j, k))
    else:
        b_spec = pl.BlockSpec((tc, tn), lambda i, j, k: (k, j))
    dims = _DOT_DIMS[mode]

    def body(a_ref, b_ref, o_ref):
        @pl.when(pl.program_id(2) == 0)
        def _():
            o_ref[...] = jnp.zeros_like(o_ref)

        o_ref[...] += lax.dot_general(a_ref[...].astype(BF16), b_ref[...].astype(BF16), dims,
                                      preferred_element_type=F32)

    return pl.pallas_call(
        body,
        grid=(m // tm, n // tn, c // tc),
        in_specs=[a_spec, b_spec],
        out_specs=pl.BlockSpec((tm, tn), lambda i, j, k: (i, j)),
        out_shape=jax.ShapeDtypeStruct((m, n), F32),
        compiler_params=pltpu.CompilerParams(
            dimension_semantics=("parallel", "parallel", "arbitrary"), vmem_limit_bytes=VMEM_LIMIT),
        name=f"mm_{mode}_{m}x{c}x{n}",
    )(a, b)


@jax.custom_vjp
def matmul(a, w, w16):
    return _mm(a, w16, 'nn')


def _matmul_fwd(a, w, w16):
    return _mm(a, w16, 'nn'), (a, w16)


def _matmul_bwd(res, dy):
    a, w16 = res
    return _mm(dy, w16, 'nt'), _mm(a, dy, 'tn'), jnp.zeros_like(w16)


matmul.defvjp(_matmul_fwd, _matmul_bwd)


Q_TILE = 512
K_BLOCK = 1024
SUB = 256
_NT = (((1,), (1,)), ((), ()))
_TN = (((0,), (0,)), ((), ()))
_NN = (((1,), (0,)), ((), ()))


def _dot(a, b, dims):
    return lax.dot_general(a, b, dims, preferred_element_type=F32)


def _att_tiles(s):
    tq = min(Q_TILE, s)
    tk = min(K_BLOCK, s)
    assert s % tk == 0 and tk % tq == 0 and tk % min(SUB, tk) == 0
    return tq, tk


def _att_specs(s, t):
    tile = pl.BlockSpec((t, HEAD_DIM), lambda h, i: (i, h))
    whole = pl.BlockSpec((s, HEAD_DIM), lambda h, i: (0, h))
    col = pl.BlockSpec((None, t, 1), lambda h, i: (h, i, 0))
    row = pl.BlockSpec((None, 1, s), lambda h, i: (h, 0, 0))
    return tile, whole, col, row


def _key_block(jb, tk):
    return pl.ds(pl.multiple_of(jb * tk, tk), tk)


def _causal(i, jb, tq, tk, strict):
    r = i * tq + lax.broadcasted_iota(jnp.int32, (tq, tk), 0)
    c = jb * tk + lax.broadcasted_iota(jnp.int32, (tq, tk), 1)
    return c < r if strict else c <= r


def _fox_fwd_call(q, k, v, c_col, c_row):
    s, w = q.shape
    tq, tk = _att_tiles(s)
    scale = HEAD_DIM ** -0.5

    def body(q_ref, k_ref, v_ref, cc_ref, cr_ref, o_ref, lse_ref):
        i = pl.program_id(1)
        n_full = lax.div(i * tq, tk)
        qb = q_ref[...]
        cq = cc_ref[...]

        def block(jb, carry, diag):
            m, l, acc = carry
            sl = _key_block(jb, tk)
            sc = _dot(qb, k_ref[sl, :], _NT) * scale + (cq - cr_ref[:, sl])
            if diag:
                sc = jnp.where(_causal(i, jb, tq, tk, False), sc, -jnp.inf)
            m_new = jnp.maximum(m, jnp.max(sc, axis=-1, keepdims=True))
            p = jnp.exp(sc - m_new)
            alpha = jnp.exp(m - m_new)
            l = alpha * l + jnp.sum(p, axis=-1, keepdims=True)
            acc = alpha * acc + _dot(p.astype(BF16), v_ref[sl, :], _NN)
            return m_new, l, acc

        init = (jnp.full((tq, 1), -jnp.inf, F32), jnp.zeros((tq, 1), F32), jnp.zeros((tq, HEAD_DIM), F32))
        carry = lax.fori_loop(0, n_full, lambda jb, cr: block(jb, cr, False), init)
        m, l, acc = block(n_full, carry, True)
        o_ref[...] = acc / l
        lse_ref[...] = m + jnp.log(l)

    tile, whole, col, row = _att_specs(s, tq)
    return pl.pallas_call(
        body,
        grid=(w // HEAD_DIM, s // tq),
        in_specs=[tile, whole, whole, col, row],
        out_specs=[tile, col],
        out_shape=[jax.ShapeDtypeStruct((s, w), F32), jax.ShapeDtypeStruct((w // HEAD_DIM, s, 1), F32)],
        compiler_params=pltpu.CompilerParams(dimension_semantics=("parallel", "parallel"),
                                             vmem_limit_bytes=VMEM_LIMIT),
        name="fox_fwd",
    )(q, k, v, c_col, c_row)


def _fox_bwd_call(q, k, v, c_col, c_row, lse, do):
    s, w = q.shape
    tq, tk = _att_tiles(s)
    scale = HEAD_DIM ** -0.5

    def body(q_ref, k_ref, v_ref, cc_ref, cr_ref, lse_ref, do_ref, dq_ref, dk_ref, dv_ref, dcr_ref):
        i = pl.program_id(1)
        n_full = lax.div(i * tq, tk)

        @pl.when(i == 0)
        def _():
            dk_ref[...] = jnp.zeros_like(dk_ref)
            dv_ref[...] = jnp.zeros_like(dv_ref)
            dcr_ref[...] = jnp.zeros_like(dcr_ref)

        qb = q_ref[...]
        do16 = do_ref[...].astype(BF16)
        lse_q = lse_ref[...]
        cq = cc_ref[...]

        def probs(jb, diag):
            sl = _key_block(jb, tk)
            ks = k_ref[sl, :]
            sc = _dot(qb, ks, _NT) * scale + (cq - cr_ref[:, sl])
            p = jnp.exp(sc - lse_q)
            if diag:
                p = jnp.where(_causal(i, jb, tq, tk, False), p, 0.0)
            return sl, ks, p, _dot(do16, v_ref[sl, :], _NT)

        def row_dot(jb, acc, diag):
            _, _, p, dp = probs(jb, diag)
            return acc + jnp.sum(p * dp, axis=-1, keepdims=True)

        delta = lax.fori_loop(0, n_full, lambda jb, a: row_dot(jb, a, False), jnp.zeros((tq, 1), F32))
        delta = row_dot(n_full, delta, True)

        def block(jb, dq, diag):
            sl, ks, p, dp = probs(jb, diag)
            ds = p * (dp - delta)
            ds16 = ds.astype(BF16)
            dv_ref[sl, :] += _dot(p.astype(BF16), do16, _TN)
            dk_ref[sl, :] += _dot(ds16, qb, _TN) * scale
            dcr_ref[:, sl] += -jnp.sum(ds, axis=0, keepdims=True)
            return dq + _dot(ds16, ks, _NN) * scale

        dq = lax.fori_loop(0, n_full, lambda jb, a: block(jb, a, False), jnp.zeros((tq, HEAD_DIM), F32))
        dq_ref[...] = block(n_full, dq, True)

    tile, whole, col, row = _att_specs(s, tq)
    full = jax.ShapeDtypeStruct((s, w), F32)
    return pl.pallas_call(
        body,
        grid=(w // HEAD_DIM, s // tq),
        in_specs=[tile, whole, whole, col, row, col, tile],
        out_specs=[tile, whole, whole, row],
        out_shape=[full, full, full, jax.ShapeDtypeStruct((w // HEAD_DIM, 1, s), F32)],
        compiler_params=pltpu.CompilerParams(dimension_semantics=("parallel", "arbitrary"),
                                             vmem_limit_bytes=VMEM_LIMIT),
        name="fox_bwd",
    )(q, k, v, c_col, c_row, lse, do)


@jax.custom_vjp
def fox_core(q, k, v, c):
    return _fox_fwd(q, k, v, c)[0]


def _fox_fwd(q, k, v, c):
    q16, k16, v16 = q.astype(BF16), k.astype(BF16), v.astype(BF16)
    c_col, c_row = c.T[:, :, None], c.T[:, None, :]
    o, lse = _fox_fwd_call(q16, k16, v16, c_col, c_row)
    return o, (q16, k16, v16, c_col, c_row, lse)


def _fox_bwd(res, do):
    dq, dk, dv, dcr = _fox_bwd_call(*res, do)
    return dq, dk, dv, dcr[:, 0, :].T


fox_core.defvjp(_fox_fwd, _fox_bwd)


def _neg_softplus(z):
    e = jnp.exp(-jnp.abs(z))
    return -(jnp.maximum(z, 0.0) + jnp.log(1.0 + e)), e


def _split_dot(x, tri):
    hi = x.astype(BF16)
    lo = (x - hi.astype(F32)).astype(BF16)
    return _dot(hi, tri, _NN) + _dot(lo, tri, _NN)


def _tri(n, fn):
    r = lax.broadcasted_iota(jnp.int32, (n, n), 0)
    c = lax.broadcasted_iota(jnp.int32, (n, n), 1)
    return fn(r, c).astype(BF16)


def _sb_fwd_call(q, k, v):
    s, w = q.shape
    tq, tk = _att_tiles(s)
    sub = min(SUB, tk)
    n_sub = tk // sub
    scale = HEAD_DIM ** -0.5

    def body(q_ref, k_ref, v_ref, o_ref, tot_ref):
        i = pl.program_id(1)
        n_full = lax.div(i * tq, tk)
        qb = q_ref[...]
        tri = _tri(sub, lambda r, c: r >= c)

        def block(jb, carry, diag):
            later, acc = carry
            sl = _key_block(jb, tk)
            z = _dot(qb, k_ref[sl, :], _NT) * scale
            lk, _ = _neg_softplus(z)
            if diag:
                mask = _causal(i, jb, tq, tk, True)
                lk = jnp.where(mask, lk, 0.0)
            pieces = [None] * n_sub
            for u in reversed(range(n_sub)):
                part = lk[:, u * sub:(u + 1) * sub]
                pieces[u] = _split_dot(part, tri) + later
                later = later + jnp.sum(part, axis=-1, keepdims=True)
            a = jnp.exp(z + jnp.concatenate(pieces, axis=1))
            if diag:
                a = jnp.where(mask, a, 0.0)
            return later, acc + _dot(a.astype(BF16), v_ref[sl, :], _NN)

        carry = block(n_full, (jnp.zeros((tq, 1), F32), jnp.zeros((tq, HEAD_DIM), F32)), True)
        later, acc = lax.fori_loop(0, n_full, lambda jj, cr: block(n_full - 1 - jj, cr, False), carry)
        o_ref[...] = acc
        tot_ref[...] = later

    tile, whole, col, _ = _att_specs(s, tq)
    return pl.pallas_call(
        body,
        grid=(w // HEAD_DIM, s // tq),
        in_specs=[tile, whole, whole],
        out_specs=[tile, col],
        out_shape=[jax.ShapeDtypeStruct((s, w), F32), jax.ShapeDtypeStruct((w // HEAD_DIM, s, 1), F32)],
        compiler_params=pltpu.CompilerParams(dimension_semantics=("parallel", "parallel"),
                                             vmem_limit_bytes=VMEM_LIMIT),
        name="sb_fwd",
    )(q, k, v)


def _sb_bwd_call(q, k, v, tot, do):
    s, w = q.shape
    tq, tk = _att_tiles(s)
    sub = min(SUB, tk)
    n_sub = tk // sub
    scale = HEAD_DIM ** -0.5

    def body(q_ref, k_ref, v_ref, tot_ref, do_ref, dq_ref, dk_ref, dv_ref):
        i = pl.program_id(1)
        n_full = lax.div(i * tq, tk)

        @pl.when(i == 0)
        def _():
            dk_ref[...] = jnp.zeros_like(dk_ref)
            dv_ref[...] = jnp.zeros_like(dv_ref)

        qb = q_ref[...]
        do16 = do_ref[...].astype(BF16)
        tot_q = tot_ref[...]
        tri_before = _tri(sub, lambda r, c: r < c)
        tri_upto = _tri(sub, lambda r, c: r <= c)

        def block(jb, carry, diag):
            before, dl_before, dq = carry
            sl = _key_block(jb, tk)
            ks = k_ref[sl, :]
            z = _dot(qb, ks, _NT) * scale
            lk, e = _neg_softplus(z)
            if diag:
                mask = _causal(i, jb, tq, tk, True)
                lk = jnp.where(mask, lk, 0.0)
            sig = jnp.where(z >= 0, 1.0, e) / (1.0 + e)
            pieces = []
            for u in range(n_sub):
                part = lk[:, u * sub:(u + 1) * sub]
                pieces.append(_split_dot(part, tri_before) + before)
                before = before + jnp.sum(part, axis=-1, keepdims=True)
            a = jnp.exp(z + (tot_q - jnp.concatenate(pieces, axis=1)))
            if diag:
                a = jnp.where(mask, a, 0.0)
            dl = a * _dot(do16, v_ref[sl, :], _NT)
            dl16 = dl.astype(BF16)
            pieces = []
            for u in range(n_sub):
                pieces.append(_dot(dl16[:, u * sub:(u + 1) * sub], tri_upto, _NN) + dl_before)
                dl_before = dl_before + jnp.sum(dl[:, u * sub:(u + 1) * sub], axis=-1, keepdims=True)
            dz = dl - sig * jnp.concatenate(pieces, axis=1)
            if diag:
                dz = jnp.where(mask, dz, 0.0)
            dz16 = dz.astype(BF16)
            dv_ref[sl, :] += _dot(a.astype(BF16), do16, _TN)
            dk_ref[sl, :] += _dot(dz16, qb, _TN) * scale
            return before, dl_before, dq + _dot(dz16, ks, _NN) * scale

        init = (jnp.zeros((tq, 1), F32), jnp.zeros((tq, 1), F32), jnp.zeros((tq, HEAD_DIM), F32))
        carry = lax.fori_loop(0, n_full, lambda jb, cr: block(jb, cr, False), init)
        dq_ref[...] = block(n_full, carry, True)[2]

    tile, whole, col, _ = _att_specs(s, tq)
    full = jax.ShapeDtypeStruct((s, w), F32)
    return pl.pallas_call(
        body,
        grid=(w // HEAD_DIM, s // tq),
        in_specs=[tile, whole, whole, col, tile],
        out_specs=[tile, whole, whole],
        out_shape=[full, full, full],
        compiler_params=pltpu.CompilerParams(dimension_semantics=("parallel", "arbitrary"),
                                             vmem_limit_bytes=VMEM_LIMIT),
        name="sb_bwd",
    )(q, k, v, tot, do)


@jax.custom_vjp
def sb_core(q, k, v):
    return _sb_fwd(q, k, v)[0]


def _sb_fwd(q, k, v):
    q16, k16, v16 = q.astype(BF16), k.astype(BF16), v.astype(BF16)
    o, tot = _sb_fwd_call(q16, k16, v16)
    return o, (q16, k16, v16, tot)


def _sb_bwd(res, do):
    return tuple(_sb_bwd_call(*res, do))


sb_core.defvjp(_sb_fwd, _sb_bwd)


def _gdn_specs(h, c, d):
    vec = pl.BlockSpec((None, h, c, d), lambda n: (n, 0, 0, 0))
    sq = pl.BlockSpec((None, h, c, c), lambda n: (n, 0, 0, 0))
    dec = pl.BlockSpec((None, h, 1, d), lambda n: (n, 0, 0, 0))
    st = pl.BlockSpec((None, h, d, d), lambda n: (n, 0, 0, 0))
    return vec, sq, dec, st


def _b16(x):
    return x.astype(BF16)


def _gdn_scan_fwd_call(qg, u, w, attn, kt, egl):
    n, h, c, d = qg.shape

    def body(qg_ref, u_ref, w_ref, attn_ref, kt_ref, egl_ref, o_ref, st_ref, state):
        @pl.when(pl.program_id(0) == 0)
        def _():
            state[...] = jnp.zeros_like(state)

        for hh in range(h):
            s0 = state[hh]
            st_ref[hh] = s0
            s16 = _b16(s0)
            vn = u_ref[hh] - _dot(_b16(w_ref[hh]), s16, _NN)
            vn16 = _b16(vn)
            o_ref[hh] = _dot(_b16(qg_ref[hh]), s16, _NN) + _dot(_b16(attn_ref[hh]), vn16, _NN)
            state[hh] = s0 * egl_ref[hh] + _dot(_b16(kt_ref[hh]), vn16, _TN)

    vec, sq, dec, st = _gdn_specs(h, c, d)
    return pl.pallas_call(
        body,
        grid=(n,),
        in_specs=[vec, vec, vec, sq, vec, dec],
        out_specs=[vec, st],
        out_shape=[jax.ShapeDtypeStruct((n, h, c, d), F32), jax.ShapeDtypeStruct((n, h, d, d), F32)],
        scratch_shapes=[pltpu.VMEM((h, d, d), F32)],
        compiler_params=pltpu.CompilerParams(dimension_semantics=("arbitrary",)),
        name="gdn_scan_fwd",
    )(qg, u, w, attn, kt, egl)


def _gdn_scan_bwd_call(qg, u, w, attn, kt, egl, states, do):
    n, h, c, d = qg.shape

    def body(qg_ref, u_ref, w_ref, attn_ref, kt_ref, egl_ref, st_ref, do_ref,
             dqg_ref, du_ref, dw_ref, dattn_ref, dkt_ref, degl_ref, dstate):
        @pl.when(pl.program_id(0) == 0)
        def _():
            dstate[...] = jnp.zeros_like(dstate)

        for hh in range(h):
            s0 = st_ref[hh]
            s16 = _b16(s0)
            big_d = dstate[hh]
            d16 = _b16(big_d)
            w16, kt16, qg16, attn16 = _b16(w_ref[hh]), _b16(kt_ref[hh]), _b16(qg_ref[hh]), _b16(attn_ref[hh])
            do16 = _b16(do_ref[hh])
            vn16 = _b16(u_ref[hh] - _dot(w16, s16, _NN))
            dvn = _dot(attn16, do16, _TN) + _dot(kt16, d16, _NN)
            dvn16 = _b16(dvn)
            du_ref[hh] = dvn
            dattn_ref[hh] = _dot(do16, vn16, _NT)
            dqg_ref[hh] = _dot(do16, s16, _NT)
            dkt_ref[hh] = _dot(vn16, d16, _NT)
            dw_ref[hh] = -_dot(dvn16, s16, _NT)
            degl_ref[hh] = jnp.sum(big_d * s0, axis=0, keepdims=True)
            dstate[hh] = big_d * egl_ref[hh] + _dot(qg16, do16, _TN) - _dot(w16, dvn16, _TN)

    vec, sq, dec, st = _gdn_specs(h, c, d)
    rev = lambda spec: pl.BlockSpec(spec.block_shape, lambda i: (n - 1 - i, 0, 0, 0))
    vec, sq, dec, st = rev(vec), rev(sq), rev(dec), rev(st)
    vshape = jax.ShapeDtypeStruct((n, h, c, d), F32)
    return pl.pallas_call(
        body,
        grid=(n,),
        in_specs=[vec, vec, vec, sq, vec, dec, st, vec],
        out_specs=[vec, vec, vec, sq, vec, dec],
        out_shape=[vshape, vshape, vshape, jax.ShapeDtypeStruct((n, h, c, c), F32), vshape,
                   jax.ShapeDtypeStruct((n, h, 1, d), F32)],
        scratch_shapes=[pltpu.VMEM((h, d, d), F32)],
        compiler_params=pltpu.CompilerParams(dimension_semantics=("arbitrary",)),
        name="gdn_scan_bwd",
    )(qg, u, w, attn, kt, egl, states, do)


@jax.custom_vjp
def gdn_scan(qg, u, w, attn, kt, egl):
    return _gdn_scan_fwd_call(qg, u, w, attn, kt, egl)[0]


def _gdn_scan_fwd(qg, u, w, attn, kt, egl):
    o, states = _gdn_scan_fwd_call(qg, u, w, attn, kt, egl)
    return o, (qg, u, w, attn, kt, egl, states)


def _gdn_scan_bwd(res, do):
    return tuple(_gdn_scan_bwd_call(*res, do))


gdn_scan.defvjp(_gdn_scan_fwd, _gdn_scan_bwd)


CONV_ROWS = 512
HALO = 8


def _sigmoid(x):
    return 1.0 / (1.0 + jnp.exp(-x))


def _shifted(ext, k, rows):
    if k == 0:
        return ext[HALO:HALO + rows]
    return pltpu.roll(ext, k % ext.shape[0], 0)[HALO:HALO + rows]


def _conv_specs(s, ch, tr, tc, off):
    per, last = tr // HALO, s // HALO - 1
    blk = pl.BlockSpec((tr, tc), lambda j, i: (i, j + off))
    prev = pl.BlockSpec((HALO, tc), lambda j, i: (jnp.maximum(i * per - 1, 0), j + off))
    nxt = pl.BlockSpec((HALO, tc), lambda j, i: (jnp.minimum((i + 1) * per, last), j + off))
    return blk, prev, nxt


def _dwconv_fwd_call(x, w, b, gated):
    s, ch = x.shape
    taps = w.shape[0]
    out_ch = ch // 2 if gated else ch
    tr = _pick(s, (CONV_ROWS, 256, 128, 64, 32, 16, 8))
    tc = _pick(out_ch, (1408, 512, 384, 256, 128))
    n_j = out_ch // tc
    parts = (0, n_j) if gated else (0,)

    def conv(x_ref, p_ref, w_ref, first):
        xb = x_ref[...]
        ext = jnp.concatenate([jnp.where(first, 0.0, p_ref[...]), xb], axis=0)
        y = w_ref[taps - 1:taps, :] * xb
        for k in range(1, taps):
            y = y + w_ref[taps - 1 - k:taps - k, :] * _shifted(ext, k, tr)
        return y

    def body(*refs):
        first = pl.program_id(1) == 0
        if gated:
            xa, pa, wa, ba, xb_, pb, wb, bb, o_ref = refs
            a = conv(xa, pa, wa, first) + ba[...]
            g = conv(xb_, pb, wb, first) + bb[...]
            o_ref[...] = a * _sigmoid(a) * g
        else:
            xa, pa, wa, o_ref = refs
            a = conv(xa, pa, wa, first)
            o_ref[...] = a * _sigmoid(a)

    in_specs, args = [], []
    for off in parts:
        blk, prev, _ = _conv_specs(s, ch, tr, tc, off)
        in_specs += [blk, prev, pl.BlockSpec((taps, tc), lambda j, i, off=off: (0, j + off))]
        args += [x, x, w]
        if gated:
            in_specs.append(pl.BlockSpec((1, tc), lambda j, i, off=off: (0, j + off)))
            args.append(b)
    return pl.pallas_call(
        body,
        grid=(n_j, s // tr),
        in_specs=in_specs,
        out_specs=pl.BlockSpec((tr, tc), lambda j, i: (i, j)),
        out_shape=jax.ShapeDtypeStruct((s, out_ch), F32),
        compiler_params=pltpu.CompilerParams(dimension_semantics=("parallel", "parallel"),
                                             vmem_limit_bytes=VMEM_LIMIT),
        name="dwconv_gate_fwd" if gated else "dwconv_silu_fwd",
    )(*args)


def _dwconv_bwd_call(x, w, b, do, gated):
    s, ch = x.shape
    taps = w.shape[0]
    out_ch = ch // 2 if gated else ch
    tr = _pick(s, (CONV_ROWS, 256, 128, 64, 32, 16, 8))
    tc = _pick(out_ch, (1408, 512, 384, 256, 128))
    n_j, n_i = out_ch // tc, s // tr
    parts = (0, n_j) if gated else (0,)
    ext_rows = tr + 2 * HALO

    def pre_act(x_ref, p_ref, n_ref, w_ref, first, last):
        ext = jnp.concatenate([jnp.where(first, 0.0, p_ref[...]), x_ref[...], n_ref[...]], axis=0)
        y = w_ref[taps - 1:taps, :] * ext
        for k in range(1, taps):
            y = y + w_ref[taps - 1 - k:taps - k, :] * pltpu.roll(ext, k, 0)
        return ext, y

    def grads(ext, dy, w_ref, dx_ref, dw_ref):
        dx = w_ref[taps - 1:taps, :] * dy[HALO:HALO + tr]
        for k in range(1, taps):
            dx = dx + w_ref[taps - 1 - k:taps - k, :] * _shifted(dy, -k, tr)
        dx_ref[...] = dx
        dyb = dy[HALO:HALO + tr]
        for k in range(taps):
            dw_ref[taps - 1 - k:taps - k, :] += jnp.sum(dyb * _shifted(ext, k, tr), axis=0, keepdims=True)
        return dyb

    def body(*refs):
        i = pl.program_id(1)
        first, last = i == 0, i == n_i - 1
        rows = lax.broadcasted_iota(jnp.int32, (ext_rows, 1), 0)
        inside = jnp.logical_and(rows >= HALO, jnp.logical_or(rows < HALO + tr, jnp.logical_not(last)))
        if gated:
            (xa, pa, na, wa, ba, xb_, pb, nb, wb, bb, do_ref, don_ref,
             dxa_ref, dxb_ref, dwa_ref, dwb_ref, dba_ref, dbb_ref) = refs
        else:
            xa, pa, na, wa, do_ref, don_ref, dxa_ref, dwa_ref = refs

        @pl.when(first)
        def _():
            dwa_ref[...] = jnp.zeros_like(dwa_ref)
            if gated:
                dwb_ref[...] = jnp.zeros_like(dwb_ref)
                dba_ref[...] = jnp.zeros_like(dba_ref)
                dbb_ref[...] = jnp.zeros_like(dbb_ref)

        d_out = jnp.concatenate([jnp.zeros((HALO, tc), F32), do_ref[...], don_ref[...]], axis=0)
        d_out = jnp.where(inside, d_out, 0.0)
        ext_a, a = pre_act(xa, pa, na, wa, first, last)
        if gated:
            a = a + ba[...]
            ext_b, g = pre_act(xb_, pb, nb, wb, first, last)
            g = g + bb[...]
            sg = _sigmoid(a)
            silu = a * sg
            dya = jnp.where(inside, d_out * g * (sg + silu * (1.0 - sg)), 0.0)
            dyg = jnp.where(inside, d_out * silu, 0.0)
            dba_ref[...] += jnp.sum(grads(ext_a, dya, wa, dxa_ref, dwa_ref), axis=0, keepdims=True)
            dbb_ref[...] += jnp.sum(grads(ext_b, dyg, wb, dxb_ref, dwb_ref), axis=0, keepdims=True)
        else:
            sg = _sigmoid(a)
            dya = jnp.where(inside, d_out * (sg + a * sg * (1.0 - sg)), 0.0)
            grads(ext_a, dya, wa, dxa_ref, dwa_ref)

    in_specs, args = [], []
    for off in parts:
        blk, prev, nxt = _conv_specs(s, ch, tr, tc, off)
        in_specs += [blk, prev, nxt, pl.BlockSpec((taps, tc), lambda j, i, off=off: (0, j + off))]
        args += [x, x, x, w]
        if gated:
            in_specs.append(pl.BlockSpec((1, tc), lambda j, i, off=off: (0, j + off)))
            args.append(b)
    blk, _, nxt = _conv_specs(s, out_ch, tr, tc, 0)
    in_specs += [blk, nxt]
    args += [do, do]
    n_half = len(parts)
    out_specs = ([blk] * n_half + [pl.BlockSpec((taps, tc), lambda j, i: (0, j))] * n_half
                 + ([pl.BlockSpec((1, tc), lambda j, i: (0, j))] * n_half if gated else []))
    out_shape = ([jax.ShapeDtypeStruct((s, out_ch), F32)] * n_half + [jax.ShapeDtypeStruct((taps, out_ch), F32)] * n_half
                 + ([jax.ShapeDtypeStruct((1, out_ch), F32)] * n_half if gated else []))
    return pl.pallas_call(
        body,
        grid=(n_j, n_i),
        in_specs=in_specs,
        out_specs=out_specs,
        out_shape=out_shape,
        compiler_params=pltpu.CompilerParams(dimension_semantics=("parallel", "arbitrary"),
                                             vmem_limit_bytes=VMEM_LIMIT),
        name="dwconv_gate_bwd" if gated else "dwconv_silu_bwd",
    )(*args)


@jax.custom_vjp
def conv_gate(u, w, b):
    return _dwconv_fwd_call(u, w, b[None], True)


def _conv_gate_fwd(u, w, b):
    return _dwconv_fwd_call(u, w, b[None], True), (u, w, b)


def _conv_gate_bwd(res, do):
    u, w, b = res
    dxa, dxb, dwa, dwb, dba, dbb = _dwconv_bwd_call(u, w, b[None], do, True)
    return (jnp.concatenate([dxa, dxb], axis=1), jnp.concatenate([dwa, dwb], axis=1),
            jnp.concatenate([dba, dbb], axis=1)[0])


conv_gate.defvjp(_conv_gate_fwd, _conv_gate_bwd)


@jax.custom_vjp
def conv_silu(x, w):
    return _dwconv_fwd_call(x, w, None, False)


def _conv_silu_fwd(x, w):
    return _dwconv_fwd_call(x, w, None, False), (x, w)


def _conv_silu_bwd(res, do):
    x, w = res
    dx, dw = _dwconv_bwd_call(x, w, None, do, False)
    return dx, dw


conv_silu.defvjp(_conv_silu_fwd, _conv_silu_bwd)


def _flat_rows(rows):
    return _pick(rows, (FLAT_TILE_ROWS, 1024, 512, 256, 128, 64, 32, 16, 8))


def _add_own_half(g, a, core):
    _, nb, rows, _ = g.shape
    tr = _flat_rows(rows)

    def body(c_ref, g_ref, a_ref, o_ref):
        o_ref[...] = g_ref[...] + a_ref[...]

    return pl.pallas_call(
        body,
        grid_spec=pltpu.PrefetchScalarGridSpec(
            num_scalar_prefetch=1,
            grid=(nb, rows // tr),
            in_specs=[pl.BlockSpec((None, None, tr, LANES), lambda j, r, c_ref: (c_ref[0], j, r, 0)),
                      pl.BlockSpec((None, tr, LANES), lambda j, r, c_ref: (j, r, 0))],
            out_specs=pl.BlockSpec((None, tr, LANES), lambda j, r, c_ref: (j, r, 0)),
        ),
        out_shape=jax.ShapeDtypeStruct((nb, rows, LANES), F32),
        compiler_params=pltpu.CompilerParams(dimension_semantics=("parallel", "parallel")),
        name="rs_add_own_half",
    )(core, g, a)


def _sum_chips(b, core):
    nb, rows, _ = b.shape
    tr = _flat_rows(rows)

    def body(c_ref, b_ref, o_ref):
        acc = b_ref[0] + b_ref[1]
        for j in range(2, nb):
            acc = acc + b_ref[j]
        o_ref[...] = acc

    return pl.pallas_call(
        body,
        grid_spec=pltpu.PrefetchScalarGridSpec(
            num_scalar_prefetch=1,
            grid=(rows // tr,),
            in_specs=[pl.BlockSpec((nb, tr, LANES), lambda r, c_ref: (0, r, 0))],
            out_specs=pl.BlockSpec((None, tr, LANES), lambda r, c_ref: (c_ref[0], r, 0)),
        ),
        out_shape=jax.ShapeDtypeStruct((2, rows, LANES), F32),
        compiler_params=pltpu.CompilerParams(dimension_semantics=("parallel",)),
        name="rs_sum_chips",
    )(core, b)


def _adamw(w, g, m, v, tag):
    rows = w.shape[0]
    tr = _flat_rows(rows)

    def body(w_ref, g_ref, m_ref, v_ref, d_ref, nm_ref, nv_ref):
        gg = g_ref[...]
        nm = ADAM_B1 * m_ref[...] + (1.0 - ADAM_B1) * gg
        nv = ADAM_B2 * v_ref[...] + (1.0 - ADAM_B2) * jnp.square(gg)
        m_hat = nm / (1.0 - ADAM_B1 ** ADAM_STEP)
        v_hat = nv / (1.0 - ADAM_B2 ** ADAM_STEP)
        d_ref[...] = -ADAM_LR * (m_hat / (jnp.sqrt(v_hat) + ADAM_EPS) + ADAM_WD * w_ref[...])
        nm_ref[...] = nm
        nv_ref[...] = nv

    spec = pl.BlockSpec((tr, LANES), lambda r: (r, 0))
    shape = jax.ShapeDtypeStruct((rows, LANES), F32)
    return pl.pallas_call(
        body,
        grid=(rows // tr,),
        in_specs=[spec] * 4,
        out_specs=[spec] * 3,
        out_shape=[shape] * 3,
        compiler_params=pltpu.CompilerParams(dimension_semantics=("parallel",)),
        name=f"adamw_{tag}",
    )(w, g, m, v)


def _loss_head(y, t):
    s, d = y.shape
    tr = _pick(s, (512, 256, 128, 64, 32, 16, 8))

    def body(y_ref, t_ref, dy_ref, l_ref):
        @pl.when(pl.program_id(0) == 0)
        def _():
            l_ref[...] = jnp.zeros_like(l_ref)

        diff = y_ref[...] - t_ref[...]
        dy_ref[...] = diff / d
        row = jnp.mean(jnp.square(diff), axis=-1, keepdims=True)
        l_ref[...] += 0.5 * jnp.sum(row, axis=0, keepdims=True)

    return pl.pallas_call(
        body,
        grid=(s // tr,),
        in_specs=[pl.BlockSpec((tr, d), lambda r: (r, 0))] * 2,
        out_specs=[pl.BlockSpec((tr, d), lambda r: (r, 0)), pl.BlockSpec((1, LANES), lambda r: (0, 0))],
        out_shape=[jax.ShapeDtypeStruct((s, d), F32), jax.ShapeDtypeStruct((1, LANES), F32)],
        compiler_params=pltpu.CompilerParams(dimension_semantics=("arbitrary",)),
        name="loss_head",
    )(y, t)


def _place():
    x, y, c = lax.axis_index("x"), lax.axis_index("y"), lax.axis_index("c")
    chips = [(1 - x, y), (x, 1 - y), (1 - x, 1 - y)]
    return x, y, c, chips


def _all_gather_chips(flat):
    _, rows, _ = flat.shape

    def body(x_ref, o_ref, send_sems, recv_sems):
        x, y, c, chips = _place()
        me = 2 * x + y
        sib = (x, y, 1 - c)

        def remote(k, src, dst, to):
            return pltpu.make_async_remote_copy(src_ref=src, dst_ref=dst, send_sem=send_sems.at[k],
                                                recv_sem=recv_sems.at[k], device_id=to, device_id_type=MESH)

        first = [remote(k, x_ref.at[c], o_ref.at[me, c], (px, py, c)) for k, (px, py) in enumerate(chips)]
        for cp in first:
            cp.start()
        passed = []
        for k, (px, py) in enumerate(chips):
            blk = o_ref.at[2 * px + py, c]
            remote(k, x_ref.at[c], blk, (px, py, c)).wait_recv()
            cp = remote(3 + k, blk, blk, sib)
            cp.start()
            passed.append(cp)
        for k, (px, py) in enumerate(chips):
            blk = o_ref.at[2 * px + py, 1 - c]
            remote(3 + k, blk, blk, sib).wait_recv()
        for cp in first + passed:
            cp.wait_send()

    return pl.pallas_call(
        body,
        in_specs=[ANY],
        out_specs=ANY,
        out_shape=jax.ShapeDtypeStruct((N_CHIPS, 2, rows, LANES), flat.dtype),
        scratch_shapes=[pltpu.SemaphoreType.DMA((6,)), pltpu.SemaphoreType.DMA((6,))],
        name=f"all_gather_chips_{jnp.dtype(flat.dtype).name}",
    )(flat)


def _rs_sibling_exchange(g):
    _, nb, rows, _ = g.shape

    def body(g_ref, a_ref, send_sem, recv_sem):
        x, y, c, _ = _place()
        cp = pltpu.make_async_remote_copy(src_ref=g_ref.at[1 - c], dst_ref=a_ref, send_sem=send_sem,
                                          recv_sem=recv_sem, device_id=(x, y, 1 - c), device_id_type=MESH)
        cp.start()
        cp.wait()

    return pl.pallas_call(
        body,
        in_specs=[ANY],
        out_specs=ANY,
        out_shape=jax.ShapeDtypeStruct((nb, rows, LANES), F32),
        scratch_shapes=[pltpu.SemaphoreType.DMA, pltpu.SemaphoreType.DMA],
        name="rs_sibling_exchange",
    )(g)


def _rs_chip_exchange(p):
    nb, rows, _ = p.shape

    def body(p_ref, b_ref, send_sems, recv_sems, local_sem):
        x, y, c, chips = _place()
        me = 2 * x + y
        mine = pltpu.make_async_copy(p_ref.at[me], b_ref.at[me], local_sem)
        mine.start()
        copies = [pltpu.make_async_remote_copy(src_ref=p_ref.at[2 * px + py], dst_ref=b_ref.at[me],
                                               send_sem=send_sems.at[k], recv_sem=recv_sems.at[k],
                                               device_id=(px, py, c), device_id_type=MESH)
                  for k, (px, py) in enumerate(chips)]
        for cp in copies:
            cp.start()
        for cp in copies:
            cp.wait()
        mine.wait()

    return pl.pallas_call(
        body,
        in_specs=[ANY],
        out_specs=ANY,
        out_shape=jax.ShapeDtypeStruct((nb, rows, LANES), F32),
        scratch_shapes=[pltpu.SemaphoreType.DMA((3,)), pltpu.SemaphoreType.DMA((3,)), pltpu.SemaphoreType.DMA],
        name="rs_chip_exchange",
    )(p)


def _rs_sibling_gather(r):
    _, rows, _ = r.shape

    def body(r_ref, o_ref, send_sem, recv_sem):
        x, y, c, _ = _place()
        cp = pltpu.make_async_remote_copy(src_ref=o_ref.at[c], dst_ref=o_ref.at[c], send_sem=send_sem,
                                          recv_sem=recv_sem, device_id=(x, y, 1 - c), device_id_type=MESH)
        cp.start()
        cp.wait()

    return pl.pallas_call(
        body,
        in_specs=[ANY],
        out_specs=ANY,
        out_shape=jax.ShapeDtypeStruct((2, rows, LANES), F32),
        input_output_aliases={0: 0},
        scratch_shapes=[pltpu.SemaphoreType.DMA, pltpu.SemaphoreType.DMA],
        name="rs_sibling_gather",
    )(r)


def _reduce_scatter(g, core):
    a = _rs_sibling_exchange(g)
    p = _add_own_half(g, a, core)
    b = _rs_chip_exchange(p)
    return _rs_sibling_gather(_sum_chips(b, core))


def _all_reduce_small(v):
    rows, _ = v.shape
    n_dev = 8

    def body(v_ref, o_ref, gath, send_sems, recv_sems):
        x, y, c, _ = _place()
        me = 4 * x + 2 * y + c
        gath[me] = v_ref[...]
        copies = []
        for mask in range(1, n_dev):
            px = 1 - x if mask & 4 else x
            py = 1 - y if mask & 2 else y
            pc = 1 - c if mask & 1 else c
            copies.append(pltpu.make_async_remote_copy(
                src_ref=v_ref, dst_ref=gath.at[me], send_sem=send_sems.at[mask - 1],
                recv_sem=recv_sems.at[mask - 1], device_id=(px, py, pc), device_id_type=MESH))
        for cp in copies:
            cp.start()
        for cp in copies:
            cp.wait()
        acc = gath[0]
        for k in range(1, n_dev):
            acc = acc + gath[k]
        o_ref[...] = acc

    return pl.pallas_call(
        body,
        in_specs=[VMEM],
        out_specs=VMEM,
        out_shape=jax.ShapeDtypeStruct((rows, LANES), F32),
        scratch_shapes=[pltpu.VMEM((n_dev, rows, LANES), F32), pltpu.SemaphoreType.DMA((n_dev - 1,)),
                        pltpu.SemaphoreType.DMA((n_dev - 1,))],
        name="all_reduce_small",
    )(v)


def _flat_len(shapes, unit_rows=FLAT_TILE_ROWS):
    n = sum(int(np.prod(s)) for s in shapes)
    unit = 2 * unit_rows * LANES
    return -(-n // unit) * unit


def _pack(arrays, total):
    flat = jnp.concatenate([a.reshape(-1) for a in arrays])
    return jnp.pad(flat, (0, total - flat.shape[0]))


def _unpack(flat, shapes):
    out, off = [], 0
    for s in shapes:
        n = int(np.prod(s))
        out.append(flat[off:off + n].reshape(s))
        off += n
    return out


def _permute_w_in(w):
    pad = jnp.zeros(w.shape[:-1] + (N_IN_PAD - N_IN,), w.dtype)
    return jnp.concatenate([w[..., 0:1536], w[..., 1540:3076], w[..., 3084:3596], w[..., 3596:5132],
                            w[..., 5132:8204], w[..., 1536:1540], w[..., 3076:3080], w[..., 3080:3084], pad],
                           axis=-1)


def _unpermute_w_in(w):
    return jnp.concatenate([w[..., 0:1536], w[..., 8192:8196], w[..., 1536:3072], w[..., 8196:8200],
                            w[..., 8200:8204], w[..., 3072:3584], w[..., 3584:5120], w[..., 5120:8192]], axis=-1)


P_FQ, P_FK, P_FV, P_GQ, P_GK, P_GV, P_GZ, P_SQ, P_SK, P_SV, P_GATES, P_FF, P_GB, P_GA = (
    0, 512, 1024, 1536, 2048, 2560, 3072, 3584, 4096, 4608, 5120, 8192, 8196, 8200)


def rmsnorm(x, g):
    xf = x.astype(F32)
    y = xf * lax.rsqrt(jnp.mean(xf * xf, axis=-1, keepdims=True) + EPS)
    return (y * g.astype(F32)).astype(x.dtype)


def l2norm(x):
    xf = x.astype(F32)
    return xf * lax.rsqrt(jnp.sum(xf * xf, axis=-1, keepdims=True) + EPS)


def heads(x, n):
    return x.reshape(x.shape[:-1] + (n, -1))


def gated_delta_rule(q, k, v, g, beta):
    B, T, H, dk = q.shape
    dv = v.shape[-1]
    N = T // CHUNK

    def chunks(a):
        a = a.astype(F32).reshape((B, N, CHUNK, H) + a.shape[3:])
        return jnp.moveaxis(a, (1, 3), (0, 2))

    qc = chunks(q) * dk ** -0.5
    kc = chunks(k)
    vc = chunks(v)
    bc = chunks(beta)
    gc = jnp.cumsum(chunks(g), axis=-1)
    idx = jnp.arange(CHUNK)
    causal = idx[:, None] >= idx[None, :]
    strict = idx[:, None] > idx[None, :]
    decay = jnp.exp(jnp.where(causal, gc[..., :, None] - gc[..., None, :], -jnp.inf))
    kk = jnp.einsum("nbhcd,nbhed->nbhce", kc, kc)
    a_mat = jnp.where(strict, bc[..., :, None] * kk * decay, 0.0) + jnp.eye(CHUNK, dtype=F32)
    rhs = jnp.concatenate([vc * bc[..., None], kc * (bc * jnp.exp(gc))[..., None]], axis=-1)
    sol = lax.linalg.triangular_solve(a_mat, rhs, left_side=True, lower=True)
    u, w = sol[..., :dv], sol[..., dv:]
    attn = jnp.where(causal, jnp.einsum("nbhcd,nbhed->nbhce", qc, kc) * decay, 0.0)
    g_last = gc[..., -1]
    k_tail = kc * jnp.exp(g_last[..., None] - gc)[..., None]

    egl = jnp.broadcast_to(jnp.exp(g_last)[:, 0, :, None, None], (N, H, 1, dv))
    o = gdn_scan((qc * jnp.exp(gc)[..., None])[:, 0], u[:, 0], w[:, 0], attn[:, 0], k_tail[:, 0], egl)
    return o.transpose(0, 2, 1, 3).reshape(B, T, H, dv)


def _layer(x, mem, p, p16):
    S, D = x.shape
    h = rmsnorm(x, p['norm_mix'])
    def mm(a, name):
        return matmul(a, p[name], p16[name])

    proj = mm(h, 'w_in')[None]

    def seg(off, width):
        return proj[..., off:off + width]

    fq = rmsnorm(heads(seg(P_FQ, 512), HEADS), p['fox_qnorm']).reshape(S, 512)
    fk = rmsnorm(heads(seg(P_FK, 512), HEADS), p['fox_knorm']).reshape(S, 512)
    logf = jax.nn.log_sigmoid((seg(P_FF, HEADS) + p['fox_fbias']).astype(F32))
    ya = fox_core(fq, fk, seg(P_FV, 512)[0], jnp.cumsum(logf[0], axis=0))

    qkv = conv_silu(seg(P_GQ, 1536)[0], p['gdn_conv'])[None]
    cq, ck, cv = jnp.split(qkv, [512, 1024], axis=-1)
    beta = jax.nn.sigmoid(seg(P_GB, HEADS).astype(F32))
    g_log = -jnp.exp(p['gdn_a_log'].astype(F32)) * jax.nn.softplus((seg(P_GA, HEADS) + p['gdn_dt_bias']).astype(F32))
    o = gated_delta_rule(l2norm(heads(cq, HEADS)), l2norm(heads(ck, HEADS)), heads(cv, HEADS), g_log, beta)
    yb = rmsnorm(o, p['gdn_onorm']) * jax.nn.silu(heads(seg(P_GZ, 512), HEADS).astype(F32))
    yb = yb.astype(x.dtype).reshape(1, S, 512)

    yc = sb_core(seg(P_SQ, 512)[0], seg(P_SK, 512)[0], seg(P_SV, 512)[0])

    g = jax.nn.sigmoid((seg(P_GATES, N_BRANCH * D) + p['gate_bias']).astype(F32)).astype(x.dtype)
    g = g.reshape(S, N_BRANCH, D)
    mixed = (g[:, 0, :] * mm(ya, 'w_oa') + g[:, 1, :] * mm(yb[0], 'w_ob') + g[:, 2, :] * mm(yc, 'w_oc'))
    x = x + mm(mixed, 'w_out')

    hq = rmsnorm(x, p['norm_xq'])
    hm = rmsnorm(mem, p['norm_mem'])
    q = rmsnorm(heads(mm(hq, 'w_mq'), HEADS), p['mq_norm'])
    kv = mm(hm, 'w_mkv')
    k, v = jnp.split(kv, 2, axis=-1)
    k = rmsnorm(heads(k, HEADS), p['mk_norm'])
    v = heads(v, HEADS)
    s = jnp.einsum("qhd,khd->hqk", q, k).astype(F32) * MEM_DIM ** -0.5
    pr = jax.nn.softmax(s, axis=-1).astype(v.dtype)
    om = jnp.einsum("hqk,khd->qhd", pr, v).reshape(S, 512)
    x = x + mm(om, 'w_mo')

    hf = rmsnorm(x, p['norm_ffn'])
    act = conv_gate(mm(hf, 'w_up'), p['ffn_conv'], p['ffn_conv_b'])
    return x + mm(act, 'w_down')


def kernel(x, mem, norm_mix, w_in, fox_fbias, fox_qnorm, fox_knorm, gdn_conv, gdn_a_log, gdn_dt_bias, gdn_onorm, gate_bias, w_oa, w_ob, w_oc, w_out, norm_xq, norm_mem, w_mq, w_mkv, mq_norm, mk_norm, w_mo, norm_ffn, w_up, ffn_conv, ffn_conv_b, w_down, loss_target, m_norm_mix, m_w_in, m_fox_fbias, m_fox_qnorm, m_fox_knorm, m_gdn_conv, m_gdn_a_log, m_gdn_dt_bias, m_gdn_onorm, m_gate_bias, m_w_oa, m_w_ob, m_w_oc, m_w_out, m_norm_xq, m_norm_mem, m_w_mq, m_w_mkv, m_mq_norm, m_mk_norm, m_w_mo, m_norm_ffn, m_w_up, m_ffn_conv, m_ffn_conv_b, m_w_down, v_norm_mix, v_w_in, v_fox_fbias, v_fox_qnorm, v_fox_knorm, v_gdn_conv, v_gdn_a_log, v_gdn_dt_bias, v_gdn_onorm, v_gate_bias, v_w_oa, v_w_ob, v_w_oc, v_w_out, v_norm_xq, v_norm_mem, v_w_mq, v_w_mkv, v_mq_norm, v_mk_norm, v_w_mo, v_norm_ffn, v_w_up, v_ffn_conv, v_ffn_conv_b, v_w_down):
    args = (x, mem, norm_mix, w_in, fox_fbias, fox_qnorm, fox_knorm, gdn_conv, gdn_a_log, gdn_dt_bias, gdn_onorm, gate_bias, w_oa, w_ob, w_oc, w_out, norm_xq, norm_mem, w_mq, w_mkv, mq_norm, mk_norm, w_mo, norm_ffn, w_up, ffn_conv, ffn_conv_b, w_down)
    moments_m = (m_norm_mix, m_w_in, m_fox_fbias, m_fox_qnorm, m_fox_knorm, m_gdn_conv, m_gdn_a_log, m_gdn_dt_bias, m_gdn_onorm, m_gate_bias, m_w_oa, m_w_ob, m_w_oc, m_w_out, m_norm_xq, m_norm_mem, m_w_mq, m_w_mkv, m_mq_norm, m_mk_norm, m_w_mo, m_norm_ffn, m_w_up, m_ffn_conv, m_ffn_conv_b, m_w_down)
    moments_v = (v_norm_mix, v_w_in, v_fox_fbias, v_fox_qnorm, v_fox_knorm, v_gdn_conv, v_gdn_a_log, v_gdn_dt_bias, v_gdn_onorm, v_gate_bias, v_w_oa, v_w_ob, v_w_oc, v_w_out, v_norm_xq, v_norm_mem, v_w_mq, v_w_mkv, v_mq_norm, v_mk_norm, v_w_mo, v_norm_ffn, v_w_up, v_ffn_conv, v_ffn_conv_b, v_w_down)
    w = dict(zip(IN_NAMES, args))
    m = dict(zip(WEIGHTS, moments_m))
    v = dict(zip(WEIGHTS, moments_v))
    xs, mems, tgt = x[0], mem[0], loss_target[0]
    core = lax.axis_index("c").astype(jnp.int32).reshape(1)

    big = list(SHARDED)
    shard_shapes = [w[n].shape for n in big]
    total = _flat_len(shard_shapes)
    half_rows = total // (2 * LANES)
    small_shapes = [w[n].shape for n in SMALL]
    n_small = sum(int(np.prod(s)) for s in small_shapes) + 1
    small_total = -(-n_small // (8 * LANES)) * (8 * LANES)

    w_flat = _pack([w[n] for n in big], total)
    my_chip = 2 * lax.axis_index("x") + lax.axis_index("y")

    def gather(names, dtype, unit_rows):
        shapes = [w[n].shape for n in names]
        tot = _flat_len(shapes, unit_rows)
        flat = _pack([w[n].astype(dtype) for n in names], tot)
        got = _all_gather_chips(flat.reshape(2, tot // (2 * LANES), LANES)).reshape(N_CHIPS, tot)
        out = {}
        for n, blocks in zip(names, zip(*[_unpack(got[j], shapes) for j in range(N_CHIPS)])):
            axis = SHARDED[n]
            out[n] = lax.dynamic_update_slice_in_dim(jnp.concatenate(blocks, axis=axis), w[n].astype(dtype),
                                                     my_chip * w[n].shape[axis], axis)
        return out

    conv_names = ['gdn_conv', 'ffn_conv']
    params16 = gather([n for n in big if n not in conv_names], BF16, FLAT_TILE_ROWS)
    params16['w_in'] = _permute_w_in(params16['w_in'])
    params = {n: a.astype(F32) for n, a in params16.items()}
    params.update(gather(conv_names, F32, 8))
    for n in SMALL:
        params[n] = w[n]

    def model(x0, pp):
        for layer in range(DEPTH):
            x0 = _layer(x0, mems, {n: a[layer] for n, a in pp.items()}, {n: a[layer] for n, a in params16.items()})
        return x0

    y, model_vjp = jax.vjp(model, xs, params)
    dy, loss_part = _loss_head(y, tgt)
    dx0, grads = model_vjp(dy)
    grads['w_in'] = _unpermute_w_in(grads['w_in'])

    def chip_blocks(g, axis):
        return jnp.stack(jnp.split(g, N_CHIPS, axis=axis)).reshape(N_CHIPS, -1)

    g_blocks = jnp.concatenate([chip_blocks(grads[n], SHARDED[n]) for n in big], axis=1)
    g_blocks = jnp.pad(g_blocks, ((0, 0), (0, total - g_blocks.shape[1])))
    g_halves = g_blocks.reshape(N_CHIPS, 2, half_rows, LANES).transpose(1, 0, 2, 3)
    g_flat = _reduce_scatter(g_halves, core).reshape(total // LANES, LANES)

    s_part = _pack([grads[n] for n in SMALL] + [loss_part[0, :1]], small_total)
    s_sum = _all_reduce_small(s_part.reshape(small_total // LANES, LANES))
    small_grads = _unpack(s_sum.reshape(-1), small_shapes + [(1,)])
    loss = small_grads.pop()[0]

    rows = total // LANES
    d_flat, nm_flat, nv_flat = _adamw(w_flat.reshape(rows, LANES), g_flat,
                                      _pack([m[n] for n in big], total).reshape(rows, LANES),
                                      _pack([v[n] for n in big], total).reshape(rows, LANES), "sharded")
    srows = small_total // LANES
    sd, snm, snv = _adamw(_pack([w[n] for n in SMALL], small_total).reshape(srows, LANES), s_sum,
                          _pack([m[n] for n in SMALL], small_total).reshape(srows, LANES),
                          _pack([v[n] for n in SMALL], small_total).reshape(srows, LANES), "replicated")

    out = {}
    for kind, flat_big, flat_small in (('grad', g_flat, s_sum), ('delta', d_flat, sd), ('new_m', nm_flat, snm),
                                       ('new_v', nv_flat, snv)):
        for n, a in zip(big, _unpack(flat_big.reshape(-1), shard_shapes)):
            out[kind, n] = a
        for n, a in zip(SMALL, _unpack(flat_small.reshape(-1), small_shapes)):
            out[kind, n] = a
    return (loss, dx0[None], *[out[kind, n] for kind in ('grad', 'delta', 'new_m', 'new_v') for n in WEIGHTS])
```

```python
import functools

import jax
import jax.numpy as jnp
import numpy as np
from jax import lax
from jax.experimental import pallas as pl
from jax.experimental.pallas import tpu as pltpu

F32 = jnp.float32
BF16 = jnp.bfloat16
MESH = pl.DeviceIdType.MESH
ANY = pl.BlockSpec(memory_space=pl.ANY)
VMEM = pl.BlockSpec(memory_space=pltpu.VMEM)

D_MODEL = 1024
DEPTH = 4
CHUNK = 64
Q_BLOCK = 128
EPS = 1e-6
HEADS = 4
HEAD_DIM = 128
GDN_CONV = 4
MEM_DIM = 128
D_FF = 2816
N_BRANCH = 3
N_IN = 8204
N_IN_PAD = 8320

ADAM_LR = 0.001
ADAM_B1 = 0.9
ADAM_B2 = 0.999
ADAM_EPS = 1e-08
ADAM_WD = 0.01
ADAM_STEP = 10

N_CHIPS = 4
LANES = 128
FLAT_TILE_ROWS = 2048
VMEM_LIMIT = 48 * 1024 * 1024

IN_NAMES = ['x', 'mem', 'norm_mix', 'w_in', 'fox_fbias', 'fox_qnorm', 'fox_knorm', 'gdn_conv', 'gdn_a_log',
            'gdn_dt_bias', 'gdn_onorm', 'gate_bias', 'w_oa', 'w_ob', 'w_oc', 'w_out', 'norm_xq', 'norm_mem',
            'w_mq', 'w_mkv', 'mq_norm', 'mk_norm', 'w_mo', 'norm_ffn', 'w_up', 'ffn_conv', 'ffn_conv_b', 'w_down']
WEIGHTS = IN_NAMES[2:]
SHARDED = {'w_in': 2, 'gdn_conv': 2, 'w_oa': 2, 'w_ob': 2, 'w_oc': 2, 'w_out': 1, 'w_mq': 1, 'w_mkv': 1,
           'w_mo': 2, 'w_up': 2, 'ffn_conv': 2, 'w_down': 1}
SMALL = [n for n in WEIGHTS if n not in SHARDED]


def _pick(n, cands):
    for c in cands:
        if n % c == 0:
            return c
    return n


_DOT_DIMS = {
    'nn': (((1,), (0,)), ((), ())),
    'nt': (((1,), (1,)), ((), ())),
    'tn': (((0,), (0,)), ((), ())),
}


def _mm(a, b, mode):
    if mode == 'nn':
        (m, c), (_, n) = a.shape, b.shape
    elif mode == 'nt':
        (m, c), (n, _) = a.shape, b.shape
    else:
        (c, m), (_, n) = a.shape, b.shape
    tm = _pick(m, (512, 256, 128)) if mode == 'tn' else _pick(m, (1024, 512, 256, 128))
    tn = _pick(n, (1664, 1408, 1024, 512, 256, 128))
    tc = _pick(c, (1024, 1408, 640, 512, 256, 128))
    if mode == 'tn':
        a_spec = pl.BlockSpec((tc, tm), lambda i, j, k: (k, i))
    else:
        a_spec = pl.BlockSpec((tm, tc), lambda i, j, k: (i, k))
    if mode == 'nt':
        b_spec = pl.BlockSpec((tn, tc), lambda i, j, k: (j, k))
    else:
        b_spec = pl.BlockSpec((tc, tn), lambda i, j, k: (k, j))
    dims = _DOT_DIMS[mode]

    def body(a_ref, b_ref, o_ref):
        @pl.when(pl.program_id(2) == 0)
        def _():
            o_ref[...] = jnp.zeros_like(o_ref)

        o_ref[...] += lax.dot_general(a_ref[...].astype(BF16), b_ref[...].astype(BF16), dims,
                                      preferred_element_type=F32)

    return pl.pallas_call(
        body,
        grid=(m // tm, n // tn, c // tc),
        in_specs=[a_spec, b_spec],
        out_specs=pl.BlockSpec((tm, tn), lambda i, j, k: (i, j)),
        out_shape=jax.ShapeDtypeStruct((m, n), F32),
        compiler_params=pltpu.CompilerParams(
            dimension_semantics=("parallel", "parallel", "arbitrary"), vmem_limit_bytes=VMEM_LIMIT),
        name=f"mm_{mode}_{m}x{c}x{n}",
    )(a, b)


@jax.custom_vjp
def matmul(a, w, w16):
    return _mm(a, w16, 'nn')


def _matmul_fwd(a, w, w16):
    return _mm(a, w16, 'nn'), (a, w16)


def _matmul_bwd(res, dy):
    a, w16 = res
    return _mm(dy, w16, 'nt'), _mm(a, dy, 'tn'), jnp.zeros_like(w16)


matmul.defvjp(_matmul_fwd, _matmul_bwd)


Q_TILE = 512
K_BLOCK = 1024
SUB = 256
_NT = (((1,), (1,)), ((), ()))
_TN = (((0,), (0,)), ((), ()))
_NN = (((1,), (0,)), ((), ()))


def _dot(a, b, dims):
    return lax.dot_general(a, b, dims, preferred_element_type=F32)


def _att_tiles(s):
    tq = min(Q_TILE, s)
    tk = min(K_BLOCK, s)
    assert s % tk == 0 and tk % tq == 0 and tk % min(SUB, tk) == 0
    return tq, tk


def _att_specs(s, t):
    tile = pl.BlockSpec((t, HEAD_DIM), lambda h, i: (i, h))
    whole = pl.BlockSpec((s, HEAD_DIM), lambda h, i: (0, h))
    col = pl.BlockSpec((None, t, 1), lambda h, i: (h, i, 0))
    row = pl.BlockSpec((None, 1, s), lambda h, i: (h, 0, 0))
    return tile, whole, col, row


def _key_block(jb, tk):
    return pl.ds(pl.multiple_of(jb * tk, tk), tk)


def _causal(i, jb, tq, tk, strict):
    r = i * tq + lax.broadcasted_iota(jnp.int32, (tq, tk), 0)
    c = jb * tk + lax.broadcasted_iota(jnp.int32, (tq, tk), 1)
    return c < r if strict else c <= r


def _fox_fwd_call(q, k, v, c_col, c_row):
    s, w = q.shape
    tq, tk = _att_tiles(s)
    scale = HEAD_DIM ** -0.5

    def body(q_ref, k_ref, v_ref, cc_ref, cr_ref, o_ref, lse_ref):
        i = pl.program_id(1)
        n_full = lax.div(i * tq, tk)
        qb = q_ref[...]
        cq = cc_ref[...]

        def block(jb, carry, diag):
            m, l, acc = carry
            sl = _key_block(jb, tk)
            sc = _dot(qb, k_ref[sl, :], _NT) * scale + (cq - cr_ref[:, sl])
            if diag:
                sc = jnp.where(_causal(i, jb, tq, tk, False), sc, -jnp.inf)
            m_new = jnp.maximum(m, jnp.max(sc, axis=-1, keepdims=True))
            p = jnp.exp(sc - m_new)
            alpha = jnp.exp(m - m_new)
            l = alpha * l + jnp.sum(p, axis=-1, keepdims=True)
            acc = alpha * acc + _dot(p.astype(BF16), v_ref[sl, :], _NN)
            return m_new, l, acc

        init = (jnp.full((tq, 1), -jnp.inf, F32), jnp.zeros((tq, 1), F32), jnp.zeros((tq, HEAD_DIM), F32))
        carry = lax.fori_loop(0, n_full, lambda jb, cr: block(jb, cr, False), init)
        m, l, acc = block(n_full, carry, True)
        o_ref[...] = acc / l
        lse_ref[...] = m + jnp.log(l)

    tile, whole, col, row = _att_specs(s, tq)
    return pl.pallas_call(
        body,
        grid=(w // HEAD_DIM, s // tq),
        in_specs=[tile, whole, whole, col, row],
        out_specs=[tile, col],
        out_shape=[jax.ShapeDtypeStruct((s, w), F32), jax.ShapeDtypeStruct((w // HEAD_DIM, s, 1), F32)],
        compiler_params=pltpu.CompilerParams(dimension_semantics=("parallel", "parallel"),
                                             vmem_limit_bytes=VMEM_LIMIT),
        name="fox_fwd",
    )(q, k, v, c_col, c_row)


def _fox_bwd_call(q, k, v, c_col, c_row, lse, do):
    s, w = q.shape
    tq, tk = _att_tiles(s)
    scale = HEAD_DIM ** -0.5

    def body(q_ref, k_ref, v_ref, cc_ref, cr_ref, lse_ref, do_ref, dq_ref, dk_ref, dv_ref, dcr_ref):
        i = pl.program_id(1)
        n_full = lax.div(i * tq, tk)

        @pl.when(i == 0)
        def _():
            dk_ref[...] = jnp.zeros_like(dk_ref)
            dv_ref[...] = jnp.zeros_like(dv_ref)
            dcr_ref[...] = jnp.zeros_like(dcr_ref)

        qb = q_ref[...]
        do16 = do_ref[...].astype(BF16)
        lse_q = lse_ref[...]
        cq = cc_ref[...]

        def probs(jb, diag):
            sl = _key_block(jb, tk)
            ks = k_ref[sl, :]
            sc = _dot(qb, ks, _NT) * scale + (cq - cr_ref[:, sl])
            p = jnp.exp(sc - lse_q)
            if diag:
                p = jnp.where(_causal(i, jb, tq, tk, False), p, 0.0)
            return sl, ks, p, _dot(do16, v_ref[sl, :], _NT)

        def row_dot(jb, acc, diag):
            _, _, p, dp = probs(jb, diag)
            return acc + jnp.sum(p * dp, axis=-1, keepdims=True)

        delta = lax.fori_loop(0, n_full, lambda jb, a: row_dot(jb, a, False), jnp.zeros((tq, 1), F32))
        delta = row_dot(n_full, delta, True)

        def block(jb, dq, diag):
            sl, ks, p, dp = probs(jb, diag)
            ds = p * (dp - delta)
            ds16 = ds.astype(BF16)
            dv_ref[sl, :] += _dot(p.astype(BF16), do16, _TN)
            dk_ref[sl, :] += _dot(ds16, qb, _TN) * scale
            dcr_ref[:, sl] += -jnp.sum(ds, axis=0, keepdims=True)
            return dq + _dot(ds16, ks, _NN) * scale

        dq = lax.fori_loop(0, n_full, lambda jb, a: block(jb, a, False), jnp.zeros((tq, HEAD_DIM), F32))
        dq_ref[...] = block(n_full, dq, True)

    tile, whole, col, row = _att_specs(s, tq)
    full = jax.ShapeDtypeStruct((s, w), F32)
    return pl.pallas_call(
        body,
        grid=(w // HEAD_DIM, s // tq),
        in_specs=[tile, whole, whole, col, row, col, tile],
        out_specs=[tile, whole, whole, row],
        out_shape=[full, full, full, jax.ShapeDtypeStruct((w // HEAD_DIM, 1, s), F32)],
        compiler_params=pltpu.CompilerParams(dimension_semantics=("parallel", "arbitrary"),
                                             vmem_limit_bytes=VMEM_LIMIT),
        name="fox_bwd",
    )(q, k, v, c_col, c_row, lse, do)


@jax.custom_vjp
def fox_core(q, k, v, c):
    return _fox_fwd(q, k, v, c)[0]


def _fox_fwd(q, k, v, c):
    q16, k16, v16 = q.astype(BF16), k.astype(BF16), v.astype(BF16)
    c_col, c_row = c.T[:, :, None], c.T[:, None, :]
    o, lse = _fox_fwd_call(q16, k16, v16, c_col, c_row)
    return o, (q16, k16, v16, c_col, c_row, lse)


def _fox_bwd(res, do):
    dq, dk, dv, dcr = _fox_bwd_call(*res, do)
    return dq, dk, dv, dcr[:, 0, :].T


fox_core.defvjp(_fox_fwd, _fox_bwd)


def _neg_softplus(z):
    e = jnp.exp(-jnp.abs(z))
    return -(jnp.maximum(z, 0.0) + jnp.log(1.0 + e)), e


def _split_dot(x, tri):
    hi = x.astype(BF16)
    lo = (x - hi.astype(F32)).astype(BF16)
    return _dot(hi, tri, _NN) + _dot(lo, tri, _NN)


def _tri(n, fn):
    r = lax.broadcasted_iota(jnp.int32, (n, n), 0)
    c = lax.broadcasted_iota(jnp.int32, (n, n), 1)
    return fn(r, c).astype(BF16)


def _sb_fwd_call(q, k, v):
    s, w = q.shape
    tq, tk = _att_tiles(s)
    sub = min(SUB, tk)
    n_sub = tk // sub
    scale = HEAD_DIM ** -0.5

    def body(q_ref, k_ref, v_ref, o_ref, tot_ref):
        i = pl.program_id(1)
        n_full = lax.div(i * tq, tk)
        qb = q_ref[...]
        tri = _tri(sub, lambda r, c: r >= c)

        def block(jb, carry, diag):
            later, acc = carry
            sl = _key_block(jb, tk)
            z = _dot(qb, k_ref[sl, :], _NT) * scale
            lk, _ = _neg_softplus(z)
            if diag:
                mask = _causal(i, jb, tq, tk, True)
                lk = jnp.where(mask, lk, 0.0)
            pieces = [None] * n_sub
            for u in reversed(range(n_sub)):
                part = lk[:, u * sub:(u + 1) * sub]
                pieces[u] = _split_dot(part, tri) + later
                later = later + jnp.sum(part, axis=-1, keepdims=True)
            a = jnp.exp(z + jnp.concatenate(pieces, axis=1))
            if diag:
                a = jnp.where(mask, a, 0.0)
            return later, acc + _dot(a.astype(BF16), v_ref[sl, :], _NN)

        carry = block(n_full, (jnp.zeros((tq, 1), F32), jnp.zeros((tq, HEAD_DIM), F32)), True)
        later, acc = lax.fori_loop(0, n_full, lambda jj, cr: block(n_full - 1 - jj, cr, False), carry)
        o_ref[...] = acc
        tot_ref[...] = later

    tile, whole, col, _ = _att_specs(s, tq)
    return pl.pallas_call(
        body,
        grid=(w // HEAD_DIM, s // tq),
        in_specs=[tile, whole, whole],
        out_specs=[tile, col],
        out_shape=[jax.ShapeDtypeStruct((s, w), F32), jax.ShapeDtypeStruct((w // HEAD_DIM, s, 1), F32)],
        compiler_params=pltpu.CompilerParams(dimension_semantics=("parallel", "parallel"),
                                             vmem_limit_bytes=VMEM_LIMIT),
        name="sb_fwd",
    )(q, k, v)


def _sb_bwd_call(q, k, v, tot, do):
    s, w = q.shape
    tq, tk = _att_tiles(s)
    sub = min(SUB, tk)
    n_sub = tk // sub
    scale = HEAD_DIM ** -0.5

    def body(q_ref, k_ref, v_ref, tot_ref, do_ref, dq_ref, dk_ref, dv_ref):
        i = pl.program_id(1)
        n_full = lax.div(i * tq, tk)

        @pl.when(i == 0)
        def _():
            dk_ref[...] = jnp.zeros_like(dk_ref)
            dv_ref[...] = jnp.zeros_like(dv_ref)

        qb = q_ref[...]
        do16 = do_ref[...].astype(BF16)
        tot_q = tot_ref[...]
        tri_before = _tri(sub, lambda r, c: r < c)
        tri_upto = _tri(sub, lambda r, c: r <= c)

        def block(jb, carry, diag):
            before, dl_before, dq = carry
            sl = _key_block(jb, tk)
            ks = k_ref[sl, :]
            z = _dot(qb, ks, _NT) * scale
            lk, e = _neg_softplus(z)
            if diag:
                mask = _causal(i, jb, tq, tk, True)
                lk = jnp.where(mask, lk, 0.0)
            sig = jnp.where(z >= 0, 1.0, e) / (1.0 + e)
            pieces = []
            for u in range(n_sub):
                part = lk[:, u * sub:(u + 1) * sub]
                pieces.append(_split_dot(part, tri_before) + before)
                before = before + jnp.sum(part, axis=-1, keepdims=True)
            a = jnp.exp(z + (tot_q - jnp.concatenate(pieces, axis=1)))
            if diag:
                a = jnp.where(mask, a, 0.0)
            dl = a * _dot(do16, v_ref[sl, :], _NT)
            dl16 = dl.astype(BF16)
            pieces = []
            for u in range(n_sub):
                pieces.append(_dot(dl16[:, u * sub:(u + 1) * sub], tri_upto, _NN) + dl_before)
                dl_before = dl_before + jnp.sum(dl[:, u * sub:(u + 1) * sub], axis=-1, keepdims=True)
            dz = dl - sig * jnp.concatenate(pieces, axis=1)
            if diag:
                dz = jnp.where(mask, dz, 0.0)
            dz16 = dz.astype(BF16)
            dv_ref[sl, :] += _dot(a.astype(BF16), do16, _TN)
            dk_ref[sl, :] += _dot(dz16, qb, _TN) * scale
            return before, dl_before, dq + _dot(dz16, ks, _NN) * scale

        init = (jnp.zeros((tq, 1), F32), jnp.zeros((tq, 1), F32), jnp.zeros((tq, HEAD_DIM), F32))
        carry = lax.fori_loop(0, n_full, lambda jb, cr: block(jb, cr, False), init)
        dq_ref[...] = block(n_full, carry, True)[2]

    tile, whole, col, _ = _att_specs(s, tq)
    full = jax.ShapeDtypeStruct((s, w), F32)
    return pl.pallas_call(
        body,
        grid=(w // HEAD_DIM, s // tq),
        in_specs=[tile, whole, whole, col, tile],
        out_specs=[tile, whole, whole],
        out_shape=[full, full, full],
        compiler_params=pltpu.CompilerParams(dimension_semantics=("parallel", "arbitrary"),
                                             vmem_limit_bytes=VMEM_LIMIT),
        name="sb_bwd",
    )(q, k, v, tot, do)


@jax.custom_vjp
def sb_core(q, k, v):
    return _sb_fwd(q, k, v)[0]


def _sb_fwd(q, k, v):
    q16, k16, v16 = q.astype(BF16), k.astype(BF16), v.astype(BF16)
    o, tot = _sb_fwd_call(q16, k16, v16)
    return o, (q16, k16, v16, tot)


def _sb_bwd(res, do):
    return tuple(_sb_bwd_call(*res, do))


sb_core.defvjp(_sb_fwd, _sb_bwd)


def _gdn_specs(h, c, d):
    vec = pl.BlockSpec((None, h, c, d), lambda n: (n, 0, 0, 0))
    sq = pl.BlockSpec((None, h, c, c), lambda n: (n, 0, 0, 0))
    dec = pl.BlockSpec((None, h, 1, d), lambda n: (n, 0, 0, 0))
    st = pl.BlockSpec((None, h, d, d), lambda n: (n, 0, 0, 0))
    return vec, sq, dec, st


def _b16(x):
    return x.astype(BF16)


def _gdn_scan_fwd_call(qg, u, w, attn, kt, egl):
    n, h, c, d = qg.shape

    def body(qg_ref, u_ref, w_ref, attn_ref, kt_ref, egl_ref, o_ref, st_ref, state):
        @pl.when(pl.program_id(0) == 0)
        def _():
            state[...] = jnp.zeros_like(state)

        for hh in range(h):
            s0 = state[hh]
            st_ref[hh] = s0
            s16 = _b16(s0)
            vn = u_ref[hh] - _dot(_b16(w_ref[hh]), s16, _NN)
            vn16 = _b16(vn)
            o_ref[hh] = _dot(_b16(qg_ref[hh]), s16, _NN) + _dot(_b16(attn_ref[hh]), vn16, _NN)
            state[hh] = s0 * egl_ref[hh] + _dot(_b16(kt_ref[hh]), vn16, _TN)

    vec, sq, dec, st = _gdn_specs(h, c, d)
    return pl.pallas_call(
        body,
        grid=(n,),
        in_specs=[vec, vec, vec, sq, vec, dec],
        out_specs=[vec, st],
        out_shape=[jax.ShapeDtypeStruct((n, h, c, d), F32), jax.ShapeDtypeStruct((n, h, d, d), F32)],
        scratch_shapes=[pltpu.VMEM((h, d, d), F32)],
        compiler_params=pltpu.CompilerParams(dimension_semantics=("arbitrary",)),
        name="gdn_scan_fwd",
    )(qg, u, w, attn, kt, egl)


def _gdn_scan_bwd_call(qg, u, w, attn, kt, egl, states, do):
    n, h, c, d = qg.shape

    def body(qg_ref, u_ref, w_ref, attn_ref, kt_ref, egl_ref, st_ref, do_ref,
             dqg_ref, du_ref, dw_ref, dattn_ref, dkt_ref, degl_ref, dstate):
        @pl.when(pl.program_id(0) == 0)
        def _():
            dstate[...] = jnp.zeros_like(dstate)

        for hh in range(h):
            s0 = st_ref[hh]
            s16 = _b16(s0)
            big_d = dstate[hh]
            d16 = _b16(big_d)
            w16, kt16, qg16, attn16 = _b16(w_ref[hh]), _b16(kt_ref[hh]), _b16(qg_ref[hh]), _b16(attn_ref[hh])
            do16 = _b16(do_ref[hh])
            vn16 = _b16(u_ref[hh] - _dot(w16, s16, _NN))
            dvn = _dot(attn16, do16, _TN) + _dot(kt16, d16, _NN)
            dvn16 = _b16(dvn)
            du_ref[hh] = dvn
            dattn_ref[hh] = _dot(do16, vn16, _NT)
            dqg_ref[hh] = _dot(do16, s16, _NT)
            dkt_ref[hh] = _dot(vn16, d16, _NT)
            dw_ref[hh] = -_dot(dvn16, s16, _NT)
            degl_ref[hh] = jnp.sum(big_d * s0, axis=0, keepdims=True)
            dstate[hh] = big_d * egl_ref[hh] + _dot(qg16, do16, _TN) - _dot(w16, dvn16, _TN)

    vec, sq, dec, st = _gdn_specs(h, c, d)
    rev = lambda spec: pl.BlockSpec(spec.block_shape, lambda i: (n - 1 - i, 0, 0, 0))
    vec, sq, dec, st = rev(vec), rev(sq), rev(dec), rev(st)
    vshape = jax.ShapeDtypeStruct((n, h, c, d), F32)
    return pl.pallas_call(
        body,
        grid=(n,),
        in_specs=[vec, vec, vec, sq, vec, dec, st, vec],
        out_specs=[vec, vec, vec, sq, vec, dec],
        out_shape=[vshape, vshape, vshape, jax.ShapeDtypeStruct((n, h, c, c), F32), vshape,
                   jax.ShapeDtypeStruct((n, h, 1, d), F32)],
        scratch_shapes=[pltpu.VMEM((h, d, d), F32)],
        compiler_params=pltpu.CompilerParams(dimension_semantics=("arbitrary",)),
        name="gdn_scan_bwd",
    )(qg, u, w, attn, kt, egl, states, do)


@jax.custom_vjp
def gdn_scan(qg, u, w, attn, kt, egl):
    return _gdn_scan_fwd_call(qg, u, w, attn, kt, egl)[0]


def _gdn_scan_fwd(qg, u, w, attn, kt, egl):
    o, states = _gdn_scan_fwd_call(qg, u, w, attn, kt, egl)
    return o, (qg, u, w, attn, kt, egl, states)


def _gdn_scan_bwd(res, do):
    return tuple(_gdn_scan_bwd_call(*res, do))


gdn_scan.defvjp(_gdn_scan_fwd, _gdn_scan_bwd)


CONV_ROWS = 512
HALO = 8


def _sigmoid(x):
    return 1.0 / (1.0 + jnp.exp(-x))


def _shifted(ext, k, rows):
    if k == 0:
        return ext[HALO:HALO + rows]
    return pltpu.roll(ext, k % ext.shape[0], 0)[HALO:HALO + rows]


def _conv_specs(s, ch, tr, tc, off):
    per, last = tr // HALO, s // HALO - 1
    blk = pl.BlockSpec((tr, tc), lambda j, i: (i, j + off))
    prev = pl.BlockSpec((HALO, tc), lambda j, i: (jnp.maximum(i * per - 1, 0), j + off))
    nxt = pl.BlockSpec((HALO, tc), lambda j, i: (jnp.minimum((i + 1) * per, last), j + off))
    return blk, prev, nxt


def _dwconv_fwd_call(x, w, b, gated):
    s, ch = x.shape
    taps = w.shape[0]
    out_ch = ch // 2 if gated else ch
    tr = _pick(s, (CONV_ROWS, 256, 128, 64, 32, 16, 8))
    tc = _pick(out_ch, (1408, 512, 384, 256, 128))
    n_j = out_ch // tc
    parts = (0, n_j) if gated else (0,)

    def conv(x_ref, p_ref, w_ref, first):
        xb = x_ref[...]
        ext = jnp.concatenate([jnp.where(first, 0.0, p_ref[...]), xb], axis=0)
        y = w_ref[taps - 1:taps, :] * xb
        for k in range(1, taps):
            y = y + w_ref[taps - 1 - k:taps - k, :] * _shifted(ext, k, tr)
        return y

    def body(*refs):
        first = pl.program_id(1) == 0
        if gated:
            xa, pa, wa, ba, xb_, pb, wb, bb, o_ref = refs
            a = conv(xa, pa, wa, first) + ba[...]
            g = conv(xb_, pb, wb, first) + bb[...]
            o_ref[...] = a * _sigmoid(a) * g
        else:
            xa, pa, wa, o_ref = refs
            a = conv(xa, pa, wa, first)
            o_ref[...] = a * _sigmoid(a)

    in_specs, args = [], []
    for off in parts:
        blk, prev, _ = _conv_specs(s, ch, tr, tc, off)
        in_specs += [blk, prev, pl.BlockSpec((taps, tc), lambda j, i, off=off: (0, j + off))]
        args += [x, x, w]
        if gated:
            in_specs.append(pl.BlockSpec((1, tc), lambda j, i, off=off: (0, j + off)))
            args.append(b)
    return pl.pallas_call(
        body,
        grid=(n_j, s // tr),
        in_specs=in_specs,
        out_specs=pl.BlockSpec((tr, tc), lambda j, i: (i, j)),
        out_shape=jax.ShapeDtypeStruct((s, out_ch), F32),
        compiler_params=pltpu.CompilerParams(dimension_semantics=("parallel", "parallel"),
                                             vmem_limit_bytes=VMEM_LIMIT),
        name="dwconv_gate_fwd" if gated else "dwconv_silu_fwd",
    )(*args)


def _dwconv_bwd_call(x, w, b, do, gated):
    s, ch = x.shape
    taps = w.shape[0]
    out_ch = ch // 2 if gated else ch
    tr = _pick(s, (CONV_ROWS, 256, 128, 64, 32, 16, 8))
    tc = _pick(out_ch, (1408, 512, 384, 256, 128))
    n_j, n_i = out_ch // tc, s // tr
    parts = (0, n_j) if gated else (0,)
    ext_rows = tr + 2 * HALO

    def pre_act(x_ref, p_ref, n_ref, w_ref, first, last):
        ext = jnp.concatenate([jnp.where(first, 0.0, p_ref[...]), x_ref[...], n_ref[...]], axis=0)
        y = w_ref[taps - 1:taps, :] * ext
        for k in range(1, taps):
            y = y + w_ref[taps - 1 - k:taps - k, :] * pltpu.roll(ext, k, 0)
        return ext, y

    def grads(ext, dy, w_ref, dx_ref, dw_ref):
        dx = w_ref[taps - 1:taps, :] * dy[HALO:HALO + tr]
        for k in range(1, taps):
            dx = dx + w_ref[taps - 1 - k:taps - k, :] * _shifted(dy, -k, tr)
        dx_ref[...] = dx
        dyb = dy[HALO:HALO + tr]
        for k in range(taps):
            dw_ref[taps - 1 - k:taps - k, :] += jnp.sum(dyb * _shifted(ext, k, tr), axis=0, keepdims=True)
        return dyb

    def body(*refs):
        i = pl.program_id(1)
        first, last = i == 0, i == n_i - 1
        rows = lax.broadcasted_iota(jnp.int32, (ext_rows, 1), 0)
        inside = jnp.logical_and(rows >= HALO, jnp.logical_or(rows < HALO + tr, jnp.logical_not(last)))
        if gated:
            (xa, pa, na, wa, ba, xb_, pb, nb, wb, bb, do_ref, don_ref,
             dxa_ref, dxb_ref, dwa_ref, dwb_ref, dba_ref, dbb_ref) = refs
        else:
            xa, pa, na, wa, do_ref, don_ref, dxa_ref, dwa_ref = refs

        @pl.when(first)
        def _():
            dwa_ref[...] = jnp.zeros_like(dwa_ref)
            if gated:
                dwb_ref[...] = jnp.zeros_like(dwb_ref)
                dba_ref[...] = jnp.zeros_like(dba_ref)
                dbb_ref[...] = jnp.zeros_like(dbb_ref)

        d_out = jnp.concatenate([jnp.zeros((HALO, tc), F32), do_ref[...], don_ref[...]], axis=0)
        d_out = jnp.where(inside, d_out, 0.0)
        ext_a, a = pre_act(xa, pa, na, wa, first, last)
        if gated:
            a = a + ba[...]
            ext_b, g = pre_act(xb_, pb, nb, wb, first, last)
            g = g + bb[...]
            sg = _sigmoid(a)
            silu = a * sg
            dya = jnp.where(inside, d_out * g * (sg + silu * (1.0 - sg)), 0.0)
            dyg = jnp.where(inside, d_out * silu, 0.0)
            dba_ref[...] += jnp.sum(grads(ext_a, dya, wa, dxa_ref, dwa_ref), axis=0, keepdims=True)
            dbb_ref[...] += jnp.sum(grads(ext_b, dyg, wb, dxb_ref, dwb_ref), axis=0, keepdims=True)
        else:
            sg = _sigmoid(a)
            dya = jnp.where(inside, d_out * (sg + a * sg * (1.0 - sg)), 0.0)
            grads(ext_a, dya, wa, dxa_ref, dwa_ref)

    in_specs, args = [], []
    for off in parts:
        blk, prev, nxt = _conv_specs(s, ch, tr, tc, off)
        in_specs += [blk, prev, nxt, pl.BlockSpec((taps, tc), lambda j, i, off=off: (0, j + off))]
        args += [x, x, x, w]
        if gated:
            in_specs.append(pl.BlockSpec((1, tc), lambda j, i, off=off: (0, j + off)))
            args.append(b)
    blk, _, nxt = _conv_specs(s, out_ch, tr, tc, 0)
    in_specs += [blk, nxt]
    args += [do, do]
    n_half = len(parts)
    out_specs = ([blk] * n_half + [pl.BlockSpec((taps, tc), lambda j, i: (0, j))] * n_half
                 + ([pl.BlockSpec((1, tc), lambda j, i: (0, j))] * n_half if gated else []))
    out_shape = ([jax.ShapeDtypeStruct((s, out_ch), F32)] * n_half + [jax.ShapeDtypeStruct((taps, out_ch), F32)] * n_half
                 + ([jax.ShapeDtypeStruct((1, out_ch), F32)] * n_half if gated else []))
    return pl.pallas_call(
        body,
        grid=(n_j, n_i),
        in_specs=in_specs,
        out_specs=out_specs,
        out_shape=out_shape,
        compiler_params=pltpu.CompilerParams(dimension_semantics=("parallel", "arbitrary"),
                                             vmem_limit_bytes=VMEM_LIMIT),
        name="dwconv_gate_bwd" if gated else "dwconv_silu_bwd",
    )(*args)


@jax.custom_vjp
def conv_gate(u, w, b):
    return _dwconv_fwd_call(u, w, b[None], True)


def _conv_gate_fwd(u, w, b):
    return _dwconv_fwd_call(u, w, b[None], True), (u, w, b)


def _conv_gate_bwd(res, do):
    u, w, b = res
    dxa, dxb, dwa, dwb, dba, dbb = _dwconv_bwd_call(u, w, b[None], do, True)
    return (jnp.concatenate([dxa, dxb], axis=1), jnp.concatenate([dwa, dwb], axis=1),
            jnp.concatenate([dba, dbb], axis=1)[0])


conv_gate.defvjp(_conv_gate_fwd, _conv_gate_bwd)


@jax.custom_vjp
def conv_silu(x, w):
    return _dwconv_fwd_call(x, w, None, False)


def _conv_silu_fwd(x, w):
    return _dwconv_fwd_call(x, w, None, False), (x, w)


def _conv_silu_bwd(res, do):
    x, w = res
    dx, dw = _dwconv_bwd_call(x, w, None, do, False)
    return dx, dw


conv_silu.defvjp(_conv_silu_fwd, _conv_silu_bwd)


ROW_TILE = 256


def _rmsnorm_fwd_call(x, g):
    s, d = x.shape
    tr = _pick(s, (ROW_TILE, 128, 64, 32, 16, 8))

    def body(x_ref, g_ref, o_ref):
        xb = x_ref[...]
        r = lax.rsqrt(jnp.mean(xb * xb, axis=-1, keepdims=True) + EPS)
        o_ref[...] = (xb * r) * g_ref[...]

    return pl.pallas_call(
        body,
        grid=(s // tr,),
        in_specs=[pl.BlockSpec((tr, d), lambda i: (i, 0)), pl.BlockSpec((1, d), lambda i: (0, 0))],
        out_specs=pl.BlockSpec((tr, d), lambda i: (i, 0)),
        out_shape=jax.ShapeDtypeStruct((s, d), F32),
        compiler_params=pltpu.CompilerParams(dimension_semantics=("parallel",)),
        name="rmsnorm_fwd",
    )(x, g)


def _rmsnorm_bwd_call(x, g, dy):
    s, d = x.shape
    tr = _pick(s, (ROW_TILE, 128, 64, 32, 16, 8))

    def body(x_ref, g_ref, dy_ref, dx_ref, dg_ref):
        @pl.when(pl.program_id(0) == 0)
        def _():
            dg_ref[...] = jnp.zeros_like(dg_ref)

        xb = x_ref[...]
        r = lax.rsqrt(jnp.mean(xb * xb, axis=-1, keepdims=True) + EPS)
        y = xb * r
        dyb = dy_ref[...]
        dg_ref[...] += jnp.sum(dyb * y, axis=0, keepdims=True)
        dn = dyb * g_ref[...]
        dx_ref[...] = r * (dn - y * jnp.mean(dn * y, axis=-1, keepdims=True))

    row = pl.BlockSpec((tr, d), lambda i: (i, 0))
    vec = pl.BlockSpec((1, d), lambda i: (0, 0))
    return pl.pallas_call(
        body,
        grid=(s // tr,),
        in_specs=[row, vec, row],
        out_specs=[row, vec],
        out_shape=[jax.ShapeDtypeStruct((s, d), F32), jax.ShapeDtypeStruct((1, d), F32)],
        compiler_params=pltpu.CompilerParams(dimension_semantics=("arbitrary",)),
        name="rmsnorm_bwd",
    )(x, g, dy)


@jax.custom_vjp
def rmsnorm_rows(x, g):
    return _rmsnorm_fwd_call(x, g[None])


def _rmsnorm_rows_fwd(x, g):
    return _rmsnorm_fwd_call(x, g[None]), (x, g)


def _rmsnorm_rows_bwd(res, dy):
    x, g = res
    dx, dg = _rmsnorm_bwd_call(x, g[None], dy)
    return dx, dg[0]


rmsnorm_rows.defvjp(_rmsnorm_rows_fwd, _rmsnorm_rows_bwd)


def _merge_specs(s, d, tr):
    gate = pl.BlockSpec((tr, N_BRANCH * d), lambda i: (i, 0))
    bias = pl.BlockSpec((1, N_BRANCH * d), lambda i: (0, 0))
    row = pl.BlockSpec((tr, d), lambda i: (i, 0))
    return gate, bias, row


def _merge_fwd_call(gates, bias, ta, tb, tc):
    s, d = ta.shape
    tr = _pick(s, (ROW_TILE, 128, 64, 32, 16, 8))

    def body(g_ref, b_ref, ta_ref, tb_ref, tc_ref, o_ref):
        acc = None
        for b, t_ref in enumerate((ta_ref, tb_ref, tc_ref)):
            gate = _sigmoid(g_ref[:, b * d:(b + 1) * d] + b_ref[:, b * d:(b + 1) * d])
            acc = gate * t_ref[...] if acc is None else acc + gate * t_ref[...]
        o_ref[...] = acc

    gate, bias_s, row = _merge_specs(s, d, tr)
    return pl.pallas_call(
        body,
        grid=(s // tr,),
        in_specs=[gate, bias_s, row, row, row],
        out_specs=row,
        out_shape=jax.ShapeDtypeStruct((s, d), F32),
        compiler_params=pltpu.CompilerParams(dimension_semantics=("parallel",), vmem_limit_bytes=VMEM_LIMIT),
        name="merge_fwd",
    )(gates, bias, ta, tb, tc)


def _merge_bwd_call(gates, bias, ta, tb, tc, dm):
    s, d = ta.shape
    tr = _pick(s, (ROW_TILE, 128, 64, 32, 16, 8))

    def body(g_ref, b_ref, ta_ref, tb_ref, tc_ref, dm_ref, dg_ref, db_ref, dta_ref, dtb_ref, dtc_ref):
        @pl.when(pl.program_id(0) == 0)
        def _():
            db_ref[...] = jnp.zeros_like(db_ref)

        dmb = dm_ref[...]
        for b, (t_ref, dt_ref) in enumerate(((ta_ref, dta_ref), (tb_ref, dtb_ref), (tc_ref, dtc_ref))):
            cols = slice(b * d, (b + 1) * d)
            gate = _sigmoid(g_ref[:, cols] + b_ref[:, cols])
            dt_ref[...] = gate * dmb
            dpre = dmb * t_ref[...] * (gate * (1.0 - gate))
            dg_ref[:, cols] = dpre
            db_ref[:, cols] += jnp.sum(dpre, axis=0, keepdims=True)

    gate, bias_s, row = _merge_specs(s, d, tr)
    rows = jax.ShapeDtypeStruct((s, d), F32)
    return pl.pallas_call(
        body,
        grid=(s // tr,),
        in_specs=[gate, bias_s, row, row, row, row],
        out_specs=[gate, bias_s, row, row, row],
        out_shape=[jax.ShapeDtypeStruct((s, N_BRANCH * d), F32), jax.ShapeDtypeStruct((1, N_BRANCH * d), F32), rows, rows, rows],
        compiler_params=pltpu.CompilerParams(dimension_semantics=("arbitrary",), vmem_limit_bytes=VMEM_LIMIT),
        name="merge_bwd",
    )(gates, bias, ta, tb, tc, dm)


@jax.custom_vjp
def gated_merge(gates, bias, ta, tb, tc):
    return _merge_fwd_call(gates, bias[None], ta, tb, tc)


def _gated_merge_fwd(gates, bias, ta, tb, tc):
    return _merge_fwd_call(gates, bias[None], ta, tb, tc), (gates, bias, ta, tb, tc)


def _gated_merge_bwd(res, dm):
    gates, bias, ta, tb, tc = res
    dg, db, dta, dtb, dtc = _merge_bwd_call(gates, bias[None], ta, tb, tc, dm)
    return dg, db[0], dta, dtb, dtc


gated_merge.defvjp(_gated_merge_fwd, _gated_merge_bwd)


def _flat_rows(rows):
    return _pick(rows, (FLAT_TILE_ROWS, 1024, 512, 256, 128, 64, 32, 16, 8))


def _add_own_half(g, a, core):
    _, nb, rows, _ = g.shape
    tr = _flat_rows(rows)

    def body(c_ref, g_ref, a_ref, o_ref):
        o_ref[...] = g_ref[...] + a_ref[...]

    return pl.pallas_call(
        body,
        grid_spec=pltpu.PrefetchScalarGridSpec(
            num_scalar_prefetch=1,
            grid=(nb, rows // tr),
            in_specs=[pl.BlockSpec((None, None, tr, LANES), lambda j, r, c_ref: (c_ref[0], j, r, 0)),
                      pl.BlockSpec((None, tr, LANES), lambda j, r, c_ref: (j, r, 0))],
            out_specs=pl.BlockSpec((None, tr, LANES), lambda j, r, c_ref: (j, r, 0)),
        ),
        out_shape=jax.ShapeDtypeStruct((nb, rows, LANES), F32),
        compiler_params=pltpu.CompilerParams(dimension_semantics=("parallel", "parallel")),
        name="rs_add_own_half",
    )(core, g, a)


def _sum_chips(b, core):
    nb, rows, _ = b.shape
    tr = _flat_rows(rows)

    def body(c_ref, b_ref, o_ref):
        acc = b_ref[0] + b_ref[1]
        for j in range(2, nb):
            acc = acc + b_ref[j]
        o_ref[...] = acc

    return pl.pallas_call(
        body,
        grid_spec=pltpu.PrefetchScalarGridSpec(
            num_scalar_prefetch=1,
            grid=(rows // tr,),
            in_specs=[pl.BlockSpec((nb, tr, LANES), lambda r, c_ref: (0, r, 0))],
            out_specs=pl.BlockSpec((None, tr, LANES), lambda r, c_ref: (c_ref[0], r, 0)),
        ),
        out_shape=jax.ShapeDtypeStruct((2, rows, LANES), F32),
        compiler_params=pltpu.CompilerParams(dimension_semantics=("parallel",)),
        name="rs_sum_chips",
    )(core, b)


ADAM_BLOCK_BYTES = 2 * 1024 * 1024


def _adamw(w, g, m, v, tag):
    rows, cols = w.shape
    tr = rows
    for cand in range(8, rows, 8):
        if rows % cand == 0 and cand * cols * 4 <= ADAM_BLOCK_BYTES:
            tr = cand

    def body(w_ref, g_ref, m_ref, v_ref, d_ref, nm_ref, nv_ref):
        gg = g_ref[...]
        nm = ADAM_B1 * m_ref[...] + (1.0 - ADAM_B1) * gg
        nv = ADAM_B2 * v_ref[...] + (1.0 - ADAM_B2) * jnp.square(gg)
        m_hat = nm / (1.0 - ADAM_B1 ** ADAM_STEP)
        v_hat = nv / (1.0 - ADAM_B2 ** ADAM_STEP)
        d_ref[...] = -ADAM_LR * (m_hat / (jnp.sqrt(v_hat) + ADAM_EPS) + ADAM_WD * w_ref[...])
        nm_ref[...] = nm
        nv_ref[...] = nv

    spec = pl.BlockSpec((tr, cols), lambda r: (r, 0))
    shape = jax.ShapeDtypeStruct((rows, cols), F32)
    return pl.pallas_call(
        body,
        grid=(rows // tr,),
        in_specs=[spec] * 4,
        out_specs=[spec] * 3,
        out_shape=[shape] * 3,
        compiler_params=pltpu.CompilerParams(dimension_semantics=("parallel",), vmem_limit_bytes=VMEM_LIMIT),
        name=f"adamw_{tag}",
    )(w, g, m, v)


def _loss_head(y, t):
    s, d = y.shape
    tr = _pick(s, (512, 256, 128, 64, 32, 16, 8))

    def body(y_ref, t_ref, dy_ref, l_ref):
        @pl.when(pl.program_id(0) == 0)
        def _():
            l_ref[...] = jnp.zeros_like(l_ref)

        diff = y_ref[...] - t_ref[...]
        dy_ref[...] = diff / d
        row = jnp.mean(jnp.square(diff), axis=-1, keepdims=True)
        l_ref[...] += 0.5 * jnp.sum(row, axis=0, keepdims=True)

    return pl.pallas_call(
        body,
        grid=(s // tr,),
        in_specs=[pl.BlockSpec((tr, d), lambda r: (r, 0))] * 2,
        out_specs=[pl.BlockSpec((tr, d), lambda r: (r, 0)), pl.BlockSpec((1, LANES), lambda r: (0, 0))],
        out_shape=[jax.ShapeDtypeStruct((s, d), F32), jax.ShapeDtypeStruct((1, LANES), F32)],
        compiler_params=pltpu.CompilerParams(dimension_semantics=("arbitrary",)),
        name="loss_head",
    )(y, t)


def _place():
    x, y, c = lax.axis_index("x"), lax.axis_index("y"), lax.axis_index("c")
    chips = [(1 - x, y), (x, 1 - y), (1 - x, 1 - y)]
    return x, y, c, chips


def _all_gather_chips(flat):
    _, rows, _ = flat.shape

    def body(x_ref, o_ref, send_sems, recv_sems):
        x, y, c, chips = _place()
        me = 2 * x + y
        sib = (x, y, 1 - c)

        def remote(k, src, dst, to):
            return pltpu.make_async_remote_copy(src_ref=src, dst_ref=dst, send_sem=send_sems.at[k],
                                                recv_sem=recv_sems.at[k], device_id=to, device_id_type=MESH)

        first = [remote(k, x_ref.at[c], o_ref.at[me, c], (px, py, c)) for k, (px, py) in enumerate(chips)]
        for cp in first:
            cp.start()
        passed = []
        for k, (px, py) in enumerate(chips):
            blk = o_ref.at[2 * px + py, c]
            remote(k, x_ref.at[c], blk, (px, py, c)).wait_recv()
            cp = remote(3 + k, blk, blk, sib)
            cp.start()
            passed.append(cp)
        for k, (px, py) in enumerate(chips):
            blk = o_ref.at[2 * px + py, 1 - c]
            remote(3 + k, blk, blk, sib).wait_recv()
        for cp in first + passed:
            cp.wait_send()

    return pl.pallas_call(
        body,
        in_specs=[ANY],
        out_specs=ANY,
        out_shape=jax.ShapeDtypeStruct((N_CHIPS, 2, rows, LANES), flat.dtype),
        scratch_shapes=[pltpu.SemaphoreType.DMA((6,)), pltpu.SemaphoreType.DMA((6,))],
        name=f"all_gather_chips_{jnp.dtype(flat.dtype).name}",
    )(flat)


def _rs_sibling_exchange(g):
    _, nb, rows, _ = g.shape

    def body(g_ref, a_ref, send_sem, recv_sem):
        x, y, c, _ = _place()
        cp = pltpu.make_async_remote_copy(src_ref=g_ref.at[1 - c], dst_ref=a_ref, send_sem=send_sem,
                                          recv_sem=recv_sem, device_id=(x, y, 1 - c), device_id_type=MESH)
        cp.start()
        cp.wait()

    return pl.pallas_call(
        body,
        in_specs=[ANY],
        out_specs=ANY,
        out_shape=jax.ShapeDtypeStruct((nb, rows, LANES), F32),
        scratch_shapes=[pltpu.SemaphoreType.DMA, pltpu.SemaphoreType.DMA],
        name="rs_sibling_exchange",
    )(g)


def _rs_chip_exchange(p):
    nb, rows, _ = p.shape

    def body(p_ref, b_ref, send_sems, recv_sems, local_sem):
        x, y, c, chips = _place()
        me = 2 * x + y
        mine = pltpu.make_async_copy(p_ref.at[me], b_ref.at[me], local_sem)
        mine.start()
        copies = [pltpu.make_async_remote_copy(src_ref=p_ref.at[2 * px + py], dst_ref=b_ref.at[me],
                                               send_sem=send_sems.at[k], recv_sem=recv_sems.at[k],
                                               device_id=(px, py, c), device_id_type=MESH)
                  for k, (px, py) in enumerate(chips)]
        for cp in copies:
            cp.start()
        for cp in copies:
            cp.wait()
        mine.wait()

    return pl.pallas_call(
        body,
        in_specs=[ANY],
        out_specs=ANY,
        out_shape=jax.ShapeDtypeStruct((nb, rows, LANES), F32),
        scratch_shapes=[pltpu.SemaphoreType.DMA((3,)), pltpu.SemaphoreType.DMA((3,)), pltpu.SemaphoreType.DMA],
        name="rs_chip_exchange",
    )(p)


def _rs_sibling_gather(r):
    _, rows, _ = r.shape

    def body(r_ref, o_ref, send_sem, recv_sem):
        x, y, c, _ = _place()
        cp = pltpu.make_async_remote_copy(src_ref=o_ref.at[c], dst_ref=o_ref.at[c], send_sem=send_sem,
                                          recv_sem=recv_sem, device_id=(x, y, 1 - c), device_id_type=MESH)
        cp.start()
        cp.wait()

    return pl.pallas_call(
        body,
        in_specs=[ANY],
        out_specs=ANY,
        out_shape=jax.ShapeDtypeStruct((2, rows, LANES), F32),
        input_output_aliases={0: 0},
        scratch_shapes=[pltpu.SemaphoreType.DMA, pltpu.SemaphoreType.DMA],
        name="rs_sibling_gather",
    )(r)


def _reduce_scatter(g, core):
    a = _rs_sibling_exchange(g)
    p = _add_own_half(g, a, core)
    b = _rs_chip_exchange(p)
    return _rs_sibling_gather(_sum_chips(b, core))


def _all_reduce_small(v):
    rows, _ = v.shape
    n_dev = 8

    def body(v_ref, o_ref, gath, send_sems, recv_sems):
        x, y, c, _ = _place()
        me = 4 * x + 2 * y + c
        gath[me] = v_ref[...]
        copies = []
        for mask in range(1, n_dev):
            px = 1 - x if mask & 4 else x
            py = 1 - y if mask & 2 else y
            pc = 1 - c if mask & 1 else c
            copies.append(pltpu.make_async_remote_copy(
                src_ref=v_ref, dst_ref=gath.at[me], send_sem=send_sems.at[mask - 1],
                recv_sem=recv_sems.at[mask - 1], device_id=(px, py, pc), device_id_type=MESH))
        for cp in copies:
            cp.start()
        for cp in copies:
            cp.wait()
        acc = gath[0]
        for k in range(1, n_dev):
            acc = acc + gath[k]
        o_ref[...] = acc

    return pl.pallas_call(
        body,
        in_specs=[VMEM],
        out_specs=VMEM,
        out_shape=jax.ShapeDtypeStruct((rows, LANES), F32),
        scratch_shapes=[pltpu.VMEM((n_dev, rows, LANES), F32), pltpu.SemaphoreType.DMA((n_dev - 1,)),
                        pltpu.SemaphoreType.DMA((n_dev - 1,))],
        name="all_reduce_small",
    )(v)


def _flat_len(shapes, unit_rows=FLAT_TILE_ROWS):
    n = sum(int(np.prod(s)) for s in shapes)
    unit = 2 * unit_rows * LANES
    return -(-n // unit) * unit


def _pack(arrays, total):
    flat = jnp.concatenate([a.reshape(-1) for a in arrays])
    return jnp.pad(flat, (0, total - flat.shape[0]))


def _unpack(flat, shapes):
    out, off = [], 0
    for s in shapes:
        n = int(np.prod(s))
        out.append(flat[off:off + n].reshape(s))
        off += n
    return out


def _permute_w_in(w):
    pad = jnp.zeros(w.shape[:-1] + (N_IN_PAD - N_IN,), w.dtype)
    return jnp.concatenate([w[..., 0:1536], w[..., 1540:3076], w[..., 3084:3596], w[..., 3596:5132],
                            w[..., 5132:8204], w[..., 1536:1540], w[..., 3076:3080], w[..., 3080:3084], pad],
                           axis=-1)


def _unpermute_w_in(w):
    return jnp.concatenate([w[..., 0:1536], w[..., 8192:8196], w[..., 1536:3072], w[..., 8196:8200],
                            w[..., 8200:8204], w[..., 3072:3584], w[..., 3584:5120], w[..., 5120:8192]], axis=-1)


P_FQ, P_FK, P_FV, P_GQ, P_GK, P_GV, P_GZ, P_SQ, P_SK, P_SV, P_GATES, P_FF, P_GB, P_GA = (
    0, 512, 1024, 1536, 2048, 2560, 3072, 3584, 4096, 4608, 5120, 8192, 8196, 8200)


def rmsnorm(x, g):
    xf = x.astype(F32)
    y = xf * lax.rsqrt(jnp.mean(xf * xf, axis=-1, keepdims=True) + EPS)
    return (y * g.astype(F32)).astype(x.dtype)


def l2norm(x):
    xf = x.astype(F32)
    return xf * lax.rsqrt(jnp.sum(xf * xf, axis=-1, keepdims=True) + EPS)


def heads(x, n):
    return x.reshape(x.shape[:-1] + (n, -1))


def gated_delta_rule(q, k, v, g, beta):
    B, T, H, dk = q.shape
    dv = v.shape[-1]
    N = T // CHUNK

    def chunks(a):
        a = a.astype(F32).reshape((B, N, CHUNK, H) + a.shape[3:])
        return jnp.moveaxis(a, (1, 3), (0, 2))

    qc = chunks(q) * dk ** -0.5
    kc = chunks(k)
    vc = chunks(v)
    bc = chunks(beta)
    gc = jnp.cumsum(chunks(g), axis=-1)
    idx = jnp.arange(CHUNK)
    causal = idx[:, None] >= idx[None, :]
    strict = idx[:, None] > idx[None, :]
    decay = jnp.exp(jnp.where(causal, gc[..., :, None] - gc[..., None, :], -jnp.inf))
    kk = jnp.einsum("nbhcd,nbhed->nbhce", kc, kc)
    a_mat = jnp.where(strict, bc[..., :, None] * kk * decay, 0.0) + jnp.eye(CHUNK, dtype=F32)
    rhs = jnp.concatenate([vc * bc[..., None], kc * (bc * jnp.exp(gc))[..., None]], axis=-1)
    sol = lax.linalg.triangular_solve(a_mat, rhs, left_side=True, lower=True)
    u, w = sol[..., :dv], sol[..., dv:]
    attn = jnp.where(causal, jnp.einsum("nbhcd,nbhed->nbhce", qc, kc) * decay, 0.0)
    g_last = gc[..., -1]
    k_tail = kc * jnp.exp(g_last[..., None] - gc)[..., None]

    egl = jnp.broadcast_to(jnp.exp(g_last)[:, 0, :, None, None], (N, H, 1, dv))
    o = gdn_scan((qc * jnp.exp(gc)[..., None])[:, 0], u[:, 0], w[:, 0], attn[:, 0], k_tail[:, 0], egl)
    return o.transpose(0, 2, 1, 3).reshape(B, T, H, dv)


def _layer(x, mem, p, p16):
    S, D = x.shape
    h = rmsnorm_rows(x, p['norm_mix'])
    def mm(a, name):
        return matmul(a, p[name], p16[name])

    proj = mm(h, 'w_in')[None]

    def seg(off, width):
        return proj[..., off:off + width]

    fq = rmsnorm(heads(seg(P_FQ, 512), HEADS), p['fox_qnorm']).reshape(S, 512)
    fk = rmsnorm(heads(seg(P_FK, 512), HEADS), p['fox_knorm']).reshape(S, 512)
    logf = jax.nn.log_sigmoid((seg(P_FF, HEADS) + p['fox_fbias']).astype(F32))
    ya = fox_core(fq, fk, seg(P_FV, 512)[0], jnp.cumsum(logf[0], axis=0))

    qkv = conv_silu(seg(P_GQ, 1536)[0], p['gdn_conv'])[None]
    cq, ck, cv = jnp.split(qkv, [512, 1024], axis=-1)
    beta = jax.nn.sigmoid(seg(P_GB, HEADS).astype(F32))
    g_log = -jnp.exp(p['gdn_a_log'].astype(F32)) * jax.nn.softplus((seg(P_GA, HEADS) + p['gdn_dt_bias']).astype(F32))
    o = gated_delta_rule(l2norm(heads(cq, HEADS)), l2norm(heads(ck, HEADS)), heads(cv, HEADS), g_log, beta)
    yb = rmsnorm(o, p['gdn_onorm']) * jax.nn.silu(heads(seg(P_GZ, 512), HEADS).astype(F32))
    yb = yb.astype(x.dtype).reshape(1, S, 512)

    yc = sb_core(seg(P_SQ, 512)[0], seg(P_SK, 512)[0], seg(P_SV, 512)[0])

    mixed = gated_merge(seg(P_GATES, N_BRANCH * D)[0], p['gate_bias'], mm(ya, 'w_oa'), mm(yb[0], 'w_ob'),
                        mm(yc, 'w_oc'))
    x = x + mm(mixed, 'w_out')

    hq = rmsnorm_rows(x, p['norm_xq'])
    hm = rmsnorm_rows(mem, p['norm_mem'])
    q = rmsnorm(heads(mm(hq, 'w_mq'), HEADS), p['mq_norm'])
    kv = mm(hm, 'w_mkv')
    k, v = jnp.split(kv, 2, axis=-1)
    k = rmsnorm(heads(k, HEADS), p['mk_norm'])
    v = heads(v, HEADS)
    s = jnp.einsum("qhd,khd->hqk", q, k).astype(F32) * MEM_DIM ** -0.5
    pr = jax.nn.softmax(s, axis=-1).astype(v.dtype)
    om = jnp.einsum("hqk,khd->qhd", pr, v).reshape(S, 512)
    x = x + mm(om, 'w_mo')

    hf = rmsnorm_rows(x, p['norm_ffn'])
    act = conv_gate(mm(hf, 'w_up'), p['ffn_conv'], p['ffn_conv_b'])
    return x + mm(act, 'w_down')


def kernel(x, mem, norm_mix, w_in, fox_fbias, fox_qnorm, fox_knorm, gdn_conv, gdn_a_log, gdn_dt_bias, gdn_onorm, gate_bias, w_oa, w_ob, w_oc, w_out, norm_xq, norm_mem, w_mq, w_mkv, mq_norm, mk_norm, w_mo, norm_ffn, w_up, ffn_conv, ffn_conv_b, w_down, loss_target, m_norm_mix, m_w_in, m_fox_fbias, m_fox_qnorm, m_fox_knorm, m_gdn_conv, m_gdn_a_log, m_gdn_dt_bias, m_gdn_onorm, m_gate_bias, m_w_oa, m_w_ob, m_w_oc, m_w_out, m_norm_xq, m_norm_mem, m_w_mq, m_w_mkv, m_mq_norm, m_mk_norm, m_w_mo, m_norm_ffn, m_w_up, m_ffn_conv, m_ffn_conv_b, m_w_down, v_norm_mix, v_w_in, v_fox_fbias, v_fox_qnorm, v_fox_knorm, v_gdn_conv, v_gdn_a_log, v_gdn_dt_bias, v_gdn_onorm, v_gate_bias, v_w_oa, v_w_ob, v_w_oc, v_w_out, v_norm_xq, v_norm_mem, v_w_mq, v_w_mkv, v_mq_norm, v_mk_norm, v_w_mo, v_norm_ffn, v_w_up, v_ffn_conv, v_ffn_conv_b, v_w_down):
    args = (x, mem, norm_mix, w_in, fox_fbias, fox_qnorm, fox_knorm, gdn_conv, gdn_a_log, gdn_dt_bias, gdn_onorm, gate_bias, w_oa, w_ob, w_oc, w_out, norm_xq, norm_mem, w_mq, w_mkv, mq_norm, mk_norm, w_mo, norm_ffn, w_up, ffn_conv, ffn_conv_b, w_down)
    moments_m = (m_norm_mix, m_w_in, m_fox_fbias, m_fox_qnorm, m_fox_knorm, m_gdn_conv, m_gdn_a_log, m_gdn_dt_bias, m_gdn_onorm, m_gate_bias, m_w_oa, m_w_ob, m_w_oc, m_w_out, m_norm_xq, m_norm_mem, m_w_mq, m_w_mkv, m_mq_norm, m_mk_norm, m_w_mo, m_norm_ffn, m_w_up, m_ffn_conv, m_ffn_conv_b, m_w_down)
    moments_v = (v_norm_mix, v_w_in, v_fox_fbias, v_fox_qnorm, v_fox_knorm, v_gdn_conv, v_gdn_a_log, v_gdn_dt_bias, v_gdn_onorm, v_gate_bias, v_w_oa, v_w_ob, v_w_oc, v_w_out, v_norm_xq, v_norm_mem, v_w_mq, v_w_mkv, v_mq_norm, v_mk_norm, v_w_mo, v_norm_ffn, v_w_up, v_ffn_conv, v_ffn_conv_b, v_w_down)
    w = dict(zip(IN_NAMES, args))
    m = dict(zip(WEIGHTS, moments_m))
    v = dict(zip(WEIGHTS, moments_v))
    xs, mems, tgt = x[0], mem[0], loss_target[0]
    core = lax.axis_index("c").astype(jnp.int32).reshape(1)

    big = list(SHARDED)
    shard_shapes = [w[n].shape for n in big]
    total = _flat_len(shard_shapes)
    half_rows = total // (2 * LANES)
    small_shapes = [w[n].shape for n in SMALL]
    n_small = sum(int(np.prod(s)) for s in small_shapes) + 1
    small_total = -(-n_small // (8 * LANES)) * (8 * LANES)

    my_chip = 2 * lax.axis_index("x") + lax.axis_index("y")

    def gather(names, dtype, unit_rows):
        shapes = [w[n].shape for n in names]
        tot = _flat_len(shapes, unit_rows)
        flat = _pack([w[n].astype(dtype) for n in names], tot)
        got = _all_gather_chips(flat.reshape(2, tot // (2 * LANES), LANES)).reshape(N_CHIPS, tot)
        out = {}
        for n, blocks in zip(names, zip(*[_unpack(got[j], shapes) for j in range(N_CHIPS)])):
            axis = SHARDED[n]
            out[n] = lax.dynamic_update_slice_in_dim(jnp.concatenate(blocks, axis=axis), w[n].astype(dtype),
                                                     my_chip * w[n].shape[axis], axis)
        return out

    conv_names = ['gdn_conv', 'ffn_conv']
    params16 = gather([n for n in big if n not in conv_names], BF16, FLAT_TILE_ROWS)
    params16['w_in'] = _permute_w_in(params16['w_in'])
    params = {n: a.astype(F32) for n, a in params16.items()}
    params.update(gather(conv_names, F32, 8))
    for n in SMALL:
        params[n] = w[n]

    def model(x0, pp):
        for layer in range(DEPTH):
            x0 = _layer(x0, mems, {n: a[layer] for n, a in pp.items()}, {n: a[layer] for n, a in params16.items()})
        return x0

    y, model_vjp = jax.vjp(model, xs, params)
    dy, loss_part = _loss_head(y, tgt)
    dx0, grads = model_vjp(dy)
    grads['w_in'] = _unpermute_w_in(grads['w_in'])

    def chip_blocks(g, axis):
        return jnp.stack(jnp.split(g, N_CHIPS, axis=axis)).reshape(N_CHIPS, -1)

    g_blocks = jnp.concatenate([chip_blocks(grads[n], SHARDED[n]) for n in big], axis=1)
    g_blocks = jnp.pad(g_blocks, ((0, 0), (0, total - g_blocks.shape[1])))
    g_halves = g_blocks.reshape(N_CHIPS, 2, half_rows, LANES).transpose(1, 0, 2, 3)
    g_flat = _reduce_scatter(g_halves, core).reshape(total // LANES, LANES)

    s_part = _pack([grads[n] for n in SMALL] + [loss_part[0, :1]], small_total)
    s_sum = _all_reduce_small(s_part.reshape(small_total // LANES, LANES))
    small_grads = _unpack(s_sum.reshape(-1), small_shapes + [(1,)])
    loss = small_grads.pop()[0]

    out = {}
    for n, g_shard in zip(big, _unpack(g_flat.reshape(-1), shard_shapes)):
        shape = w[n].shape
        rows_of = lambda a: a.reshape(-1, shape[-1])
        out['grad', n] = g_shard
        for kind, a in zip(('delta', 'new_m', 'new_v'),
                           _adamw(rows_of(w[n]), rows_of(g_shard), rows_of(m[n]), rows_of(v[n]), n)):
            out[kind, n] = a.reshape(shape)
    srows = small_total // LANES
    sd, snm, snv = _adamw(_pack([w[n] for n in SMALL], small_total).reshape(srows, LANES), s_sum,
                          _pack([m[n] for n in SMALL], small_total).reshape(srows, LANES),
                          _pack([v[n] for n in SMALL], small_total).reshape(srows, LANES), "replicated")
    for kind, flat_small in (('grad', s_sum), ('delta', sd), ('new_m', snm), ('new_v', snv)):
        for n, a in zip(SMALL, _unpack(flat_small.reshape(-1), small_shapes)):
            out[kind, n] = a
    return (loss, dx0[None], *[out[kind, n] for kind in ('grad', 'delta', 'new_m', 'new_v') for n in WEIGHTS])
```

```python
import functools

import jax
import jax.numpy as jnp
import numpy as np
from jax import lax
from jax.experimental import pallas as pl
from jax.experimental.pallas import tpu as pltpu

F32 = jnp.float32
BF16 = jnp.bfloat16
MESH = pl.DeviceIdType.MESH
ANY = pl.BlockSpec(memory_space=pl.ANY)
VMEM = pl.BlockSpec(memory_space=pltpu.VMEM)

D_MODEL = 1024
DEPTH = 4
CHUNK = 64
Q_BLOCK = 128
EPS = 1e-6
HEADS = 4
HEAD_DIM = 128
GDN_CONV = 4
MEM_DIM = 128
D_FF = 2816
N_BRANCH = 3
N_IN = 8204
N_IN_PAD = 8320

ADAM_LR = 0.001
ADAM_B1 = 0.9
ADAM_B2 = 0.999
ADAM_EPS = 1e-08
ADAM_WD = 0.01
ADAM_STEP = 10

N_CHIPS = 4
LANES = 128
FLAT_TILE_ROWS = 2048
VMEM_LIMIT = 48 * 1024 * 1024

IN_NAMES = ['x', 'mem', 'norm_mix', 'w_in', 'fox_fbias', 'fox_qnorm', 'fox_knorm', 'gdn_conv', 'gdn_a_log',
            'gdn_dt_bias', 'gdn_onorm', 'gate_bias', 'w_oa', 'w_ob', 'w_oc', 'w_out', 'norm_xq', 'norm_mem',
            'w_mq', 'w_mkv', 'mq_norm', 'mk_norm', 'w_mo', 'norm_ffn', 'w_up', 'ffn_conv', 'ffn_conv_b', 'w_down']
WEIGHTS = IN_NAMES[2:]
SHARDED = {'w_in': 2, 'gdn_conv': 2, 'w_oa': 2, 'w_ob': 2, 'w_oc': 2, 'w_out': 1, 'w_mq': 1, 'w_mkv': 1,
           'w_mo': 2, 'w_up': 2, 'ffn_conv': 2, 'w_down': 1}
SMALL = [n for n in WEIGHTS if n not in SHARDED]


def _pick(n, cands):
    for c in cands:
        if n % c == 0:
            return c
    return n


_DOT_DIMS = {
    'nn': (((1,), (0,)), ((), ())),
    'nt': (((1,), (1,)), ((), ())),
    'tn': (((0,), (0,)), ((), ())),
}


def _mm(a, b, mode):
    if mode == 'nn':
        (m, c), (_, n) = a.shape, b.shape
    elif mode == 'nt':
        (m, c), (n, _) = a.shape, b.shape
    else:
        (c, m), (_, n) = a.shape, b.shape
    tm = _pick(m, (512, 256, 128)) if mode == 'tn' else _pick(m, (1024, 512, 256, 128))
    tn = _pick(n, (1664, 1408, 1024, 512, 256, 128))
    tc = _pick(c, (1024, 1408, 640, 512, 256, 128))
    if mode == 'tn':
        a_spec = pl.BlockSpec((tc, tm), lambda i, j, k: (k, i))
    else:
        a_spec = pl.BlockSpec((tm, tc), lambda i, j, k: (i, k))
    if mode == 'nt':
        b_spec = pl.BlockSpec((tn, tc), lambda i, j, k: (j, k))
    else:
        b_spec = pl.BlockSpec((tc, tn), lambda i, j, k: (k, j))
    dims = _DOT_DIMS[mode]

    def body(a_ref, b_ref, o_ref):
        @pl.when(pl.program_id(2) == 0)
        def _():
            o_ref[...] = jnp.zeros_like(o_ref)

        o_ref[...] += lax.dot_general(a_ref[...].astype(BF16), b_ref[...].astype(BF16), dims,
                                      preferred_element_type=F32)

    return pl.pallas_call(
        body,
        grid=(m // tm, n // tn, c // tc),
        in_specs=[a_spec, b_spec],
        out_specs=pl.BlockSpec((tm, tn), lambda i, j, k: (i, j)),
        out_shape=jax.ShapeDtypeStruct((m, n), F32),
        compiler_params=pltpu.CompilerParams(
            dimension_semantics=("parallel", "parallel", "arbitrary"), vmem_limit_bytes=VMEM_LIMIT),
        name=f"mm_{mode}_{m}x{c}x{n}",
    )(a, b)


@jax.custom_vjp
def matmul(a, w, w16):
    return _mm(a, w16, 'nn')


def _matmul_fwd(a, w, w16):
    return _mm(a, w16, 'nn'), (a, w16)


def _matmul_bwd(res, dy):
    a, w16 = res
    return _mm(dy, w16, 'nt'), _mm(a, dy, 'tn'), jnp.zeros_like(w16)


matmul.defvjp(_matmul_fwd, _matmul_bwd)


Q_TILE = 512
K_BLOCK = 1024
SUB = 256
_NT = (((1,), (1,)), ((), ()))
_TN = (((0,), (0,)), ((), ()))
_NN = (((1,), (0,)), ((), ()))


def _dot(a, b, dims):
    return lax.dot_general(a, b, dims, preferred_element_type=F32)


def _att_tiles(s):
    tq = min(Q_TILE, s)
    tk = min(K_BLOCK, s)
    assert s % tk == 0 and tk % tq == 0 and tk % min(SUB, tk) == 0
    return tq, tk


def _att_specs(s, t):
    tile = pl.BlockSpec((t, HEAD_DIM), lambda h, i: (i, h))
    whole = pl.BlockSpec((s, HEAD_DIM), lambda h, i: (0, h))
    col = pl.BlockSpec((None, t, 1), lambda h, i: (h, i, 0))
    row = pl.BlockSpec((None, 1, s), lambda h, i: (h, 0, 0))
    return tile, whole, col, row


def _key_block(jb, tk):
    return pl.ds(pl.multiple_of(jb * tk, tk), tk)


def _causal(i, jb, tq, tk, strict):
    r = i * tq + lax.broadcasted_iota(jnp.int32, (tq, tk), 0)
    c = jb * tk + lax.broadcasted_iota(jnp.int32, (tq, tk), 1)
    return c < r if strict else c <= r


def _fox_fwd_call(q, k, v, c_col, c_row):
    s, w = q.shape
    tq, tk = _att_tiles(s)
    scale = HEAD_DIM ** -0.5

    def body(q_ref, k_ref, v_ref, cc_ref, cr_ref, o_ref, lse_ref):
        i = pl.program_id(1)
        n_full = lax.div(i * tq, tk)
        qb = q_ref[...]
        cq = cc_ref[...]

        def block(jb, carry, diag):
            m, l, acc = carry
            sl = _key_block(jb, tk)
            sc = _dot(qb, k_ref[sl, :], _NT) * scale + (cq - cr_ref[:, sl])
            if diag:
                sc = jnp.where(_causal(i, jb, tq, tk, False), sc, -jnp.inf)
            m_new = jnp.maximum(m, jnp.max(sc, axis=-1, keepdims=True))
            p = jnp.exp(sc - m_new)
            alpha = jnp.exp(m - m_new)
            l = alpha * l + jnp.sum(p, axis=-1, keepdims=True)
            acc = alpha * acc + _dot(p.astype(BF16), v_ref[sl, :], _NN)
            return m_new, l, acc

        init = (jnp.full((tq, 1), -jnp.inf, F32), jnp.zeros((tq, 1), F32), jnp.zeros((tq, HEAD_DIM), F32))
        carry = lax.fori_loop(0, n_full, lambda jb, cr: block(jb, cr, False), init)
        m, l, acc = block(n_full, carry, True)
        o_ref[...] = acc / l
        lse_ref[...] = m + jnp.log(l)

    tile, whole, col, row = _att_specs(s, tq)
    return pl.pallas_call(
        body,
        grid=(w // HEAD_DIM, s // tq),
        in_specs=[tile, whole, whole, col, row],
        out_specs=[tile, col],
        out_shape=[jax.ShapeDtypeStruct((s, w), F32), jax.ShapeDtypeStruct((w // HEAD_DIM, s, 1), F32)],
        compiler_params=pltpu.CompilerParams(dimension_semantics=("parallel", "parallel"),
                                             vmem_limit_bytes=VMEM_LIMIT),
        name="fox_fwd",
    )(q, k, v, c_col, c_row)


def _fox_bwd_call(q, k, v, c_col, c_row, lse, do):
    s, w = q.shape
    tq, tk = _att_tiles(s)
    scale = HEAD_DIM ** -0.5

    def body(q_ref, k_ref, v_ref, cc_ref, cr_ref, lse_ref, do_ref, dq_ref, dk_ref, dv_ref, dcr_ref):
        i = pl.program_id(1)
        n_full = lax.div(i * tq, tk)

        @pl.when(i == 0)
        def _():
            dk_ref[...] = jnp.zeros_like(dk_ref)
            dv_ref[...] = jnp.zeros_like(dv_ref)
            dcr_ref[...] = jnp.zeros_like(dcr_ref)

        qb = q_ref[...]
        do16 = do_ref[...].astype(BF16)
        lse_q = lse_ref[...]
        cq = cc_ref[...]

        def probs(jb, diag):
            sl = _key_block(jb, tk)
            ks = k_ref[sl, :]
            sc = _dot(qb, ks, _NT) * scale + (cq - cr_ref[:, sl])
            p = jnp.exp(sc - lse_q)
            if diag:
                p = jnp.where(_causal(i, jb, tq, tk, False), p, 0.0)
            return sl, ks, p, _dot(do16, v_ref[sl, :], _NT)

        def row_dot(jb, acc, diag):
            _, _, p, dp = probs(jb, diag)
            return acc + jnp.sum(p * dp, axis=-1, keepdims=True)

        delta = lax.fori_loop(0, n_full, lambda jb, a: row_dot(jb, a, False), jnp.zeros((tq, 1), F32))
        delta = row_dot(n_full, delta, True)

        def block(jb, dq, diag):
            sl, ks, p, dp = probs(jb, diag)
            ds = p * (dp - delta)
            ds16 = ds.astype(BF16)
            dv_ref[sl, :] += _dot(p.astype(BF16), do16, _TN)
            dk_ref[sl, :] += _dot(ds16, qb, _TN) * scale
            dcr_ref[:, sl] += -jnp.sum(ds, axis=0, keepdims=True)
            return dq + _dot(ds16, ks, _NN) * scale

        dq = lax.fori_loop(0, n_full, lambda jb, a: block(jb, a, False), jnp.zeros((tq, HEAD_DIM), F32))
        dq_ref[...] = block(n_full, dq, True)

    tile, whole, col, row = _att_specs(s, tq)
    full = jax.ShapeDtypeStruct((s, w), F32)
    return pl.pallas_call(
        body,
        grid=(w // HEAD_DIM, s // tq),
        in_specs=[tile, whole, whole, col, row, col, tile],
        out_specs=[tile, whole, whole, row],
        out_shape=[full, full, full, jax.ShapeDtypeStruct((w // HEAD_DIM, 1, s), F32)],
        compiler_params=pltpu.CompilerParams(dimension_semantics=("parallel", "arbitrary"),
                                             vmem_limit_bytes=VMEM_LIMIT),
        name="fox_bwd",
    )(q, k, v, c_col, c_row, lse, do)


@jax.custom_vjp
def fox_core(q, k, v, c):
    return _fox_fwd(q, k, v, c)[0]


def _fox_fwd(q, k, v, c):
    q16, k16, v16 = q.astype(BF16), k.astype(BF16), v.astype(BF16)
    c_col, c_row = c.T[:, :, None], c.T[:, None, :]
    o, lse = _fox_fwd_call(q16, k16, v16, c_col, c_row)
    return o, (q16, k16, v16, c_col, c_row, lse)


def _fox_bwd(res, do):
    dq, dk, dv, dcr = _fox_bwd_call(*res, do)
    return dq, dk, dv, dcr[:, 0, :].T


fox_core.defvjp(_fox_fwd, _fox_bwd)


def _neg_softplus(z):
    e = jnp.exp(-jnp.abs(z))
    return -(jnp.maximum(z, 0.0) + jnp.log(1.0 + e)), e


def _split_dot(x, tri):
    hi = x.astype(BF16)
    lo = (x - hi.astype(F32)).astype(BF16)
    return _dot(hi, tri, _NN) + _dot(lo, tri, _NN)


def _tri(n, fn):
    r = lax.broadcasted_iota(jnp.int32, (n, n), 0)
    c = lax.broadcasted_iota(jnp.int32, (n, n), 1)
    return fn(r, c).astype(BF16)


def _sb_fwd_call(q, k, v):
    s, w = q.shape
    tq, tk = _att_tiles(s)
    sub = min(SUB, tk)
    n_sub = tk // sub
    scale = HEAD_DIM ** -0.5

    def body(q_ref, k_ref, v_ref, o_ref, tot_ref):
        i = pl.program_id(1)
        n_full = lax.div(i * tq, tk)
        qb = q_ref[...]
        tri = _tri(sub, lambda r, c: r >= c)

        def block(jb, carry, diag):
            later, acc = carry
            sl = _key_block(jb, tk)
            z = _dot(qb, k_ref[sl, :], _NT) * scale
            lk, _ = _neg_softplus(z)
            if diag:
                mask = _causal(i, jb, tq, tk, True)
                lk = jnp.where(mask, lk, 0.0)
            pieces = [None] * n_sub
            for u in reversed(range(n_sub)):
                part = lk[:, u * sub:(u + 1) * sub]
                pieces[u] = _split_dot(part, tri) + later
                later = later + jnp.sum(part, axis=-1, keepdims=True)
            a = jnp.exp(z + jnp.concatenate(pieces, axis=1))
            if diag:
                a = jnp.where(mask, a, 0.0)
            return later, acc + _dot(a.astype(BF16), v_ref[sl, :], _NN)

        carry = block(n_full, (jnp.zeros((tq, 1), F32), jnp.zeros((tq, HEAD_DIM), F32)), True)
        later, acc = lax.fori_loop(0, n_full, lambda jj, cr: block(n_full - 1 - jj, cr, False), carry)
        o_ref[...] = acc
        tot_ref[...] = later

    tile, whole, col, _ = _att_specs(s, tq)
    return pl.pallas_call(
        body,
        grid=(w // HEAD_DIM, s // tq),
        in_specs=[tile, whole, whole],
        out_specs=[tile, col],
        out_shape=[jax.ShapeDtypeStruct((s, w), F32), jax.ShapeDtypeStruct((w // HEAD_DIM, s, 1), F32)],
        compiler_params=pltpu.CompilerParams(dimension_semantics=("parallel", "parallel"),
                                             vmem_limit_bytes=VMEM_LIMIT),
        name="sb_fwd",
    )(q, k, v)


def _sb_bwd_call(q, k, v, tot, do):
    s, w = q.shape
    tq, tk = _att_tiles(s)
    sub = min(SUB, tk)
    n_sub = tk // sub
    scale = HEAD_DIM ** -0.5

    def body(q_ref, k_ref, v_ref, tot_ref, do_ref, dq_ref, dk_ref, dv_ref):
        i = pl.program_id(1)
        n_full = lax.div(i * tq, tk)

        @pl.when(i == 0)
        def _():
            dk_ref[...] = jnp.zeros_like(dk_ref)
            dv_ref[...] = jnp.zeros_like(dv_ref)

        qb = q_ref[...]
        do16 = do_ref[...].astype(BF16)
        tot_q = tot_ref[...]
        tri_before = _tri(sub, lambda r, c: r < c)
        tri_upto = _tri(sub, lambda r, c: r <= c)

        def block(jb, carry, diag):
            before, dl_before, dq = carry
            sl = _key_block(jb, tk)
            ks = k_ref[sl, :]
            z = _dot(qb, ks, _NT) * scale
            lk, e = _neg_softplus(z)
            if diag:
                mask = _causal(i, jb, tq, tk, True)
                lk = jnp.where(mask, lk, 0.0)
            sig = jnp.where(z >= 0, 1.0, e) / (1.0 + e)
            pieces = []
            for u in range(n_sub):
                part = lk[:, u * sub:(u + 1) * sub]
                pieces.append(_split_dot(part, tri_before) + before)
                before = before + jnp.sum(part, axis=-1, keepdims=True)
            a = jnp.exp(z + (tot_q - jnp.concatenate(pieces, axis=1)))
            if diag:
                a = jnp.where(mask, a, 0.0)
            dl = a * _dot(do16, v_ref[sl, :], _NT)
            dl16 = dl.astype(BF16)
            pieces = []
            for u in range(n_sub):
                pieces.append(_dot(dl16[:, u * sub:(u + 1) * sub], tri_upto, _NN) + dl_before)
                dl_before = dl_before + jnp.sum(dl[:, u * sub:(u + 1) * sub], axis=-1, keepdims=True)
            dz = dl - sig * jnp.concatenate(pieces, axis=1)
            if diag:
                dz = jnp.where(mask, dz, 0.0)
            dz16 = dz.astype(BF16)
            dv_ref[sl, :] += _dot(a.astype(BF16), do16, _TN)
            dk_ref[sl, :] += _dot(dz16, qb, _TN) * scale
            return before, dl_before, dq + _dot(dz16, ks, _NN) * scale

        init = (jnp.zeros((tq, 1), F32), jnp.zeros((tq, 1), F32), jnp.zeros((tq, HEAD_DIM), F32))
        carry = lax.fori_loop(0, n_full, lambda jb, cr: block(jb, cr, False), init)
        dq_ref[...] = block(n_full, carry, True)[2]

    tile, whole, col, _ = _att_specs(s, tq)
    full = jax.ShapeDtypeStruct((s, w), F32)
    return pl.pallas_call(
        body,
        grid=(w // HEAD_DIM, s // tq),
        in_specs=[tile, whole, whole, col, tile],
        out_specs=[tile, whole, whole],
        out_shape=[full, full, full],
        compiler_params=pltpu.CompilerParams(dimension_semantics=("parallel", "arbitrary"),
                                             vmem_limit_bytes=VMEM_LIMIT),
        name="sb_bwd",
    )(q, k, v, tot, do)


@jax.custom_vjp
def sb_core(q, k, v):
    return _sb_fwd(q, k, v)[0]


def _sb_fwd(q, k, v):
    q16, k16, v16 = q.astype(BF16), k.astype(BF16), v.astype(BF16)
    o, tot = _sb_fwd_call(q16, k16, v16)
    return o, (q16, k16, v16, tot)


def _sb_bwd(res, do):
    return tuple(_sb_bwd_call(*res, do))


sb_core.defvjp(_sb_fwd, _sb_bwd)


def _gdn_specs(h, c, d):
    vec = pl.BlockSpec((None, h, c, d), lambda n: (n, 0, 0, 0))
    sq = pl.BlockSpec((None, h, c, c), lambda n: (n, 0, 0, 0))
    dec = pl.BlockSpec((None, h, 1, d), lambda n: (n, 0, 0, 0))
    st = pl.BlockSpec((None, h, d, d), lambda n: (n, 0, 0, 0))
    return vec, sq, dec, st


def _b16(x):
    return x.astype(BF16)


def _gdn_scan_fwd_call(qg, u, w, attn, kt, egl):
    n, h, c, d = qg.shape

    def body(qg_ref, u_ref, w_ref, attn_ref, kt_ref, egl_ref, o_ref, st_ref, state):
        @pl.when(pl.program_id(0) == 0)
        def _():
            state[...] = jnp.zeros_like(state)

        for hh in range(h):
            s0 = state[hh]
            st_ref[hh] = s0
            s16 = _b16(s0)
            vn = u_ref[hh] - _dot(_b16(w_ref[hh]), s16, _NN)
            vn16 = _b16(vn)
            o_ref[hh] = _dot(_b16(qg_ref[hh]), s16, _NN) + _dot(_b16(attn_ref[hh]), vn16, _NN)
            state[hh] = s0 * egl_ref[hh] + _dot(_b16(kt_ref[hh]), vn16, _TN)

    vec, sq, dec, st = _gdn_specs(h, c, d)
    return pl.pallas_call(
        body,
        grid=(n,),
        in_specs=[vec, vec, vec, sq, vec, dec],
        out_specs=[vec, st],
        out_shape=[jax.ShapeDtypeStruct((n, h, c, d), F32), jax.ShapeDtypeStruct((n, h, d, d), F32)],
        scratch_shapes=[pltpu.VMEM((h, d, d), F32)],
        compiler_params=pltpu.CompilerParams(dimension_semantics=("arbitrary",)),
        name="gdn_scan_fwd",
    )(qg, u, w, attn, kt, egl)


def _gdn_scan_bwd_call(qg, u, w, attn, kt, egl, states, do):
    n, h, c, d = qg.shape

    def body(qg_ref, u_ref, w_ref, attn_ref, kt_ref, egl_ref, st_ref, do_ref,
             dqg_ref, du_ref, dw_ref, dattn_ref, dkt_ref, degl_ref, dstate):
        @pl.when(pl.program_id(0) == 0)
        def _():
            dstate[...] = jnp.zeros_like(dstate)

        for hh in range(h):
            s0 = st_ref[hh]
            s16 = _b16(s0)
            big_d = dstate[hh]
            d16 = _b16(big_d)
            w16, kt16, qg16, attn16 = _b16(w_ref[hh]), _b16(kt_ref[hh]), _b16(qg_ref[hh]), _b16(attn_ref[hh])
            do16 = _b16(do_ref[hh])
            vn16 = _b16(u_ref[hh] - _dot(w16, s16, _NN))
            dvn = _dot(attn16, do16, _TN) + _dot(kt16, d16, _NN)
            dvn16 = _b16(dvn)
            du_ref[hh] = dvn
            dattn_ref[hh] = _dot(do16, vn16, _NT)
            dqg_ref[hh] = _dot(do16, s16, _NT)
            dkt_ref[hh] = _dot(vn16, d16, _NT)
            dw_ref[hh] = -_dot(dvn16, s16, _NT)
            degl_ref[hh] = jnp.sum(big_d * s0, axis=0, keepdims=True)
            dstate[hh] = big_d * egl_ref[hh] + _dot(qg16, do16, _TN) - _dot(w16, dvn16, _TN)

    vec, sq, dec, st = _gdn_specs(h, c, d)
    rev = lambda spec: pl.BlockSpec(spec.block_shape, lambda i: (n - 1 - i, 0, 0, 0))
    vec, sq, dec, st = rev(vec), rev(sq), rev(dec), rev(st)
    vshape = jax.ShapeDtypeStruct((n, h, c, d), F32)
    return pl.pallas_call(
        body,
        grid=(n,),
        in_specs=[vec, vec, vec, sq, vec, dec, st, vec],
        out_specs=[vec, vec, vec, sq, vec, dec],
        out_shape=[vshape, vshape, vshape, jax.ShapeDtypeStruct((n, h, c, c), F32), vshape,
                   jax.ShapeDtypeStruct((n, h, 1, d), F32)],
        scratch_shapes=[pltpu.VMEM((h, d, d), F32)],
        compiler_params=pltpu.CompilerParams(dimension_semantics=("arbitrary",)),
        name="gdn_scan_bwd",
    )(qg, u, w, attn, kt, egl, states, do)


@jax.custom_vjp
def gdn_scan(qg, u, w, attn, kt, egl):
    return _gdn_scan_fwd_call(qg, u, w, attn, kt, egl)[0]


def _gdn_scan_fwd(qg, u, w, attn, kt, egl):
    o, states = _gdn_scan_fwd_call(qg, u, w, attn, kt, egl)
    return o, (qg, u, w, attn, kt, egl, states)


def _gdn_scan_bwd(res, do):
    return tuple(_gdn_scan_bwd_call(*res, do))


gdn_scan.defvjp(_gdn_scan_fwd, _gdn_scan_bwd)


CONV_ROWS = 512
HALO = 8


def _sigmoid(x):
    return 1.0 / (1.0 + jnp.exp(-x))


def _shifted(ext, k, rows):
    if k == 0:
        return ext[HALO:HALO + rows]
    return pltpu.roll(ext, k % ext.shape[0], 0)[HALO:HALO + rows]


def _conv_specs(s, ch, tr, tc, off):
    per, last = tr // HALO, s // HALO - 1
    blk = pl.BlockSpec((tr, tc), lambda j, i: (i, j + off))
    prev = pl.BlockSpec((HALO, tc), lambda j, i: (jnp.maximum(i * per - 1, 0), j + off))
    nxt = pl.BlockSpec((HALO, tc), lambda j, i: (jnp.minimum((i + 1) * per, last), j + off))
    return blk, prev, nxt


def _dwconv_fwd_call(x, w, b, gated):
    s, ch = x.shape
    taps = w.shape[0]
    out_ch = ch // 2 if gated else ch
    tr = _pick(s, (CONV_ROWS, 256, 128, 64, 32, 16, 8))
    tc = _pick(out_ch, (1408, 512, 384, 256, 128))
    n_j = out_ch // tc
    parts = (0, n_j) if gated else (0,)

    def conv(x_ref, p_ref, w_ref, first):
        xb = x_ref[...]
        ext = jnp.concatenate([jnp.where(first, 0.0, p_ref[...]), xb], axis=0)
        y = w_ref[taps - 1:taps, :] * xb
        for k in range(1, taps):
            y = y + w_ref[taps - 1 - k:taps - k, :] * _shifted(ext, k, tr)
        return y

    def body(*refs):
        first = pl.program_id(1) == 0
        if gated:
            xa, pa, wa, ba, xb_, pb, wb, bb, o_ref = refs
            a = conv(xa, pa, wa, first) + ba[...]
            g = conv(xb_, pb, wb, first) + bb[...]
            o_ref[...] = a * _sigmoid(a) * g
        else:
            xa, pa, wa, o_ref = refs
            a = conv(xa, pa, wa, first)
            o_ref[...] = a * _sigmoid(a)

    in_specs, args = [], []
    for off in parts:
        blk, prev, _ = _conv_specs(s, ch, tr, tc, off)
        in_specs += [blk, prev, pl.BlockSpec((taps, tc), lambda j, i, off=off: (0, j + off))]
        args += [x, x, w]
        if gated:
            in_specs.append(pl.BlockSpec((1, tc), lambda j, i, off=off: (0, j + off)))
            args.append(b)
    return pl.pallas_call(
        body,
        grid=(n_j, s // tr),
        in_specs=in_specs,
        out_specs=pl.BlockSpec((tr, tc), lambda j, i: (i, j)),
        out_shape=jax.ShapeDtypeStruct((s, out_ch), F32),
        compiler_params=pltpu.CompilerParams(dimension_semantics=("parallel", "parallel"),
                                             vmem_limit_bytes=VMEM_LIMIT),
        name="dwconv_gate_fwd" if gated else "dwconv_silu_fwd",
    )(*args)


def _dwconv_bwd_call(x, w, b, do, gated):
    s, ch = x.shape
    taps = w.shape[0]
    out_ch = ch // 2 if gated else ch
    tr = _pick(s, (CONV_ROWS, 256, 128, 64, 32, 16, 8))
    tc = _pick(out_ch, (1408, 512, 384, 256, 128))
    n_j, n_i = out_ch // tc, s // tr
    parts = (0, n_j) if gated else (0,)
    ext_rows = tr + 2 * HALO

    def pre_act(x_ref, p_ref, n_ref, w_ref, first, last):
        ext = jnp.concatenate([jnp.where(first, 0.0, p_ref[...]), x_ref[...], n_ref[...]], axis=0)
        y = w_ref[taps - 1:taps, :] * ext
        for k in range(1, taps):
            y = y + w_ref[taps - 1 - k:taps - k, :] * pltpu.roll(ext, k, 0)
        return ext, y

    def grads(ext, dy, w_ref, dx_ref, dw_ref):
        dx = w_ref[taps - 1:taps, :] * dy[HALO:HALO + tr]
        for k in range(1, taps):
            dx = dx + w_ref[taps - 1 - k:taps - k, :] * _shifted(dy, -k, tr)
        dx_ref[...] = dx
        dyb = dy[HALO:HALO + tr]
        for k in range(taps):
            dw_ref[taps - 1 - k:taps - k, :] += jnp.sum(dyb * _shifted(ext, k, tr), axis=0, keepdims=True)
        return dyb

    def body(*refs):
        i = pl.program_id(1)
        first, last = i == 0, i == n_i - 1
        rows = lax.broadcasted_iota(jnp.int32, (ext_rows, 1), 0)
        inside = jnp.logical_and(rows >= HALO, jnp.logical_or(rows < HALO + tr, jnp.logical_not(last)))
        if gated:
            (xa, pa, na, wa, ba, xb_, pb, nb, wb, bb, do_ref, don_ref,
             dxa_ref, dxb_ref, dwa_ref, dwb_ref, dba_ref, dbb_ref) = refs
        else:
            xa, pa, na, wa, do_ref, don_ref, dxa_ref, dwa_ref = refs

        @pl.when(first)
        def _():
            dwa_ref[...] = jnp.zeros_like(dwa_ref)
            if gated:
                dwb_ref[...] = jnp.zeros_like(dwb_ref)
                dba_ref[...] = jnp.zeros_like(dba_ref)
                dbb_ref[...] = jnp.zeros_like(dbb_ref)

        d_out = jnp.concatenate([jnp.zeros((HALO, tc), F32), do_ref[...], don_ref[...]], axis=0)
        d_out = jnp.where(inside, d_out, 0.0)
        ext_a, a = pre_act(xa, pa, na, wa, first, last)
        if gated:
            a = a + ba[...]
            ext_b, g = pre_act(xb_, pb, nb, wb, first, last)
            g = g + bb[...]
            sg = _sigmoid(a)
            silu = a * sg
            dya = jnp.where(inside, d_out * g * (sg + silu * (1.0 - sg)), 0.0)
            dyg = jnp.where(inside, d_out * silu, 0.0)
            dba_ref[...] += jnp.sum(grads(ext_a, dya, wa, dxa_ref, dwa_ref), axis=0, keepdims=True)
            dbb_ref[...] += jnp.sum(grads(ext_b, dyg, wb, dxb_ref, dwb_ref), axis=0, keepdims=True)
        else:
            sg = _sigmoid(a)
            dya = jnp.where(inside, d_out * (sg + a * sg * (1.0 - sg)), 0.0)
            grads(ext_a, dya, wa, dxa_ref, dwa_ref)

    in_specs, args = [], []
    for off in parts:
        blk, prev, nxt = _conv_specs(s, ch, tr, tc, off)
        in_specs += [blk, prev, nxt, pl.BlockSpec((taps, tc), lambda j, i, off=off: (0, j + off))]
        args += [x, x, x, w]
        if gated:
            in_specs.append(pl.BlockSpec((1, tc), lambda j, i, off=off: (0, j + off)))
            args.append(b)
    blk, _, nxt = _conv_specs(s, out_ch, tr, tc, 0)
    in_specs += [blk, nxt]
    args += [do, do]
    n_half = len(parts)
    out_specs = ([blk] * n_half + [pl.BlockSpec((taps, tc), lambda j, i: (0, j))] * n_half
                 + ([pl.BlockSpec((1, tc), lambda j, i: (0, j))] * n_half if gated else []))
    out_shape = ([jax.ShapeDtypeStruct((s, out_ch), F32)] * n_half + [jax.ShapeDtypeStruct((taps, out_ch), F32)] * n_half
                 + ([jax.ShapeDtypeStruct((1, out_ch), F32)] * n_half if gated else []))
    return pl.pallas_call(
        body,
        grid=(n_j, n_i),
        in_specs=in_specs,
        out_specs=out_specs,
        out_shape=out_shape,
        compiler_params=pltpu.CompilerParams(dimension_semantics=("parallel", "arbitrary"),
                                             vmem_limit_bytes=VMEM_LIMIT),
        name="dwconv_gate_bwd" if gated else "dwconv_silu_bwd",
    )(*args)


@jax.custom_vjp
def conv_gate(u, w, b):
    return _dwconv_fwd_call(u, w, b[None], True)


def _conv_gate_fwd(u, w, b):
    return _dwconv_fwd_call(u, w, b[None], True), (u, w, b)


def _conv_gate_bwd(res, do):
    u, w, b = res
    dxa, dxb, dwa, dwb, dba, dbb = _dwconv_bwd_call(u, w, b[None], do, True)
    return (jnp.concatenate([dxa, dxb], axis=1), jnp.concatenate([dwa, dwb], axis=1),
            jnp.concatenate([dba, dbb], axis=1)[0])


conv_gate.defvjp(_conv_gate_fwd, _conv_gate_bwd)


@jax.custom_vjp
def conv_silu(x, w):
    return _dwconv_fwd_call(x, w, None, False)


def _conv_silu_fwd(x, w):
    return _dwconv_fwd_call(x, w, None, False), (x, w)


def _conv_silu_bwd(res, do):
    x, w = res
    dx, dw = _dwconv_bwd_call(x, w, None, do, False)
    return dx, dw


conv_silu.defvjp(_conv_silu_fwd, _conv_silu_bwd)


ROW_TILE = 256


def _rmsnorm_fwd_call(x, g):
    s, d = x.shape
    tr = _pick(s, (ROW_TILE, 128, 64, 32, 16, 8))

    def body(x_ref, g_ref, o_ref):
        xb = x_ref[...]
        r = lax.rsqrt(jnp.mean(xb * xb, axis=-1, keepdims=True) + EPS)
        o_ref[...] = (xb * r) * g_ref[...]

    return pl.pallas_call(
        body,
        grid=(s // tr,),
        in_specs=[pl.BlockSpec((tr, d), lambda i: (i, 0)), pl.BlockSpec((1, d), lambda i: (0, 0))],
        out_specs=pl.BlockSpec((tr, d), lambda i: (i, 0)),
        out_shape=jax.ShapeDtypeStruct((s, d), F32),
        compiler_params=pltpu.CompilerParams(dimension_semantics=("parallel",)),
        name="rmsnorm_fwd",
    )(x, g)


def _rmsnorm_bwd_call(x, g, dy):
    s, d = x.shape
    tr = _pick(s, (ROW_TILE, 128, 64, 32, 16, 8))

    def body(x_ref, g_ref, dy_ref, dx_ref, dg_ref):
        @pl.when(pl.program_id(0) == 0)
        def _():
            dg_ref[...] = jnp.zeros_like(dg_ref)

        xb = x_ref[...]
        r = lax.rsqrt(jnp.mean(xb * xb, axis=-1, keepdims=True) + EPS)
        y = xb * r
        dyb = dy_ref[...]
        dg_ref[...] += jnp.sum(dyb * y, axis=0, keepdims=True)
        dn = dyb * g_ref[...]
        dx_ref[...] = r * (dn - y * jnp.mean(dn * y, axis=-1, keepdims=True))

    row = pl.BlockSpec((tr, d), lambda i: (i, 0))
    vec = pl.BlockSpec((1, d), lambda i: (0, 0))
    return pl.pallas_call(
        body,
        grid=(s // tr,),
        in_specs=[row, vec, row],
        out_specs=[row, vec],
        out_shape=[jax.ShapeDtypeStruct((s, d), F32), jax.ShapeDtypeStruct((1, d), F32)],
        compiler_params=pltpu.CompilerParams(dimension_semantics=("arbitrary",)),
        name="rmsnorm_bwd",
    )(x, g, dy)


@jax.custom_vjp
def rmsnorm_rows(x, g):
    return _rmsnorm_fwd_call(x, g[None])


def _rmsnorm_rows_fwd(x, g):
    return _rmsnorm_fwd_call(x, g[None]), (x, g)


def _rmsnorm_rows_bwd(res, dy):
    x, g = res
    dx, dg = _rmsnorm_bwd_call(x, g[None], dy)
    return dx, dg[0]


rmsnorm_rows.defvjp(_rmsnorm_rows_fwd, _rmsnorm_rows_bwd)


def _merge_specs(s, d, tr):
    gate = pl.BlockSpec((tr, N_BRANCH * d), lambda i: (i, 0))
    bias = pl.BlockSpec((1, N_BRANCH * d), lambda i: (0, 0))
    row = pl.BlockSpec((tr, d), lambda i: (i, 0))
    return gate, bias, row


def _merge_fwd_call(gates, bias, ta, tb, tc):
    s, d = ta.shape
    tr = _pick(s, (ROW_TILE, 128, 64, 32, 16, 8))

    def body(g_ref, b_ref, ta_ref, tb_ref, tc_ref, o_ref):
        acc = None
        for b, t_ref in enumerate((ta_ref, tb_ref, tc_ref)):
            gate = _sigmoid(g_ref[:, b * d:(b + 1) * d] + b_ref[:, b * d:(b + 1) * d])
            acc = gate * t_ref[...] if acc is None else acc + gate * t_ref[...]
        o_ref[...] = acc

    gate, bias_s, row = _merge_specs(s, d, tr)
    return pl.pallas_call(
        body,
        grid=(s // tr,),
        in_specs=[gate, bias_s, row, row, row],
        out_specs=row,
        out_shape=jax.ShapeDtypeStruct((s, d), F32),
        compiler_params=pltpu.CompilerParams(dimension_semantics=("parallel",), vmem_limit_bytes=VMEM_LIMIT),
        name="merge_fwd",
    )(gates, bias, ta, tb, tc)


def _merge_bwd_call(gates, bias, ta, tb, tc, dm):
    s, d = ta.shape
    tr = _pick(s, (ROW_TILE, 128, 64, 32, 16, 8))

    def body(g_ref, b_ref, ta_ref, tb_ref, tc_ref, dm_ref, dg_ref, db_ref, dta_ref, dtb_ref, dtc_ref):
        @pl.when(pl.program_id(0) == 0)
        def _():
            db_ref[...] = jnp.zeros_like(db_ref)

        dmb = dm_ref[...]
        for b, (t_ref, dt_ref) in enumerate(((ta_ref, dta_ref), (tb_ref, dtb_ref), (tc_ref, dtc_ref))):
            cols = slice(b * d, (b + 1) * d)
            gate = _sigmoid(g_ref[:, cols] + b_ref[:, cols])
            dt_ref[...] = gate * dmb
            dpre = dmb * t_ref[...] * (gate * (1.0 - gate))
            dg_ref[:, cols] = dpre
            db_ref[:, cols] += jnp.sum(dpre, axis=0, keepdims=True)

    gate, bias_s, row = _merge_specs(s, d, tr)
    rows = jax.ShapeDtypeStruct((s, d), F32)
    return pl.pallas_call(
        body,
        grid=(s // tr,),
        in_specs=[gate, bias_s, row, row, row, row],
        out_specs=[gate, bias_s, row, row, row],
        out_shape=[jax.ShapeDtypeStruct((s, N_BRANCH * d), F32), jax.ShapeDtypeStruct((1, N_BRANCH * d), F32), rows, rows, rows],
        compiler_params=pltpu.CompilerParams(dimension_semantics=("arbitrary",), vmem_limit_bytes=VMEM_LIMIT),
        name="merge_bwd",
    )(gates, bias, ta, tb, tc, dm)


@jax.custom_vjp
def gated_merge(gates, bias, ta, tb, tc):
    return _merge_fwd_call(gates, bias[None], ta, tb, tc)


def _gated_merge_fwd(gates, bias, ta, tb, tc):
    return _merge_fwd_call(gates, bias[None], ta, tb, tc), (gates, bias, ta, tb, tc)


def _gated_merge_bwd(res, dm):
    gates, bias, ta, tb, tc = res
    dg, db, dta, dtb, dtc = _merge_bwd_call(gates, bias[None], ta, tb, tc, dm)
    return dg, db[0], dta, dtb, dtc


gated_merge.defvjp(_gated_merge_fwd, _gated_merge_bwd)


HEAD_ROWS = 512


def _head_stats(xs, kind):
    sq = xs * xs
    ms = jnp.sum(sq, axis=-1, keepdims=True) if kind == 'l2' else jnp.mean(sq, axis=-1, keepdims=True)
    return lax.rsqrt(ms + EPS)


def _headnorm_fwd_call(x, g, z, kind):
    s, w = x.shape
    tr = _pick(s, (HEAD_ROWS, 256, 128, 64, 32, 16, 8))

    def body(*refs):
        x_ref, o_ref = refs[0], refs[-1]
        for h in range(w // HEAD_DIM):
            cols = slice(h * HEAD_DIM, (h + 1) * HEAD_DIM)
            xs = x_ref[:, cols]
            y = xs * _head_stats(xs, kind)
            if kind != 'l2':
                y = y * refs[1][...]
            if kind == 'rms_gate':
                zs = refs[2][:, cols]
                y = y * (zs * _sigmoid(zs))
            o_ref[:, cols] = y

    row = pl.BlockSpec((tr, w), lambda i: (i, 0))
    vec = pl.BlockSpec((1, HEAD_DIM), lambda i: (0, 0))
    in_specs, args = [row], [x]
    if kind != 'l2':
        in_specs.append(vec); args.append(g)
    if kind == 'rms_gate':
        in_specs.append(row); args.append(z)
    return pl.pallas_call(
        body,
        grid=(s // tr,),
        in_specs=in_specs,
        out_specs=row,
        out_shape=jax.ShapeDtypeStruct((s, w), F32),
        compiler_params=pltpu.CompilerParams(dimension_semantics=("parallel",)),
        name=f"headnorm_{kind}_fwd",
    )(*args)


def _headnorm_bwd_call(x, g, z, dy, kind):
    s, w = x.shape
    tr = _pick(s, (HEAD_ROWS, 256, 128, 64, 32, 16, 8))
    gain, gated = kind != 'l2', kind == 'rms_gate'

    def body(*refs):
        n_in = 2 + gain + gated
        x_ref, dy_ref = refs[0], refs[n_in - 1]
        outs = refs[n_in:]
        dx_ref = outs[0]
        if gain:
            g_ref, dg_ref = refs[1], outs[1]

            @pl.when(pl.program_id(0) == 0)
            def _():
                dg_ref[...] = jnp.zeros_like(dg_ref)

        for h in range(w // HEAD_DIM):
            cols = slice(h * HEAD_DIM, (h + 1) * HEAD_DIM)
            xs = x_ref[:, cols]
            r = _head_stats(xs, kind)
            y = xs * r
            dn = dy_ref[:, cols]
            if gated:
                zs = refs[2][:, cols]
                sg = _sigmoid(zs)
                silu = zs * sg
                outs[2][:, cols] = dn * (y * g_ref[...]) * (sg + silu * (1.0 - sg))
                dn = dn * silu
            if gain:
                dg_ref[...] += jnp.sum(dn * y, axis=0, keepdims=True)
                dn = dn * g_ref[...]
            proj = jnp.sum(dn * y, axis=-1, keepdims=True)
            if kind != 'l2':
                proj = proj / HEAD_DIM
            dx_ref[:, cols] = r * (dn - y * proj)

    row = pl.BlockSpec((tr, w), lambda i: (i, 0))
    vec = pl.BlockSpec((1, HEAD_DIM), lambda i: (0, 0))
    rows, vecs = jax.ShapeDtypeStruct((s, w), F32), jax.ShapeDtypeStruct((1, HEAD_DIM), F32)
    in_specs, args = [row], [x]
    if gain:
        in_specs.append(vec); args.append(g)
    if gated:
        in_specs.append(row); args.append(z)
    in_specs.append(row); args.append(dy)
    out_specs, out_shape = [row], [rows]
    if gain:
        out_specs.append(vec); out_shape.append(vecs)
    if gated:
        out_specs.append(row); out_shape.append(rows)
    return pl.pallas_call(
        body,
        grid=(s // tr,),
        in_specs=in_specs,
        out_specs=out_specs,
        out_shape=out_shape,
        compiler_params=pltpu.CompilerParams(dimension_semantics=("arbitrary",) if gain else ("parallel",)),
        name=f"headnorm_{kind}_bwd",
    )(*args)


@jax.custom_vjp
def head_rms(x, g):
    return _headnorm_fwd_call(x, g[None], None, 'rms')


def _head_rms_fwd(x, g):
    return _headnorm_fwd_call(x, g[None], None, 'rms'), (x, g)


def _head_rms_bwd(res, dy):
    x, g = res
    dx, dg = _headnorm_bwd_call(x, g[None], None, dy, 'rms')
    return dx, dg[0]


head_rms.defvjp(_head_rms_fwd, _head_rms_bwd)


@jax.custom_vjp
def head_l2(x):
    return _headnorm_fwd_call(x, None, None, 'l2')


def _head_l2_fwd(x):
    return _headnorm_fwd_call(x, None, None, 'l2'), (x,)


def _head_l2_bwd(res, dy):
    return tuple(_headnorm_bwd_call(res[0], None, None, dy, 'l2'))


head_l2.defvjp(_head_l2_fwd, _head_l2_bwd)


@jax.custom_vjp
def head_rms_gate(x, g, z):
    return _headnorm_fwd_call(x, g[None], z, 'rms_gate')


def _head_rms_gate_fwd(x, g, z):
    return _headnorm_fwd_call(x, g[None], z, 'rms_gate'), (x, g, z)


def _head_rms_gate_bwd(res, dy):
    x, g, z = res
    dx, dg, dz = _headnorm_bwd_call(x, g[None], z, dy, 'rms_gate')
    return dx, dg[0], dz


head_rms_gate.defvjp(_head_rms_gate_fwd, _head_rms_gate_bwd)


def _flat_rows(rows):
    return _pick(rows, (FLAT_TILE_ROWS, 1024, 512, 256, 128, 64, 32, 16, 8))


def _add_own_half(g, a, core):
    _, nb, rows, _ = g.shape
    tr = _flat_rows(rows)

    def body(c_ref, g_ref, a_ref, o_ref):
        o_ref[...] = g_ref[...] + a_ref[...]

    return pl.pallas_call(
        body,
        grid_spec=pltpu.PrefetchScalarGridSpec(
            num_scalar_prefetch=1,
            grid=(nb, rows // tr),
            in_specs=[pl.BlockSpec((None, None, tr, LANES), lambda j, r, c_ref: (c_ref[0], j, r, 0)),
                      pl.BlockSpec((None, tr, LANES), lambda j, r, c_ref: (j, r, 0))],
            out_specs=pl.BlockSpec((None, tr, LANES), lambda j, r, c_ref: (j, r, 0)),
        ),
        out_shape=jax.ShapeDtypeStruct((nb, rows, LANES), F32),
        compiler_params=pltpu.CompilerParams(dimension_semantics=("parallel", "parallel")),
        name="rs_add_own_half",
    )(core, g, a)


def _sum_chips(b, core):
    nb, rows, _ = b.shape
    tr = _flat_rows(rows)

    def body(c_ref, b_ref, o_ref):
        acc = b_ref[0] + b_ref[1]
        for j in range(2, nb):
            acc = acc + b_ref[j]
        o_ref[...] = acc

    return pl.pallas_call(
        body,
        grid_spec=pltpu.PrefetchScalarGridSpec(
            num_scalar_prefetch=1,
            grid=(rows // tr,),
            in_specs=[pl.BlockSpec((nb, tr, LANES), lambda r, c_ref: (0, r, 0))],
            out_specs=pl.BlockSpec((None, tr, LANES), lambda r, c_ref: (c_ref[0], r, 0)),
        ),
        out_shape=jax.ShapeDtypeStruct((2, rows, LANES), F32),
        compiler_params=pltpu.CompilerParams(dimension_semantics=("parallel",)),
        name="rs_sum_chips",
    )(core, b)


ADAM_BLOCK_BYTES = 2 * 1024 * 1024


def _adamw(w, g, m, v, tag):
    rows, cols = w.shape
    tr = rows
    for cand in range(8, rows, 8):
        if rows % cand == 0 and cand * cols * 4 <= ADAM_BLOCK_BYTES:
            tr = cand

    def body(w_ref, g_ref, m_ref, v_ref, d_ref, nm_ref, nv_ref):
        gg = g_ref[...]
        nm = ADAM_B1 * m_ref[...] + (1.0 - ADAM_B1) * gg
        nv = ADAM_B2 * v_ref[...] + (1.0 - ADAM_B2) * jnp.square(gg)
        m_hat = nm / (1.0 - ADAM_B1 ** ADAM_STEP)
        v_hat = nv / (1.0 - ADAM_B2 ** ADAM_STEP)
        d_ref[...] = -ADAM_LR * (m_hat / (jnp.sqrt(v_hat) + ADAM_EPS) + ADAM_WD * w_ref[...])
        nm_ref[...] = nm
        nv_ref[...] = nv

    spec = pl.BlockSpec((tr, cols), lambda r: (r, 0))
    shape = jax.ShapeDtypeStruct((rows, cols), F32)
    return pl.pallas_call(
        body,
        grid=(rows // tr,),
        in_specs=[spec] * 4,
        out_specs=[spec] * 3,
        out_shape=[shape] * 3,
        compiler_params=pltpu.CompilerParams(dimension_semantics=("parallel",), vmem_limit_bytes=VMEM_LIMIT),
        name=f"adamw_{tag}",
    )(w, g, m, v)


def _loss_head(y, t):
    s, d = y.shape
    tr = _pick(s, (512, 256, 128, 64, 32, 16, 8))

    def body(y_ref, t_ref, dy_ref, l_ref):
        @pl.when(pl.program_id(0) == 0)
        def _():
            l_ref[...] = jnp.zeros_like(l_ref)

        diff = y_ref[...] - t_ref[...]
        dy_ref[...] = diff / d
        row = jnp.mean(jnp.square(diff), axis=-1, keepdims=True)
        l_ref[...] += 0.5 * jnp.sum(row, axis=0, keepdims=True)

    return pl.pallas_call(
        body,
        grid=(s // tr,),
        in_specs=[pl.BlockSpec((tr, d), lambda r: (r, 0))] * 2,
        out_specs=[pl.BlockSpec((tr, d), lambda r: (r, 0)), pl.BlockSpec((1, LANES), lambda r: (0, 0))],
        out_shape=[jax.ShapeDtypeStruct((s, d), F32), jax.ShapeDtypeStruct((1, LANES), F32)],
        compiler_params=pltpu.CompilerParams(dimension_semantics=("arbitrary",)),
        name="loss_head",
    )(y, t)


def _place():
    x, y, c = lax.axis_index("x"), lax.axis_index("y"), lax.axis_index("c")
    chips = [(1 - x, y), (x, 1 - y), (1 - x, 1 - y)]
    return x, y, c, chips


def _all_gather_chips(flat):
    _, rows, _ = flat.shape

    def body(x_ref, o_ref, send_sems, recv_sems):
        x, y, c, chips = _place()
        me = 2 * x + y
        sib = (x, y, 1 - c)

        def remote(k, src, dst, to):
            return pltpu.make_async_remote_copy(src_ref=src, dst_ref=dst, send_sem=send_sems.at[k],
                                                recv_sem=recv_sems.at[k], device_id=to, device_id_type=MESH)

        first = [remote(k, x_ref.at[c], o_ref.at[me, c], (px, py, c)) for k, (px, py) in enumerate(chips)]
        for cp in first:
            cp.start()
        passed = []
        for k, (px, py) in enumerate(chips):
            blk = o_ref.at[2 * px + py, c]
            remote(k, x_ref.at[c], blk, (px, py, c)).wait_recv()
            cp = remote(3 + k, blk, blk, sib)
            cp.start()
            passed.append(cp)
        for k, (px, py) in enumerate(chips):
            blk = o_ref.at[2 * px + py, 1 - c]
            remote(3 + k, blk, blk, sib).wait_recv()
        for cp in first + passed:
            cp.wait_send()

    return pl.pallas_call(
        body,
        in_specs=[ANY],
        out_specs=ANY,
        out_shape=jax.ShapeDtypeStruct((N_CHIPS, 2, rows, LANES), flat.dtype),
        scratch_shapes=[pltpu.SemaphoreType.DMA((6,)), pltpu.SemaphoreType.DMA((6,))],
        name=f"all_gather_chips_{jnp.dtype(flat.dtype).name}",
    )(flat)


def _rs_sibling_exchange(g):
    _, nb, rows, _ = g.shape

    def body(g_ref, a_ref, send_sem, recv_sem):
        x, y, c, _ = _place()
        cp = pltpu.make_async_remote_copy(src_ref=g_ref.at[1 - c], dst_ref=a_ref, send_sem=send_sem,
                                          recv_sem=recv_sem, device_id=(x, y, 1 - c), device_id_type=MESH)
        cp.start()
        cp.wait()

    return pl.pallas_call(
        body,
        in_specs=[ANY],
        out_specs=ANY,
        out_shape=jax.ShapeDtypeStruct((nb, rows, LANES), F32),
        scratch_shapes=[pltpu.SemaphoreType.DMA, pltpu.SemaphoreType.DMA],
        name="rs_sibling_exchange",
    )(g)


def _rs_chip_exchange(p):
    nb, rows, _ = p.shape

    def body(p_ref, b_ref, send_sems, recv_sems, local_sem):
        x, y, c, chips = _place()
        me = 2 * x + y
        mine = pltpu.make_async_copy(p_ref.at[me], b_ref.at[me], local_sem)
        mine.start()
        copies = [pltpu.make_async_remote_copy(src_ref=p_ref.at[2 * px + py], dst_ref=b_ref.at[me],
                                               send_sem=send_sems.at[k], recv_sem=recv_sems.at[k],
                                               device_id=(px, py, c), device_id_type=MESH)
                  for k, (px, py) in enumerate(chips)]
        for cp in copies:
            cp.start()
        for cp in copies:
            cp.wait()
        mine.wait()

    return pl.pallas_call(
        body,
        in_specs=[ANY],
        out_specs=ANY,
        out_shape=jax.ShapeDtypeStruct((nb, rows, LANES), F32),
        scratch_shapes=[pltpu.SemaphoreType.DMA((3,)), pltpu.SemaphoreType.DMA((3,)), pltpu.SemaphoreType.DMA],
        name="rs_chip_exchange",
    )(p)


def _rs_sibling_gather(r):
    _, rows, _ = r.shape

    def body(r_ref, o_ref, send_sem, recv_sem):
        x, y, c, _ = _place()
        cp = pltpu.make_async_remote_copy(src_ref=o_ref.at[c], dst_ref=o_ref.at[c], send_sem=send_sem,
                                          recv_sem=recv_sem, device_id=(x, y, 1 - c), device_id_type=MESH)
        cp.start()
        cp.wait()

    return pl.pallas_call(
        body,
        in_specs=[ANY],
        out_specs=ANY,
        out_shape=jax.ShapeDtypeStruct((2, rows, LANES), F32),
        input_output_aliases={0: 0},
        scratch_shapes=[pltpu.SemaphoreType.DMA, pltpu.SemaphoreType.DMA],
        name="rs_sibling_gather",
    )(r)


def _reduce_scatter(g, core):
    a = _rs_sibling_exchange(g)
    p = _add_own_half(g, a, core)
    b = _rs_chip_exchange(p)
    return _rs_sibling_gather(_sum_chips(b, core))


def _all_reduce_small(v):
    rows, _ = v.shape
    n_dev = 8

    def body(v_ref, o_ref, gath, send_sems, recv_sems):
        x, y, c, _ = _place()
        me = 4 * x + 2 * y + c
        gath[me] = v_ref[...]
        copies = []
        for mask in range(1, n_dev):
            px = 1 - x if mask & 4 else x
            py = 1 - y if mask & 2 else y
            pc = 1 - c if mask & 1 else c
            copies.append(pltpu.make_async_remote_copy(
                src_ref=v_ref, dst_ref=gath.at[me], send_sem=send_sems.at[mask - 1],
                recv_sem=recv_sems.at[mask - 1], device_id=(px, py, pc), device_id_type=MESH))
        for cp in copies:
            cp.start()
        for cp in copies:
            cp.wait()
        acc = gath[0]
        for k in range(1, n_dev):
            acc = acc + gath[k]
        o_ref[...] = acc

    return pl.pallas_call(
        body,
        in_specs=[VMEM],
        out_specs=VMEM,
        out_shape=jax.ShapeDtypeStruct((rows, LANES), F32),
        scratch_shapes=[pltpu.VMEM((n_dev, rows, LANES), F32), pltpu.SemaphoreType.DMA((n_dev - 1,)),
                        pltpu.SemaphoreType.DMA((n_dev - 1,))],
        name="all_reduce_small",
    )(v)


def _flat_len(shapes, unit_rows=FLAT_TILE_ROWS):
    n = sum(int(np.prod(s)) for s in shapes)
    unit = 2 * unit_rows * LANES
    return -(-n // unit) * unit


def _pack(arrays, total):
    flat = jnp.concatenate([a.reshape(-1) for a in arrays])
    return jnp.pad(flat, (0, total - flat.shape[0]))


def _unpack(flat, shapes):
    out, off = [], 0
    for s in shapes:
        n = int(np.prod(s))
        out.append(flat[off:off + n].reshape(s))
        off += n
    return out


def _permute_w_in(w):
    pad = jnp.zeros(w.shape[:-1] + (N_IN_PAD - N_IN,), w.dtype)
    return jnp.concatenate([w[..., 0:1536], w[..., 1540:3076], w[..., 3084:3596], w[..., 3596:5132],
                            w[..., 5132:8204], w[..., 1536:1540], w[..., 3076:3080], w[..., 3080:3084], pad],
                           axis=-1)


def _unpermute_w_in(w):
    return jnp.concatenate([w[..., 0:1536], w[..., 8192:8196], w[..., 1536:3072], w[..., 8196:8200],
                            w[..., 8200:8204], w[..., 3072:3584], w[..., 3584:5120], w[..., 5120:8192]], axis=-1)


PROJ_WIDTHS = (512, 512, 512, 1536, 512, 512, 512, 512, 3072, 4, 4, 4)


@jax.custom_vjp
def split_proj(proj):
    offs = np.cumsum((0,) + PROJ_WIDTHS)
    return tuple(proj[:, o:o + wd] for o, wd in zip(offs, PROJ_WIDTHS))


def _split_proj_fwd(proj):
    return split_proj(proj), None


def _split_proj_bwd(_, cts):
    pad = jnp.zeros((cts[0].shape[0], N_IN_PAD - sum(PROJ_WIDTHS)), F32)
    return (jnp.concatenate(list(cts) + [pad], axis=1),)


split_proj.defvjp(_split_proj_fwd, _split_proj_bwd)


def heads(x, n):
    return x.reshape(x.shape[:-1] + (n, -1))


def gated_delta_rule(q, k, v, g, beta):
    B, T, H, dk = q.shape
    dv = v.shape[-1]
    N = T // CHUNK

    def chunks(a):
        a = a.astype(F32).reshape((B, N, CHUNK, H) + a.shape[3:])
        return jnp.moveaxis(a, (1, 3), (0, 2))

    qc = chunks(q) * dk ** -0.5
    kc = chunks(k)
    vc = chunks(v)
    bc = chunks(beta)
    gc = jnp.cumsum(chunks(g), axis=-1)
    idx = jnp.arange(CHUNK)
    causal = idx[:, None] >= idx[None, :]
    strict = idx[:, None] > idx[None, :]
    decay = jnp.exp(jnp.where(causal, gc[..., :, None] - gc[..., None, :], -jnp.inf))
    kk = jnp.einsum("nbhcd,nbhed->nbhce", kc, kc)
    a_mat = jnp.where(strict, bc[..., :, None] * kk * decay, 0.0) + jnp.eye(CHUNK, dtype=F32)
    rhs = jnp.concatenate([vc * bc[..., None], kc * (bc * jnp.exp(gc))[..., None]], axis=-1)
    sol = lax.linalg.triangular_solve(a_mat, rhs, left_side=True, lower=True)
    u, w = sol[..., :dv], sol[..., dv:]
    attn = jnp.where(causal, jnp.einsum("nbhcd,nbhed->nbhce", qc, kc) * decay, 0.0)
    g_last = gc[..., -1]
    k_tail = kc * jnp.exp(g_last[..., None] - gc)[..., None]

    egl = jnp.broadcast_to(jnp.exp(g_last)[:, 0, :, None, None], (N, H, 1, dv))
    o = gdn_scan((qc * jnp.exp(gc)[..., None])[:, 0], u[:, 0], w[:, 0], attn[:, 0], k_tail[:, 0], egl)
    return o.transpose(0, 2, 1, 3).reshape(B, T, H, dv)


def _layer(x, mem, p, p16):
    S, D = x.shape
    h = rmsnorm_rows(x, p['norm_mix'])
    def mm(a, name):
        return matmul(a, p[name], p16[name])

    fq, fk, fv, gqkv, gz, sq, sk, sv, gates, ff, gb, ga = split_proj(mm(h, 'w_in'))

    logf = jax.nn.log_sigmoid((ff + p['fox_fbias']).astype(F32))
    ya = fox_core(head_rms(fq, p['fox_qnorm']), head_rms(fk, p['fox_knorm']), fv, jnp.cumsum(logf, axis=0))

    qkv = conv_silu(gqkv, p['gdn_conv'])
    cq, ck, cv = qkv[:, :512], qkv[:, 512:1024], qkv[:, 1024:]
    beta = jax.nn.sigmoid(gb.astype(F32))
    g_log = -jnp.exp(p['gdn_a_log'].astype(F32)) * jax.nn.softplus((ga + p['gdn_dt_bias']).astype(F32))
    o = gated_delta_rule(heads(head_l2(cq), HEADS)[None], heads(head_l2(ck), HEADS)[None], heads(cv, HEADS)[None],
                         g_log[None], beta[None])
    yb = head_rms_gate(o.reshape(S, 512), p['gdn_onorm'], gz)

    yc = sb_core(sq, sk, sv)

    mixed = gated_merge(gates, p['gate_bias'], mm(ya, 'w_oa'), mm(yb, 'w_ob'), mm(yc, 'w_oc'))
    x = x + mm(mixed, 'w_out')

    hq = rmsnorm_rows(x, p['norm_xq'])
    hm = rmsnorm_rows(mem, p['norm_mem'])
    q = heads(head_rms(mm(hq, 'w_mq'), p['mq_norm']), HEADS)
    kv = mm(hm, 'w_mkv')
    k, v = jnp.split(kv, 2, axis=-1)
    k = heads(head_rms(k, p['mk_norm']), HEADS)
    v = heads(v, HEADS)
    s = jnp.einsum("qhd,khd->hqk", q, k).astype(F32) * MEM_DIM ** -0.5
    pr = jax.nn.softmax(s, axis=-1).astype(v.dtype)
    om = jnp.einsum("hqk,khd->qhd", pr, v).reshape(S, 512)
    x = x + mm(om, 'w_mo')

    hf = rmsnorm_rows(x, p['norm_ffn'])
    act = conv_gate(mm(hf, 'w_up'), p['ffn_conv'], p['ffn_conv_b'])
    return x + mm(act, 'w_down')


def kernel(x, mem, norm_mix, w_in, fox_fbias, fox_qnorm, fox_knorm, gdn_conv, gdn_a_log, gdn_dt_bias, gdn_onorm, gate_bias, w_oa, w_ob, w_oc, w_out, norm_xq, norm_mem, w_mq, w_mkv, mq_norm, mk_norm, w_mo, norm_ffn, w_up, ffn_conv, ffn_conv_b, w_down, loss_target, m_norm_mix, m_w_in, m_fox_fbias, m_fox_qnorm, m_fox_knorm, m_gdn_conv, m_gdn_a_log, m_gdn_dt_bias, m_gdn_onorm, m_gate_bias, m_w_oa, m_w_ob, m_w_oc, m_w_out, m_norm_xq, m_norm_mem, m_w_mq, m_w_mkv, m_mq_norm, m_mk_norm, m_w_mo, m_norm_ffn, m_w_up, m_ffn_conv, m_ffn_conv_b, m_w_down, v_norm_mix, v_w_in, v_fox_fbias, v_fox_qnorm, v_fox_knorm, v_gdn_conv, v_gdn_a_log, v_gdn_dt_bias, v_gdn_onorm, v_gate_bias, v_w_oa, v_w_ob, v_w_oc, v_w_out, v_norm_xq, v_norm_mem, v_w_mq, v_w_mkv, v_mq_norm, v_mk_norm, v_w_mo, v_norm_ffn, v_w_up, v_ffn_conv, v_ffn_conv_b, v_w_down):
    args = (x, mem, norm_mix, w_in, fox_fbias, fox_qnorm, fox_knorm, gdn_conv, gdn_a_log, gdn_dt_bias, gdn_onorm, gate_bias, w_oa, w_ob, w_oc, w_out, norm_xq, norm_mem, w_mq, w_mkv, mq_norm, mk_norm, w_mo, norm_ffn, w_up, ffn_conv, ffn_conv_b, w_down)
    moments_m = (m_norm_mix, m_w_in, m_fox_fbias, m_fox_qnorm, m_fox_knorm, m_gdn_conv, m_gdn_a_log, m_gdn_dt_bias, m_gdn_onorm, m_gate_bias, m_w_oa, m_w_ob, m_w_oc, m_w_out, m_norm_xq, m_norm_mem, m_w_mq, m_w_mkv, m_mq_norm, m_mk_norm, m_w_mo, m_norm_ffn, m_w_up, m_ffn_conv, m_ffn_conv_b, m_w_down)
    moments_v = (v_norm_mix, v_w_in, v_fox_fbias, v_fox_qnorm, v_fox_knorm, v_gdn_conv, v_gdn_a_log, v_gdn_dt_bias, v_gdn_onorm, v_gate_bias, v_w_oa, v_w_ob, v_w_oc, v_w_out, v_norm_xq, v_norm_mem, v_w_mq, v_w_mkv, v_mq_norm, v_mk_norm, v_w_mo, v_norm_ffn, v_w_up, v_ffn_conv, v_ffn_conv_b, v_w_down)
    w = dict(zip(IN_NAMES, args))
    m = dict(zip(WEIGHTS, moments_m))
    v = dict(zip(WEIGHTS, moments_v))
    xs, mems, tgt = x[0], mem[0], loss_target[0]
    core = lax.axis_index("c").astype(jnp.int32).reshape(1)

    big = list(SHARDED)
    shard_shapes = [w[n].shape for n in big]
    total = _flat_len(shard_shapes)
    half_rows = total // (2 * LANES)
    small_shapes = [w[n].shape for n in SMALL]
    n_small = sum(int(np.prod(s)) for s in small_shapes) + 1
    small_total = -(-n_small // (8 * LANES)) * (8 * LANES)

    my_chip = 2 * lax.axis_index("x") + lax.axis_index("y")

    def gather(names, dtype, unit_rows):
        shapes = [w[n].shape for n in names]
        tot = _flat_len(shapes, unit_rows)
        flat = _pack([w[n].astype(dtype) for n in names], tot)
        got = _all_gather_chips(flat.reshape(2, tot // (2 * LANES), LANES)).reshape(N_CHIPS, tot)
        out = {}
        for n, blocks in zip(names, zip(*[_unpack(got[j], shapes) for j in range(N_CHIPS)])):
            axis = SHARDED[n]
            out[n] = lax.dynamic_update_slice_in_dim(jnp.concatenate(blocks, axis=axis), w[n].astype(dtype),
                                                     my_chip * w[n].shape[axis], axis)
        return out

    conv_names = ['gdn_conv', 'ffn_conv']
    params16 = gather([n for n in big if n not in conv_names], BF16, FLAT_TILE_ROWS)
    params16['w_in'] = _permute_w_in(params16['w_in'])
    params = {n: a.astype(F32) for n, a in params16.items()}
    params.update(gather(conv_names, F32, 8))
    for n in SMALL:
        params[n] = w[n]

    def model(x0, pp):
        for layer in range(DEPTH):
            x0 = _layer(x0, mems, {n: a[layer] for n, a in pp.items()}, {n: a[layer] for n, a in params16.items()})
        return x0

    y, model_vjp = jax.vjp(model, xs, params)
    dy, loss_part = _loss_head(y, tgt)
    dx0, grads = model_vjp(dy)
    grads['w_in'] = _unpermute_w_in(grads['w_in'])

    def chip_blocks(g, axis):
        return jnp.stack(jnp.split(g, N_CHIPS, axis=axis)).reshape(N_CHIPS, -1)

    g_blocks = jnp.concatenate([chip_blocks(grads[n], SHARDED[n]) for n in big], axis=1)
    g_blocks = jnp.pad(g_blocks, ((0, 0), (0, total - g_blocks.shape[1])))
    g_halves = g_blocks.reshape(N_CHIPS, 2, half_rows, LANES).transpose(1, 0, 2, 3)
    g_flat = _reduce_scatter(g_halves, core).reshape(total // LANES, LANES)

    s_part = _pack([grads[n] for n in SMALL] + [loss_part[0, :1]], small_total)
    s_sum = _all_reduce_small(s_part.reshape(small_total // LANES, LANES))
    small_grads = _unpack(s_sum.reshape(-1), small_shapes + [(1,)])
    loss = small_grads.pop()[0]

    out = {}
    for n, g_shard in zip(big, _unpack(g_flat.reshape(-1), shard_shapes)):
        shape = w[n].shape
        rows_of = lambda a: a.reshape(-1, shape[-1])
        out['grad', n] = g_shard
        for kind, a in zip(('delta', 'new_m', 'new_v'),
                           _adamw(rows_of(w[n]), rows_of(g_shard), rows_of(m[n]), rows_of(v[n]), n)):
            out[kind, n] = a.reshape(shape)
    srows = small_total // LANES
    sd, snm, snv = _adamw(_pack([w[n] for n in SMALL], small_total).reshape(srows, LANES), s_sum,
                          _pack([m[n] for n in SMALL], small_total).reshape(srows, LANES),
                          _pack([v[n] for n in SMALL], small_total).reshape(srows, LANES), "replicated")
    for kind, flat_small in (('grad', s_sum), ('delta', sd), ('new_m', snm), ('new_v', snv)):
        for n, a in zip(SMALL, _unpack(flat_small.reshape(-1), small_shapes)):
            out[kind, n] = a
    return (loss, dx0[None], *[out[kind, n] for kind in ('grad', 'delta', 'new_m', 'new_v') for n in WEIGHTS])
```

```python
import functools

import jax
import jax.numpy as jnp
import numpy as np
from jax import lax
from jax.experimental import pallas as pl
from jax.experimental.pallas import tpu as pltpu

F32 = jnp.float32
BF16 = jnp.bfloat16
MESH = pl.DeviceIdType.MESH
ANY = pl.BlockSpec(memory_space=pl.ANY)
VMEM = pl.BlockSpec(memory_space=pltpu.VMEM)

D_MODEL = 1024
DEPTH = 4
CHUNK = 64
Q_BLOCK = 128
EPS = 1e-6
HEADS = 4
HEAD_DIM = 128
GDN_CONV = 4
MEM_DIM = 128
D_FF = 2816
N_BRANCH = 3
N_IN = 8204
N_IN_PAD = 8320

ADAM_LR = 0.001
ADAM_B1 = 0.9
ADAM_B2 = 0.999
ADAM_EPS = 1e-08
ADAM_WD = 0.01
ADAM_STEP = 10

N_CHIPS = 4
LANES = 128
FLAT_TILE_ROWS = 2048
VMEM_LIMIT = 48 * 1024 * 1024

IN_NAMES = ['x', 'mem', 'norm_mix', 'w_in', 'fox_fbias', 'fox_qnorm', 'fox_knorm', 'gdn_conv', 'gdn_a_log',
            'gdn_dt_bias', 'gdn_onorm', 'gate_bias', 'w_oa', 'w_ob', 'w_oc', 'w_out', 'norm_xq', 'norm_mem',
            'w_mq', 'w_mkv', 'mq_norm', 'mk_norm', 'w_mo', 'norm_ffn', 'w_up', 'ffn_conv', 'ffn_conv_b', 'w_down']
WEIGHTS = IN_NAMES[2:]
SHARDED = {'w_in': 2, 'gdn_conv': 2, 'w_oa': 2, 'w_ob': 2, 'w_oc': 2, 'w_out': 1, 'w_mq': 1, 'w_mkv': 1,
           'w_mo': 2, 'w_up': 2, 'ffn_conv': 2, 'w_down': 1}
SMALL = [n for n in WEIGHTS if n not in SHARDED]


def _pick(n, cands):
    for c in cands:
        if n % c == 0:
            return c
    return n


_DOT_DIMS = {
    'nn': (((1,), (0,)), ((), ())),
    'nt': (((1,), (1,)), ((), ())),
    'tn': (((0,), (0,)), ((), ())),
}


def _mm(a, b, mode):
    if mode == 'nn':
        (m, c), (_, n) = a.shape, b.shape
    elif mode == 'nt':
        (m, c), (n, _) = a.shape, b.shape
    else:
        (c, m), (_, n) = a.shape, b.shape
    tm = _pick(m, (1024, 512, 256, 128))
    tn = _pick(n, (1664, 1408, 1024, 512, 256, 128))
    tc = _pick(c, (512, 256, 128)) if mode == 'tn' else _pick(c, (1024, 1408, 640, 512, 256, 128))
    if mode == 'tn':
        a_spec = pl.BlockSpec((tc, tm), lambda i, j, k: (k, i))
    else:
        a_spec = pl.BlockSpec((tm, tc), lambda i, j, k: (i, k))
    if mode == 'nt':
        b_spec = pl.BlockSpec((tn, tc), lambda i, j, k: (j, k))
    else:
        b_spec = pl.BlockSpec((tc, tn), lambda i, j, k: (k, j))
    dims = _DOT_DIMS[mode]

    def body(a_ref, b_ref, o_ref):
        @pl.when(pl.program_id(2) == 0)
        def _():
            o_ref[...] = jnp.zeros_like(o_ref)

        o_ref[...] += lax.dot_general(a_ref[...].astype(BF16), b_ref[...].astype(BF16), dims,
                                      preferred_element_type=F32)

    return pl.pallas_call(
        body,
        grid=(m // tm, n // tn, c // tc),
        in_specs=[a_spec, b_spec],
        out_specs=pl.BlockSpec((tm, tn), lambda i, j, k: (i, j)),
        out_shape=jax.ShapeDtypeStruct((m, n), F32),
        compiler_params=pltpu.CompilerParams(
            dimension_semantics=("parallel", "parallel", "arbitrary"), vmem_limit_bytes=VMEM_LIMIT),
        name=f"mm_{mode}_{m}x{c}x{n}",
    )(a, b)


@jax.custom_vjp
def matmul(a, w, w16):
    return _mm(a, w16, 'nn')


def _matmul_fwd(a, w, w16):
    return _mm(a, w16, 'nn'), (a, w16)


def _matmul_bwd(res, dy):
    a, w16 = res
    return _mm(dy, w16, 'nt'), _mm(a, dy, 'tn'), jnp.zeros_like(w16)


matmul.defvjp(_matmul_fwd, _matmul_bwd)


Q_TILE = 512
K_BLOCK = 1024
SUB = 256
_NT = (((1,), (1,)), ((), ()))
_TN = (((0,), (0,)), ((), ()))
_NN = (((1,), (0,)), ((), ()))


def _dot(a, b, dims):
    return lax.dot_general(a, b, dims, preferred_element_type=F32)


def _att_tiles(s):
    tq = min(Q_TILE, s)
    tk = min(K_BLOCK, s)
    assert s % tk == 0 and tk % tq == 0 and tk % min(SUB, tk) == 0
    return tq, tk


def _att_specs(s, t):
    tile = pl.BlockSpec((t, HEAD_DIM), lambda h, i: (i, h))
    whole = pl.BlockSpec((s, HEAD_DIM), lambda h, i: (0, h))
    col = pl.BlockSpec((None, t, 1), lambda h, i: (h, i, 0))
    row = pl.BlockSpec((None, 1, s), lambda h, i: (h, 0, 0))
    return tile, whole, col, row


def _key_block(jb, tk):
    return pl.ds(pl.multiple_of(jb * tk, tk), tk)


def _causal(i, jb, tq, tk, strict):
    r = i * tq + lax.broadcasted_iota(jnp.int32, (tq, tk), 0)
    c = jb * tk + lax.broadcasted_iota(jnp.int32, (tq, tk), 1)
    return c < r if strict else c <= r


def _fox_fwd_call(q, k, v, c_col, c_row):
    s, w = q.shape
    tq, tk = _att_tiles(s)
    scale = HEAD_DIM ** -0.5

    def body(q_ref, k_ref, v_ref, cc_ref, cr_ref, o_ref, lse_ref):
        i = pl.program_id(1)
        n_full = lax.div(i * tq, tk)
        qb = q_ref[...]
        cq = cc_ref[...]

        def block(jb, carry, diag):
            m, l, acc = carry
            sl = _key_block(jb, tk)
            sc = _dot(qb, k_ref[sl, :], _NT) * scale + (cq - cr_ref[:, sl])
            if diag:
                sc = jnp.where(_causal(i, jb, tq, tk, False), sc, -jnp.inf)
            m_new = jnp.maximum(m, jnp.max(sc, axis=-1, keepdims=True))
            p = jnp.exp(sc - m_new)
            alpha = jnp.exp(m - m_new)
            l = alpha * l + jnp.sum(p, axis=-1, keepdims=True)
            acc = alpha * acc + _dot(p.astype(BF16), v_ref[sl, :], _NN)
            return m_new, l, acc

        init = (jnp.full((tq, 1), -jnp.inf, F32), jnp.zeros((tq, 1), F32), jnp.zeros((tq, HEAD_DIM), F32))
        carry = lax.fori_loop(0, n_full, lambda jb, cr: block(jb, cr, False), init)
        m, l, acc = block(n_full, carry, True)
        o_ref[...] = acc / l
        lse_ref[...] = m + jnp.log(l)

    tile, whole, col, row = _att_specs(s, tq)
    return pl.pallas_call(
        body,
        grid=(w // HEAD_DIM, s // tq),
        in_specs=[tile, whole, whole, col, row],
        out_specs=[tile, col],
        out_shape=[jax.ShapeDtypeStruct((s, w), F32), jax.ShapeDtypeStruct((w // HEAD_DIM, s, 1), F32)],
        compiler_params=pltpu.CompilerParams(dimension_semantics=("parallel", "parallel"),
                                             vmem_limit_bytes=VMEM_LIMIT),
        name="fox_fwd",
    )(q, k, v, c_col, c_row)


def _fox_bwd_call(q, k, v, c_col, c_row, lse, do):
    s, w = q.shape
    tq, tk = _att_tiles(s)
    scale = HEAD_DIM ** -0.5

    def body(q_ref, k_ref, v_ref, cc_ref, cr_ref, lse_ref, do_ref, dq_ref, dk_ref, dv_ref, dcr_ref):
        i = pl.program_id(1)
        n_full = lax.div(i * tq, tk)

        @pl.when(i == 0)
        def _():
            dk_ref[...] = jnp.zeros_like(dk_ref)
            dv_ref[...] = jnp.zeros_like(dv_ref)
            dcr_ref[...] = jnp.zeros_like(dcr_ref)

        qb = q_ref[...]
        do16 = do_ref[...].astype(BF16)
        lse_q = lse_ref[...]
        cq = cc_ref[...]

        def probs(jb, diag):
            sl = _key_block(jb, tk)
            ks = k_ref[sl, :]
            sc = _dot(qb, ks, _NT) * scale + (cq - cr_ref[:, sl])
            p = jnp.exp(sc - lse_q)
            if diag:
                p = jnp.where(_causal(i, jb, tq, tk, False), p, 0.0)
            return sl, ks, p, _dot(do16, v_ref[sl, :], _NT)

        def row_dot(jb, acc, diag):
            _, _, p, dp = probs(jb, diag)
            return acc + jnp.sum(p * dp, axis=-1, keepdims=True)

        delta = lax.fori_loop(0, n_full, lambda jb, a: row_dot(jb, a, False), jnp.zeros((tq, 1), F32))
        delta = row_dot(n_full, delta, True)

        def block(jb, dq, diag):
            sl, ks, p, dp = probs(jb, diag)
            ds = p * (dp - delta)
            ds16 = ds.astype(BF16)
            dv_ref[sl, :] += _dot(p.astype(BF16), do16, _TN)
            dk_ref[sl, :] += _dot(ds16, qb, _TN) * scale
            dcr_ref[:, sl] += -jnp.sum(ds, axis=0, keepdims=True)
            return dq + _dot(ds16, ks, _NN) * scale

        dq = lax.fori_loop(0, n_full, lambda jb, a: block(jb, a, False), jnp.zeros((tq, HEAD_DIM), F32))
        dq_ref[...] = block(n_full, dq, True)

    tile, whole, col, row = _att_specs(s, tq)
    full = jax.ShapeDtypeStruct((s, w), F32)
    return pl.pallas_call(
        body,
        grid=(w // HEAD_DIM, s // tq),
        in_specs=[tile, whole, whole, col, row, col, tile],
        out_specs=[tile, whole, whole, row],
        out_shape=[full, full, full, jax.ShapeDtypeStruct((w // HEAD_DIM, 1, s), F32)],
        compiler_params=pltpu.CompilerParams(dimension_semantics=("parallel", "arbitrary"),
                                             vmem_limit_bytes=VMEM_LIMIT),
        name="fox_bwd",
    )(q, k, v, c_col, c_row, lse, do)


@jax.custom_vjp
def fox_core(q, k, v, c):
    return _fox_fwd(q, k, v, c)[0]


def _fox_fwd(q, k, v, c):
    q16, k16, v16 = q.astype(BF16), k.astype(BF16), v.astype(BF16)
    c_col, c_row = c.T[:, :, None], c.T[:, None, :]
    o, lse = _fox_fwd_call(q16, k16, v16, c_col, c_row)
    return o, (q16, k16, v16, c_col, c_row, lse)


def _fox_bwd(res, do):
    dq, dk, dv, dcr = _fox_bwd_call(*res, do)
    return dq, dk, dv, dcr[:, 0, :].T


fox_core.defvjp(_fox_fwd, _fox_bwd)


def _neg_softplus(z):
    e = jnp.exp(-jnp.abs(z))
    return -(jnp.maximum(z, 0.0) + jnp.log(1.0 + e)), e


def _split_dot(x, tri):
    hi = x.astype(BF16)
    lo = (x - hi.astype(F32)).astype(BF16)
    return _dot(hi, tri, _NN) + _dot(lo, tri, _NN)


def _tri(n, fn):
    r = lax.broadcasted_iota(jnp.int32, (n, n), 0)
    c = lax.broadcasted_iota(jnp.int32, (n, n), 1)
    return fn(r, c).astype(BF16)


def _sb_fwd_call(q, k, v):
    s, w = q.shape
    tq, tk = _att_tiles(s)
    sub = min(SUB, tk)
    n_sub = tk // sub
    scale = HEAD_DIM ** -0.5

    def body(q_ref, k_ref, v_ref, o_ref, tot_ref):
        i = pl.program_id(1)
        n_full = lax.div(i * tq, tk)
        qb = q_ref[...]
        tri = _tri(sub, lambda r, c: r >= c)

        def block(jb, carry, diag):
            later, acc = carry
            sl = _key_block(jb, tk)
            z = _dot(qb, k_ref[sl, :], _NT) * scale
            lk, _ = _neg_softplus(z)
            if diag:
                mask = _causal(i, jb, tq, tk, True)
                lk = jnp.where(mask, lk, 0.0)
            pieces = [None] * n_sub
            for u in reversed(range(n_sub)):
                part = lk[:, u * sub:(u + 1) * sub]
                pieces[u] = _split_dot(part, tri) + later
                later = later + jnp.sum(part, axis=-1, keepdims=True)
            a = jnp.exp(z + jnp.concatenate(pieces, axis=1))
            if diag:
                a = jnp.where(mask, a, 0.0)
            return later, acc + _dot(a.astype(BF16), v_ref[sl, :], _NN)

        carry = block(n_full, (jnp.zeros((tq, 1), F32), jnp.zeros((tq, HEAD_DIM), F32)), True)
        later, acc = lax.fori_loop(0, n_full, lambda jj, cr: block(n_full - 1 - jj, cr, False), carry)
        o_ref[...] = acc
        tot_ref[...] = later

    tile, whole, col, _ = _att_specs(s, tq)
    return pl.pallas_call(
        body,
        grid=(w // HEAD_DIM, s // tq),
        in_specs=[tile, whole, whole],
        out_specs=[tile, col],
        out_shape=[jax.ShapeDtypeStruct((s, w), F32), jax.ShapeDtypeStruct((w // HEAD_DIM, s, 1), F32)],
        compiler_params=pltpu.CompilerParams(dimension_semantics=("parallel", "parallel"),
                                             vmem_limit_bytes=VMEM_LIMIT),
        name="sb_fwd",
    )(q, k, v)


def _sb_bwd_call(q, k, v, tot, do):
    s, w = q.shape
    tq, tk = _att_tiles(s)
    sub = min(SUB, tk)
    n_sub = tk // sub
    scale = HEAD_DIM ** -0.5

    def body(q_ref, k_ref, v_ref, tot_ref, do_ref, dq_ref, dk_ref, dv_ref):
        i = pl.program_id(1)
        n_full = lax.div(i * tq, tk)

        @pl.when(i == 0)
        def _():
            dk_ref[...] = jnp.zeros_like(dk_ref)
            dv_ref[...] = jnp.zeros_like(dv_ref)

        qb = q_ref[...]
        do16 = do_ref[...].astype(BF16)
        tot_q = tot_ref[...]
        tri_before = _tri(sub, lambda r, c: r < c)
        tri_upto = _tri(sub, lambda r, c: r <= c)

        def block(jb, carry, diag):
            before, dl_before, dq = carry
            sl = _key_block(jb, tk)
            ks = k_ref[sl, :]
            z = _dot(qb, ks, _NT) * scale
            lk, e = _neg_softplus(z)
            if diag:
                mask = _causal(i, jb, tq, tk, True)
                lk = jnp.where(mask, lk, 0.0)
            sig = 1.0 - jnp.exp(lk)
            pieces = []
            for u in range(n_sub):
                part = lk[:, u * sub:(u + 1) * sub]
                pieces.append(_split_dot(part, tri_before) + before)
                before = before + jnp.sum(part, axis=-1, keepdims=True)
            a = jnp.exp(z + (tot_q - jnp.concatenate(pieces, axis=1)))
            if diag:
                a = jnp.where(mask, a, 0.0)
            dl = a * _dot(do16, v_ref[sl, :], _NT)
            dl16 = dl.astype(BF16)
            pieces = []
            for u in range(n_sub):
                pieces.append(_dot(dl16[:, u * sub:(u + 1) * sub], tri_upto, _NN) + dl_before)
                dl_before = dl_before + jnp.sum(dl[:, u * sub:(u + 1) * sub], axis=-1, keepdims=True)
            dz = dl - sig * jnp.concatenate(pieces, axis=1)
            if diag:
                dz = jnp.where(mask, dz, 0.0)
            dz16 = dz.astype(BF16)
            dv_ref[sl, :] += _dot(a.astype(BF16), do16, _TN)
            dk_ref[sl, :] += _dot(dz16, qb, _TN) * scale
            return before, dl_before, dq + _dot(dz16, ks, _NN) * scale

        init = (jnp.zeros((tq, 1), F32), jnp.zeros((tq, 1), F32), jnp.zeros((tq, HEAD_DIM), F32))
        carry = lax.fori_loop(0, n_full, lambda jb, cr: block(jb, cr, False), init)
        dq_ref[...] = block(n_full, carry, True)[2]

    tile, whole, col, _ = _att_specs(s, tq)
    full = jax.ShapeDtypeStruct((s, w), F32)
    return pl.pallas_call(
        body,
        grid=(w // HEAD_DIM, s // tq),
        in_specs=[tile, whole, whole, col, tile],
        out_specs=[tile, whole, whole],
        out_shape=[full, full, full],
        compiler_params=pltpu.CompilerParams(dimension_semantics=("parallel", "arbitrary"),
                                             vmem_limit_bytes=VMEM_LIMIT),
        name="sb_bwd",
    )(q, k, v, tot, do)


@jax.custom_vjp
def sb_core(q, k, v):
    return _sb_fwd(q, k, v)[0]


def _sb_fwd(q, k, v):
    q16, k16, v16 = q.astype(BF16), k.astype(BF16), v.astype(BF16)
    o, tot = _sb_fwd_call(q16, k16, v16)
    return o, (q16, k16, v16, tot)


def _sb_bwd(res, do):
    return tuple(_sb_bwd_call(*res, do))


sb_core.defvjp(_sb_fwd, _sb_bwd)


def _gdn_specs(h, c, d):
    vec = pl.BlockSpec((None, h, c, d), lambda n: (n, 0, 0, 0))
    sq = pl.BlockSpec((None, h, c, c), lambda n: (n, 0, 0, 0))
    dec = pl.BlockSpec((None, h, 1, d), lambda n: (n, 0, 0, 0))
    st = pl.BlockSpec((None, h, d, d), lambda n: (n, 0, 0, 0))
    return vec, sq, dec, st


def _b16(x):
    return x.astype(BF16)


def _gdn_scan_fwd_call(qg, u, w, attn, kt, egl):
    n, h, c, d = qg.shape

    def body(qg_ref, u_ref, w_ref, attn_ref, kt_ref, egl_ref, o_ref, st_ref, state):
        @pl.when(pl.program_id(0) == 0)
        def _():
            state[...] = jnp.zeros_like(state)

        for hh in range(h):
            s0 = state[hh]
            st_ref[hh] = s0
            s16 = _b16(s0)
            vn = u_ref[hh] - _dot(_b16(w_ref[hh]), s16, _NN)
            vn16 = _b16(vn)
            o_ref[hh] = _dot(_b16(qg_ref[hh]), s16, _NN) + _dot(_b16(attn_ref[hh]), vn16, _NN)
            state[hh] = s0 * egl_ref[hh] + _dot(_b16(kt_ref[hh]), vn16, _TN)

    vec, sq, dec, st = _gdn_specs(h, c, d)
    return pl.pallas_call(
        body,
        grid=(n,),
        in_specs=[vec, vec, vec, sq, vec, dec],
        out_specs=[vec, st],
        out_shape=[jax.ShapeDtypeStruct((n, h, c, d), F32), jax.ShapeDtypeStruct((n, h, d, d), F32)],
        scratch_shapes=[pltpu.VMEM((h, d, d), F32)],
        compiler_params=pltpu.CompilerParams(dimension_semantics=("arbitrary",)),
        name="gdn_scan_fwd",
    )(qg, u, w, attn, kt, egl)


def _gdn_scan_bwd_call(qg, u, w, attn, kt, egl, states, do):
    n, h, c, d = qg.shape

    def body(qg_ref, u_ref, w_ref, attn_ref, kt_ref, egl_ref, st_ref, do_ref,
             dqg_ref, du_ref, dw_ref, dattn_ref, dkt_ref, degl_ref, dstate):
        @pl.when(pl.program_id(0) == 0)
        def _():
            dstate[...] = jnp.zeros_like(dstate)

        for hh in range(h):
            s0 = st_ref[hh]
            s16 = _b16(s0)
            big_d = dstate[hh]
            d16 = _b16(big_d)
            w16, kt16, qg16, attn16 = _b16(w_ref[hh]), _b16(kt_ref[hh]), _b16(qg_ref[hh]), _b16(attn_ref[hh])
            do16 = _b16(do_ref[hh])
            vn16 = _b16(u_ref[hh] - _dot(w16, s16, _NN))
            dvn = _dot(attn16, do16, _TN) + _dot(kt16, d16, _NN)
            dvn16 = _b16(dvn)
            du_ref[hh] = dvn
            dattn_ref[hh] = _dot(do16, vn16, _NT)
            dqg_ref[hh] = _dot(do16, s16, _NT)
            dkt_ref[hh] = _dot(vn16, d16, _NT)
            dw_ref[hh] = -_dot(dvn16, s16, _NT)
            degl_ref[hh] = jnp.sum(big_d * s0, axis=0, keepdims=True)
            dstate[hh] = big_d * egl_ref[hh] + _dot(qg16, do16, _TN) - _dot(w16, dvn16, _TN)

    vec, sq, dec, st = _gdn_specs(h, c, d)
    rev = lambda spec: pl.BlockSpec(spec.block_shape, lambda i: (n - 1 - i, 0, 0, 0))
    vec, sq, dec, st = rev(vec), rev(sq), rev(dec), rev(st)
    vshape = jax.ShapeDtypeStruct((n, h, c, d), F32)
    return pl.pallas_call(
        body,
        grid=(n,),
        in_specs=[vec, vec, vec, sq, vec, dec, st, vec],
        out_specs=[vec, vec, vec, sq, vec, dec],
        out_shape=[vshape, vshape, vshape, jax.ShapeDtypeStruct((n, h, c, c), F32), vshape,
                   jax.ShapeDtypeStruct((n, h, 1, d), F32)],
        scratch_shapes=[pltpu.VMEM((h, d, d), F32)],
        compiler_params=pltpu.CompilerParams(dimension_semantics=("arbitrary",)),
        name="gdn_scan_bwd",
    )(qg, u, w, attn, kt, egl, states, do)


@jax.custom_vjp
def gdn_scan(qg, u, w, attn, kt, egl):
    return _gdn_scan_fwd_call(qg, u, w, attn, kt, egl)[0]


def _gdn_scan_fwd(qg, u, w, attn, kt, egl):
    o, states = _gdn_scan_fwd_call(qg, u, w, attn, kt, egl)
    return o, (qg, u, w, attn, kt, egl, states)


def _gdn_scan_bwd(res, do):
    return tuple(_gdn_scan_bwd_call(*res, do))


gdn_scan.defvjp(_gdn_scan_fwd, _gdn_scan_bwd)


CONV_ROWS = 512
HALO = 8


def _sigmoid(x):
    return 1.0 / (1.0 + jnp.exp(-x))


def _shifted(ext, k, rows):
    if k == 0:
        return ext[HALO:HALO + rows]
    return pltpu.roll(ext, k % ext.shape[0], 0)[HALO:HALO + rows]


def _conv_specs(s, ch, tr, tc, off):
    per, last = tr // HALO, s // HALO - 1
    blk = pl.BlockSpec((tr, tc), lambda j, i: (i, j + off))
    prev = pl.BlockSpec((HALO, tc), lambda j, i: (jnp.maximum(i * per - 1, 0), j + off))
    nxt = pl.BlockSpec((HALO, tc), lambda j, i: (jnp.minimum((i + 1) * per, last), j + off))
    return blk, prev, nxt


def _dwconv_fwd_call(x, w, b, gated):
    s, ch = x.shape
    taps = w.shape[0]
    out_ch = ch // 2 if gated else ch
    tr = _pick(s, (CONV_ROWS, 256, 128, 64, 32, 16, 8))
    tc = _pick(out_ch, (1408, 512, 384, 256, 128))
    n_j = out_ch // tc
    parts = (0, n_j) if gated else (0,)

    def conv(x_ref, p_ref, w_ref, first):
        xb = x_ref[...]
        ext = jnp.concatenate([jnp.where(first, 0.0, p_ref[...]), xb], axis=0)
        y = w_ref[taps - 1:taps, :] * xb
        for k in range(1, taps):
            y = y + w_ref[taps - 1 - k:taps - k, :] * _shifted(ext, k, tr)
        return y

    def body(*refs):
        first = pl.program_id(1) == 0
        if gated:
            xa, pa, wa, ba, xb_, pb, wb, bb, o_ref = refs
            a = conv(xa, pa, wa, first) + ba[...]
            g = conv(xb_, pb, wb, first) + bb[...]
            o_ref[...] = a * _sigmoid(a) * g
        else:
            xa, pa, wa, o_ref = refs
            a = conv(xa, pa, wa, first)
            o_ref[...] = a * _sigmoid(a)

    in_specs, args = [], []
    for off in parts:
        blk, prev, _ = _conv_specs(s, ch, tr, tc, off)
        in_specs += [blk, prev, pl.BlockSpec((taps, tc), lambda j, i, off=off: (0, j + off))]
        args += [x, x, w]
        if gated:
            in_specs.append(pl.BlockSpec((1, tc), lambda j, i, off=off: (0, j + off)))
            args.append(b)
    return pl.pallas_call(
        body,
        grid=(n_j, s // tr),
        in_specs=in_specs,
        out_specs=pl.BlockSpec((tr, tc), lambda j, i: (i, j)),
        out_shape=jax.ShapeDtypeStruct((s, out_ch), F32),
        compiler_params=pltpu.CompilerParams(dimension_semantics=("parallel", "parallel"),
                                             vmem_limit_bytes=VMEM_LIMIT),
        name="dwconv_gate_fwd" if gated else "dwconv_silu_fwd",
    )(*args)


def _dwconv_bwd_call(x, w, b, do, gated):
    s, ch = x.shape
    taps = w.shape[0]
    out_ch = ch // 2 if gated else ch
    tr = _pick(s, (CONV_ROWS, 256, 128, 64, 32, 16, 8))
    tc = _pick(out_ch, (1408, 512, 384, 256, 128))
    n_j, n_i = out_ch // tc, s // tr
    parts = (0, n_j) if gated else (0,)
    ext_rows = tr + 2 * HALO

    def pre_act(x_ref, p_ref, n_ref, w_ref, first, last):
        ext = jnp.concatenate([jnp.where(first, 0.0, p_ref[...]), x_ref[...], n_ref[...]], axis=0)
        y = w_ref[taps - 1:taps, :] * ext
        for k in range(1, taps):
            y = y + w_ref[taps - 1 - k:taps - k, :] * pltpu.roll(ext, k, 0)
        return ext, y

    def grads(ext, dy, w_ref, dx_ref, dw_ref):
        dx = w_ref[taps - 1:taps, :] * dy[HALO:HALO + tr]
        for k in range(1, taps):
            dx = dx + w_ref[taps - 1 - k:taps - k, :] * _shifted(dy, -k, tr)
        dx_ref[...] = dx
        dyb = dy[HALO:HALO + tr]
        for k in range(taps):
            dw_ref[taps - 1 - k:taps - k, :] += jnp.sum(dyb * _shifted(ext, k, tr), axis=0, keepdims=True)
        return dyb

    def body(*refs):
        i = pl.program_id(1)
        first, last = i == 0, i == n_i - 1
        rows = lax.broadcasted_iota(jnp.int32, (ext_rows, 1), 0)
        inside = jnp.logical_and(rows >= HALO, jnp.logical_or(rows < HALO + tr, jnp.logical_not(last)))
        if gated:
            (xa, pa, na, wa, ba, xb_, pb, nb, wb, bb, do_ref, don_ref,
             dxa_ref, dxb_ref, dwa_ref, dwb_ref, dba_ref, dbb_ref) = refs
        else:
            xa, pa, na, wa, do_ref, don_ref, dxa_ref, dwa_ref = refs

        @pl.when(first)
        def _():
            dwa_ref[...] = jnp.zeros_like(dwa_ref)
            if gated:
                dwb_ref[...] = jnp.zeros_like(dwb_ref)
                dba_ref[...] = jnp.zeros_like(dba_ref)
                dbb_ref[...] = jnp.zeros_like(dbb_ref)

        d_out = jnp.concatenate([jnp.zeros((HALO, tc), F32), do_ref[...], don_ref[...]], axis=0)
        d_out = jnp.where(inside, d_out, 0.0)
        ext_a, a = pre_act(xa, pa, na, wa, first, last)
        if gated:
            a = a + ba[...]
            ext_b, g = pre_act(xb_, pb, nb, wb, first, last)
            g = g + bb[...]
            sg = _sigmoid(a)
            silu = a * sg
            dya = jnp.where(inside, d_out * g * (sg + silu * (1.0 - sg)), 0.0)
            dyg = jnp.where(inside, d_out * silu, 0.0)
            dba_ref[...] += jnp.sum(grads(ext_a, dya, wa, dxa_ref, dwa_ref), axis=0, keepdims=True)
            dbb_ref[...] += jnp.sum(grads(ext_b, dyg, wb, dxb_ref, dwb_ref), axis=0, keepdims=True)
        else:
            sg = _sigmoid(a)
            dya = jnp.where(inside, d_out * (sg + a * sg * (1.0 - sg)), 0.0)
            grads(ext_a, dya, wa, dxa_ref, dwa_ref)

    in_specs, args = [], []
    for off in parts:
        blk, prev, nxt = _conv_specs(s, ch, tr, tc, off)
        in_specs += [blk, prev, nxt, pl.BlockSpec((taps, tc), lambda j, i, off=off: (0, j + off))]
        args += [x, x, x, w]
        if gated:
            in_specs.append(pl.BlockSpec((1, tc), lambda j, i, off=off: (0, j + off)))
            args.append(b)
    blk, _, nxt = _conv_specs(s, out_ch, tr, tc, 0)
    in_specs += [blk, nxt]
    args += [do, do]
    n_half = len(parts)
    out_specs = ([blk] * n_half + [pl.BlockSpec((taps, tc), lambda j, i: (0, j))] * n_half
                 + ([pl.BlockSpec((1, tc), lambda j, i: (0, j))] * n_half if gated else []))
    out_shape = ([jax.ShapeDtypeStruct((s, out_ch), F32)] * n_half + [jax.ShapeDtypeStruct((taps, out_ch), F32)] * n_half
                 + ([jax.ShapeDtypeStruct((1, out_ch), F32)] * n_half if gated else []))
    return pl.pallas_call(
        body,
        grid=(n_j, n_i),
        in_specs=in_specs,
        out_specs=out_specs,
        out_shape=out_shape,
        compiler_params=pltpu.CompilerParams(dimension_semantics=("parallel", "arbitrary"),
                                             vmem_limit_bytes=VMEM_LIMIT),
        name="dwconv_gate_bwd" if gated else "dwconv_silu_bwd",
    )(*args)


@jax.custom_vjp
def conv_gate(u, w, b):
    return _dwconv_fwd_call(u, w, b[None], True)


def _conv_gate_fwd(u, w, b):
    return _dwconv_fwd_call(u, w, b[None], True), (u, w, b)


def _conv_gate_bwd(res, do):
    u, w, b = res
    dxa, dxb, dwa, dwb, dba, dbb = _dwconv_bwd_call(u, w, b[None], do, True)
    return (jnp.concatenate([dxa, dxb], axis=1), jnp.concatenate([dwa, dwb], axis=1),
            jnp.concatenate([dba, dbb], axis=1)[0])


conv_gate.defvjp(_conv_gate_fwd, _conv_gate_bwd)


@jax.custom_vjp
def conv_silu(x, w):
    return _dwconv_fwd_call(x, w, None, False)


def _conv_silu_fwd(x, w):
    return _dwconv_fwd_call(x, w, None, False), (x, w)


def _conv_silu_bwd(res, do):
    x, w = res
    dx, dw = _dwconv_bwd_call(x, w, None, do, False)
    return dx, dw


conv_silu.defvjp(_conv_silu_fwd, _conv_silu_bwd)


ROW_TILE = 256


def _rmsnorm_fwd_call(x, g):
    s, d = x.shape
    tr = _pick(s, (ROW_TILE, 128, 64, 32, 16, 8))

    def body(x_ref, g_ref, o_ref):
        xb = x_ref[...]
        r = lax.rsqrt(jnp.mean(xb * xb, axis=-1, keepdims=True) + EPS)
        o_ref[...] = (xb * r) * g_ref[...]

    return pl.pallas_call(
        body,
        grid=(s // tr,),
        in_specs=[pl.BlockSpec((tr, d), lambda i: (i, 0)), pl.BlockSpec((1, d), lambda i: (0, 0))],
        out_specs=pl.BlockSpec((tr, d), lambda i: (i, 0)),
        out_shape=jax.ShapeDtypeStruct((s, d), F32),
        compiler_params=pltpu.CompilerParams(dimension_semantics=("parallel",)),
        name="rmsnorm_fwd",
    )(x, g)


def _rmsnorm_bwd_call(x, g, dy):
    s, d = x.shape
    tr = _pick(s, (ROW_TILE, 128, 64, 32, 16, 8))

    def body(x_ref, g_ref, dy_ref, dx_ref, dg_ref):
        @pl.when(pl.program_id(0) == 0)
        def _():
            dg_ref[...] = jnp.zeros_like(dg_ref)

        xb = x_ref[...]
        r = lax.rsqrt(jnp.mean(xb * xb, axis=-1, keepdims=True) + EPS)
        y = xb * r
        dyb = dy_ref[...]
        dg_ref[...] += jnp.sum(dyb * y, axis=0, keepdims=True)
        dn = dyb * g_ref[...]
        dx_ref[...] = r * (dn - y * jnp.mean(dn * y, axis=-1, keepdims=True))

    row = pl.BlockSpec((tr, d), lambda i: (i, 0))
    vec = pl.BlockSpec((1, d), lambda i: (0, 0))
    return pl.pallas_call(
        body,
        grid=(s // tr,),
        in_specs=[row, vec, row],
        out_specs=[row, vec],
        out_shape=[jax.ShapeDtypeStruct((s, d), F32), jax.ShapeDtypeStruct((1, d), F32)],
        compiler_params=pltpu.CompilerParams(dimension_semantics=("arbitrary",)),
        name="rmsnorm_bwd",
    )(x, g, dy)


@jax.custom_vjp
def rmsnorm_rows(x, g):
    return _rmsnorm_fwd_call(x, g[None])


def _rmsnorm_rows_fwd(x, g):
    return _rmsnorm_fwd_call(x, g[None]), (x, g)


def _rmsnorm_rows_bwd(res, dy):
    x, g = res
    dx, dg = _rmsnorm_bwd_call(x, g[None], dy)
    return dx, dg[0]


rmsnorm_rows.defvjp(_rmsnorm_rows_fwd, _rmsnorm_rows_bwd)


def _merge_specs(s, d, tr):
    gate = pl.BlockSpec((tr, N_BRANCH * d), lambda i: (i, 0))
    bias = pl.BlockSpec((1, N_BRANCH * d), lambda i: (0, 0))
    row = pl.BlockSpec((tr, d), lambda i: (i, 0))
    return gate, bias, row


def _merge_fwd_call(gates, bias, ta, tb, tc):
    s, d = ta.shape
    tr = _pick(s, (ROW_TILE, 128, 64, 32, 16, 8))

    def body(g_ref, b_ref, ta_ref, tb_ref, tc_ref, o_ref):
        acc = None
        for b, t_ref in enumerate((ta_ref, tb_ref, tc_ref)):
            gate = _sigmoid(g_ref[:, b * d:(b + 1) * d] + b_ref[:, b * d:(b + 1) * d])
            acc = gate * t_ref[...] if acc is None else acc + gate * t_ref[...]
        o_ref[...] = acc

    gate, bias_s, row = _merge_specs(s, d, tr)
    return pl.pallas_call(
        body,
        grid=(s // tr,),
        in_specs=[gate, bias_s, row, row, row],
        out_specs=row,
        out_shape=jax.ShapeDtypeStruct((s, d), F32),
        compiler_params=pltpu.CompilerParams(dimension_semantics=("parallel",), vmem_limit_bytes=VMEM_LIMIT),
        name="merge_fwd",
    )(gates, bias, ta, tb, tc)


def _merge_bwd_call(gates, bias, ta, tb, tc, dm):
    s, d = ta.shape
    tr = _pick(s, (ROW_TILE, 128, 64, 32, 16, 8))

    def body(g_ref, b_ref, ta_ref, tb_ref, tc_ref, dm_ref, dg_ref, db_ref, dta_ref, dtb_ref, dtc_ref):
        @pl.when(pl.program_id(0) == 0)
        def _():
            db_ref[...] = jnp.zeros_like(db_ref)

        dmb = dm_ref[...]
        for b, (t_ref, dt_ref) in enumerate(((ta_ref, dta_ref), (tb_ref, dtb_ref), (tc_ref, dtc_ref))):
            cols = slice(b * d, (b + 1) * d)
            gate = _sigmoid(g_ref[:, cols] + b_ref[:, cols])
            dt_ref[...] = gate * dmb
            dpre = dmb * t_ref[...] * (gate * (1.0 - gate))
            dg_ref[:, cols] = dpre
            db_ref[:, cols] += jnp.sum(dpre, axis=0, keepdims=True)

    gate, bias_s, row = _merge_specs(s, d, tr)
    rows = jax.ShapeDtypeStruct((s, d), F32)
    return pl.pallas_call(
        body,
        grid=(s // tr,),
        in_specs=[gate, bias_s, row, row, row, row],
        out_specs=[gate, bias_s, row, row, row],
        out_shape=[jax.ShapeDtypeStruct((s, N_BRANCH * d), F32), jax.ShapeDtypeStruct((1, N_BRANCH * d), F32), rows, rows, rows],
        compiler_params=pltpu.CompilerParams(dimension_semantics=("arbitrary",), vmem_limit_bytes=VMEM_LIMIT),
        name="merge_bwd",
    )(gates, bias, ta, tb, tc, dm)


@jax.custom_vjp
def gated_merge(gates, bias, ta, tb, tc):
    return _merge_fwd_call(gates, bias[None], ta, tb, tc)


def _gated_merge_fwd(gates, bias, ta, tb, tc):
    return _merge_fwd_call(gates, bias[None], ta, tb, tc), (gates, bias, ta, tb, tc)


def _gated_merge_bwd(res, dm):
    gates, bias, ta, tb, tc = res
    dg, db, dta, dtb, dtc = _merge_bwd_call(gates, bias[None], ta, tb, tc, dm)
    return dg, db[0], dta, dtb, dtc


gated_merge.defvjp(_gated_merge_fwd, _gated_merge_bwd)


HEAD_ROWS = 512


def _head_stats(xs, kind):
    sq = xs * xs
    ms = jnp.sum(sq, axis=-1, keepdims=True) if kind == 'l2' else jnp.mean(sq, axis=-1, keepdims=True)
    return lax.rsqrt(ms + EPS)


def _headnorm_fwd_call(x, g, z, kind):
    s, w = x.shape
    tr = _pick(s, (HEAD_ROWS, 256, 128, 64, 32, 16, 8))

    def body(*refs):
        x_ref, o_ref = refs[0], refs[-1]
        for h in range(w // HEAD_DIM):
            cols = slice(h * HEAD_DIM, (h + 1) * HEAD_DIM)
            xs = x_ref[:, cols]
            y = xs * _head_stats(xs, kind)
            if kind != 'l2':
                y = y * refs[1][...]
            if kind == 'rms_gate':
                zs = refs[2][:, cols]
                y = y * (zs * _sigmoid(zs))
            o_ref[:, cols] = y

    row = pl.BlockSpec((tr, w), lambda i: (i, 0))
    vec = pl.BlockSpec((1, HEAD_DIM), lambda i: (0, 0))
    in_specs, args = [row], [x]
    if kind != 'l2':
        in_specs.append(vec); args.append(g)
    if kind == 'rms_gate':
        in_specs.append(row); args.append(z)
    return pl.pallas_call(
        body,
        grid=(s // tr,),
        in_specs=in_specs,
        out_specs=row,
        out_shape=jax.ShapeDtypeStruct((s, w), F32),
        compiler_params=pltpu.CompilerParams(dimension_semantics=("parallel",)),
        name=f"headnorm_{kind}_fwd",
    )(*args)


def _headnorm_bwd_call(x, g, z, dy, kind):
    s, w = x.shape
    tr = _pick(s, (HEAD_ROWS, 256, 128, 64, 32, 16, 8))
    gain, gated = kind != 'l2', kind == 'rms_gate'

    def body(*refs):
        n_in = 2 + gain + gated
        x_ref, dy_ref = refs[0], refs[n_in - 1]
        outs = refs[n_in:]
        dx_ref = outs[0]
        if gain:
            g_ref, dg_ref = refs[1], outs[1]

            @pl.when(pl.program_id(0) == 0)
            def _():
                dg_ref[...] = jnp.zeros_like(dg_ref)

        for h in range(w // HEAD_DIM):
            cols = slice(h * HEAD_DIM, (h + 1) * HEAD_DIM)
            xs = x_ref[:, cols]
            r = _head_stats(xs, kind)
            y = xs * r
            dn = dy_ref[:, cols]
            if gated:
                zs = refs[2][:, cols]
                sg = _sigmoid(zs)
                silu = zs * sg
                outs[2][:, cols] = dn * (y * g_ref[...]) * (sg + silu * (1.0 - sg))
                dn = dn * silu
            if gain:
                dg_ref[...] += jnp.sum(dn * y, axis=0, keepdims=True)
                dn = dn * g_ref[...]
            proj = jnp.sum(dn * y, axis=-1, keepdims=True)
            if kind != 'l2':
                proj = proj / HEAD_DIM
            dx_ref[:, cols] = r * (dn - y * proj)

    row = pl.BlockSpec((tr, w), lambda i: (i, 0))
    vec = pl.BlockSpec((1, HEAD_DIM), lambda i: (0, 0))
    rows, vecs = jax.ShapeDtypeStruct((s, w), F32), jax.ShapeDtypeStruct((1, HEAD_DIM), F32)
    in_specs, args = [row], [x]
    if gain:
        in_specs.append(vec); args.append(g)
    if gated:
        in_specs.append(row); args.append(z)
    in_specs.append(row); args.append(dy)
    out_specs, out_shape = [row], [rows]
    if gain:
        out_specs.append(vec); out_shape.append(vecs)
    if gated:
        out_specs.append(row); out_shape.append(rows)
    return pl.pallas_call(
        body,
        grid=(s // tr,),
        in_specs=in_specs,
        out_specs=out_specs,
        out_shape=out_shape,
        compiler_params=pltpu.CompilerParams(dimension_semantics=("arbitrary",) if gain else ("parallel",)),
        name=f"headnorm_{kind}_bwd",
    )(*args)


@jax.custom_vjp
def head_rms(x, g):
    return _headnorm_fwd_call(x, g[None], None, 'rms')


def _head_rms_fwd(x, g):
    return _headnorm_fwd_call(x, g[None], None, 'rms'), (x, g)


def _head_rms_bwd(res, dy):
    x, g = res
    dx, dg = _headnorm_bwd_call(x, g[None], None, dy, 'rms')
    return dx, dg[0]


head_rms.defvjp(_head_rms_fwd, _head_rms_bwd)


@jax.custom_vjp
def head_l2(x):
    return _headnorm_fwd_call(x, None, None, 'l2')


def _head_l2_fwd(x):
    return _headnorm_fwd_call(x, None, None, 'l2'), (x,)


def _head_l2_bwd(res, dy):
    return tuple(_headnorm_bwd_call(res[0], None, None, dy, 'l2'))


head_l2.defvjp(_head_l2_fwd, _head_l2_bwd)


@jax.custom_vjp
def head_rms_gate(x, g, z):
    return _headnorm_fwd_call(x, g[None], z, 'rms_gate')


def _head_rms_gate_fwd(x, g, z):
    return _headnorm_fwd_call(x, g[None], z, 'rms_gate'), (x, g, z)


def _head_rms_gate_bwd(res, dy):
    x, g, z = res
    dx, dg, dz = _headnorm_bwd_call(x, g[None], z, dy, 'rms_gate')
    return dx, dg[0], dz


head_rms_gate.defvjp(_head_rms_gate_fwd, _head_rms_gate_bwd)


def _flat_rows(rows):
    return _pick(rows, (FLAT_TILE_ROWS, 1024, 512, 256, 128, 64, 32, 16, 8))


def _add_own_half(g, a, core):
    _, nb, rows, _ = g.shape
    tr = _flat_rows(rows)

    def body(c_ref, g_ref, a_ref, o_ref):
        o_ref[...] = g_ref[...] + a_ref[...]

    return pl.pallas_call(
        body,
        grid_spec=pltpu.PrefetchScalarGridSpec(
            num_scalar_prefetch=1,
            grid=(nb, rows // tr),
            in_specs=[pl.BlockSpec((None, None, tr, LANES), lambda j, r, c_ref: (c_ref[0], j, r, 0)),
                      pl.BlockSpec((None, tr, LANES), lambda j, r, c_ref: (j, r, 0))],
            out_specs=pl.BlockSpec((None, tr, LANES), lambda j, r, c_ref: (j, r, 0)),
        ),
        out_shape=jax.ShapeDtypeStruct((nb, rows, LANES), F32),
        compiler_params=pltpu.CompilerParams(dimension_semantics=("parallel", "parallel")),
        name="rs_add_own_half",
    )(core, g, a)


def _sum_chips(b, core):
    nb, rows, _ = b.shape
    tr = _flat_rows(rows)

    def body(c_ref, b_ref, o_ref):
        acc = b_ref[0] + b_ref[1]
        for j in range(2, nb):
            acc = acc + b_ref[j]
        o_ref[...] = acc

    return pl.pallas_call(
        body,
        grid_spec=pltpu.PrefetchScalarGridSpec(
            num_scalar_prefetch=1,
            grid=(rows // tr,),
            in_specs=[pl.BlockSpec((nb, tr, LANES), lambda r, c_ref: (0, r, 0))],
            out_specs=pl.BlockSpec((None, tr, LANES), lambda r, c_ref: (c_ref[0], r, 0)),
        ),
        out_shape=jax.ShapeDtypeStruct((2, rows, LANES), F32),
        compiler_params=pltpu.CompilerParams(dimension_semantics=("parallel",)),
        name="rs_sum_chips",
    )(core, b)


ADAM_BLOCK_BYTES = 2 * 1024 * 1024


def _adamw(w, g, m, v, tag):
    rows, cols = w.shape
    tr = rows
    for cand in range(8, rows, 8):
        if rows % cand == 0 and cand * cols * 4 <= ADAM_BLOCK_BYTES:
            tr = cand

    def body(w_ref, g_ref, m_ref, v_ref, d_ref, nm_ref, nv_ref):
        gg = g_ref[...]
        nm = ADAM_B1 * m_ref[...] + (1.0 - ADAM_B1) * gg
        nv = ADAM_B2 * v_ref[...] + (1.0 - ADAM_B2) * jnp.square(gg)
        m_hat = nm / (1.0 - ADAM_B1 ** ADAM_STEP)
        v_hat = nv / (1.0 - ADAM_B2 ** ADAM_STEP)
        d_ref[...] = -ADAM_LR * (m_hat / (jnp.sqrt(v_hat) + ADAM_EPS) + ADAM_WD * w_ref[...])
        nm_ref[...] = nm
        nv_ref[...] = nv

    spec = pl.BlockSpec((tr, cols), lambda r: (r, 0))
    shape = jax.ShapeDtypeStruct((rows, cols), F32)
    return pl.pallas_call(
        body,
        grid=(rows // tr,),
        in_specs=[spec] * 4,
        out_specs=[spec] * 3,
        out_shape=[shape] * 3,
        compiler_params=pltpu.CompilerParams(dimension_semantics=("parallel",), vmem_limit_bytes=VMEM_LIMIT),
        name=f"adamw_{tag}",
    )(w, g, m, v)


def _loss_head(y, t):
    s, d = y.shape
    tr = _pick(s, (512, 256, 128, 64, 32, 16, 8))

    def body(y_ref, t_ref, dy_ref, l_ref):
        @pl.when(pl.program_id(0) == 0)
        def _():
            l_ref[...] = jnp.zeros_like(l_ref)

        diff = y_ref[...] - t_ref[...]
        dy_ref[...] = diff / d
        row = jnp.mean(jnp.square(diff), axis=-1, keepdims=True)
        l_ref[...] += 0.5 * jnp.sum(row, axis=0, keepdims=True)

    return pl.pallas_call(
        body,
        grid=(s // tr,),
        in_specs=[pl.BlockSpec((tr, d), lambda r: (r, 0))] * 2,
        out_specs=[pl.BlockSpec((tr, d), lambda r: (r, 0)), pl.BlockSpec((1, LANES), lambda r: (0, 0))],
        out_shape=[jax.ShapeDtypeStruct((s, d), F32), jax.ShapeDtypeStruct((1, LANES), F32)],
        compiler_params=pltpu.CompilerParams(dimension_semantics=("arbitrary",)),
        name="loss_head",
    )(y, t)


def _place():
    x, y, c = lax.axis_index("x"), lax.axis_index("y"), lax.axis_index("c")
    chips = [(1 - x, y), (x, 1 - y), (1 - x, 1 - y)]
    return x, y, c, chips


def _all_gather_chips(flat):
    _, rows, _ = flat.shape

    def body(x_ref, o_ref, send_sems, recv_sems):
        x, y, c, chips = _place()
        me = 2 * x + y
        sib = (x, y, 1 - c)

        def remote(k, src, dst, to):
            return pltpu.make_async_remote_copy(src_ref=src, dst_ref=dst, send_sem=send_sems.at[k],
                                                recv_sem=recv_sems.at[k], device_id=to, device_id_type=MESH)

        first = [remote(k, x_ref.at[c], o_ref.at[me, c], (px, py, c)) for k, (px, py) in enumerate(chips)]
        for cp in first:
            cp.start()
        passed = []
        for k, (px, py) in enumerate(chips):
            blk = o_ref.at[2 * px + py, c]
            remote(k, x_ref.at[c], blk, (px, py, c)).wait_recv()
            cp = remote(3 + k, blk, blk, sib)
            cp.start()
            passed.append(cp)
        for k, (px, py) in enumerate(chips):
            blk = o_ref.at[2 * px + py, 1 - c]
            remote(3 + k, blk, blk, sib).wait_recv()
        for cp in first + passed:
            cp.wait_send()

    return pl.pallas_call(
        body,
        in_specs=[ANY],
        out_specs=ANY,
        out_shape=jax.ShapeDtypeStruct((N_CHIPS, 2, rows, LANES), flat.dtype),
        scratch_shapes=[pltpu.SemaphoreType.DMA((6,)), pltpu.SemaphoreType.DMA((6,))],
        name=f"all_gather_chips_{jnp.dtype(flat.dtype).name}",
    )(flat)


def _rs_sibling_exchange(g):
    _, nb, rows, _ = g.shape

    def body(g_ref, a_ref, send_sem, recv_sem):
        x, y, c, _ = _place()
        cp = pltpu.make_async_remote_copy(src_ref=g_ref.at[1 - c], dst_ref=a_ref, send_sem=send_sem,
                                          recv_sem=recv_sem, device_id=(x, y, 1 - c), device_id_type=MESH)
        cp.start()
        cp.wait()

    return pl.pallas_call(
        body,
        in_specs=[ANY],
        out_specs=ANY,
        out_shape=jax.ShapeDtypeStruct((nb, rows, LANES), F32),
        scratch_shapes=[pltpu.SemaphoreType.DMA, pltpu.SemaphoreType.DMA],
        name="rs_sibling_exchange",
    )(g)


def _rs_chip_exchange(p):
    nb, rows, _ = p.shape

    def body(p_ref, b_ref, send_sems, recv_sems, local_sem):
        x, y, c, chips = _place()
        me = 2 * x + y
        mine = pltpu.make_async_copy(p_ref.at[me], b_ref.at[me], local_sem)
        mine.start()
        copies = [pltpu.make_async_remote_copy(src_ref=p_ref.at[2 * px + py], dst_ref=b_ref.at[me],
                                               send_sem=send_sems.at[k], recv_sem=recv_sems.at[k],
                                               device_id=(px, py, c), device_id_type=MESH)
                  for k, (px, py) in enumerate(chips)]
        for cp in copies:
            cp.start()
        for cp in copies:
            cp.wait()
        mine.wait()

    return pl.pallas_call(
        body,
        in_specs=[ANY],
        out_specs=ANY,
        out_shape=jax.ShapeDtypeStruct((nb, rows, LANES), F32),
        scratch_shapes=[pltpu.SemaphoreType.DMA((3,)), pltpu.SemaphoreType.DMA((3,)), pltpu.SemaphoreType.DMA],
        name="rs_chip_exchange",
    )(p)


def _rs_sibling_gather(r):
    _, rows, _ = r.shape

    def body(r_ref, o_ref, send_sem, recv_sem):
        x, y, c, _ = _place()
        cp = pltpu.make_async_remote_copy(src_ref=o_ref.at[c], dst_ref=o_ref.at[c], send_sem=send_sem,
                                          recv_sem=recv_sem, device_id=(x, y, 1 - c), device_id_type=MESH)
        cp.start()
        cp.wait()

    return pl.pallas_call(
        body,
        in_specs=[ANY],
        out_specs=ANY,
        out_shape=jax.ShapeDtypeStruct((2, rows, LANES), F32),
        input_output_aliases={0: 0},
        scratch_shapes=[pltpu.SemaphoreType.DMA, pltpu.SemaphoreType.DMA],
        name="rs_sibling_gather",
    )(r)


def _reduce_scatter(g, core):
    a = _rs_sibling_exchange(g)
    p = _add_own_half(g, a, core)
    b = _rs_chip_exchange(p)
    return _rs_sibling_gather(_sum_chips(b, core))


def _all_reduce_small(v):
    rows, _ = v.shape
    n_dev = 8

    def body(v_ref, o_ref, gath, send_sems, recv_sems):
        x, y, c, _ = _place()
        me = 4 * x + 2 * y + c
        gath[me] = v_ref[...]
        copies = []
        for mask in range(1, n_dev):
            px = 1 - x if mask & 4 else x
            py = 1 - y if mask & 2 else y
            pc = 1 - c if mask & 1 else c
            copies.append(pltpu.make_async_remote_copy(
                src_ref=v_ref, dst_ref=gath.at[me], send_sem=send_sems.at[mask - 1],
                recv_sem=recv_sems.at[mask - 1], device_id=(px, py, pc), device_id_type=MESH))
        for cp in copies:
            cp.start()
        for cp in copies:
            cp.wait()
        acc = gath[0]
        for k in range(1, n_dev):
            acc = acc + gath[k]
        o_ref[...] = acc

    return pl.pallas_call(
        body,
        in_specs=[VMEM],
        out_specs=VMEM,
        out_shape=jax.ShapeDtypeStruct((rows, LANES), F32),
        scratch_shapes=[pltpu.VMEM((n_dev, rows, LANES), F32), pltpu.SemaphoreType.DMA((n_dev - 1,)),
                        pltpu.SemaphoreType.DMA((n_dev - 1,))],
        name="all_reduce_small",
    )(v)


def _flat_len(shapes, unit_rows=FLAT_TILE_ROWS):
    n = sum(int(np.prod(s)) for s in shapes)
    unit = 2 * unit_rows * LANES
    return -(-n // unit) * unit


def _pack(arrays, total):
    flat = jnp.concatenate([a.reshape(-1) for a in arrays])
    return jnp.pad(flat, (0, total - flat.shape[0]))


def _unpack(flat, shapes):
    out, off = [], 0
    for s in shapes:
        n = int(np.prod(s))
        out.append(flat[off:off + n].reshape(s))
        off += n
    return out


def _permute_w_in(w):
    pad = jnp.zeros(w.shape[:-1] + (N_IN_PAD - N_IN,), w.dtype)
    return jnp.concatenate([w[..., 0:1536], w[..., 1540:3076], w[..., 3084:3596], w[..., 3596:5132],
                            w[..., 5132:8204], w[..., 1536:1540], w[..., 3076:3080], w[..., 3080:3084], pad],
                           axis=-1)


def _unpermute_w_in(w):
    return jnp.concatenate([w[..., 0:1536], w[..., 8192:8196], w[..., 1536:3072], w[..., 8196:8200],
                            w[..., 8200:8204], w[..., 3072:3584], w[..., 3584:5120], w[..., 5120:8192]], axis=-1)


PROJ_WIDTHS = (512, 512, 512, 1536, 512, 512, 512, 512, 3072, 4, 4, 4)


@jax.custom_vjp
def split_proj(proj):
    offs = np.cumsum((0,) + PROJ_WIDTHS)
    return tuple(proj[:, o:o + wd] for o, wd in zip(offs, PROJ_WIDTHS))


def _split_proj_fwd(proj):
    return split_proj(proj), None


def _split_proj_bwd(_, cts):
    pad = jnp.zeros((cts[0].shape[0], N_IN_PAD - sum(PROJ_WIDTHS)), F32)
    return (jnp.concatenate(list(cts) + [pad], axis=1),)


split_proj.defvjp(_split_proj_fwd, _split_proj_bwd)


def heads(x, n):
    return x.reshape(x.shape[:-1] + (n, -1))


def gated_delta_rule(q, k, v, g, beta):
    B, T, H, dk = q.shape
    dv = v.shape[-1]
    N = T // CHUNK

    def chunks(a):
        a = a.astype(F32).reshape((B, N, CHUNK, H) + a.shape[3:])
        return jnp.moveaxis(a, (1, 3), (0, 2))

    qc = chunks(q) * dk ** -0.5
    kc = chunks(k)
    vc = chunks(v)
    bc = chunks(beta)
    gc = jnp.cumsum(chunks(g), axis=-1)
    idx = jnp.arange(CHUNK)
    causal = idx[:, None] >= idx[None, :]
    strict = idx[:, None] > idx[None, :]
    decay = jnp.exp(jnp.where(causal, gc[..., :, None] - gc[..., None, :], -jnp.inf))
    kk = jnp.einsum("nbhcd,nbhed->nbhce", kc, kc)
    a_mat = jnp.where(strict, bc[..., :, None] * kk * decay, 0.0) + jnp.eye(CHUNK, dtype=F32)
    rhs = jnp.concatenate([vc * bc[..., None], kc * (bc * jnp.exp(gc))[..., None]], axis=-1)
    sol = lax.linalg.triangular_solve(a_mat, rhs, left_side=True, lower=True)
    u, w = sol[..., :dv], sol[..., dv:]
    attn = jnp.where(causal, jnp.einsum("nbhcd,nbhed->nbhce", qc, kc) * decay, 0.0)
    g_last = gc[..., -1]
    k_tail = kc * jnp.exp(g_last[..., None] - gc)[..., None]

    egl = jnp.broadcast_to(jnp.exp(g_last)[:, 0, :, None, None], (N, H, 1, dv))
    o = gdn_scan((qc * jnp.exp(gc)[..., None])[:, 0], u[:, 0], w[:, 0], attn[:, 0], k_tail[:, 0], egl)
    return o.transpose(0, 2, 1, 3).reshape(B, T, H, dv)


def _layer(x, mem, p, p16):
    S, D = x.shape
    h = rmsnorm_rows(x, p['norm_mix'])
    def mm(a, name):
        return matmul(a, p[name], p16[name])

    fq, fk, fv, gqkv, gz, sq, sk, sv, gates, ff, gb, ga = split_proj(mm(h, 'w_in'))

    logf = jax.nn.log_sigmoid((ff + p['fox_fbias']).astype(F32))
    ya = fox_core(head_rms(fq, p['fox_qnorm']), head_rms(fk, p['fox_knorm']), fv, jnp.cumsum(logf, axis=0))

    qkv = conv_silu(gqkv, p['gdn_conv'])
    cq, ck, cv = qkv[:, :512], qkv[:, 512:1024], qkv[:, 1024:]
    beta = jax.nn.sigmoid(gb.astype(F32))
    g_log = -jnp.exp(p['gdn_a_log'].astype(F32)) * jax.nn.softplus((ga + p['gdn_dt_bias']).astype(F32))
    o = gated_delta_rule(heads(head_l2(cq), HEADS)[None], heads(head_l2(ck), HEADS)[None], heads(cv, HEADS)[None],
                         g_log[None], beta[None])
    yb = head_rms_gate(o.reshape(S, 512), p['gdn_onorm'], gz)

    yc = sb_core(sq, sk, sv)

    mixed = gated_merge(gates, p['gate_bias'], mm(ya, 'w_oa'), mm(yb, 'w_ob'), mm(yc, 'w_oc'))
    x = x + mm(mixed, 'w_out')

    hq = rmsnorm_rows(x, p['norm_xq'])
    hm = rmsnorm_rows(mem, p['norm_mem'])
    q = heads(head_rms(mm(hq, 'w_mq'), p['mq_norm']), HEADS)
    kv = mm(hm, 'w_mkv')
    k, v = jnp.split(kv, 2, axis=-1)
    k = heads(head_rms(k, p['mk_norm']), HEADS)
    v = heads(v, HEADS)
    s = jnp.einsum("qhd,khd->hqk", q, k).astype(F32) * MEM_DIM ** -0.5
    pr = jax.nn.softmax(s, axis=-1).astype(v.dtype)
    om = jnp.einsum("hqk,khd->qhd", pr, v).reshape(S, 512)
    x = x + mm(om, 'w_mo')

    hf = rmsnorm_rows(x, p['norm_ffn'])
    act = conv_gate(mm(hf, 'w_up'), p['ffn_conv'], p['ffn_conv_b'])
    return x + mm(act, 'w_down')


def kernel(x, mem, norm_mix, w_in, fox_fbias, fox_qnorm, fox_knorm, gdn_conv, gdn_a_log, gdn_dt_bias, gdn_onorm, gate_bias, w_oa, w_ob, w_oc, w_out, norm_xq, norm_mem, w_mq, w_mkv, mq_norm, mk_norm, w_mo, norm_ffn, w_up, ffn_conv, ffn_conv_b, w_down, loss_target, m_norm_mix, m_w_in, m_fox_fbias, m_fox_qnorm, m_fox_knorm, m_gdn_conv, m_gdn_a_log, m_gdn_dt_bias, m_gdn_onorm, m_gate_bias, m_w_oa, m_w_ob, m_w_oc, m_w_out, m_norm_xq, m_norm_mem, m_w_mq, m_w_mkv, m_mq_norm, m_mk_norm, m_w_mo, m_norm_ffn, m_w_up, m_ffn_conv, m_ffn_conv_b, m_w_down, v_norm_mix, v_w_in, v_fox_fbias, v_fox_qnorm, v_fox_knorm, v_gdn_conv, v_gdn_a_log, v_gdn_dt_bias, v_gdn_onorm, v_gate_bias, v_w_oa, v_w_ob, v_w_oc, v_w_out, v_norm_xq, v_norm_mem, v_w_mq, v_w_mkv, v_mq_norm, v_mk_norm, v_w_mo, v_norm_ffn, v_w_up, v_ffn_conv, v_ffn_conv_b, v_w_down):
    args = (x, mem, norm_mix, w_in, fox_fbias, fox_qnorm, fox_knorm, gdn_conv, gdn_a_log, gdn_dt_bias, gdn_onorm, gate_bias, w_oa, w_ob, w_oc, w_out, norm_xq, norm_mem, w_mq, w_mkv, mq_norm, mk_norm, w_mo, norm_ffn, w_up, ffn_conv, ffn_conv_b, w_down)
    moments_m = (m_norm_mix, m_w_in, m_fox_fbias, m_fox_qnorm, m_fox_knorm, m_gdn_conv, m_gdn_a_log, m_gdn_dt_bias, m_gdn_onorm, m_gate_bias, m_w_oa, m_w_ob, m_w_oc, m_w_out, m_norm_xq, m_norm_mem, m_w_mq, m_w_mkv, m_mq_norm, m_mk_norm, m_w_mo, m_norm_ffn, m_w_up, m_ffn_conv, m_ffn_conv_b, m_w_down)
    moments_v = (v_norm_mix, v_w_in, v_fox_fbias, v_fox_qnorm, v_fox_knorm, v_gdn_conv, v_gdn_a_log, v_gdn_dt_bias, v_gdn_onorm, v_gate_bias, v_w_oa, v_w_ob, v_w_oc, v_w_out, v_norm_xq, v_norm_mem, v_w_mq, v_w_mkv, v_mq_norm, v_mk_norm, v_w_mo, v_norm_ffn, v_w_up, v_ffn_conv, v_ffn_conv_b, v_w_down)
    w = dict(zip(IN_NAMES, args))
    m = dict(zip(WEIGHTS, moments_m))
    v = dict(zip(WEIGHTS, moments_v))
    xs, mems, tgt = x[0], mem[0], loss_target[0]
    core = lax.axis_index("c").astype(jnp.int32).reshape(1)

    big = list(SHARDED)
    shard_shapes = [w[n].shape for n in big]
    total = _flat_len(shard_shapes)
    half_rows = total // (2 * LANES)
    small_shapes = [w[n].shape for n in SMALL]
    n_small = sum(int(np.prod(s)) for s in small_shapes) + 1
    small_total = -(-n_small // (8 * LANES)) * (8 * LANES)

    my_chip = 2 * lax.axis_index("x") + lax.axis_index("y")

    def gather(names, dtype, unit_rows):
        shapes = [w[n].shape for n in names]
        tot = _flat_len(shapes, unit_rows)
        flat = _pack([w[n].astype(dtype) for n in names], tot)
        got = _all_gather_chips(flat.reshape(2, tot // (2 * LANES), LANES)).reshape(N_CHIPS, tot)
        out = {}
        for n, blocks in zip(names, zip(*[_unpack(got[j], shapes) for j in range(N_CHIPS)])):
            own = w[n].astype(dtype)
            out[n] = jnp.concatenate([jnp.where(my_chip == j, own, b) for j, b in enumerate(blocks)], axis=SHARDED[n])
        return out

    conv_names = ['gdn_conv', 'ffn_conv']
    params16 = gather([n for n in big if n not in conv_names], BF16, FLAT_TILE_ROWS)
    params16['w_in'] = _permute_w_in(params16['w_in'])
    params = {n: a.astype(F32) for n, a in params16.items()}
    params.update(gather(conv_names, F32, 8))
    for n in SMALL:
        params[n] = w[n]

    def model(x0, pp):
        for layer in range(DEPTH):
            x0 = _layer(x0, mems, {n: a[layer] for n, a in pp.items()}, {n: a[layer] for n, a in params16.items()})
        return x0

    y, model_vjp = jax.vjp(model, xs, params)
    dy, loss_part = _loss_head(y, tgt)
    dx0, grads = model_vjp(dy)
    grads['w_in'] = _unpermute_w_in(grads['w_in'])

    def chip_blocks(g, axis):
        return jnp.stack(jnp.split(g, N_CHIPS, axis=axis)).reshape(N_CHIPS, -1)

    g_blocks = jnp.concatenate([chip_blocks(grads[n], SHARDED[n]) for n in big], axis=1)
    g_blocks = jnp.pad(g_blocks, ((0, 0), (0, total - g_blocks.shape[1])))
    g_halves = g_blocks.reshape(N_CHIPS, 2, half_rows, LANES).transpose(1, 0, 2, 3)
    g_flat = _reduce_scatter(g_halves, core).reshape(total // LANES, LANES)

    s_part = _pack([grads[n] for n in SMALL] + [loss_part[0, :1]], small_total)
    s_sum = _all_reduce_small(s_part.reshape(small_total // LANES, LANES))
    small_grads = _unpack(s_sum.reshape(-1), small_shapes + [(1,)])
    loss = small_grads.pop()[0]

    out = {}
    for n, g_shard in zip(big, _unpack(g_flat.reshape(-1), shard_shapes)):
        shape = w[n].shape
        rows_of = lambda a: a.reshape(-1, shape[-1])
        out['grad', n] = g_shard
        for kind, a in zip(('delta', 'new_m', 'new_v'),
                           _adamw(rows_of(w[n]), rows_of(g_shard), rows_of(m[n]), rows_of(v[n]), n)):
            out[kind, n] = a.reshape(shape)
    srows = small_total // LANES
    sd, snm, snv = _adamw(_pack([w[n] for n in SMALL], small_total).reshape(srows, LANES), s_sum,
                          _pack([m[n] for n in SMALL], small_total).reshape(srows, LANES),
                          _pack([v[n] for n in SMALL], small_total).reshape(srows, LANES), "replicated")
    for kind, flat_small in (('grad', s_sum), ('delta', sd), ('new_m', snm), ('new_v', snv)):
        for n, a in zip(SMALL, _unpack(flat_small.reshape(-1), small_shapes)):
            out[kind, n] = a
    return (loss, dx0[None], *[out[kind, n] for kind in ('grad', 'delta', 'new_m', 'new_v') for n in WEIGHTS])
```

```python
import functools

import jax
import jax.numpy as jnp
import numpy as np
from jax import lax
from jax.experimental import pallas as pl
from jax.experimental.pallas import tpu as pltpu

F32 = jnp.float32
BF16 = jnp.bfloat16
MESH = pl.DeviceIdType.MESH
ANY = pl.BlockSpec(memory_space=pl.ANY)
VMEM = pl.BlockSpec(memory_space=pltpu.VMEM)

D_MODEL = 1024
DEPTH = 4
CHUNK = 64
Q_BLOCK = 128
EPS = 1e-6
HEADS = 4
HEAD_DIM = 128
GDN_CONV = 4
MEM_DIM = 128
D_FF = 2816
N_BRANCH = 3
N_IN = 8204
N_IN_PAD = 8320

ADAM_LR = 0.001
ADAM_B1 = 0.9
ADAM_B2 = 0.999
ADAM_EPS = 1e-08
ADAM_WD = 0.01
ADAM_STEP = 10

N_CHIPS = 4
LANES = 128
FLAT_TILE_ROWS = 2048
VMEM_LIMIT = 48 * 1024 * 1024

IN_NAMES = ['x', 'mem', 'norm_mix', 'w_in', 'fox_fbias', 'fox_qnorm', 'fox_knorm', 'gdn_conv', 'gdn_a_log',
            'gdn_dt_bias', 'gdn_onorm', 'gate_bias', 'w_oa', 'w_ob', 'w_oc', 'w_out', 'norm_xq', 'norm_mem',
            'w_mq', 'w_mkv', 'mq_norm', 'mk_norm', 'w_mo', 'norm_ffn', 'w_up', 'ffn_conv', 'ffn_conv_b', 'w_down']
WEIGHTS = IN_NAMES[2:]
SHARDED = {'w_in': 2, 'gdn_conv': 2, 'w_oa': 2, 'w_ob': 2, 'w_oc': 2, 'w_out': 1, 'w_mq': 1, 'w_mkv': 1,
           'w_mo': 2, 'w_up': 2, 'ffn_conv': 2, 'w_down': 1}
SMALL = [n for n in WEIGHTS if n not in SHARDED]


def _pick(n, cands):
    for c in cands:
        if n % c == 0:
            return c
    return n


_DOT_DIMS = {
    'nn': (((1,), (0,)), ((), ())),
    'nt': (((1,), (1,)), ((), ())),
    'tn': (((0,), (0,)), ((), ())),
}


def _mm(a, b, mode):
    if mode == 'nn':
        (m, c), (_, n) = a.shape, b.shape
    elif mode == 'nt':
        (m, c), (n, _) = a.shape, b.shape
    else:
        (c, m), (_, n) = a.shape, b.shape
    tm = _pick(m, (1024, 512, 256, 128))
    tn = _pick(n, (1664, 1408, 1024, 512, 256, 128))
    tc = _pick(c, (512, 256, 128)) if mode == 'tn' else _pick(c, (1024, 1408, 640, 512, 256, 128))
    if mode == 'tn':
        a_spec = pl.BlockSpec((tc, tm), lambda i, j, k: (k, i))
    else:
        a_spec = pl.BlockSpec((tm, tc), lambda i, j, k: (i, k))
    if mode == 'nt':
        b_spec = pl.BlockSpec((tn, tc), lambda i, j, k: (j, k))
    else:
        b_spec = pl.BlockSpec((tc, tn), lambda i, j, k: (k, j))
    dims = _DOT_DIMS[mode]

    def body(a_ref, b_ref, o_ref):
        @pl.when(pl.program_id(2) == 0)
        def _():
            o_ref[...] = jnp.zeros_like(o_ref)

        o_ref[...] += lax.dot_general(a_ref[...].astype(BF16), b_ref[...].astype(BF16), dims,
                                      preferred_element_type=F32)

    return pl.pallas_call(
        body,
        grid=(m // tm, n // tn, c // tc),
        in_specs=[a_spec, b_spec],
        out_specs=pl.BlockSpec((tm, tn), lambda i, j, k: (i, j)),
        out_shape=jax.ShapeDtypeStruct((m, n), F32),
        compiler_params=pltpu.CompilerParams(
            dimension_semantics=("parallel", "parallel", "arbitrary"), vmem_limit_bytes=VMEM_LIMIT),
        name=f"mm_{mode}_{m}x{c}x{n}",
    )(a, b)


@jax.custom_vjp
def matmul(a, w, w16):
    return _mm(a, w16, 'nn')


def _matmul_fwd(a, w, w16):
    return _mm(a, w16, 'nn'), (a, w16)


def _matmul_bwd(res, dy):
    a, w16 = res
    return _mm(dy, w16, 'nt'), _mm(a, dy, 'tn'), jnp.zeros_like(w16)


matmul.defvjp(_matmul_fwd, _matmul_bwd)


Q_TILE = 512
K_BLOCK = 1024
SUB = 256
_NT = (((1,), (1,)), ((), ()))
_TN = (((0,), (0,)), ((), ()))
_NN = (((1,), (0,)), ((), ()))


def _dot(a, b, dims):
    return lax.dot_general(a, b, dims, preferred_element_type=F32)


def _att_tiles(s):
    tq = min(Q_TILE, s)
    tk = min(K_BLOCK, s)
    assert s % tk == 0 and tk % tq == 0 and tk % min(SUB, tk) == 0
    return tq, tk


def _att_specs(s, t):
    tile = pl.BlockSpec((t, HEAD_DIM), lambda h, i: (i, h))
    whole = pl.BlockSpec((s, HEAD_DIM), lambda h, i: (0, h))
    col = pl.BlockSpec((None, t, 1), lambda h, i: (h, i, 0))
    row = pl.BlockSpec((None, 1, s), lambda h, i: (h, 0, 0))
    return tile, whole, col, row


def _key_block(jb, tk):
    return pl.ds(pl.multiple_of(jb * tk, tk), tk)


def _causal(i, jb, tq, tk, strict):
    r = i * tq + lax.broadcasted_iota(jnp.int32, (tq, tk), 0)
    c = jb * tk + lax.broadcasted_iota(jnp.int32, (tq, tk), 1)
    return c < r if strict else c <= r


def _fox_fwd_call(q, k, v, c_col, c_row):
    s, w = q.shape
    tq, tk = _att_tiles(s)
    scale = HEAD_DIM ** -0.5

    def body(q_ref, k_ref, v_ref, cc_ref, cr_ref, o_ref, lse_ref):
        i = pl.program_id(1)
        n_full = lax.div(i * tq, tk)
        qb = q_ref[...]
        cq = cc_ref[...]

        def block(jb, carry, diag):
            m, l, acc = carry
            sl = _key_block(jb, tk)
            sc = _dot(qb, k_ref[sl, :], _NT) * scale + (cq - cr_ref[:, sl])
            if diag:
                sc = jnp.where(_causal(i, jb, tq, tk, False), sc, -jnp.inf)
            m_new = jnp.maximum(m, jnp.max(sc, axis=-1, keepdims=True))
            p = jnp.exp(sc - m_new)
            alpha = jnp.exp(m - m_new)
            l = alpha * l + jnp.sum(p, axis=-1, keepdims=True)
            acc = alpha * acc + _dot(p.astype(BF16), v_ref[sl, :], _NN)
            return m_new, l, acc

        init = (jnp.full((tq, 1), -jnp.inf, F32), jnp.zeros((tq, 1), F32), jnp.zeros((tq, HEAD_DIM), F32))
        carry = lax.fori_loop(0, n_full, lambda jb, cr: block(jb, cr, False), init)
        m, l, acc = block(n_full, carry, True)
        o_ref[...] = acc / l
        lse_ref[...] = m + jnp.log(l)

    tile, whole, col, row = _att_specs(s, tq)
    return pl.pallas_call(
        body,
        grid=(w // HEAD_DIM, s // tq),
        in_specs=[tile, whole, whole, col, row],
        out_specs=[tile, col],
        out_shape=[jax.ShapeDtypeStruct((s, w), F32), jax.ShapeDtypeStruct((w // HEAD_DIM, s, 1), F32)],
        compiler_params=pltpu.CompilerParams(dimension_semantics=("parallel", "parallel"),
                                             vmem_limit_bytes=VMEM_LIMIT),
        name="fox_fwd",
    )(q, k, v, c_col, c_row)


def _fox_bwd_call(q, k, v, c_col, c_row, lse, do):
    s, w = q.shape
    tq, tk = _att_tiles(s)
    scale = HEAD_DIM ** -0.5

    def body(q_ref, k_ref, v_ref, cc_ref, cr_ref, lse_ref, do_ref, dq_ref, dk_ref, dv_ref, dcr_ref):
        i = pl.program_id(1)
        n_full = lax.div(i * tq, tk)

        @pl.when(i == 0)
        def _():
            dk_ref[...] = jnp.zeros_like(dk_ref)
            dv_ref[...] = jnp.zeros_like(dv_ref)
            dcr_ref[...] = jnp.zeros_like(dcr_ref)

        qb = q_ref[...]
        do16 = do_ref[...].astype(BF16)
        lse_q = lse_ref[...]
        cq = cc_ref[...]

        def probs(jb, diag):
            sl = _key_block(jb, tk)
            ks = k_ref[sl, :]
            sc = _dot(qb, ks, _NT) * scale + (cq - cr_ref[:, sl])
            p = jnp.exp(sc - lse_q)
            if diag:
                p = jnp.where(_causal(i, jb, tq, tk, False), p, 0.0)
            return sl, ks, p, _dot(do16, v_ref[sl, :], _NT)

        def row_dot(jb, acc, diag):
            _, _, p, dp = probs(jb, diag)
            return acc + jnp.sum(p * dp, axis=-1, keepdims=True)

        delta = lax.fori_loop(0, n_full, lambda jb, a: row_dot(jb, a, False), jnp.zeros((tq, 1), F32))
        delta = row_dot(n_full, delta, True)

        def block(jb, dq, diag):
            sl, ks, p, dp = probs(jb, diag)
            ds = p * (dp - delta)
            ds16 = ds.astype(BF16)
            dv_ref[sl, :] += _dot(p.astype(BF16), do16, _TN)
            dk_ref[sl, :] += _dot(ds16, qb, _TN) * scale
            dcr_ref[:, sl] += -jnp.sum(ds, axis=0, keepdims=True)
            return dq + _dot(ds16, ks, _NN) * scale

        dq = lax.fori_loop(0, n_full, lambda jb, a: block(jb, a, False), jnp.zeros((tq, HEAD_DIM), F32))
        dq_ref[...] = block(n_full, dq, True)

    tile, whole, col, row = _att_specs(s, tq)
    full = jax.ShapeDtypeStruct((s, w), F32)
    return pl.pallas_call(
        body,
        grid=(w // HEAD_DIM, s // tq),
        in_specs=[tile, whole, whole, col, row, col, tile],
        out_specs=[tile, whole, whole, row],
        out_shape=[full, full, full, jax.ShapeDtypeStruct((w // HEAD_DIM, 1, s), F32)],
        compiler_params=pltpu.CompilerParams(dimension_semantics=("parallel", "arbitrary"),
                                             vmem_limit_bytes=VMEM_LIMIT),
        name="fox_bwd",
    )(q, k, v, c_col, c_row, lse, do)


@jax.custom_vjp
def fox_core(q, k, v, c):
    return _fox_fwd(q, k, v, c)[0]


def _fox_fwd(q, k, v, c):
    q16, k16, v16 = q.astype(BF16), k.astype(BF16), v.astype(BF16)
    c_col, c_row = c.T[:, :, None], c.T[:, None, :]
    o, lse = _fox_fwd_call(q16, k16, v16, c_col, c_row)
    return o, (q16, k16, v16, c_col, c_row, lse)


def _fox_bwd(res, do):
    dq, dk, dv, dcr = _fox_bwd_call(*res, do)
    return dq, dk, dv, dcr[:, 0, :].T


fox_core.defvjp(_fox_fwd, _fox_bwd)


def _neg_softplus(z):
    e = jnp.exp(-jnp.abs(z))
    return -(jnp.maximum(z, 0.0) + jnp.log(1.0 + e)), e


def _split_dot(x, tri):
    hi = x.astype(BF16)
    lo = (x - hi.astype(F32)).astype(BF16)
    return _dot(hi, tri, _NN) + _dot(lo, tri, _NN)


def _tri(n, fn):
    r = lax.broadcasted_iota(jnp.int32, (n, n), 0)
    c = lax.broadcasted_iota(jnp.int32, (n, n), 1)
    return fn(r, c).astype(BF16)


def _sb_fwd_call(q, k, v):
    s, w = q.shape
    tq, tk = _att_tiles(s)
    sub = min(SUB, tk)
    n_sub = tk // sub
    scale = HEAD_DIM ** -0.5

    def body(q_ref, k_ref, v_ref, o_ref, tot_ref):
        i = pl.program_id(1)
        n_full = lax.div(i * tq, tk)
        qb = q_ref[...]
        tri = _tri(sub, lambda r, c: r >= c)

        def block(jb, carry, diag):
            later, acc = carry
            sl = _key_block(jb, tk)
            z = _dot(qb, k_ref[sl, :], _NT) * scale
            lk, _ = _neg_softplus(z)
            if diag:
                mask = _causal(i, jb, tq, tk, True)
                lk = jnp.where(mask, lk, 0.0)
            pieces = [None] * n_sub
            for u in reversed(range(n_sub)):
                part = lk[:, u * sub:(u + 1) * sub]
                pieces[u] = _split_dot(part, tri) + later
                later = later + jnp.sum(part, axis=-1, keepdims=True)
            a = jnp.exp(z + jnp.concatenate(pieces, axis=1))
            if diag:
                a = jnp.where(mask, a, 0.0)
            return later, acc + _dot(a.astype(BF16), v_ref[sl, :], _NN)

        carry = block(n_full, (jnp.zeros((tq, 1), F32), jnp.zeros((tq, HEAD_DIM), F32)), True)
        later, acc = lax.fori_loop(0, n_full, lambda jj, cr: block(n_full - 1 - jj, cr, False), carry)
        o_ref[...] = acc
        tot_ref[...] = later

    tile, whole, col, _ = _att_specs(s, tq)
    return pl.pallas_call(
        body,
        grid=(w // HEAD_DIM, s // tq),
        in_specs=[tile, whole, whole],
        out_specs=[tile, col],
        out_shape=[jax.ShapeDtypeStruct((s, w), F32), jax.ShapeDtypeStruct((w // HEAD_DIM, s, 1), F32)],
        compiler_params=pltpu.CompilerParams(dimension_semantics=("parallel", "parallel"),
                                             vmem_limit_bytes=VMEM_LIMIT),
        name="sb_fwd",
    )(q, k, v)


def _sb_bwd_call(q, k, v, tot, do):
    s, w = q.shape
    tq, tk = _att_tiles(s)
    sub = min(SUB, tk)
    n_sub = tk // sub
    scale = HEAD_DIM ** -0.5

    def body(q_ref, k_ref, v_ref, tot_ref, do_ref, dq_ref, dk_ref, dv_ref):
        i = pl.program_id(1)
        n_full = lax.div(i * tq, tk)

        @pl.when(i == 0)
        def _():
            dk_ref[...] = jnp.zeros_like(dk_ref)
            dv_ref[...] = jnp.zeros_like(dv_ref)

        qb = q_ref[...]
        do16 = do_ref[...].astype(BF16)
        tot_q = tot_ref[...]
        tri_before = _tri(sub, lambda r, c: r < c)
        tri_upto = _tri(sub, lambda r, c: r <= c)

        def block(jb, carry, diag):
            before, dl_before, dq = carry
            sl = _key_block(jb, tk)
            ks = k_ref[sl, :]
            z = _dot(qb, ks, _NT) * scale
            lk, e = _neg_softplus(z)
            if diag:
                mask = _causal(i, jb, tq, tk, True)
                lk = jnp.where(mask, lk, 0.0)
            sig = 1.0 - jnp.exp(lk)
            pieces = []
            for u in range(n_sub):
                part = lk[:, u * sub:(u + 1) * sub]
                pieces.append(_split_dot(part, tri_before) + before)
                before = before + jnp.sum(part, axis=-1, keepdims=True)
            a = jnp.exp(z + (tot_q - jnp.concatenate(pieces, axis=1)))
            if diag:
                a = jnp.where(mask, a, 0.0)
            dl = a * _dot(do16, v_ref[sl, :], _NT)
            dl16 = dl.astype(BF16)
            pieces = []
            for u in range(n_sub):
                pieces.append(_dot(dl16[:, u * sub:(u + 1) * sub], tri_upto, _NN) + dl_before)
                dl_before = dl_before + jnp.sum(dl[:, u * sub:(u + 1) * sub], axis=-1, keepdims=True)
            dz = dl - sig * jnp.concatenate(pieces, axis=1)
            if diag:
                dz = jnp.where(mask, dz, 0.0)
            dz16 = dz.astype(BF16)
            dv_ref[sl, :] += _dot(a.astype(BF16), do16, _TN)
            dk_ref[sl, :] += _dot(dz16, qb, _TN) * scale
            return before, dl_before, dq + _dot(dz16, ks, _NN) * scale

        init = (jnp.zeros((tq, 1), F32), jnp.zeros((tq, 1), F32), jnp.zeros((tq, HEAD_DIM), F32))
        carry = lax.fori_loop(0, n_full, lambda jb, cr: block(jb, cr, False), init)
        dq_ref[...] = block(n_full, carry, True)[2]

    tile, whole, col, _ = _att_specs(s, tq)
    full = jax.ShapeDtypeStruct((s, w), F32)
    return pl.pallas_call(
        body,
        grid=(w // HEAD_DIM, s // tq),
        in_specs=[tile, whole, whole, col, tile],
        out_specs=[tile, whole, whole],
        out_shape=[full, full, full],
        compiler_params=pltpu.CompilerParams(dimension_semantics=("parallel", "arbitrary"),
                                             vmem_limit_bytes=VMEM_LIMIT),
        name="sb_bwd",
    )(q, k, v, tot, do)


@jax.custom_vjp
def sb_core(q, k, v):
    return _sb_fwd(q, k, v)[0]


def _sb_fwd(q, k, v):
    q16, k16, v16 = q.astype(BF16), k.astype(BF16), v.astype(BF16)
    o, tot = _sb_fwd_call(q16, k16, v16)
    return o, (q16, k16, v16, tot)


def _sb_bwd(res, do):
    return tuple(_sb_bwd_call(*res, do))


sb_core.defvjp(_sb_fwd, _sb_bwd)


def _gdn_specs(h, c, d):
    vec = pl.BlockSpec((None, h, c, d), lambda n: (n, 0, 0, 0))
    sq = pl.BlockSpec((None, h, c, c), lambda n: (n, 0, 0, 0))
    dec = pl.BlockSpec((None, h, 1, d), lambda n: (n, 0, 0, 0))
    st = pl.BlockSpec((None, h, d, d), lambda n: (n, 0, 0, 0))
    return vec, sq, dec, st


def _b16(x):
    return x.astype(BF16)


def _gdn_scan_fwd_call(qg, u, w, attn, kt, egl):
    n, h, c, d = qg.shape

    def body(qg_ref, u_ref, w_ref, attn_ref, kt_ref, egl_ref, o_ref, st_ref, state):
        @pl.when(pl.program_id(0) == 0)
        def _():
            state[...] = jnp.zeros_like(state)

        for hh in range(h):
            s0 = state[hh]
            st_ref[hh] = s0
            s16 = _b16(s0)
            vn = u_ref[hh] - _dot(_b16(w_ref[hh]), s16, _NN)
            vn16 = _b16(vn)
            o_ref[hh] = _dot(_b16(qg_ref[hh]), s16, _NN) + _dot(_b16(attn_ref[hh]), vn16, _NN)
            state[hh] = s0 * egl_ref[hh] + _dot(_b16(kt_ref[hh]), vn16, _TN)

    vec, sq, dec, st = _gdn_specs(h, c, d)
    return pl.pallas_call(
        body,
        grid=(n,),
        in_specs=[vec, vec, vec, sq, vec, dec],
        out_specs=[vec, st],
        out_shape=[jax.ShapeDtypeStruct((n, h, c, d), F32), jax.ShapeDtypeStruct((n, h, d, d), F32)],
        scratch_shapes=[pltpu.VMEM((h, d, d), F32)],
        compiler_params=pltpu.CompilerParams(dimension_semantics=("arbitrary",)),
        name="gdn_scan_fwd",
    )(qg, u, w, attn, kt, egl)


def _gdn_scan_bwd_call(qg, u, w, attn, kt, egl, states, do):
    n, h, c, d = qg.shape

    def body(qg_ref, u_ref, w_ref, attn_ref, kt_ref, egl_ref, st_ref, do_ref,
             dqg_ref, du_ref, dw_ref, dattn_ref, dkt_ref, degl_ref, dstate):
        @pl.when(pl.program_id(0) == 0)
        def _():
            dstate[...] = jnp.zeros_like(dstate)

        for hh in range(h):
            s0 = st_ref[hh]
            s16 = _b16(s0)
            big_d = dstate[hh]
            d16 = _b16(big_d)
            w16, kt16, qg16, attn16 = _b16(w_ref[hh]), _b16(kt_ref[hh]), _b16(qg_ref[hh]), _b16(attn_ref[hh])
            do16 = _b16(do_ref[hh])
            vn16 = _b16(u_ref[hh] - _dot(w16, s16, _NN))
            dvn = _dot(attn16, do16, _TN) + _dot(kt16, d16, _NN)
            dvn16 = _b16(dvn)
            du_ref[hh] = dvn
            dattn_ref[hh] = _dot(do16, vn16, _NT)
            dqg_ref[hh] = _dot(do16, s16, _NT)
            dkt_ref[hh] = _dot(vn16, d16, _NT)
            dw_ref[hh] = -_dot(dvn16, s16, _NT)
            degl_ref[hh] = jnp.sum(big_d * s0, axis=0, keepdims=True)
            dstate[hh] = big_d * egl_ref[hh] + _dot(qg16, do16, _TN) - _dot(w16, dvn16, _TN)

    vec, sq, dec, st = _gdn_specs(h, c, d)
    rev = lambda spec: pl.BlockSpec(spec.block_shape, lambda i: (n - 1 - i, 0, 0, 0))
    vec, sq, dec, st = rev(vec), rev(sq), rev(dec), rev(st)
    vshape = jax.ShapeDtypeStruct((n, h, c, d), F32)
    return pl.pallas_call(
        body,
        grid=(n,),
        in_specs=[vec, vec, vec, sq, vec, dec, st, vec],
        out_specs=[vec, vec, vec, sq, vec, dec],
        out_shape=[vshape, vshape, vshape, jax.ShapeDtypeStruct((n, h, c, c), F32), vshape,
                   jax.ShapeDtypeStruct((n, h, 1, d), F32)],
        scratch_shapes=[pltpu.VMEM((h, d, d), F32)],
        compiler_params=pltpu.CompilerParams(dimension_semantics=("arbitrary",)),
        name="gdn_scan_bwd",
    )(qg, u, w, attn, kt, egl, states, do)


@jax.custom_vjp
def gdn_scan(qg, u, w, attn, kt, egl):
    return _gdn_scan_fwd_call(qg, u, w, attn, kt, egl)[0]


def _gdn_scan_fwd(qg, u, w, attn, kt, egl):
    o, states = _gdn_scan_fwd_call(qg, u, w, attn, kt, egl)
    return o, (qg, u, w, attn, kt, egl, states)


def _gdn_scan_bwd(res, do):
    return tuple(_gdn_scan_bwd_call(*res, do))


gdn_scan.defvjp(_gdn_scan_fwd, _gdn_scan_bwd)


CONV_ROWS = 512
HALO = 8


def _sigmoid(x):
    return 1.0 / (1.0 + jnp.exp(-x))


def _shifted(ext, k, rows):
    if k == 0:
        return ext[HALO:HALO + rows]
    return pltpu.roll(ext, k % ext.shape[0], 0)[HALO:HALO + rows]


def _conv_specs(s, ch, tr, tc, off):
    per, last = tr // HALO, s // HALO - 1
    blk = pl.BlockSpec((tr, tc), lambda j, i: (i, j + off))
    prev = pl.BlockSpec((HALO, tc), lambda j, i: (jnp.maximum(i * per - 1, 0), j + off))
    nxt = pl.BlockSpec((HALO, tc), lambda j, i: (jnp.minimum((i + 1) * per, last), j + off))
    return blk, prev, nxt


def _dwconv_fwd_call(x, w, b, gated):
    s, ch = x.shape
    taps = w.shape[0]
    out_ch = ch // 2 if gated else ch
    tr = _pick(s, (CONV_ROWS, 256, 128, 64, 32, 16, 8))
    tc = _pick(out_ch, (1408, 512, 384, 256, 128))
    n_j = out_ch // tc
    parts = (0, n_j) if gated else (0,)

    def conv(x_ref, p_ref, w_ref, first):
        xb = x_ref[...]
        ext = jnp.concatenate([jnp.where(first, 0.0, p_ref[...]), xb], axis=0)
        y = w_ref[taps - 1:taps, :] * xb
        for k in range(1, taps):
            y = y + w_ref[taps - 1 - k:taps - k, :] * _shifted(ext, k, tr)
        return y

    def body(*refs):
        first = pl.program_id(1) == 0
        if gated:
            xa, pa, wa, ba, xb_, pb, wb, bb, o_ref = refs
            a = conv(xa, pa, wa, first) + ba[...]
            g = conv(xb_, pb, wb, first) + bb[...]
            o_ref[...] = a * _sigmoid(a) * g
        else:
            xa, pa, wa, o_ref = refs
            a = conv(xa, pa, wa, first)
            o_ref[...] = a * _sigmoid(a)

    in_specs, args = [], []
    for off in parts:
        blk, prev, _ = _conv_specs(s, ch, tr, tc, off)
        in_specs += [blk, prev, pl.BlockSpec((taps, tc), lambda j, i, off=off: (0, j + off))]
        args += [x, x, w]
        if gated:
            in_specs.append(pl.BlockSpec((1, tc), lambda j, i, off=off: (0, j + off)))
            args.append(b)
    return pl.pallas_call(
        body,
        grid=(n_j, s // tr),
        in_specs=in_specs,
        out_specs=pl.BlockSpec((tr, tc), lambda j, i: (i, j)),
        out_shape=jax.ShapeDtypeStruct((s, out_ch), F32),
        compiler_params=pltpu.CompilerParams(dimension_semantics=("parallel", "parallel"),
                                             vmem_limit_bytes=VMEM_LIMIT),
        name="dwconv_gate_fwd" if gated else "dwconv_silu_fwd",
    )(*args)


def _dwconv_bwd_call(x, w, b, do, gated):
    s, ch = x.shape
    taps = w.shape[0]
    out_ch = ch // 2 if gated else ch
    tr = _pick(s, (CONV_ROWS, 256, 128, 64, 32, 16, 8))
    tc = _pick(out_ch, (1408, 512, 384, 256, 128))
    n_j, n_i = out_ch // tc, s // tr
    parts = (0, n_j) if gated else (0,)
    ext_rows = tr + 2 * HALO

    def pre_act(x_ref, p_ref, n_ref, w_ref, first, last):
        ext = jnp.concatenate([jnp.where(first, 0.0, p_ref[...]), x_ref[...], n_ref[...]], axis=0)
        y = w_ref[taps - 1:taps, :] * ext
        for k in range(1, taps):
            y = y + w_ref[taps - 1 - k:taps - k, :] * pltpu.roll(ext, k, 0)
        return ext, y

    def grads(ext, dy, w_ref, dx_ref, dw_ref):
        dx = w_ref[taps - 1:taps, :] * dy[HALO:HALO + tr]
        for k in range(1, taps):
            dx = dx + w_ref[taps - 1 - k:taps - k, :] * _shifted(dy, -k, tr)
        dx_ref[...] = dx
        dyb = dy[HALO:HALO + tr]
        for k in range(taps):
            dw_ref[taps - 1 - k:taps - k, :] += jnp.sum(dyb * _shifted(ext, k, tr), axis=0, keepdims=True)
        return dyb

    def body(*refs):
        i = pl.program_id(1)
        first, last = i == 0, i == n_i - 1
        rows = lax.broadcasted_iota(jnp.int32, (ext_rows, 1), 0)
        inside = jnp.logical_and(rows >= HALO, jnp.logical_or(rows < HALO + tr, jnp.logical_not(last)))
        if gated:
            (xa, pa, na, wa, ba, xb_, pb, nb, wb, bb, do_ref, don_ref,
             dxa_ref, dxb_ref, dwa_ref, dwb_ref, dba_ref, dbb_ref) = refs
        else:
            xa, pa, na, wa, do_ref, don_ref, dxa_ref, dwa_ref = refs

        @pl.when(first)
        def _():
            dwa_ref[...] = jnp.zeros_like(dwa_ref)
            if gated:
                dwb_ref[...] = jnp.zeros_like(dwb_ref)
                dba_ref[...] = jnp.zeros_like(dba_ref)
                dbb_ref[...] = jnp.zeros_like(dbb_ref)

        d_out = jnp.concatenate([jnp.zeros((HALO, tc), F32), do_ref[...], don_ref[...]], axis=0)
        d_out = jnp.where(inside, d_out, 0.0)
        ext_a, a = pre_act(xa, pa, na, wa, first, last)
        if gated:
            a = a + ba[...]
            ext_b, g = pre_act(xb_, pb, nb, wb, first, last)
            g = g + bb[...]
            sg = _sigmoid(a)
            silu = a * sg
            dya = jnp.where(inside, d_out * g * (sg + silu * (1.0 - sg)), 0.0)
            dyg = jnp.where(inside, d_out * silu, 0.0)
            dba_ref[...] += jnp.sum(grads(ext_a, dya, wa, dxa_ref, dwa_ref), axis=0, keepdims=True)
            dbb_ref[...] += jnp.sum(grads(ext_b, dyg, wb, dxb_ref, dwb_ref), axis=0, keepdims=True)
        else:
            sg = _sigmoid(a)
            dya = jnp.where(inside, d_out * (sg + a * sg * (1.0 - sg)), 0.0)
            grads(ext_a, dya, wa, dxa_ref, dwa_ref)

    in_specs, args = [], []
    for off in parts:
        blk, prev, nxt = _conv_specs(s, ch, tr, tc, off)
        in_specs += [blk, prev, nxt, pl.BlockSpec((taps, tc), lambda j, i, off=off: (0, j + off))]
        args += [x, x, x, w]
        if gated:
            in_specs.append(pl.BlockSpec((1, tc), lambda j, i, off=off: (0, j + off)))
            args.append(b)
    blk, _, nxt = _conv_specs(s, out_ch, tr, tc, 0)
    in_specs += [blk, nxt]
    args += [do, do]
    n_half = len(parts)
    out_specs = ([blk] * n_half + [pl.BlockSpec((taps, tc), lambda j, i: (0, j))] * n_half
                 + ([pl.BlockSpec((1, tc), lambda j, i: (0, j))] * n_half if gated else []))
    out_shape = ([jax.ShapeDtypeStruct((s, out_ch), F32)] * n_half + [jax.ShapeDtypeStruct((taps, out_ch), F32)] * n_half
                 + ([jax.ShapeDtypeStruct((1, out_ch), F32)] * n_half if gated else []))
    return pl.pallas_call(
        body,
        grid=(n_j, n_i),
        in_specs=in_specs,
        out_specs=out_specs,
        out_shape=out_shape,
        compiler_params=pltpu.CompilerParams(dimension_semantics=("parallel", "arbitrary"),
                                             vmem_limit_bytes=VMEM_LIMIT),
        name="dwconv_gate_bwd" if gated else "dwconv_silu_bwd",
    )(*args)


@jax.custom_vjp
def conv_gate(u, w, b):
    return _dwconv_fwd_call(u, w, b[None], True)


def _conv_gate_fwd(u, w, b):
    return _dwconv_fwd_call(u, w, b[None], True), (u, w, b)


def _conv_gate_bwd(res, do):
    u, w, b = res
    dxa, dxb, dwa, dwb, dba, dbb = _dwconv_bwd_call(u, w, b[None], do, True)
    return (jnp.concatenate([dxa, dxb], axis=1), jnp.concatenate([dwa, dwb], axis=1),
            jnp.concatenate([dba, dbb], axis=1)[0])


conv_gate.defvjp(_conv_gate_fwd, _conv_gate_bwd)


@jax.custom_vjp
def conv_silu(x, w):
    return _dwconv_fwd_call(x, w, None, False)


def _conv_silu_fwd(x, w):
    return _dwconv_fwd_call(x, w, None, False), (x, w)


def _conv_silu_bwd(res, do):
    x, w = res
    dx, dw = _dwconv_bwd_call(x, w, None, do, False)
    return dx, dw


conv_silu.defvjp(_conv_silu_fwd, _conv_silu_bwd)


ROW_TILE = 256


def _rmsnorm_fwd_call(x, g):
    s, d = x.shape
    tr = _pick(s, (ROW_TILE, 128, 64, 32, 16, 8))

    def body(x_ref, g_ref, o_ref):
        xb = x_ref[...]
        r = lax.rsqrt(jnp.mean(xb * xb, axis=-1, keepdims=True) + EPS)
        o_ref[...] = (xb * r) * g_ref[...]

    return pl.pallas_call(
        body,
        grid=(s // tr,),
        in_specs=[pl.BlockSpec((tr, d), lambda i: (i, 0)), pl.BlockSpec((1, d), lambda i: (0, 0))],
        out_specs=pl.BlockSpec((tr, d), lambda i: (i, 0)),
        out_shape=jax.ShapeDtypeStruct((s, d), F32),
        compiler_params=pltpu.CompilerParams(dimension_semantics=("parallel",)),
        name="rmsnorm_fwd",
    )(x, g)


def _rmsnorm_bwd_call(x, g, dy):
    s, d = x.shape
    tr = _pick(s, (ROW_TILE, 128, 64, 32, 16, 8))

    def body(x_ref, g_ref, dy_ref, dx_ref, dg_ref):
        @pl.when(pl.program_id(0) == 0)
        def _():
            dg_ref[...] = jnp.zeros_like(dg_ref)

        xb = x_ref[...]
        r = lax.rsqrt(jnp.mean(xb * xb, axis=-1, keepdims=True) + EPS)
        y = xb * r
        dyb = dy_ref[...]
        dg_ref[...] += jnp.sum(dyb * y, axis=0, keepdims=True)
        dn = dyb * g_ref[...]
        dx_ref[...] = r * (dn - y * jnp.mean(dn * y, axis=-1, keepdims=True))

    row = pl.BlockSpec((tr, d), lambda i: (i, 0))
    vec = pl.BlockSpec((1, d), lambda i: (0, 0))
    return pl.pallas_call(
        body,
        grid=(s // tr,),
        in_specs=[row, vec, row],
        out_specs=[row, vec],
        out_shape=[jax.ShapeDtypeStruct((s, d), F32), jax.ShapeDtypeStruct((1, d), F32)],
        compiler_params=pltpu.CompilerParams(dimension_semantics=("arbitrary",)),
        name="rmsnorm_bwd",
    )(x, g, dy)


@jax.custom_vjp
def rmsnorm_rows(x, g):
    return _rmsnorm_fwd_call(x, g[None])


def _rmsnorm_rows_fwd(x, g):
    return _rmsnorm_fwd_call(x, g[None]), (x, g)


def _rmsnorm_rows_bwd(res, dy):
    x, g = res
    dx, dg = _rmsnorm_bwd_call(x, g[None], dy)
    return dx, dg[0]


rmsnorm_rows.defvjp(_rmsnorm_rows_fwd, _rmsnorm_rows_bwd)


def _merge_specs(s, d, tr):
    gate = pl.BlockSpec((tr, N_BRANCH * d), lambda i: (i, 0))
    bias = pl.BlockSpec((1, N_BRANCH * d), lambda i: (0, 0))
    row = pl.BlockSpec((tr, d), lambda i: (i, 0))
    return gate, bias, row


def _merge_fwd_call(gates, bias, ta, tb, tc):
    s, d = ta.shape
    tr = _pick(s, (ROW_TILE, 128, 64, 32, 16, 8))

    def body(g_ref, b_ref, ta_ref, tb_ref, tc_ref, o_ref):
        acc = None
        for b, t_ref in enumerate((ta_ref, tb_ref, tc_ref)):
            gate = _sigmoid(g_ref[:, b * d:(b + 1) * d] + b_ref[:, b * d:(b + 1) * d])
            acc = gate * t_ref[...] if acc is None else acc + gate * t_ref[...]
        o_ref[...] = acc

    gate, bias_s, row = _merge_specs(s, d, tr)
    return pl.pallas_call(
        body,
        grid=(s // tr,),
        in_specs=[gate, bias_s, row, row, row],
        out_specs=row,
        out_shape=jax.ShapeDtypeStruct((s, d), F32),
        compiler_params=pltpu.CompilerParams(dimension_semantics=("parallel",), vmem_limit_bytes=VMEM_LIMIT),
        name="merge_fwd",
    )(gates, bias, ta, tb, tc)


def _merge_bwd_call(gates, bias, ta, tb, tc, dm):
    s, d = ta.shape
    tr = _pick(s, (ROW_TILE, 128, 64, 32, 16, 8))

    def body(g_ref, b_ref, ta_ref, tb_ref, tc_ref, dm_ref, dg_ref, db_ref, dta_ref, dtb_ref, dtc_ref):
        @pl.when(pl.program_id(0) == 0)
        def _():
            db_ref[...] = jnp.zeros_like(db_ref)

        dmb = dm_ref[...]
        for b, (t_ref, dt_ref) in enumerate(((ta_ref, dta_ref), (tb_ref, dtb_ref), (tc_ref, dtc_ref))):
            cols = slice(b * d, (b + 1) * d)
            gate = _sigmoid(g_ref[:, cols] + b_ref[:, cols])
            dt_ref[...] = gate * dmb
            dpre = dmb * t_ref[...] * (gate * (1.0 - gate))
            dg_ref[:, cols] = dpre
            db_ref[:, cols] += jnp.sum(dpre, axis=0, keepdims=True)

    gate, bias_s, row = _merge_specs(s, d, tr)
    rows = jax.ShapeDtypeStruct((s, d), F32)
    return pl.pallas_call(
        body,
        grid=(s // tr,),
        in_specs=[gate, bias_s, row, row, row, row],
        out_specs=[gate, bias_s, row, row, row],
        out_shape=[jax.ShapeDtypeStruct((s, N_BRANCH * d), F32), jax.ShapeDtypeStruct((1, N_BRANCH * d), F32), rows, rows, rows],
        compiler_params=pltpu.CompilerParams(dimension_semantics=("arbitrary",), vmem_limit_bytes=VMEM_LIMIT),
        name="merge_bwd",
    )(gates, bias, ta, tb, tc, dm)


@jax.custom_vjp
def gated_merge(gates, bias, ta, tb, tc):
    return _merge_fwd_call(gates, bias[None], ta, tb, tc)


def _gated_merge_fwd(gates, bias, ta, tb, tc):
    return _merge_fwd_call(gates, bias[None], ta, tb, tc), (gates, bias, ta, tb, tc)


def _gated_merge_bwd(res, dm):
    gates, bias, ta, tb, tc = res
    dg, db, dta, dtb, dtc = _merge_bwd_call(gates, bias[None], ta, tb, tc, dm)
    return dg, db[0], dta, dtb, dtc


gated_merge.defvjp(_gated_merge_fwd, _gated_merge_bwd)


HEAD_ROWS = 512


def _head_stats(xs, kind):
    sq = xs * xs
    ms = jnp.sum(sq, axis=-1, keepdims=True) if kind == 'l2' else jnp.mean(sq, axis=-1, keepdims=True)
    return lax.rsqrt(ms + EPS)


def _headnorm_fwd_call(x, g, z, kind):
    s, w = x.shape
    tr = _pick(s, (HEAD_ROWS, 256, 128, 64, 32, 16, 8))

    def body(*refs):
        x_ref, o_ref = refs[0], refs[-1]
        for h in range(w // HEAD_DIM):
            cols = slice(h * HEAD_DIM, (h + 1) * HEAD_DIM)
            xs = x_ref[:, cols]
            y = xs * _head_stats(xs, kind)
            if kind != 'l2':
                y = y * refs[1][...]
            if kind == 'rms_gate':
                zs = refs[2][:, cols]
                y = y * (zs * _sigmoid(zs))
            o_ref[:, cols] = y

    row = pl.BlockSpec((tr, w), lambda i: (i, 0))
    vec = pl.BlockSpec((1, HEAD_DIM), lambda i: (0, 0))
    in_specs, args = [row], [x]
    if kind != 'l2':
        in_specs.append(vec); args.append(g)
    if kind == 'rms_gate':
        in_specs.append(row); args.append(z)
    return pl.pallas_call(
        body,
        grid=(s // tr,),
        in_specs=in_specs,
        out_specs=row,
        out_shape=jax.ShapeDtypeStruct((s, w), F32),
        compiler_params=pltpu.CompilerParams(dimension_semantics=("parallel",)),
        name=f"headnorm_{kind}_fwd",
    )(*args)


def _headnorm_bwd_call(x, g, z, dy, kind):
    s, w = x.shape
    tr = _pick(s, (HEAD_ROWS, 256, 128, 64, 32, 16, 8))
    gain, gated = kind != 'l2', kind == 'rms_gate'

    def body(*refs):
        n_in = 2 + gain + gated
        x_ref, dy_ref = refs[0], refs[n_in - 1]
        outs = refs[n_in:]
        dx_ref = outs[0]
        if gain:
            g_ref, dg_ref = refs[1], outs[1]

            @pl.when(pl.program_id(0) == 0)
            def _():
                dg_ref[...] = jnp.zeros_like(dg_ref)

        for h in range(w // HEAD_DIM):
            cols = slice(h * HEAD_DIM, (h + 1) * HEAD_DIM)
            xs = x_ref[:, cols]
            r = _head_stats(xs, kind)
            y = xs * r
            dn = dy_ref[:, cols]
            if gated:
                zs = refs[2][:, cols]
                sg = _sigmoid(zs)
                silu = zs * sg
                outs[2][:, cols] = dn * (y * g_ref[...]) * (sg + silu * (1.0 - sg))
                dn = dn * silu
            if gain:
                dg_ref[...] += jnp.sum(dn * y, axis=0, keepdims=True)
                dn = dn * g_ref[...]
            proj = jnp.sum(dn * y, axis=-1, keepdims=True)
            if kind != 'l2':
                proj = proj / HEAD_DIM
            dx_ref[:, cols] = r * (dn - y * proj)

    row = pl.BlockSpec((tr, w), lambda i: (i, 0))
    vec = pl.BlockSpec((1, HEAD_DIM), lambda i: (0, 0))
    rows, vecs = jax.ShapeDtypeStruct((s, w), F32), jax.ShapeDtypeStruct((1, HEAD_DIM), F32)
    in_specs, args = [row], [x]
    if gain:
        in_specs.append(vec); args.append(g)
    if gated:
        in_specs.append(row); args.append(z)
    in_specs.append(row); args.append(dy)
    out_specs, out_shape = [row], [rows]
    if gain:
        out_specs.append(vec); out_shape.append(vecs)
    if gated:
        out_specs.append(row); out_shape.append(rows)
    return pl.pallas_call(
        body,
        grid=(s // tr,),
        in_specs=in_specs,
        out_specs=out_specs,
        out_shape=out_shape,
        compiler_params=pltpu.CompilerParams(dimension_semantics=("arbitrary",) if gain else ("parallel",)),
        name=f"headnorm_{kind}_bwd",
    )(*args)


@jax.custom_vjp
def head_rms(x, g):
    return _headnorm_fwd_call(x, g[None], None, 'rms')


def _head_rms_fwd(x, g):
    return _headnorm_fwd_call(x, g[None], None, 'rms'), (x, g)


def _head_rms_bwd(res, dy):
    x, g = res
    dx, dg = _headnorm_bwd_call(x, g[None], None, dy, 'rms')
    return dx, dg[0]


head_rms.defvjp(_head_rms_fwd, _head_rms_bwd)


@jax.custom_vjp
def head_l2(x):
    return _headnorm_fwd_call(x, None, None, 'l2')


def _head_l2_fwd(x):
    return _headnorm_fwd_call(x, None, None, 'l2'), (x,)


def _head_l2_bwd(res, dy):
    return tuple(_headnorm_bwd_call(res[0], None, None, dy, 'l2'))


head_l2.defvjp(_head_l2_fwd, _head_l2_bwd)


@jax.custom_vjp
def head_rms_gate(x, g, z):
    return _headnorm_fwd_call(x, g[None], z, 'rms_gate')


def _head_rms_gate_fwd(x, g, z):
    return _headnorm_fwd_call(x, g[None], z, 'rms_gate'), (x, g, z)


def _head_rms_gate_bwd(res, dy):
    x, g, z = res
    dx, dg, dz = _headnorm_bwd_call(x, g[None], z, dy, 'rms_gate')
    return dx, dg[0], dz


head_rms_gate.defvjp(_head_rms_gate_fwd, _head_rms_gate_bwd)


MEM_Q_TILE = 512


def _mem_specs(s, m, t):
    tile = pl.BlockSpec((t, HEAD_DIM), lambda h, i: (i, h))
    bank = pl.BlockSpec((m, HEAD_DIM), lambda h, i: (0, h))
    return tile, bank


def _mem_probs(q_ref, k_ref):
    sc = _dot(q_ref[...], k_ref[...], _NT) * HEAD_DIM ** -0.5
    e = jnp.exp(sc - jnp.max(sc, axis=-1, keepdims=True))
    return e / jnp.sum(e, axis=-1, keepdims=True)


def _mem_attn_fwd_call(q, k, v):
    s, w = q.shape
    t = _pick(s, (MEM_Q_TILE, 256, 128, 64, 32, 16, 8))

    def body(q_ref, k_ref, v_ref, o_ref):
        o_ref[...] = _dot(_mem_probs(q_ref, k_ref).astype(BF16), v_ref[...], _NN)

    tile, bank = _mem_specs(s, k.shape[0], t)
    return pl.pallas_call(
        body,
        grid=(w // HEAD_DIM, s // t),
        in_specs=[tile, bank, bank],
        out_specs=tile,
        out_shape=jax.ShapeDtypeStruct((s, w), F32),
        compiler_params=pltpu.CompilerParams(dimension_semantics=("parallel", "parallel")),
        name="mem_attn_fwd",
    )(q, k, v)


def _mem_attn_bwd_call(q, k, v, do):
    s, w = q.shape
    m = k.shape[0]
    t = _pick(s, (MEM_Q_TILE, 256, 128, 64, 32, 16, 8))
    scale = HEAD_DIM ** -0.5

    def body(q_ref, k_ref, v_ref, do_ref, dq_ref, dk_ref, dv_ref):
        @pl.when(pl.program_id(1) == 0)
        def _():
            dk_ref[...] = jnp.zeros_like(dk_ref)
            dv_ref[...] = jnp.zeros_like(dv_ref)

        p = _mem_probs(q_ref, k_ref)
        do16 = do_ref[...].astype(BF16)
        dp = _dot(do16, v_ref[...], _NT)
        ds16 = (p * (dp - jnp.sum(p * dp, axis=-1, keepdims=True))).astype(BF16)
        dq_ref[...] = _dot(ds16, k_ref[...], _NN) * scale
        dk_ref[...] += _dot(ds16, q_ref[...], _TN) * scale
        dv_ref[...] += _dot(p.astype(BF16), do16, _TN)

    tile, bank = _mem_specs(s, m, t)
    return pl.pallas_call(
        body,
        grid=(w // HEAD_DIM, s // t),
        in_specs=[tile, bank, bank, tile],
        out_specs=[tile, bank, bank],
        out_shape=[jax.ShapeDtypeStruct((s, w), F32), jax.ShapeDtypeStruct((m, w), F32), jax.ShapeDtypeStruct((m, w), F32)],
        compiler_params=pltpu.CompilerParams(dimension_semantics=("parallel", "arbitrary")),
        name="mem_attn_bwd",
    )(q, k, v, do)


@jax.custom_vjp
def mem_attn(q, k, v):
    return _mem_attn_fwd_call(q.astype(BF16), k.astype(BF16), v.astype(BF16))


def _mem_attn_fwd(q, k, v):
    q16, k16, v16 = q.astype(BF16), k.astype(BF16), v.astype(BF16)
    return _mem_attn_fwd_call(q16, k16, v16), (q16, k16, v16)


def _mem_attn_bwd(res, do):
    return tuple(_mem_attn_bwd_call(*res, do))


mem_attn.defvjp(_mem_attn_fwd, _mem_attn_bwd)


def _flat_rows(rows):
    return _pick(rows, (FLAT_TILE_ROWS, 1024, 512, 256, 128, 64, 32, 16, 8))


def _add_own_half(g, a, core):
    _, nb, rows, _ = g.shape
    tr = _flat_rows(rows)

    def body(c_ref, g_ref, a_ref, o_ref):
        o_ref[...] = g_ref[...] + a_ref[...]

    return pl.pallas_call(
        body,
        grid_spec=pltpu.PrefetchScalarGridSpec(
            num_scalar_prefetch=1,
            grid=(nb, rows // tr),
            in_specs=[pl.BlockSpec((None, None, tr, LANES), lambda j, r, c_ref: (c_ref[0], j, r, 0)),
                      pl.BlockSpec((None, tr, LANES), lambda j, r, c_ref: (j, r, 0))],
            out_specs=pl.BlockSpec((None, tr, LANES), lambda j, r, c_ref: (j, r, 0)),
        ),
        out_shape=jax.ShapeDtypeStruct((nb, rows, LANES), F32),
        compiler_params=pltpu.CompilerParams(dimension_semantics=("parallel", "parallel")),
        name="rs_add_own_half",
    )(core, g, a)


def _sum_chips(b, core):
    nb, rows, _ = b.shape
    tr = _flat_rows(rows)

    def body(c_ref, b_ref, o_ref):
        acc = b_ref[0] + b_ref[1]
        for j in range(2, nb):
            acc = acc + b_ref[j]
        o_ref[...] = acc

    return pl.pallas_call(
        body,
        grid_spec=pltpu.PrefetchScalarGridSpec(
            num_scalar_prefetch=1,
            grid=(rows // tr,),
            in_specs=[pl.BlockSpec((nb, tr, LANES), lambda r, c_ref: (0, r, 0))],
            out_specs=pl.BlockSpec((None, tr, LANES), lambda r, c_ref: (c_ref[0], r, 0)),
        ),
        out_shape=jax.ShapeDtypeStruct((2, rows, LANES), F32),
        compiler_params=pltpu.CompilerParams(dimension_semantics=("parallel",)),
        name="rs_sum_chips",
    )(core, b)


ADAM_BLOCK_BYTES = 2 * 1024 * 1024


def _adamw(w, g, m, v, tag):
    rows, cols = w.shape
    tr = rows
    for cand in range(8, rows, 8):
        if rows % cand == 0 and cand * cols * 4 <= ADAM_BLOCK_BYTES:
            tr = cand

    def body(w_ref, g_ref, m_ref, v_ref, d_ref, nm_ref, nv_ref):
        gg = g_ref[...]
        nm = ADAM_B1 * m_ref[...] + (1.0 - ADAM_B1) * gg
        nv = ADAM_B2 * v_ref[...] + (1.0 - ADAM_B2) * jnp.square(gg)
        m_hat = nm / (1.0 - ADAM_B1 ** ADAM_STEP)
        v_hat = nv / (1.0 - ADAM_B2 ** ADAM_STEP)
        d_ref[...] = -ADAM_LR * (m_hat / (jnp.sqrt(v_hat) + ADAM_EPS) + ADAM_WD * w_ref[...])
        nm_ref[...] = nm
        nv_ref[...] = nv

    spec = pl.BlockSpec((tr, cols), lambda r: (r, 0))
    shape = jax.ShapeDtypeStruct((rows, cols), F32)
    return pl.pallas_call(
        body,
        grid=(rows // tr,),
        in_specs=[spec] * 4,
        out_specs=[spec] * 3,
        out_shape=[shape] * 3,
        compiler_params=pltpu.CompilerParams(dimension_semantics=("parallel",), vmem_limit_bytes=VMEM_LIMIT),
        name=f"adamw_{tag}",
    )(w, g, m, v)


def _loss_head(y, t):
    s, d = y.shape
    tr = _pick(s, (512, 256, 128, 64, 32, 16, 8))

    def body(y_ref, t_ref, dy_ref, l_ref):
        @pl.when(pl.program_id(0) == 0)
        def _():
            l_ref[...] = jnp.zeros_like(l_ref)

        diff = y_ref[...] - t_ref[...]
        dy_ref[...] = diff / d
        row = jnp.mean(jnp.square(diff), axis=-1, keepdims=True)
        l_ref[...] += 0.5 * jnp.sum(row, axis=0, keepdims=True)

    return pl.pallas_call(
        body,
        grid=(s // tr,),
        in_specs=[pl.BlockSpec((tr, d), lambda r: (r, 0))] * 2,
        out_specs=[pl.BlockSpec((tr, d), lambda r: (r, 0)), pl.BlockSpec((1, LANES), lambda r: (0, 0))],
        out_shape=[jax.ShapeDtypeStruct((s, d), F32), jax.ShapeDtypeStruct((1, LANES), F32)],
        compiler_params=pltpu.CompilerParams(dimension_semantics=("arbitrary",)),
        name="loss_head",
    )(y, t)


def _place():
    x, y, c = lax.axis_index("x"), lax.axis_index("y"), lax.axis_index("c")
    chips = [(1 - x, y), (x, 1 - y), (1 - x, 1 - y)]
    return x, y, c, chips


def _all_gather_chips(flat):
    _, rows, _ = flat.shape

    def body(x_ref, o_ref, send_sems, recv_sems):
        x, y, c, chips = _place()
        me = 2 * x + y
        sib = (x, y, 1 - c)

        def remote(k, src, dst, to):
            return pltpu.make_async_remote_copy(src_ref=src, dst_ref=dst, send_sem=send_sems.at[k],
                                                recv_sem=recv_sems.at[k], device_id=to, device_id_type=MESH)

        first = [remote(k, x_ref.at[c], o_ref.at[me, c], (px, py, c)) for k, (px, py) in enumerate(chips)]
        for cp in first:
            cp.start()
        passed = []
        for k, (px, py) in enumerate(chips):
            blk = o_ref.at[2 * px + py, c]
            remote(k, x_ref.at[c], blk, (px, py, c)).wait_recv()
            cp = remote(3 + k, blk, blk, sib)
            cp.start()
            passed.append(cp)
        for k, (px, py) in enumerate(chips):
            blk = o_ref.at[2 * px + py, 1 - c]
            remote(3 + k, blk, blk, sib).wait_recv()
        for cp in first + passed:
            cp.wait_send()

    return pl.pallas_call(
        body,
        in_specs=[ANY],
        out_specs=ANY,
        out_shape=jax.ShapeDtypeStruct((N_CHIPS, 2, rows, LANES), flat.dtype),
        scratch_shapes=[pltpu.SemaphoreType.DMA((6,)), pltpu.SemaphoreType.DMA((6,))],
        name=f"all_gather_chips_{jnp.dtype(flat.dtype).name}",
    )(flat)


def _rs_sibling_exchange(g):
    _, nb, rows, _ = g.shape

    def body(g_ref, a_ref, send_sem, recv_sem):
        x, y, c, _ = _place()
        cp = pltpu.make_async_remote_copy(src_ref=g_ref.at[1 - c], dst_ref=a_ref, send_sem=send_sem,
                                          recv_sem=recv_sem, device_id=(x, y, 1 - c), device_id_type=MESH)
        cp.start()
        cp.wait()

    return pl.pallas_call(
        body,
        in_specs=[ANY],
        out_specs=ANY,
        out_shape=jax.ShapeDtypeStruct((nb, rows, LANES), F32),
        scratch_shapes=[pltpu.SemaphoreType.DMA, pltpu.SemaphoreType.DMA],
        name="rs_sibling_exchange",
    )(g)


def _rs_chip_exchange(p):
    nb, rows, _ = p.shape

    def body(p_ref, b_ref, send_sems, recv_sems, local_sem):
        x, y, c, chips = _place()
        me = 2 * x + y
        mine = pltpu.make_async_copy(p_ref.at[me], b_ref.at[me], local_sem)
        mine.start()
        copies = [pltpu.make_async_remote_copy(src_ref=p_ref.at[2 * px + py], dst_ref=b_ref.at[me],
                                               send_sem=send_sems.at[k], recv_sem=recv_sems.at[k],
                                               device_id=(px, py, c), device_id_type=MESH)
                  for k, (px, py) in enumerate(chips)]
        for cp in copies:
            cp.start()
        for cp in copies:
            cp.wait()
        mine.wait()

    return pl.pallas_call(
        body,
        in_specs=[ANY],
        out_specs=ANY,
        out_shape=jax.ShapeDtypeStruct((nb, rows, LANES), F32),
        scratch_shapes=[pltpu.SemaphoreType.DMA((3,)), pltpu.SemaphoreType.DMA((3,)), pltpu.SemaphoreType.DMA],
        name="rs_chip_exchange",
    )(p)


def _rs_sibling_gather(r):
    _, rows, _ = r.shape

    def body(r_ref, o_ref, send_sem, recv_sem):
        x, y, c, _ = _place()
        cp = pltpu.make_async_remote_copy(src_ref=o_ref.at[c], dst_ref=o_ref.at[c], send_sem=send_sem,
                                          recv_sem=recv_sem, device_id=(x, y, 1 - c), device_id_type=MESH)
        cp.start()
        cp.wait()

    return pl.pallas_call(
        body,
        in_specs=[ANY],
        out_specs=ANY,
        out_shape=jax.ShapeDtypeStruct((2, rows, LANES), F32),
        input_output_aliases={0: 0},
        scratch_shapes=[pltpu.SemaphoreType.DMA, pltpu.SemaphoreType.DMA],
        name="rs_sibling_gather",
    )(r)


def _reduce_scatter(g, core):
    a = _rs_sibling_exchange(g)
    p = _add_own_half(g, a, core)
    b = _rs_chip_exchange(p)
    return _rs_sibling_gather(_sum_chips(b, core))


def _all_reduce_small(v):
    rows, _ = v.shape
    n_dev = 8

    def body(v_ref, o_ref, gath, send_sems, recv_sems):
        x, y, c, _ = _place()
        me = 4 * x + 2 * y + c
        gath[me] = v_ref[...]
        copies = []
        for mask in range(1, n_dev):
            px = 1 - x if mask & 4 else x
            py = 1 - y if mask & 2 else y
            pc = 1 - c if mask & 1 else c
            copies.append(pltpu.make_async_remote_copy(
                src_ref=v_ref, dst_ref=gath.at[me], send_sem=send_sems.at[mask - 1],
                recv_sem=recv_sems.at[mask - 1], device_id=(px, py, pc), device_id_type=MESH))
        for cp in copies:
            cp.start()
        for cp in copies:
            cp.wait()
        acc = gath[0]
        for k in range(1, n_dev):
            acc = acc + gath[k]
        o_ref[...] = acc

    return pl.pallas_call(
        body,
        in_specs=[VMEM],
        out_specs=VMEM,
        out_shape=jax.ShapeDtypeStruct((rows, LANES), F32),
        scratch_shapes=[pltpu.VMEM((n_dev, rows, LANES), F32), pltpu.SemaphoreType.DMA((n_dev - 1,)),
                        pltpu.SemaphoreType.DMA((n_dev - 1,))],
        name="all_reduce_small",
    )(v)


def _flat_len(shapes, unit_rows=FLAT_TILE_ROWS):
    n = sum(int(np.prod(s)) for s in shapes)
    unit = 2 * unit_rows * LANES
    return -(-n // unit) * unit


def _pack(arrays, total):
    flat = jnp.concatenate([a.reshape(-1) for a in arrays])
    return jnp.pad(flat, (0, total - flat.shape[0]))


def _unpack(flat, shapes):
    out, off = [], 0
    for s in shapes:
        n = int(np.prod(s))
        out.append(flat[off:off + n].reshape(s))
        off += n
    return out


def _permute_w_in(w):
    pad = jnp.zeros(w.shape[:-1] + (N_IN_PAD - N_IN,), w.dtype)
    return jnp.concatenate([w[..., 0:1536], w[..., 1540:3076], w[..., 3084:3596], w[..., 3596:5132],
                            w[..., 5132:8204], w[..., 1536:1540], w[..., 3076:3080], w[..., 3080:3084], pad],
                           axis=-1)


def _unpermute_w_in(w):
    return jnp.concatenate([w[..., 0:1536], w[..., 8192:8196], w[..., 1536:3072], w[..., 8196:8200],
                            w[..., 8200:8204], w[..., 3072:3584], w[..., 3584:5120], w[..., 5120:8192]], axis=-1)


PROJ_WIDTHS = (512, 512, 512, 1536, 512, 512, 512, 512, 3072, 4, 4, 4)


@jax.custom_vjp
def split_proj(proj):
    offs = np.cumsum((0,) + PROJ_WIDTHS)
    return tuple(proj[:, o:o + wd] for o, wd in zip(offs, PROJ_WIDTHS))


def _split_proj_fwd(proj):
    return split_proj(proj), None


def _split_proj_bwd(_, cts):
    pad = jnp.zeros((cts[0].shape[0], N_IN_PAD - sum(PROJ_WIDTHS)), F32)
    return (jnp.concatenate(list(cts) + [pad], axis=1),)


split_proj.defvjp(_split_proj_fwd, _split_proj_bwd)


def heads(x, n):
    return x.reshape(x.shape[:-1] + (n, -1))


def gated_delta_rule(q, k, v, g, beta):
    B, T, H, dk = q.shape
    dv = v.shape[-1]
    N = T // CHUNK

    def chunks(a):
        a = a.astype(F32).reshape((B, N, CHUNK, H) + a.shape[3:])
        return jnp.moveaxis(a, (1, 3), (0, 2))

    qc = chunks(q) * dk ** -0.5
    kc = chunks(k)
    vc = chunks(v)
    bc = chunks(beta)
    gc = jnp.cumsum(chunks(g), axis=-1)
    idx = jnp.arange(CHUNK)
    causal = idx[:, None] >= idx[None, :]
    strict = idx[:, None] > idx[None, :]
    decay = jnp.exp(jnp.where(causal, gc[..., :, None] - gc[..., None, :], -jnp.inf))
    kk = jnp.einsum("nbhcd,nbhed->nbhce", kc, kc)
    a_mat = jnp.where(strict, bc[..., :, None] * kk * decay, 0.0) + jnp.eye(CHUNK, dtype=F32)
    rhs = jnp.concatenate([vc * bc[..., None], kc * (bc * jnp.exp(gc))[..., None]], axis=-1)
    sol = lax.linalg.triangular_solve(a_mat, rhs, left_side=True, lower=True)
    u, w = sol[..., :dv], sol[..., dv:]
    attn = jnp.where(causal, jnp.einsum("nbhcd,nbhed->nbhce", qc, kc) * decay, 0.0)
    g_last = gc[..., -1]
    k_tail = kc * jnp.exp(g_last[..., None] - gc)[..., None]

    egl = jnp.broadcast_to(jnp.exp(g_last)[:, 0, :, None, None], (N, H, 1, dv))
    o = gdn_scan((qc * jnp.exp(gc)[..., None])[:, 0], u[:, 0], w[:, 0], attn[:, 0], k_tail[:, 0], egl)
    return o.transpose(0, 2, 1, 3).reshape(B, T, H, dv)


def _layer(x, mem, p, p16):
    S, D = x.shape
    h = rmsnorm_rows(x, p['norm_mix'])
    def mm(a, name):
        return matmul(a, p[name], p16[name])

    fq, fk, fv, gqkv, gz, sq, sk, sv, gates, ff, gb, ga = split_proj(mm(h, 'w_in'))

    logf = jax.nn.log_sigmoid((ff + p['fox_fbias']).astype(F32))
    ya = fox_core(head_rms(fq, p['fox_qnorm']), head_rms(fk, p['fox_knorm']), fv, jnp.cumsum(logf, axis=0))

    qkv = conv_silu(gqkv, p['gdn_conv'])
    cq, ck, cv = qkv[:, :512], qkv[:, 512:1024], qkv[:, 1024:]
    beta = jax.nn.sigmoid(gb.astype(F32))
    g_log = -jnp.exp(p['gdn_a_log'].astype(F32)) * jax.nn.softplus((ga + p['gdn_dt_bias']).astype(F32))
    o = gated_delta_rule(heads(head_l2(cq), HEADS)[None], heads(head_l2(ck), HEADS)[None], heads(cv, HEADS)[None],
                         g_log[None], beta[None])
    yb = head_rms_gate(o.reshape(S, 512), p['gdn_onorm'], gz)

    yc = sb_core(sq, sk, sv)

    mixed = gated_merge(gates, p['gate_bias'], mm(ya, 'w_oa'), mm(yb, 'w_ob'), mm(yc, 'w_oc'))
    x = x + mm(mixed, 'w_out')

    hq = rmsnorm_rows(x, p['norm_xq'])
    hm = rmsnorm_rows(mem, p['norm_mem'])
    kv = mm(hm, 'w_mkv')
    om = mem_attn(head_rms(mm(hq, 'w_mq'), p['mq_norm']), head_rms(kv[:, :512], p['mk_norm']), kv[:, 512:])
    x = x + mm(om, 'w_mo')

    hf = rmsnorm_rows(x, p['norm_ffn'])
    act = conv_gate(mm(hf, 'w_up'), p['ffn_conv'], p['ffn_conv_b'])
    return x + mm(act, 'w_down')


def kernel(x, mem, norm_mix, w_in, fox_fbias, fox_qnorm, fox_knorm, gdn_conv, gdn_a_log, gdn_dt_bias, gdn_onorm, gate_bias, w_oa, w_ob, w_oc, w_out, norm_xq, norm_mem, w_mq, w_mkv, mq_norm, mk_norm, w_mo, norm_ffn, w_up, ffn_conv, ffn_conv_b, w_down, loss_target, m_norm_mix, m_w_in, m_fox_fbias, m_fox_qnorm, m_fox_knorm, m_gdn_conv, m_gdn_a_log, m_gdn_dt_bias, m_gdn_onorm, m_gate_bias, m_w_oa, m_w_ob, m_w_oc, m_w_out, m_norm_xq, m_norm_mem, m_w_mq, m_w_mkv, m_mq_norm, m_mk_norm, m_w_mo, m_norm_ffn, m_w_up, m_ffn_conv, m_ffn_conv_b, m_w_down, v_norm_mix, v_w_in, v_fox_fbias, v_fox_qnorm, v_fox_knorm, v_gdn_conv, v_gdn_a_log, v_gdn_dt_bias, v_gdn_onorm, v_gate_bias, v_w_oa, v_w_ob, v_w_oc, v_w_out, v_norm_xq, v_norm_mem, v_w_mq, v_w_mkv, v_mq_norm, v_mk_norm, v_w_mo, v_norm_ffn, v_w_up, v_ffn_conv, v_ffn_conv_b, v_w_down):
    args = (x, mem, norm_mix, w_in, fox_fbias, fox_qnorm, fox_knorm, gdn_conv, gdn_a_log, gdn_dt_bias, gdn_onorm, gate_bias, w_oa, w_ob, w_oc, w_out, norm_xq, norm_mem, w_mq, w_mkv, mq_norm, mk_norm, w_mo, norm_ffn, w_up, ffn_conv, ffn_conv_b, w_down)
    moments_m = (m_norm_mix, m_w_in, m_fox_fbias, m_fox_qnorm, m_fox_knorm, m_gdn_conv, m_gdn_a_log, m_gdn_dt_bias, m_gdn_onorm, m_gate_bias, m_w_oa, m_w_ob, m_w_oc, m_w_out, m_norm_xq, m_norm_mem, m_w_mq, m_w_mkv, m_mq_norm, m_mk_norm, m_w_mo, m_norm_ffn, m_w_up, m_ffn_conv, m_ffn_conv_b, m_w_down)
    moments_v = (v_norm_mix, v_w_in, v_fox_fbias, v_fox_qnorm, v_fox_knorm, v_gdn_conv, v_gdn_a_log, v_gdn_dt_bias, v_gdn_onorm, v_gate_bias, v_w_oa, v_w_ob, v_w_oc, v_w_out, v_norm_xq, v_norm_mem, v_w_mq, v_w_mkv, v_mq_norm, v_mk_norm, v_w_mo, v_norm_ffn, v_w_up, v_ffn_conv, v_ffn_conv_b, v_w_down)
    w = dict(zip(IN_NAMES, args))
    m = dict(zip(WEIGHTS, moments_m))
    v = dict(zip(WEIGHTS, moments_v))
    xs, mems, tgt = x[0], mem[0], loss_target[0]
    core = lax.axis_index("c").astype(jnp.int32).reshape(1)

    big = list(SHARDED)
    shard_shapes = [w[n].shape for n in big]
    total = _flat_len(shard_shapes)
    half_rows = total // (2 * LANES)
    small_shapes = [w[n].shape for n in SMALL]
    n_small = sum(int(np.prod(s)) for s in small_shapes) + 1
    small_total = -(-n_small // (8 * LANES)) * (8 * LANES)

    my_chip = 2 * lax.axis_index("x") + lax.axis_index("y")

    def gather(names, dtype, unit_rows):
        shapes = [w[n].shape for n in names]
        tot = _flat_len(shapes, unit_rows)
        flat = _pack([w[n].astype(dtype) for n in names], tot)
        got = _all_gather_chips(flat.reshape(2, tot // (2 * LANES), LANES)).reshape(N_CHIPS, tot)
        out = {}
        for n, blocks in zip(names, zip(*[_unpack(got[j], shapes) for j in range(N_CHIPS)])):
            own = w[n].astype(dtype)
            out[n] = jnp.concatenate([jnp.where(my_chip == j, own, b) for j, b in enumerate(blocks)], axis=SHARDED[n])
        return out

    conv_names = ['gdn_conv', 'ffn_conv']
    params16 = gather([n for n in big if n not in conv_names], BF16, FLAT_TILE_ROWS)
    params16['w_in'] = _permute_w_in(params16['w_in'])
    params = {n: a.astype(F32) for n, a in params16.items()}
    params.update(gather(conv_names, F32, 8))
    for n in SMALL:
        params[n] = w[n]

    def model(x0, pp):
        for layer in range(DEPTH):
            x0 = _layer(x0, mems, {n: a[layer] for n, a in pp.items()}, {n: a[layer] for n, a in params16.items()})
        return x0

    y, model_vjp = jax.vjp(model, xs, params)
    dy, loss_part = _loss_head(y, tgt)
    dx0, grads = model_vjp(dy)
    grads['w_in'] = _unpermute_w_in(grads['w_in'])

    def chip_blocks(g, axis):
        return jnp.stack(jnp.split(g, N_CHIPS, axis=axis)).reshape(N_CHIPS, -1)

    g_blocks = jnp.concatenate([chip_blocks(grads[n], SHARDED[n]) for n in big], axis=1)
    g_blocks = jnp.pad(g_blocks, ((0, 0), (0, total - g_blocks.shape[1])))
    g_halves = g_blocks.reshape(N_CHIPS, 2, half_rows, LANES).transpose(1, 0, 2, 3)
    g_flat = _reduce_scatter(g_halves, core).reshape(total // LANES, LANES)

    s_part = _pack([grads[n] for n in SMALL] + [loss_part[0, :1]], small_total)
    s_sum = _all_reduce_small(s_part.reshape(small_total // LANES, LANES))
    small_grads = _unpack(s_sum.reshape(-1), small_shapes + [(1,)])
    loss = small_grads.pop()[0]

    out = {}
    for n, g_shard in zip(big, _unpack(g_flat.reshape(-1), shard_shapes)):
        shape = w[n].shape
        rows_of = lambda a: a.reshape(-1, shape[-1])
        out['grad', n] = g_shard
        for kind, a in zip(('delta', 'new_m', 'new_v'),
                           _adamw(rows_of(w[n]), rows_of(g_shard), rows_of(m[n]), rows_of(v[n]), n)):
            out[kind, n] = a.reshape(shape)
    srows = small_total // LANES
    sd, snm, snv = _adamw(_pack([w[n] for n in SMALL], small_total).reshape(srows, LANES), s_sum,
                          _pack([m[n] for n in SMALL], small_total).reshape(srows, LANES),
                          _pack([v[n] for n in SMALL], small_total).reshape(srows, LANES), "replicated")
    for kind, flat_small in (('grad', s_sum), ('delta', sd), ('new_m', snm), ('new_v', snv)):
        for n, a in zip(SMALL, _unpack(flat_small.reshape(-1), small_shapes)):
            out[kind, n] = a
    return (loss, dx0[None], *[out[kind, n] for kind in ('grad', 'delta', 'new_m', 'new_v') for n in WEIGHTS])
```

```python
import functools

import jax
import jax.numpy as jnp
import numpy as np
from jax import lax
from jax.experimental import pallas as pl
from jax.experimental.pallas import tpu as pltpu

F32 = jnp.float32
BF16 = jnp.bfloat16
MESH = pl.DeviceIdType.MESH
ANY = pl.BlockSpec(memory_space=pl.ANY)
VMEM = pl.BlockSpec(memory_space=pltpu.VMEM)

D_MODEL = 1024
DEPTH = 4
CHUNK = 64
Q_BLOCK = 128
EPS = 1e-6
HEADS = 4
HEAD_DIM = 128
GDN_CONV = 4
MEM_DIM = 128
D_FF = 2816
N_BRANCH = 3
N_IN = 8204
N_IN_PAD = 8320

ADAM_LR = 0.001
ADAM_B1 = 0.9
ADAM_B2 = 0.999
ADAM_EPS = 1e-08
ADAM_WD = 0.01
ADAM_STEP = 10

N_CHIPS = 4
LANES = 128
FLAT_TILE_ROWS = 2048
VMEM_LIMIT = 48 * 1024 * 1024

IN_NAMES = ['x', 'mem', 'norm_mix', 'w_in', 'fox_fbias', 'fox_qnorm', 'fox_knorm', 'gdn_conv', 'gdn_a_log',
            'gdn_dt_bias', 'gdn_onorm', 'gate_bias', 'w_oa', 'w_ob', 'w_oc', 'w_out', 'norm_xq', 'norm_mem',
            'w_mq', 'w_mkv', 'mq_norm', 'mk_norm', 'w_mo', 'norm_ffn', 'w_up', 'ffn_conv', 'ffn_conv_b', 'w_down']
WEIGHTS = IN_NAMES[2:]
SHARDED = {'w_in': 2, 'gdn_conv': 2, 'w_oa': 2, 'w_ob': 2, 'w_oc': 2, 'w_out': 1, 'w_mq': 1, 'w_mkv': 1,
           'w_mo': 2, 'w_up': 2, 'ffn_conv': 2, 'w_down': 1}
SMALL = [n for n in WEIGHTS if n not in SHARDED]


def _pick(n, cands):
    for c in cands:
        if n % c == 0:
            return c
    return n


_DOT_DIMS = {
    'nn': (((1,), (0,)), ((), ())),
    'nt': (((1,), (1,)), ((), ())),
    'tn': (((0,), (0,)), ((), ())),
}


def _mm(a, b, mode):
    if mode == 'nn':
        (m, c), (_, n) = a.shape, b.shape
    elif mode == 'nt':
        (m, c), (n, _) = a.shape, b.shape
    else:
        (c, m), (_, n) = a.shape, b.shape
    tm = _pick(m, (1024, 512, 256, 128))
    tn = _pick(n, (1664, 1408, 1024, 512, 256, 128))
    tc = _pick(c, (512, 256, 128)) if mode == 'tn' else _pick(c, (1024, 1408, 640, 512, 256, 128))
    if mode == 'tn':
        a_spec = pl.BlockSpec((tc, tm), lambda i, j, k: (k, i))
    else:
        a_spec = pl.BlockSpec((tm, tc), lambda i, j, k: (i, k))
    if mode == 'nt':
        b_spec = pl.BlockSpec((tn, tc), lambda i, j, k: (j, k))
    else:
        b_spec = pl.BlockSpec((tc, tn), lambda i, j, k: (k, j))
    dims = _DOT_DIMS[mode]

    def body(a_ref, b_ref, o_ref):
        @pl.when(pl.program_id(2) == 0)
        def _():
            o_ref[...] = jnp.zeros_like(o_ref)

        o_ref[...] += lax.dot_general(a_ref[...].astype(BF16), b_ref[...].astype(BF16), dims,
                                      preferred_element_type=F32)

    return pl.pallas_call(
        body,
        grid=(m // tm, n // tn, c // tc),
        in_specs=[a_spec, b_spec],
        out_specs=pl.BlockSpec((tm, tn), lambda i, j, k: (i, j)),
        out_shape=jax.ShapeDtypeStruct((m, n), F32),
        compiler_params=pltpu.CompilerParams(
            dimension_semantics=("parallel", "parallel", "arbitrary"), vmem_limit_bytes=VMEM_LIMIT),
        name=f"mm_{mode}_{m}x{c}x{n}",
    )(a, b)


@jax.custom_vjp
def matmul(a, w, w16):
    return _mm(a, w16, 'nn')


def _matmul_fwd(a, w, w16):
    return _mm(a, w16, 'nn'), (a, w16)


def _matmul_bwd(res, dy):
    a, w16 = res
    return _mm(dy, w16, 'nt'), _mm(a, dy, 'tn'), jnp.zeros_like(w16)


matmul.defvjp(_matmul_fwd, _matmul_bwd)


Q_TILE = 512
K_BLOCK = 1024
SUB = 256
_NT = (((1,), (1,)), ((), ()))
_TN = (((0,), (0,)), ((), ()))
_NN = (((1,), (0,)), ((), ()))


def _dot(a, b, dims):
    return lax.dot_general(a, b, dims, preferred_element_type=F32)


def _att_tiles(s):
    tq = min(Q_TILE, s)
    tk = min(K_BLOCK, s)
    assert s % tk == 0 and tk in (tq, 2 * tq) and tq % min(SUB, tq) == 0
    return tq, tk


def _att_specs(s, t):
    tile = pl.BlockSpec((t, HEAD_DIM), lambda h, i: (i, h))
    whole = pl.BlockSpec((s, HEAD_DIM), lambda h, i: (0, h))
    col = pl.BlockSpec((None, t, 1), lambda h, i: (h, i, 0))
    row = pl.BlockSpec((None, 1, s), lambda h, i: (h, 0, 0))
    return tile, whole, col, row


def _key_span(start, width):
    return pl.ds(pl.multiple_of(start, width), width)


def _causal(i, start, tq, width, strict):
    r = i * tq + lax.broadcasted_iota(jnp.int32, (tq, width), 0)
    c = start + lax.broadcasted_iota(jnp.int32, (tq, width), 1)
    return c < r if strict else c <= r


def _odd_half(i, tq, tk, fn, carry):
    if tk == tq:
        return carry
    return lax.cond(lax.rem(i * tq, tk) != 0, fn, lambda cr: cr, carry)


def _fox_fwd_call(q, k, v, c_col, c_row):
    s, w = q.shape
    tq, tk = _att_tiles(s)
    scale = HEAD_DIM ** -0.5

    def body(q_ref, k_ref, v_ref, cc_ref, cr_ref, o_ref, lse_ref):
        i = pl.program_id(1)
        n_full = lax.div(i * tq, tk)
        qb = q_ref[...]
        cq = cc_ref[...]

        def block(start, width, carry, diag):
            m, l, acc = carry
            sl = _key_span(start, width)
            sc = _dot(qb, k_ref[sl, :], _NT) * scale + (cq - cr_ref[:, sl])
            if diag:
                sc = jnp.where(_causal(i, start, tq, width, False), sc, -jnp.inf)
            m_new = jnp.maximum(m, jnp.max(sc, axis=-1, keepdims=True))
            p = jnp.exp(sc - m_new)
            alpha = jnp.exp(m - m_new)
            l = alpha * l + jnp.sum(p, axis=-1, keepdims=True)
            acc = alpha * acc + _dot(p.astype(BF16), v_ref[sl, :], _NN)
            return m_new, l, acc

        init = (jnp.full((tq, 1), -jnp.inf, F32), jnp.zeros((tq, 1), F32), jnp.zeros((tq, HEAD_DIM), F32))
        carry = lax.fori_loop(0, n_full, lambda jb, cr: block(jb * tk, tk, cr, False), init)
        carry = _odd_half(i, tq, tk, lambda cr: block(n_full * tk, tq, cr, False), carry)
        m, l, acc = block(i * tq, tq, carry, True)
        o_ref[...] = acc / l
        lse_ref[...] = m + jnp.log(l)

    tile, whole, col, row = _att_specs(s, tq)
    return pl.pallas_call(
        body,
        grid=(w // HEAD_DIM, s // tq),
        in_specs=[tile, whole, whole, col, row],
        out_specs=[tile, col],
        out_shape=[jax.ShapeDtypeStruct((s, w), F32), jax.ShapeDtypeStruct((w // HEAD_DIM, s, 1), F32)],
        compiler_params=pltpu.CompilerParams(dimension_semantics=("parallel", "parallel"),
                                             vmem_limit_bytes=VMEM_LIMIT),
        name="fox_fwd",
    )(q, k, v, c_col, c_row)


def _fox_bwd_call(q, k, v, c_col, c_row, lse, do):
    s, w = q.shape
    tq, tk = _att_tiles(s)
    scale = HEAD_DIM ** -0.5

    def body(q_ref, k_ref, v_ref, cc_ref, cr_ref, lse_ref, do_ref, dq_ref, dk_ref, dv_ref, dcr_ref):
        i = pl.program_id(1)
        n_full = lax.div(i * tq, tk)

        @pl.when(i == 0)
        def _():
            dk_ref[...] = jnp.zeros_like(dk_ref)
            dv_ref[...] = jnp.zeros_like(dv_ref)
            dcr_ref[...] = jnp.zeros_like(dcr_ref)

        qb = q_ref[...]
        do16 = do_ref[...].astype(BF16)
        lse_q = lse_ref[...]
        cq = cc_ref[...]

        def probs(start, width, diag):
            sl = _key_span(start, width)
            ks = k_ref[sl, :]
            sc = _dot(qb, ks, _NT) * scale + (cq - cr_ref[:, sl])
            p = jnp.exp(sc - lse_q)
            if diag:
                p = jnp.where(_causal(i, start, tq, width, False), p, 0.0)
            return sl, ks, p, _dot(do16, v_ref[sl, :], _NT)

        def row_dot(start, width, acc, diag):
            _, _, p, dp = probs(start, width, diag)
            return acc + jnp.sum(p * dp, axis=-1, keepdims=True)

        delta = lax.fori_loop(0, n_full, lambda jb, a: row_dot(jb * tk, tk, a, False), jnp.zeros((tq, 1), F32))
        delta = _odd_half(i, tq, tk, lambda a: row_dot(n_full * tk, tq, a, False), delta)
        delta = row_dot(i * tq, tq, delta, True)

        def block(start, width, dq, diag):
            sl, ks, p, dp = probs(start, width, diag)
            ds = p * (dp - delta)
            ds16 = ds.astype(BF16)
            dv_ref[sl, :] += _dot(p.astype(BF16), do16, _TN)
            dk_ref[sl, :] += _dot(ds16, qb, _TN) * scale
            dcr_ref[:, sl] += -jnp.sum(ds, axis=0, keepdims=True)
            return dq + _dot(ds16, ks, _NN) * scale

        dq = lax.fori_loop(0, n_full, lambda jb, a: block(jb * tk, tk, a, False), jnp.zeros((tq, HEAD_DIM), F32))
        dq = _odd_half(i, tq, tk, lambda a: block(n_full * tk, tq, a, False), dq)
        dq_ref[...] = block(i * tq, tq, dq, True)

    tile, whole, col, row = _att_specs(s, tq)
    full = jax.ShapeDtypeStruct((s, w), F32)
    return pl.pallas_call(
        body,
        grid=(w // HEAD_DIM, s // tq),
        in_specs=[tile, whole, whole, col, row, col, tile],
        out_specs=[tile, whole, whole, row],
        out_shape=[full, full, full, jax.ShapeDtypeStruct((w // HEAD_DIM, 1, s), F32)],
        compiler_params=pltpu.CompilerParams(dimension_semantics=("parallel", "arbitrary"),
                                             vmem_limit_bytes=VMEM_LIMIT),
        name="fox_bwd",
    )(q, k, v, c_col, c_row, lse, do)


@jax.custom_vjp
def fox_core(q, k, v, c):
    return _fox_fwd(q, k, v, c)[0]


def _fox_fwd(q, k, v, c):
    q16, k16, v16 = q.astype(BF16), k.astype(BF16), v.astype(BF16)
    c_col, c_row = c.T[:, :, None], c.T[:, None, :]
    o, lse = _fox_fwd_call(q16, k16, v16, c_col, c_row)
    return o, (q16, k16, v16, c_col, c_row, lse)


def _fox_bwd(res, do):
    dq, dk, dv, dcr = _fox_bwd_call(*res, do)
    return dq, dk, dv, dcr[:, 0, :].T


fox_core.defvjp(_fox_fwd, _fox_bwd)


def _neg_softplus(z):
    e = jnp.exp(-jnp.abs(z))
    return -(jnp.maximum(z, 0.0) + jnp.log(1.0 + e)), e


def _split_dot(x, tri):
    hi = x.astype(BF16)
    lo = (x - hi.astype(F32)).astype(BF16)
    return _dot(hi, tri, _NN) + _dot(lo, tri, _NN)


def _tri(n, fn):
    r = lax.broadcasted_iota(jnp.int32, (n, n), 0)
    c = lax.broadcasted_iota(jnp.int32, (n, n), 1)
    return fn(r, c).astype(BF16)


def _sb_fwd_call(q, k, v):
    s, w = q.shape
    tq, tk = _att_tiles(s)
    sub = min(SUB, tq)
    scale = HEAD_DIM ** -0.5

    def body(q_ref, k_ref, v_ref, o_ref, tot_ref):
        i = pl.program_id(1)
        n_full = lax.div(i * tq, tk)
        qb = q_ref[...]
        tri = _tri(sub, lambda r, c: r >= c)

        def block(start, width, carry, diag):
            later, acc = carry
            sl = _key_span(start, width)
            n_sub = width // sub
            z = _dot(qb, k_ref[sl, :], _NT) * scale
            lk, _ = _neg_softplus(z)
            if diag:
                mask = _causal(i, start, tq, width, True)
                lk = jnp.where(mask, lk, 0.0)
            pieces = [None] * n_sub
            for u in reversed(range(n_sub)):
                part = lk[:, u * sub:(u + 1) * sub]
                pieces[u] = _split_dot(part, tri) + later
                later = later + jnp.sum(part, axis=-1, keepdims=True)
            a = jnp.exp(z + jnp.concatenate(pieces, axis=1))
            if diag:
                a = jnp.where(mask, a, 0.0)
            return later, acc + _dot(a.astype(BF16), v_ref[sl, :], _NN)

        carry = block(i * tq, tq, (jnp.zeros((tq, 1), F32), jnp.zeros((tq, HEAD_DIM), F32)), True)
        carry = _odd_half(i, tq, tk, lambda cr: block(n_full * tk, tq, cr, False), carry)
        later, acc = lax.fori_loop(0, n_full, lambda jj, cr: block((n_full - 1 - jj) * tk, tk, cr, False), carry)
        o_ref[...] = acc
        tot_ref[...] = later

    tile, whole, col, _ = _att_specs(s, tq)
    return pl.pallas_call(
        body,
        grid=(w // HEAD_DIM, s // tq),
        in_specs=[tile, whole, whole],
        out_specs=[tile, col],
        out_shape=[jax.ShapeDtypeStruct((s, w), F32), jax.ShapeDtypeStruct((w // HEAD_DIM, s, 1), F32)],
        compiler_params=pltpu.CompilerParams(dimension_semantics=("parallel", "parallel"),
                                             vmem_limit_bytes=VMEM_LIMIT),
        name="sb_fwd",
    )(q, k, v)


def _sb_bwd_call(q, k, v, tot, do):
    s, w = q.shape
    tq, tk = _att_tiles(s)
    sub = min(SUB, tq)
    scale = HEAD_DIM ** -0.5

    def body(q_ref, k_ref, v_ref, tot_ref, do_ref, dq_ref, dk_ref, dv_ref):
        i = pl.program_id(1)
        n_full = lax.div(i * tq, tk)

        @pl.when(i == 0)
        def _():
            dk_ref[...] = jnp.zeros_like(dk_ref)
            dv_ref[...] = jnp.zeros_like(dv_ref)

        qb = q_ref[...]
        do16 = do_ref[...].astype(BF16)
        tot_q = tot_ref[...]
        tri_before = _tri(sub, lambda r, c: r < c)
        tri_upto = _tri(sub, lambda r, c: r <= c)

        def block(start, width, carry, diag):
            before, dl_before, dq = carry
            sl = _key_span(start, width)
            n_sub = width // sub
            ks = k_ref[sl, :]
            z = _dot(qb, ks, _NT) * scale
            lk, e = _neg_softplus(z)
            if diag:
                mask = _causal(i, start, tq, width, True)
                lk = jnp.where(mask, lk, 0.0)
            sig = 1.0 - jnp.exp(lk)
            pieces = []
            for u in range(n_sub):
                part = lk[:, u * sub:(u + 1) * sub]
                pieces.append(_split_dot(part, tri_before) + before)
                before = before + jnp.sum(part, axis=-1, keepdims=True)
            a = jnp.exp(z + (tot_q - jnp.concatenate(pieces, axis=1)))
            if diag:
                a = jnp.where(mask, a, 0.0)
            dl = a * _dot(do16, v_ref[sl, :], _NT)
            dl16 = dl.astype(BF16)
            pieces = []
            for u in range(n_sub):
                pieces.append(_dot(dl16[:, u * sub:(u + 1) * sub], tri_upto, _NN) + dl_before)
                dl_before = dl_before + jnp.sum(dl[:, u * sub:(u + 1) * sub], axis=-1, keepdims=True)
            dz = dl - sig * jnp.concatenate(pieces, axis=1)
            if diag:
                dz = jnp.where(mask, dz, 0.0)
            dz16 = dz.astype(BF16)
            dv_ref[sl, :] += _dot(a.astype(BF16), do16, _TN)
            dk_ref[sl, :] += _dot(dz16, qb, _TN) * scale
            return before, dl_before, dq + _dot(dz16, ks, _NN) * scale

        init = (jnp.zeros((tq, 1), F32), jnp.zeros((tq, 1), F32), jnp.zeros((tq, HEAD_DIM), F32))
        carry = lax.fori_loop(0, n_full, lambda jb, cr: block(jb * tk, tk, cr, False), init)
        carry = _odd_half(i, tq, tk, lambda cr: block(n_full * tk, tq, cr, False), carry)
        dq_ref[...] = block(i * tq, tq, carry, True)[2]

    tile, whole, col, _ = _att_specs(s, tq)
    full = jax.ShapeDtypeStruct((s, w), F32)
    return pl.pallas_call(
        body,
        grid=(w // HEAD_DIM, s // tq),
        in_specs=[tile, whole, whole, col, tile],
        out_specs=[tile, whole, whole],
        out_shape=[full, full, full],
        compiler_params=pltpu.CompilerParams(dimension_semantics=("parallel", "arbitrary"),
                                             vmem_limit_bytes=VMEM_LIMIT),
        name="sb_bwd",
    )(q, k, v, tot, do)


@jax.custom_vjp
def sb_core(q, k, v):
    return _sb_fwd(q, k, v)[0]


def _sb_fwd(q, k, v):
    q16, k16, v16 = q.astype(BF16), k.astype(BF16), v.astype(BF16)
    o, tot = _sb_fwd_call(q16, k16, v16)
    return o, (q16, k16, v16, tot)


def _sb_bwd(res, do):
    return tuple(_sb_bwd_call(*res, do))


sb_core.defvjp(_sb_fwd, _sb_bwd)


def _gdn_specs(h, c, d):
    vec = pl.BlockSpec((None, h, c, d), lambda n: (n, 0, 0, 0))
    sq = pl.BlockSpec((None, h, c, c), lambda n: (n, 0, 0, 0))
    dec = pl.BlockSpec((None, h, 1, d), lambda n: (n, 0, 0, 0))
    st = pl.BlockSpec((None, h, d, d), lambda n: (n, 0, 0, 0))
    return vec, sq, dec, st


def _b16(x):
    return x.astype(BF16)


def _gdn_scan_fwd_call(qg, u, w, attn, kt, egl):
    n, h, c, d = qg.shape

    def body(qg_ref, u_ref, w_ref, attn_ref, kt_ref, egl_ref, o_ref, st_ref, state):
        @pl.when(pl.program_id(0) == 0)
        def _():
            state[...] = jnp.zeros_like(state)

        for hh in range(h):
            s0 = state[hh]
            st_ref[hh] = s0
            s16 = _b16(s0)
            vn = u_ref[hh] - _dot(_b16(w_ref[hh]), s16, _NN)
            vn16 = _b16(vn)
            o_ref[hh] = _dot(_b16(qg_ref[hh]), s16, _NN) + _dot(_b16(attn_ref[hh]), vn16, _NN)
            state[hh] = s0 * egl_ref[hh] + _dot(_b16(kt_ref[hh]), vn16, _TN)

    vec, sq, dec, st = _gdn_specs(h, c, d)
    return pl.pallas_call(
        body,
        grid=(n,),
        in_specs=[vec, vec, vec, sq, vec, dec],
        out_specs=[vec, st],
        out_shape=[jax.ShapeDtypeStruct((n, h, c, d), F32), jax.ShapeDtypeStruct((n, h, d, d), F32)],
        scratch_shapes=[pltpu.VMEM((h, d, d), F32)],
        compiler_params=pltpu.CompilerParams(dimension_semantics=("arbitrary",)),
        name="gdn_scan_fwd",
    )(qg, u, w, attn, kt, egl)


def _gdn_scan_bwd_call(qg, u, w, attn, kt, egl, states, do):
    n, h, c, d = qg.shape

    def body(qg_ref, u_ref, w_ref, attn_ref, kt_ref, egl_ref, st_ref, do_ref,
             dqg_ref, du_ref, dw_ref, dattn_ref, dkt_ref, degl_ref, dstate):
        @pl.when(pl.program_id(0) == 0)
        def _():
            dstate[...] = jnp.zeros_like(dstate)

        for hh in range(h):
            s0 = st_ref[hh]
            s16 = _b16(s0)
            big_d = dstate[hh]
            d16 = _b16(big_d)
            w16, kt16, qg16, attn16 = _b16(w_ref[hh]), _b16(kt_ref[hh]), _b16(qg_ref[hh]), _b16(attn_ref[hh])
            do16 = _b16(do_ref[hh])
            vn16 = _b16(u_ref[hh] - _dot(w16, s16, _NN))
            dvn = _dot(attn16, do16, _TN) + _dot(kt16, d16, _NN)
            dvn16 = _b16(dvn)
            du_ref[hh] = dvn
            dattn_ref[hh] = _dot(do16, vn16, _NT)
            dqg_ref[hh] = _dot(do16, s16, _NT)
            dkt_ref[hh] = _dot(vn16, d16, _NT)
            dw_ref[hh] = -_dot(dvn16, s16, _NT)
            degl_ref[hh] = jnp.sum(big_d * s0, axis=0, keepdims=True)
            dstate[hh] = big_d * egl_ref[hh] + _dot(qg16, do16, _TN) - _dot(w16, dvn16, _TN)

    vec, sq, dec, st = _gdn_specs(h, c, d)
    rev = lambda spec: pl.BlockSpec(spec.block_shape, lambda i: (n - 1 - i, 0, 0, 0))
    vec, sq, dec, st = rev(vec), rev(sq), rev(dec), rev(st)
    vshape = jax.ShapeDtypeStruct((n, h, c, d), F32)
    return pl.pallas_call(
        body,
        grid=(n,),
        in_specs=[vec, vec, vec, sq, vec, dec, st, vec],
        out_specs=[vec, vec, vec, sq, vec, dec],
        out_shape=[vshape, vshape, vshape, jax.ShapeDtypeStruct((n, h, c, c), F32), vshape,
                   jax.ShapeDtypeStruct((n, h, 1, d), F32)],
        scratch_shapes=[pltpu.VMEM((h, d, d), F32)],
        compiler_params=pltpu.CompilerParams(dimension_semantics=("arbitrary",)),
        name="gdn_scan_bwd",
    )(qg, u, w, attn, kt, egl, states, do)


@jax.custom_vjp
def gdn_scan(qg, u, w, attn, kt, egl):
    return _gdn_scan_fwd_call(qg, u, w, attn, kt, egl)[0]


def _gdn_scan_fwd(qg, u, w, attn, kt, egl):
    o, states = _gdn_scan_fwd_call(qg, u, w, attn, kt, egl)
    return o, (qg, u, w, attn, kt, egl, states)


def _gdn_scan_bwd(res, do):
    return tuple(_gdn_scan_bwd_call(*res, do))


gdn_scan.defvjp(_gdn_scan_fwd, _gdn_scan_bwd)


CONV_ROWS = 512
HALO = 8


def _sigmoid(x):
    return 1.0 / (1.0 + jnp.exp(-x))


def _shifted(ext, k, rows):
    if k == 0:
        return ext[HALO:HALO + rows]
    return pltpu.roll(ext, k % ext.shape[0], 0)[HALO:HALO + rows]


def _conv_specs(s, ch, tr, tc, off):
    per, last = tr // HALO, s // HALO - 1
    blk = pl.BlockSpec((tr, tc), lambda j, i: (i, j + off))
    prev = pl.BlockSpec((HALO, tc), lambda j, i: (jnp.maximum(i * per - 1, 0), j + off))
    nxt = pl.BlockSpec((HALO, tc), lambda j, i: (jnp.minimum((i + 1) * per, last), j + off))
    return blk, prev, nxt


def _dwconv_fwd_call(x, w, b, gated):
    s, ch = x.shape
    taps = w.shape[0]
    out_ch = ch // 2 if gated else ch
    tr = _pick(s, (CONV_ROWS, 256, 128, 64, 32, 16, 8))
    tc = _pick(out_ch, (1408, 512, 384, 256, 128))
    n_j = out_ch // tc
    parts = (0, n_j) if gated else (0,)

    def conv(x_ref, p_ref, w_ref, first):
        xb = x_ref[...]
        ext = jnp.concatenate([jnp.where(first, 0.0, p_ref[...]), xb], axis=0)
        y = w_ref[taps - 1:taps, :] * xb
        for k in range(1, taps):
            y = y + w_ref[taps - 1 - k:taps - k, :] * _shifted(ext, k, tr)
        return y

    def body(*refs):
        first = pl.program_id(1) == 0
        if gated:
            xa, pa, wa, ba, xb_, pb, wb, bb, o_ref = refs
            a = conv(xa, pa, wa, first) + ba[...]
            g = conv(xb_, pb, wb, first) + bb[...]
            o_ref[...] = a * _sigmoid(a) * g
        else:
            xa, pa, wa, o_ref = refs
            a = conv(xa, pa, wa, first)
            o_ref[...] = a * _sigmoid(a)

    in_specs, args = [], []
    for off in parts:
        blk, prev, _ = _conv_specs(s, ch, tr, tc, off)
        in_specs += [blk, prev, pl.BlockSpec((taps, tc), lambda j, i, off=off: (0, j + off))]
        args += [x, x, w]
        if gated:
            in_specs.append(pl.BlockSpec((1, tc), lambda j, i, off=off: (0, j + off)))
            args.append(b)
    return pl.pallas_call(
        body,
        grid=(n_j, s // tr),
        in_specs=in_specs,
        out_specs=pl.BlockSpec((tr, tc), lambda j, i: (i, j)),
        out_shape=jax.ShapeDtypeStruct((s, out_ch), F32),
        compiler_params=pltpu.CompilerParams(dimension_semantics=("parallel", "parallel"),
                                             vmem_limit_bytes=VMEM_LIMIT),
        name="dwconv_gate_fwd" if gated else "dwconv_silu_fwd",
    )(*args)


def _dwconv_bwd_call(x, w, b, do, gated):
    s, ch = x.shape
    taps = w.shape[0]
    out_ch = ch // 2 if gated else ch
    tr = _pick(s, (CONV_ROWS, 256, 128, 64, 32, 16, 8))
    tc = _pick(out_ch, (1408, 512, 384, 256, 128))
    n_j, n_i = out_ch // tc, s // tr
    parts = (0, n_j) if gated else (0,)
    ext_rows = tr + 2 * HALO

    def pre_act(x_ref, p_ref, n_ref, w_ref, first, last):
        ext = jnp.concatenate([jnp.where(first, 0.0, p_ref[...]), x_ref[...], n_ref[...]], axis=0)
        y = w_ref[taps - 1:taps, :] * ext
        for k in range(1, taps):
            y = y + w_ref[taps - 1 - k:taps - k, :] * pltpu.roll(ext, k, 0)
        return ext, y

    def grads(ext, dy, w_ref, dx_ref, dw_ref):
        dx = w_ref[taps - 1:taps, :] * dy[HALO:HALO + tr]
        for k in range(1, taps):
            dx = dx + w_ref[taps - 1 - k:taps - k, :] * _shifted(dy, -k, tr)
        dx_ref[...] = dx
        dyb = dy[HALO:HALO + tr]
        for k in range(taps):
            dw_ref[taps - 1 - k:taps - k, :] += jnp.sum(dyb * _shifted(ext, k, tr), axis=0, keepdims=True)
        return dyb

    def body(*refs):
        i = pl.program_id(1)
        first, last = i == 0, i == n_i - 1
        rows = lax.broadcasted_iota(jnp.int32, (ext_rows, 1), 0)
        inside = jnp.logical_and(rows >= HALO, jnp.logical_or(rows < HALO + tr, jnp.logical_not(last)))
        if gated:
            (xa, pa, na, wa, ba, xb_, pb, nb, wb, bb, do_ref, don_ref,
             dxa_ref, dxb_ref, dwa_ref, dwb_ref, dba_ref, dbb_ref) = refs
        else:
            xa, pa, na, wa, do_ref, don_ref, dxa_ref, dwa_ref = refs

        @pl.when(first)
        def _():
            dwa_ref[...] = jnp.zeros_like(dwa_ref)
            if gated:
                dwb_ref[...] = jnp.zeros_like(dwb_ref)
                dba_ref[...] = jnp.zeros_like(dba_ref)
                dbb_ref[...] = jnp.zeros_like(dbb_ref)

        d_out = jnp.concatenate([jnp.zeros((HALO, tc), F32), do_ref[...], don_ref[...]], axis=0)
        d_out = jnp.where(inside, d_out, 0.0)
        ext_a, a = pre_act(xa, pa, na, wa, first, last)
        if gated:
            a = a + ba[...]
            ext_b, g = pre_act(xb_, pb, nb, wb, first, last)
            g = g + bb[...]
            sg = _sigmoid(a)
            silu = a * sg
            dya = jnp.where(inside, d_out * g * (sg + silu * (1.0 - sg)), 0.0)
            dyg = jnp.where(inside, d_out * silu, 0.0)
            dba_ref[...] += jnp.sum(grads(ext_a, dya, wa, dxa_ref, dwa_ref), axis=0, keepdims=True)
            dbb_ref[...] += jnp.sum(grads(ext_b, dyg, wb, dxb_ref, dwb_ref), axis=0, keepdims=True)
        else:
            sg = _sigmoid(a)
            dya = jnp.where(inside, d_out * (sg + a * sg * (1.0 - sg)), 0.0)
            grads(ext_a, dya, wa, dxa_ref, dwa_ref)

    in_specs, args = [], []
    for off in parts:
        blk, prev, nxt = _conv_specs(s, ch, tr, tc, off)
        in_specs += [blk, prev, nxt, pl.BlockSpec((taps, tc), lambda j, i, off=off: (0, j + off))]
        args += [x, x, x, w]
        if gated:
            in_specs.append(pl.BlockSpec((1, tc), lambda j, i, off=off: (0, j + off)))
            args.append(b)
    blk, _, nxt = _conv_specs(s, out_ch, tr, tc, 0)
    in_specs += [blk, nxt]
    args += [do, do]
    n_half = len(parts)
    out_specs = ([blk] * n_half + [pl.BlockSpec((taps, tc), lambda j, i: (0, j))] * n_half
                 + ([pl.BlockSpec((1, tc), lambda j, i: (0, j))] * n_half if gated else []))
    out_shape = ([jax.ShapeDtypeStruct((s, out_ch), F32)] * n_half + [jax.ShapeDtypeStruct((taps, out_ch), F32)] * n_half
                 + ([jax.ShapeDtypeStruct((1, out_ch), F32)] * n_half if gated else []))
    return pl.pallas_call(
        body,
        grid=(n_j, n_i),
        in_specs=in_specs,
        out_specs=out_specs,
        out_shape=out_shape,
        compiler_params=pltpu.CompilerParams(dimension_semantics=("parallel", "arbitrary"),
                                             vmem_limit_bytes=VMEM_LIMIT),
        name="dwconv_gate_bwd" if gated else "dwconv_silu_bwd",
    )(*args)


@jax.custom_vjp
def conv_gate(u, w, b):
    return _dwconv_fwd_call(u, w, b[None], True)


def _conv_gate_fwd(u, w, b):
    return _dwconv_fwd_call(u, w, b[None], True), (u, w, b)


def _conv_gate_bwd(res, do):
    u, w, b = res
    dxa, dxb, dwa, dwb, dba, dbb = _dwconv_bwd_call(u, w, b[None], do, True)
    return (jnp.concatenate([dxa, dxb], axis=1), jnp.concatenate([dwa, dwb], axis=1),
            jnp.concatenate([dba, dbb], axis=1)[0])


conv_gate.defvjp(_conv_gate_fwd, _conv_gate_bwd)


@jax.custom_vjp
def conv_silu(x, w):
    return _dwconv_fwd_call(x, w, None, False)


def _conv_silu_fwd(x, w):
    return _dwconv_fwd_call(x, w, None, False), (x, w)


def _conv_silu_bwd(res, do):
    x, w = res
    dx, dw = _dwconv_bwd_call(x, w, None, do, False)
    return dx, dw


conv_silu.defvjp(_conv_silu_fwd, _conv_silu_bwd)


ROW_TILE = 256


def _rmsnorm_fwd_call(x, g):
    s, d = x.shape
    tr = _pick(s, (ROW_TILE, 128, 64, 32, 16, 8))

    def body(x_ref, g_ref, o_ref):
        xb = x_ref[...]
        r = lax.rsqrt(jnp.mean(xb * xb, axis=-1, keepdims=True) + EPS)
        o_ref[...] = (xb * r) * g_ref[...]

    return pl.pallas_call(
        body,
        grid=(s // tr,),
        in_specs=[pl.BlockSpec((tr, d), lambda i: (i, 0)), pl.BlockSpec((1, d), lambda i: (0, 0))],
        out_specs=pl.BlockSpec((tr, d), lambda i: (i, 0)),
        out_shape=jax.ShapeDtypeStruct((s, d), F32),
        compiler_params=pltpu.CompilerParams(dimension_semantics=("parallel",)),
        name="rmsnorm_fwd",
    )(x, g)


def _rmsnorm_bwd_call(x, g, dy):
    s, d = x.shape
    tr = _pick(s, (ROW_TILE, 128, 64, 32, 16, 8))

    def body(x_ref, g_ref, dy_ref, dx_ref, dg_ref):
        @pl.when(pl.program_id(0) == 0)
        def _():
            dg_ref[...] = jnp.zeros_like(dg_ref)

        xb = x_ref[...]
        r = lax.rsqrt(jnp.mean(xb * xb, axis=-1, keepdims=True) + EPS)
        y = xb * r
        dyb = dy_ref[...]
        dg_ref[...] += jnp.sum(dyb * y, axis=0, keepdims=True)
        dn = dyb * g_ref[...]
        dx_ref[...] = r * (dn - y * jnp.mean(dn * y, axis=-1, keepdims=True))

    row = pl.BlockSpec((tr, d), lambda i: (i, 0))
    vec = pl.BlockSpec((1, d), lambda i: (0, 0))
    return pl.pallas_call(
        body,
        grid=(s // tr,),
        in_specs=[row, vec, row],
        out_specs=[row, vec],
        out_shape=[jax.ShapeDtypeStruct((s, d), F32), jax.ShapeDtypeStruct((1, d), F32)],
        compiler_params=pltpu.CompilerParams(dimension_semantics=("arbitrary",)),
        name="rmsnorm_bwd",
    )(x, g, dy)


@jax.custom_vjp
def rmsnorm_rows(x, g):
    return _rmsnorm_fwd_call(x, g[None])


def _rmsnorm_rows_fwd(x, g):
    return _rmsnorm_fwd_call(x, g[None]), (x, g)


def _rmsnorm_rows_bwd(res, dy):
    x, g = res
    dx, dg = _rmsnorm_bwd_call(x, g[None], dy)
    return dx, dg[0]


rmsnorm_rows.defvjp(_rmsnorm_rows_fwd, _rmsnorm_rows_bwd)


def _merge_specs(s, d, tr):
    gate = pl.BlockSpec((tr, N_BRANCH * d), lambda i: (i, 0))
    bias = pl.BlockSpec((1, N_BRANCH * d), lambda i: (0, 0))
    row = pl.BlockSpec((tr, d), lambda i: (i, 0))
    return gate, bias, row


def _merge_fwd_call(gates, bias, ta, tb, tc):
    s, d = ta.shape
    tr = _pick(s, (ROW_TILE, 128, 64, 32, 16, 8))

    def body(g_ref, b_ref, ta_ref, tb_ref, tc_ref, o_ref):
        acc = None
        for b, t_ref in enumerate((ta_ref, tb_ref, tc_ref)):
            gate = _sigmoid(g_ref[:, b * d:(b + 1) * d] + b_ref[:, b * d:(b + 1) * d])
            acc = gate * t_ref[...] if acc is None else acc + gate * t_ref[...]
        o_ref[...] = acc

    gate, bias_s, row = _merge_specs(s, d, tr)
    return pl.pallas_call(
        body,
        grid=(s // tr,),
        in_specs=[gate, bias_s, row, row, row],
        out_specs=row,
        out_shape=jax.ShapeDtypeStruct((s, d), F32),
        compiler_params=pltpu.CompilerParams(dimension_semantics=("parallel",), vmem_limit_bytes=VMEM_LIMIT),
        name="merge_fwd",
    )(gates, bias, ta, tb, tc)


def _merge_bwd_call(gates, bias, ta, tb, tc, dm):
    s, d = ta.shape
    tr = _pick(s, (ROW_TILE, 128, 64, 32, 16, 8))

    def body(g_ref, b_ref, ta_ref, tb_ref, tc_ref, dm_ref, dg_ref, db_ref, dta_ref, dtb_ref, dtc_ref):
        @pl.when(pl.program_id(0) == 0)
        def _():
            db_ref[...] = jnp.zeros_like(db_ref)

        dmb = dm_ref[...]
        for b, (t_ref, dt_ref) in enumerate(((ta_ref, dta_ref), (tb_ref, dtb_ref), (tc_ref, dtc_ref))):
            cols = slice(b * d, (b + 1) * d)
            gate = _sigmoid(g_ref[:, cols] + b_ref[:, cols])
            dt_ref[...] = gate * dmb
            dpre = dmb * t_ref[...] * (gate * (1.0 - gate))
            dg_ref[:, cols] = dpre
            db_ref[:, cols] += jnp.sum(dpre, axis=0, keepdims=True)

    gate, bias_s, row = _merge_specs(s, d, tr)
    rows = jax.ShapeDtypeStruct((s, d), F32)
    return pl.pallas_call(
        body,
        grid=(s // tr,),
        in_specs=[gate, bias_s, row, row, row, row],
        out_specs=[gate, bias_s, row, row, row],
        out_shape=[jax.ShapeDtypeStruct((s, N_BRANCH * d), F32), jax.ShapeDtypeStruct((1, N_BRANCH * d), F32), rows, rows, rows],
        compiler_params=pltpu.CompilerParams(dimension_semantics=("arbitrary",), vmem_limit_bytes=VMEM_LIMIT),
        name="merge_bwd",
    )(gates, bias, ta, tb, tc, dm)


@jax.custom_vjp
def gated_merge(gates, bias, ta, tb, tc):
    return _merge_fwd_call(gates, bias[None], ta, tb, tc)


def _gated_merge_fwd(gates, bias, ta, tb, tc):
    return _merge_fwd_call(gates, bias[None], ta, tb, tc), (gates, bias, ta, tb, tc)


def _gated_merge_bwd(res, dm):
    gates, bias, ta, tb, tc = res
    dg, db, dta, dtb, dtc = _merge_bwd_call(gates, bias[None], ta, tb, tc, dm)
    return dg, db[0], dta, dtb, dtc


gated_merge.defvjp(_gated_merge_fwd, _gated_merge_bwd)


HEAD_ROWS = 512


def _head_stats(xs, kind):
    sq = xs * xs
    ms = jnp.sum(sq, axis=-1, keepdims=True) if kind == 'l2' else jnp.mean(sq, axis=-1, keepdims=True)
    return lax.rsqrt(ms + EPS)


def _headnorm_fwd_call(x, g, z, kind):
    s, w = x.shape
    tr = _pick(s, (HEAD_ROWS, 256, 128, 64, 32, 16, 8))

    def body(*refs):
        x_ref, o_ref = refs[0], refs[-1]
        for h in range(w // HEAD_DIM):
            cols = slice(h * HEAD_DIM, (h + 1) * HEAD_DIM)
            xs = x_ref[:, cols]
            y = xs * _head_stats(xs, kind)
            if kind != 'l2':
                y = y * refs[1][...]
            if kind == 'rms_gate':
                zs = refs[2][:, cols]
                y = y * (zs * _sigmoid(zs))
            o_ref[:, cols] = y

    row = pl.BlockSpec((tr, w), lambda i: (i, 0))
    vec = pl.BlockSpec((1, HEAD_DIM), lambda i: (0, 0))
    in_specs, args = [row], [x]
    if kind != 'l2':
        in_specs.append(vec); args.append(g)
    if kind == 'rms_gate':
        in_specs.append(row); args.append(z)
    return pl.pallas_call(
        body,
        grid=(s // tr,),
        in_specs=in_specs,
        out_specs=row,
        out_shape=jax.ShapeDtypeStruct((s, w), F32),
        compiler_params=pltpu.CompilerParams(dimension_semantics=("parallel",)),
        name=f"headnorm_{kind}_fwd",
    )(*args)


def _headnorm_bwd_call(x, g, z, dy, kind):
    s, w = x.shape
    tr = _pick(s, (HEAD_ROWS, 256, 128, 64, 32, 16, 8))
    gain, gated = kind != 'l2', kind == 'rms_gate'

    def body(*refs):
        n_in = 2 + gain + gated
        x_ref, dy_ref = refs[0], refs[n_in - 1]
        outs = refs[n_in:]
        dx_ref = outs[0]
        if gain:
            g_ref, dg_ref = refs[1], outs[1]

            @pl.when(pl.program_id(0) == 0)
            def _():
                dg_ref[...] = jnp.zeros_like(dg_ref)

        for h in range(w // HEAD_DIM):
            cols = slice(h * HEAD_DIM, (h + 1) * HEAD_DIM)
            xs = x_ref[:, cols]
            r = _head_stats(xs, kind)
            y = xs * r
            dn = dy_ref[:, cols]
            if gated:
                zs = refs[2][:, cols]
                sg = _sigmoid(zs)
                silu = zs * sg
                outs[2][:, cols] = dn * (y * g_ref[...]) * (sg + silu * (1.0 - sg))
                dn = dn * silu
            if gain:
                dg_ref[...] += jnp.sum(dn * y, axis=0, keepdims=True)
                dn = dn * g_ref[...]
            proj = jnp.sum(dn * y, axis=-1, keepdims=True)
            if kind != 'l2':
                proj = proj / HEAD_DIM
            dx_ref[:, cols] = r * (dn - y * proj)

    row = pl.BlockSpec((tr, w), lambda i: (i, 0))
    vec = pl.BlockSpec((1, HEAD_DIM), lambda i: (0, 0))
    rows, vecs = jax.ShapeDtypeStruct((s, w), F32), jax.ShapeDtypeStruct((1, HEAD_DIM), F32)
    in_specs, args = [row], [x]
    if gain:
        in_specs.append(vec); args.append(g)
    if gated:
        in_specs.append(row); args.append(z)
    in_specs.append(row); args.append(dy)
    out_specs, out_shape = [row], [rows]
    if gain:
        out_specs.append(vec); out_shape.append(vecs)
    if gated:
        out_specs.append(row); out_shape.append(rows)
    return pl.pallas_call(
        body,
        grid=(s // tr,),
        in_specs=in_specs,
        out_specs=out_specs,
        out_shape=out_shape,
        compiler_params=pltpu.CompilerParams(dimension_semantics=("arbitrary",) if gain else ("parallel",)),
        name=f"headnorm_{kind}_bwd",
    )(*args)


@jax.custom_vjp
def head_rms(x, g):
    return _headnorm_fwd_call(x, g[None], None, 'rms')


def _head_rms_fwd(x, g):
    return _headnorm_fwd_call(x, g[None], None, 'rms'), (x, g)


def _head_rms_bwd(res, dy):
    x, g = res
    dx, dg = _headnorm_bwd_call(x, g[None], None, dy, 'rms')
    return dx, dg[0]


head_rms.defvjp(_head_rms_fwd, _head_rms_bwd)


@jax.custom_vjp
def head_l2(x):
    return _headnorm_fwd_call(x, None, None, 'l2')


def _head_l2_fwd(x):
    return _headnorm_fwd_call(x, None, None, 'l2'), (x,)


def _head_l2_bwd(res, dy):
    return tuple(_headnorm_bwd_call(res[0], None, None, dy, 'l2'))


head_l2.defvjp(_head_l2_fwd, _head_l2_bwd)


@jax.custom_vjp
def head_rms_gate(x, g, z):
    return _headnorm_fwd_call(x, g[None], z, 'rms_gate')


def _head_rms_gate_fwd(x, g, z):
    return _headnorm_fwd_call(x, g[None], z, 'rms_gate'), (x, g, z)


def _head_rms_gate_bwd(res, dy):
    x, g, z = res
    dx, dg, dz = _headnorm_bwd_call(x, g[None], z, dy, 'rms_gate')
    return dx, dg[0], dz


head_rms_gate.defvjp(_head_rms_gate_fwd, _head_rms_gate_bwd)


MEM_Q_TILE = 512


def _mem_specs(s, m, t):
    tile = pl.BlockSpec((t, HEAD_DIM), lambda h, i: (i, h))
    bank = pl.BlockSpec((m, HEAD_DIM), lambda h, i: (0, h))
    return tile, bank


def _mem_probs(q_ref, k_ref):
    sc = _dot(q_ref[...], k_ref[...], _NT) * HEAD_DIM ** -0.5
    e = jnp.exp(sc - jnp.max(sc, axis=-1, keepdims=True))
    return e / jnp.sum(e, axis=-1, keepdims=True)


def _mem_attn_fwd_call(q, k, v):
    s, w = q.shape
    t = _pick(s, (MEM_Q_TILE, 256, 128, 64, 32, 16, 8))

    def body(q_ref, k_ref, v_ref, o_ref):
        o_ref[...] = _dot(_mem_probs(q_ref, k_ref).astype(BF16), v_ref[...], _NN)

    tile, bank = _mem_specs(s, k.shape[0], t)
    return pl.pallas_call(
        body,
        grid=(w // HEAD_DIM, s // t),
        in_specs=[tile, bank, bank],
        out_specs=tile,
        out_shape=jax.ShapeDtypeStruct((s, w), F32),
        compiler_params=pltpu.CompilerParams(dimension_semantics=("parallel", "parallel")),
        name="mem_attn_fwd",
    )(q, k, v)


def _mem_attn_bwd_call(q, k, v, do):
    s, w = q.shape
    m = k.shape[0]
    t = _pick(s, (MEM_Q_TILE, 256, 128, 64, 32, 16, 8))
    scale = HEAD_DIM ** -0.5

    def body(q_ref, k_ref, v_ref, do_ref, dq_ref, dk_ref, dv_ref):
        @pl.when(pl.program_id(1) == 0)
        def _():
            dk_ref[...] = jnp.zeros_like(dk_ref)
            dv_ref[...] = jnp.zeros_like(dv_ref)

        p = _mem_probs(q_ref, k_ref)
        do16 = do_ref[...].astype(BF16)
        dp = _dot(do16, v_ref[...], _NT)
        ds16 = (p * (dp - jnp.sum(p * dp, axis=-1, keepdims=True))).astype(BF16)
        dq_ref[...] = _dot(ds16, k_ref[...], _NN) * scale
        dk_ref[...] += _dot(ds16, q_ref[...], _TN) * scale
        dv_ref[...] += _dot(p.astype(BF16), do16, _TN)

    tile, bank = _mem_specs(s, m, t)
    return pl.pallas_call(
        body,
        grid=(w // HEAD_DIM, s // t),
        in_specs=[tile, bank, bank, tile],
        out_specs=[tile, bank, bank],
        out_shape=[jax.ShapeDtypeStruct((s, w), F32), jax.ShapeDtypeStruct((m, w), F32), jax.ShapeDtypeStruct((m, w), F32)],
        compiler_params=pltpu.CompilerParams(dimension_semantics=("parallel", "arbitrary")),
        name="mem_attn_bwd",
    )(q, k, v, do)


@jax.custom_vjp
def mem_attn(q, k, v):
    return _mem_attn_fwd_call(q.astype(BF16), k.astype(BF16), v.astype(BF16))


def _mem_attn_fwd(q, k, v):
    q16, k16, v16 = q.astype(BF16), k.astype(BF16), v.astype(BF16)
    return _mem_attn_fwd_call(q16, k16, v16), (q16, k16, v16)


def _mem_attn_bwd(res, do):
    return tuple(_mem_attn_bwd_call(*res, do))


mem_attn.defvjp(_mem_attn_fwd, _mem_attn_bwd)


def _flat_rows(rows):
    return _pick(rows, (FLAT_TILE_ROWS, 1024, 512, 256, 128, 64, 32, 16, 8))


def _add_own_half(g, a, core):
    _, nb, rows, _ = g.shape
    tr = _flat_rows(rows)

    def body(c_ref, g_ref, a_ref, o_ref):
        o_ref[...] = g_ref[...] + a_ref[...]

    return pl.pallas_call(
        body,
        grid_spec=pltpu.PrefetchScalarGridSpec(
            num_scalar_prefetch=1,
            grid=(nb, rows // tr),
            in_specs=[pl.BlockSpec((None, None, tr, LANES), lambda j, r, c_ref: (c_ref[0], j, r, 0)),
                      pl.BlockSpec((None, tr, LANES), lambda j, r, c_ref: (j, r, 0))],
            out_specs=pl.BlockSpec((None, tr, LANES), lambda j, r, c_ref: (j, r, 0)),
        ),
        out_shape=jax.ShapeDtypeStruct((nb, rows, LANES), F32),
        compiler_params=pltpu.CompilerParams(dimension_semantics=("parallel", "parallel")),
        name="rs_add_own_half",
    )(core, g, a)


def _sum_chips(b, core):
    nb, rows, _ = b.shape
    tr = _flat_rows(rows)

    def body(c_ref, b_ref, o_ref):
        acc = b_ref[0] + b_ref[1]
        for j in range(2, nb):
            acc = acc + b_ref[j]
        o_ref[...] = acc

    return pl.pallas_call(
        body,
        grid_spec=pltpu.PrefetchScalarGridSpec(
            num_scalar_prefetch=1,
            grid=(rows // tr,),
            in_specs=[pl.BlockSpec((nb, tr, LANES), lambda r, c_ref: (0, r, 0))],
            out_specs=pl.BlockSpec((None, tr, LANES), lambda r, c_ref: (c_ref[0], r, 0)),
        ),
        out_shape=jax.ShapeDtypeStruct((2, rows, LANES), F32),
        compiler_params=pltpu.CompilerParams(dimension_semantics=("parallel",)),
        name="rs_sum_chips",
    )(core, b)


ADAM_BLOCK_BYTES = 2 * 1024 * 1024


def _adamw(w, g, m, v, tag):
    rows, cols = w.shape
    tr = rows
    for cand in range(8, rows, 8):
        if rows % cand == 0 and cand * cols * 4 <= ADAM_BLOCK_BYTES:
            tr = cand

    def body(w_ref, g_ref, m_ref, v_ref, d_ref, nm_ref, nv_ref):
        gg = g_ref[...]
        nm = ADAM_B1 * m_ref[...] + (1.0 - ADAM_B1) * gg
        nv = ADAM_B2 * v_ref[...] + (1.0 - ADAM_B2) * jnp.square(gg)
        m_hat = nm / (1.0 - ADAM_B1 ** ADAM_STEP)
        v_hat = nv / (1.0 - ADAM_B2 ** ADAM_STEP)
        d_ref[...] = -ADAM_LR * (m_hat / (jnp.sqrt(v_hat) + ADAM_EPS) + ADAM_WD * w_ref[...])
        nm_ref[...] = nm
        nv_ref[...] = nv

    spec = pl.BlockSpec((tr, cols), lambda r: (r, 0))
    shape = jax.ShapeDtypeStruct((rows, cols), F32)
    return pl.pallas_call(
        body,
        grid=(rows // tr,),
        in_specs=[spec] * 4,
        out_specs=[spec] * 3,
        out_shape=[shape] * 3,
        compiler_params=pltpu.CompilerParams(dimension_semantics=("parallel",), vmem_limit_bytes=VMEM_LIMIT),
        name=f"adamw_{tag}",
    )(w, g, m, v)


def _loss_head(y, t):
    s, d = y.shape
    tr = _pick(s, (512, 256, 128, 64, 32, 16, 8))

    def body(y_ref, t_ref, dy_ref, l_ref):
        @pl.when(pl.program_id(0) == 0)
        def _():
            l_ref[...] = jnp.zeros_like(l_ref)

        diff = y_ref[...] - t_ref[...]
        dy_ref[...] = diff / d
        row = jnp.mean(jnp.square(diff), axis=-1, keepdims=True)
        l_ref[...] += 0.5 * jnp.sum(row, axis=0, keepdims=True)

    return pl.pallas_call(
        body,
        grid=(s // tr,),
        in_specs=[pl.BlockSpec((tr, d), lambda r: (r, 0))] * 2,
        out_specs=[pl.BlockSpec((tr, d), lambda r: (r, 0)), pl.BlockSpec((1, LANES), lambda r: (0, 0))],
        out_shape=[jax.ShapeDtypeStruct((s, d), F32), jax.ShapeDtypeStruct((1, LANES), F32)],
        compiler_params=pltpu.CompilerParams(dimension_semantics=("arbitrary",)),
        name="loss_head",
    )(y, t)


def _place():
    x, y, c = lax.axis_index("x"), lax.axis_index("y"), lax.axis_index("c")
    chips = [(1 - x, y), (x, 1 - y), (1 - x, 1 - y)]
    return x, y, c, chips


def _all_gather_chips(flat):
    _, rows, _ = flat.shape

    def body(x_ref, o_ref, send_sems, recv_sems):
        x, y, c, chips = _place()
        me = 2 * x + y
        sib = (x, y, 1 - c)

        def remote(k, src, dst, to):
            return pltpu.make_async_remote_copy(src_ref=src, dst_ref=dst, send_sem=send_sems.at[k],
                                                recv_sem=recv_sems.at[k], device_id=to, device_id_type=MESH)

        first = [remote(k, x_ref.at[c], o_ref.at[me, c], (px, py, c)) for k, (px, py) in enumerate(chips)]
        for cp in first:
            cp.start()
        passed = []
        for k, (px, py) in enumerate(chips):
            blk = o_ref.at[2 * px + py, c]
            remote(k, x_ref.at[c], blk, (px, py, c)).wait_recv()
            cp = remote(3 + k, blk, blk, sib)
            cp.start()
            passed.append(cp)
        for k, (px, py) in enumerate(chips):
            blk = o_ref.at[2 * px + py, 1 - c]
            remote(3 + k, blk, blk, sib).wait_recv()
        for cp in first + passed:
            cp.wait_send()

    return pl.pallas_call(
        body,
        in_specs=[ANY],
        out_specs=ANY,
        out_shape=jax.ShapeDtypeStruct((N_CHIPS, 2, rows, LANES), flat.dtype),
        scratch_shapes=[pltpu.SemaphoreType.DMA((6,)), pltpu.SemaphoreType.DMA((6,))],
        name=f"all_gather_chips_{jnp.dtype(flat.dtype).name}",
    )(flat)


def _rs_sibling_exchange(g):
    _, nb, rows, _ = g.shape

    def body(g_ref, a_ref, send_sem, recv_sem):
        x, y, c, _ = _place()
        cp = pltpu.make_async_remote_copy(src_ref=g_ref.at[1 - c], dst_ref=a_ref, send_sem=send_sem,
                                          recv_sem=recv_sem, device_id=(x, y, 1 - c), device_id_type=MESH)
        cp.start()
        cp.wait()

    return pl.pallas_call(
        body,
        in_specs=[ANY],
        out_specs=ANY,
        out_shape=jax.ShapeDtypeStruct((nb, rows, LANES), F32),
        scratch_shapes=[pltpu.SemaphoreType.DMA, pltpu.SemaphoreType.DMA],
        name="rs_sibling_exchange",
    )(g)


def _rs_chip_exchange(p):
    nb, rows, _ = p.shape

    def body(p_ref, b_ref, send_sems, recv_sems, local_sem):
        x, y, c, chips = _place()
        me = 2 * x + y
        mine = pltpu.make_async_copy(p_ref.at[me], b_ref.at[me], local_sem)
        mine.start()
        copies = [pltpu.make_async_remote_copy(src_ref=p_ref.at[2 * px + py], dst_ref=b_ref.at[me],
                                               send_sem=send_sems.at[k], recv_sem=recv_sems.at[k],
                                               device_id=(px, py, c), device_id_type=MESH)
                  for k, (px, py) in enumerate(chips)]
        for cp in copies:
            cp.start()
        for cp in copies:
            cp.wait()
        mine.wait()

    return pl.pallas_call(
        body,
        in_specs=[ANY],
        out_specs=ANY,
        out_shape=jax.ShapeDtypeStruct((nb, rows, LANES), F32),
        scratch_shapes=[pltpu.SemaphoreType.DMA((3,)), pltpu.SemaphoreType.DMA((3,)), pltpu.SemaphoreType.DMA],
        name="rs_chip_exchange",
    )(p)


def _rs_sibling_gather(r):
    _, rows, _ = r.shape

    def body(r_ref, o_ref, send_sem, recv_sem):
        x, y, c, _ = _place()
        cp = pltpu.make_async_remote_copy(src_ref=o_ref.at[c], dst_ref=o_ref.at[c], send_sem=send_sem,
                                          recv_sem=recv_sem, device_id=(x, y, 1 - c), device_id_type=MESH)
        cp.start()
        cp.wait()

    return pl.pallas_call(
        body,
        in_specs=[ANY],
        out_specs=ANY,
        out_shape=jax.ShapeDtypeStruct((2, rows, LANES), F32),
        input_output_aliases={0: 0},
        scratch_shapes=[pltpu.SemaphoreType.DMA, pltpu.SemaphoreType.DMA],
        name="rs_sibling_gather",
    )(r)


def _reduce_scatter(g, core):
    a = _rs_sibling_exchange(g)
    p = _add_own_half(g, a, core)
    b = _rs_chip_exchange(p)
    return _rs_sibling_gather(_sum_chips(b, core))


def _all_reduce_small(v):
    rows, _ = v.shape
    n_dev = 8

    def body(v_ref, o_ref, gath, send_sems, recv_sems):
        x, y, c, _ = _place()
        me = 4 * x + 2 * y + c
        gath[me] = v_ref[...]
        copies = []
        for mask in range(1, n_dev):
            px = 1 - x if mask & 4 else x
            py = 1 - y if mask & 2 else y
            pc = 1 - c if mask & 1 else c
            copies.append(pltpu.make_async_remote_copy(
                src_ref=v_ref, dst_ref=gath.at[me], send_sem=send_sems.at[mask - 1],
                recv_sem=recv_sems.at[mask - 1], device_id=(px, py, pc), device_id_type=MESH))
        for cp in copies:
            cp.start()
        for cp in copies:
            cp.wait()
        acc = gath[0]
        for k in range(1, n_dev):
            acc = acc + gath[k]
        o_ref[...] = acc

    return pl.pallas_call(
        body,
        in_specs=[VMEM],
        out_specs=VMEM,
        out_shape=jax.ShapeDtypeStruct((rows, LANES), F32),
        scratch_shapes=[pltpu.VMEM((n_dev, rows, LANES), F32), pltpu.SemaphoreType.DMA((n_dev - 1,)),
                        pltpu.SemaphoreType.DMA((n_dev - 1,))],
        name="all_reduce_small",
    )(v)


def _flat_len(shapes, unit_rows=FLAT_TILE_ROWS):
    n = sum(int(np.prod(s)) for s in shapes)
    unit = 2 * unit_rows * LANES
    return -(-n // unit) * unit


def _pack(arrays, total):
    flat = jnp.concatenate([a.reshape(-1) for a in arrays])
    return jnp.pad(flat, (0, total - flat.shape[0]))


def _unpack(flat, shapes):
    out, off = [], 0
    for s in shapes:
        n = int(np.prod(s))
        out.append(flat[off:off + n].reshape(s))
        off += n
    return out


def _permute_w_in(w):
    pad = jnp.zeros(w.shape[:-1] + (N_IN_PAD - N_IN,), w.dtype)
    return jnp.concatenate([w[..., 0:1536], w[..., 1540:3076], w[..., 3084:3596], w[..., 3596:5132],
                            w[..., 5132:8204], w[..., 1536:1540], w[..., 3076:3080], w[..., 3080:3084], pad],
                           axis=-1)


def _unpermute_w_in(w):
    return jnp.concatenate([w[..., 0:1536], w[..., 8192:8196], w[..., 1536:3072], w[..., 8196:8200],
                            w[..., 8200:8204], w[..., 3072:3584], w[..., 3584:5120], w[..., 5120:8192]], axis=-1)


PROJ_WIDTHS = (512, 512, 512, 1536, 512, 512, 512, 512, 3072, 4, 4, 4)


@jax.custom_vjp
def split_proj(proj):
    offs = np.cumsum((0,) + PROJ_WIDTHS)
    return tuple(proj[:, o:o + wd] for o, wd in zip(offs, PROJ_WIDTHS))


def _split_proj_fwd(proj):
    return split_proj(proj), None


def _split_proj_bwd(_, cts):
    pad = jnp.zeros((cts[0].shape[0], N_IN_PAD - sum(PROJ_WIDTHS)), F32)
    return (jnp.concatenate(list(cts) + [pad], axis=1),)


split_proj.defvjp(_split_proj_fwd, _split_proj_bwd)


def heads(x, n):
    return x.reshape(x.shape[:-1] + (n, -1))


def gated_delta_rule(q, k, v, g, beta):
    B, T, H, dk = q.shape
    dv = v.shape[-1]
    N = T // CHUNK

    def chunks(a):
        a = a.astype(F32).reshape((B, N, CHUNK, H) + a.shape[3:])
        return jnp.moveaxis(a, (1, 3), (0, 2))

    qc = chunks(q) * dk ** -0.5
    kc = chunks(k)
    vc = chunks(v)
    bc = chunks(beta)
    gc = jnp.cumsum(chunks(g), axis=-1)
    idx = jnp.arange(CHUNK)
    causal = idx[:, None] >= idx[None, :]
    strict = idx[:, None] > idx[None, :]
    decay = jnp.exp(jnp.where(causal, gc[..., :, None] - gc[..., None, :], -jnp.inf))
    kk = jnp.einsum("nbhcd,nbhed->nbhce", kc, kc)
    a_mat = jnp.where(strict, bc[..., :, None] * kk * decay, 0.0) + jnp.eye(CHUNK, dtype=F32)
    rhs = jnp.concatenate([vc * bc[..., None], kc * (bc * jnp.exp(gc))[..., None]], axis=-1)
    sol = lax.linalg.triangular_solve(a_mat, rhs, left_side=True, lower=True)
    u, w = sol[..., :dv], sol[..., dv:]
    attn = jnp.where(causal, jnp.einsum("nbhcd,nbhed->nbhce", qc, kc) * decay, 0.0)
    g_last = gc[..., -1]
    k_tail = kc * jnp.exp(g_last[..., None] - gc)[..., None]

    egl = jnp.broadcast_to(jnp.exp(g_last)[:, 0, :, None, None], (N, H, 1, dv))
    o = gdn_scan((qc * jnp.exp(gc)[..., None])[:, 0], u[:, 0], w[:, 0], attn[:, 0], k_tail[:, 0], egl)
    return o.transpose(0, 2, 1, 3).reshape(B, T, H, dv)


def _layer(x, mem, p, p16):
    S, D = x.shape
    h = rmsnorm_rows(x, p['norm_mix'])
    def mm(a, name):
        return matmul(a, p[name], p16[name])

    fq, fk, fv, gqkv, gz, sq, sk, sv, gates, ff, gb, ga = split_proj(mm(h, 'w_in'))

    logf = jax.nn.log_sigmoid((ff + p['fox_fbias']).astype(F32))
    ya = fox_core(head_rms(fq, p['fox_qnorm']), head_rms(fk, p['fox_knorm']), fv, jnp.cumsum(logf, axis=0))

    qkv = conv_silu(gqkv, p['gdn_conv'])
    cq, ck, cv = qkv[:, :512], qkv[:, 512:1024], qkv[:, 1024:]
    beta = jax.nn.sigmoid(gb.astype(F32))
    g_log = -jnp.exp(p['gdn_a_log'].astype(F32)) * jax.nn.softplus((ga + p['gdn_dt_bias']).astype(F32))
    o = gated_delta_rule(heads(head_l2(cq), HEADS)[None], heads(head_l2(ck), HEADS)[None], heads(cv, HEADS)[None],
                         g_log[None], beta[None])
    yb = head_rms_gate(o.reshape(S, 512), p['gdn_onorm'], gz)

    yc = sb_core(sq, sk, sv)

    mixed = gated_merge(gates, p['gate_bias'], mm(ya, 'w_oa'), mm(yb, 'w_ob'), mm(yc, 'w_oc'))
    x = x + mm(mixed, 'w_out')

    hq = rmsnorm_rows(x, p['norm_xq'])
    hm = rmsnorm_rows(mem, p['norm_mem'])
    kv = mm(hm, 'w_mkv')
    om = mem_attn(head_rms(mm(hq, 'w_mq'), p['mq_norm']), head_rms(kv[:, :512], p['mk_norm']), kv[:, 512:])
    x = x + mm(om, 'w_mo')

    hf = rmsnorm_rows(x, p['norm_ffn'])
    act = conv_gate(mm(hf, 'w_up'), p['ffn_conv'], p['ffn_conv_b'])
    return x + mm(act, 'w_down')


def kernel(x, mem, norm_mix, w_in, fox_fbias, fox_qnorm, fox_knorm, gdn_conv, gdn_a_log, gdn_dt_bias, gdn_onorm, gate_bias, w_oa, w_ob, w_oc, w_out, norm_xq, norm_mem, w_mq, w_mkv, mq_norm, mk_norm, w_mo, norm_ffn, w_up, ffn_conv, ffn_conv_b, w_down, loss_target, m_norm_mix, m_w_in, m_fox_fbias, m_fox_qnorm, m_fox_knorm, m_gdn_conv, m_gdn_a_log, m_gdn_dt_bias, m_gdn_onorm, m_gate_bias, m_w_oa, m_w_ob, m_w_oc, m_w_out, m_norm_xq, m_norm_mem, m_w_mq, m_w_mkv, m_mq_norm, m_mk_norm, m_w_mo, m_norm_ffn, m_w_up, m_ffn_conv, m_ffn_conv_b, m_w_down, v_norm_mix, v_w_in, v_fox_fbias, v_fox_qnorm, v_fox_knorm, v_gdn_conv, v_gdn_a_log, v_gdn_dt_bias, v_gdn_onorm, v_gate_bias, v_w_oa, v_w_ob, v_w_oc, v_w_out, v_norm_xq, v_norm_mem, v_w_mq, v_w_mkv, v_mq_norm, v_mk_norm, v_w_mo, v_norm_ffn, v_w_up, v_ffn_conv, v_ffn_conv_b, v_w_down):
    args = (x, mem, norm_mix, w_in, fox_fbias, fox_qnorm, fox_knorm, gdn_conv, gdn_a_log, gdn_dt_bias, gdn_onorm, gate_bias, w_oa, w_ob, w_oc, w_out, norm_xq, norm_mem, w_mq, w_mkv, mq_norm, mk_norm, w_mo, norm_ffn, w_up, ffn_conv, ffn_conv_b, w_down)
    moments_m = (m_norm_mix, m_w_in, m_fox_fbias, m_fox_qnorm, m_fox_knorm, m_gdn_conv, m_gdn_a_log, m_gdn_dt_bias, m_gdn_onorm, m_gate_bias, m_w_oa, m_w_ob, m_w_oc, m_w_out, m_norm_xq, m_norm_mem, m_w_mq, m_w_mkv, m_mq_norm, m_mk_norm, m_w_mo, m_norm_ffn, m_w_up, m_ffn_conv, m_ffn_conv_b, m_w_down)
    moments_v = (v_norm_mix, v_w_in, v_fox_fbias, v_fox_qnorm, v_fox_knorm, v_gdn_conv, v_gdn_a_log, v_gdn_dt_bias, v_gdn_onorm, v_gate_bias, v_w_oa, v_w_ob, v_w_oc, v_w_out, v_norm_xq, v_norm_mem, v_w_mq, v_w_mkv, v_mq_norm, v_mk_norm, v_w_mo, v_norm_ffn, v_w_up, v_ffn_conv, v_ffn_conv_b, v_w_down)
    w = dict(zip(IN_NAMES, args))
    m = dict(zip(WEIGHTS, moments_m))
    v = dict(zip(WEIGHTS, moments_v))
    xs, mems, tgt = x[0], mem[0], loss_target[0]
    core = lax.axis_index("c").astype(jnp.int32).reshape(1)

    big = list(SHARDED)
    shard_shapes = [w[n].shape for n in big]
    total = _flat_len(shard_shapes)
    half_rows = total // (2 * LANES)
    small_shapes = [w[n].shape for n in SMALL]
    n_small = sum(int(np.prod(s)) for s in small_shapes) + 1
    small_total = -(-n_small // (8 * LANES)) * (8 * LANES)

    my_chip = 2 * lax.axis_index("x") + lax.axis_index("y")

    def gather(names, dtype, unit_rows):
        shapes = [w[n].shape for n in names]
        tot = _flat_len(shapes, unit_rows)
        flat = _pack([w[n].astype(dtype) for n in names], tot)
        got = _all_gather_chips(flat.reshape(2, tot // (2 * LANES), LANES)).reshape(N_CHIPS, tot)
        out = {}
        for n, blocks in zip(names, zip(*[_unpack(got[j], shapes) for j in range(N_CHIPS)])):
            own = w[n].astype(dtype)
            out[n] = jnp.concatenate([jnp.where(my_chip == j, own, b) for j, b in enumerate(blocks)], axis=SHARDED[n])
        return out

    conv_names = ['gdn_conv', 'ffn_conv']
    params16 = gather([n for n in big if n not in conv_names], BF16, FLAT_TILE_ROWS)
    params16['w_in'] = _permute_w_in(params16['w_in'])
    params = {n: a.astype(F32) for n, a in params16.items()}
    params.update(gather(conv_names, F32, 8))
    for n in SMALL:
        params[n] = w[n]

    def model(x0, pp):
        for layer in range(DEPTH):
            x0 = _layer(x0, mems, {n: a[layer] for n, a in pp.items()}, {n: a[layer] for n, a in params16.items()})
        return x0

    y, model_vjp = jax.vjp(model, xs, params)
    dy, loss_part = _loss_head(y, tgt)
    dx0, grads = model_vjp(dy)
    grads['w_in'] = _unpermute_w_in(grads['w_in'])

    def chip_blocks(g, axis):
        return jnp.stack(jnp.split(g, N_CHIPS, axis=axis)).reshape(N_CHIPS, -1)

    g_blocks = jnp.concatenate([chip_blocks(grads[n], SHARDED[n]) for n in big], axis=1)
    g_blocks = jnp.pad(g_blocks, ((0, 0), (0, total - g_blocks.shape[1])))
    g_halves = g_blocks.reshape(N_CHIPS, 2, half_rows, LANES).transpose(1, 0, 2, 3)
    g_flat = _reduce_scatter(g_halves, core).reshape(total // LANES, LANES)

    s_part = _pack([grads[n] for n in SMALL] + [loss_part[0, :1]], small_total)
    s_sum = _all_reduce_small(s_part.reshape(small_total // LANES, LANES))
    small_grads = _unpack(s_sum.reshape(-1), small_shapes + [(1,)])
    loss = small_grads.pop()[0]

    out = {}
    for n, g_shard in zip(big, _unpack(g_flat.reshape(-1), shard_shapes)):
        shape = w[n].shape
        rows_of = lambda a: a.reshape(-1, shape[-1])
        out['grad', n] = g_shard
        for kind, a in zip(('delta', 'new_m', 'new_v'),
                           _adamw(rows_of(w[n]), rows_of(g_shard), rows_of(m[n]), rows_of(v[n]), n)):
            out[kind, n] = a.reshape(shape)
    srows = small_total // LANES
    sd, snm, snv = _adamw(_pack([w[n] for n in SMALL], small_total).reshape(srows, LANES), s_sum,
                          _pack([m[n] for n in SMALL], small_total).reshape(srows, LANES),
                          _pack([v[n] for n in SMALL], small_total).reshape(srows, LANES), "replicated")
    for kind, flat_small in (('grad', s_sum), ('delta', sd), ('new_m', snm), ('new_v', snv)):
        for n, a in zip(SMALL, _unpack(flat_small.reshape(-1), small_shapes)):
            out[kind, n] = a
    return (loss, dx0[None], *[out[kind, n] for kind in ('grad', 'delta', 'new_m', 'new_v') for n in WEIGHTS])
```

```python
import functools

import jax
import jax.numpy as jnp
import numpy as np
from jax import lax
from jax.experimental import pallas as pl
from jax.experimental.pallas import tpu as pltpu

F32 = jnp.float32
BF16 = jnp.bfloat16
MESH = pl.DeviceIdType.MESH
ANY = pl.BlockSpec(memory_space=pl.ANY)
VMEM = pl.BlockSpec(memory_space=pltpu.VMEM)

D_MODEL = 1024
DEPTH = 4
CHUNK = 64
Q_BLOCK = 128
EPS = 1e-6
HEADS = 4
HEAD_DIM = 128
GDN_CONV = 4
MEM_DIM = 128
D_FF = 2816
N_BRANCH = 3
N_IN = 8204
N_IN_PAD = 8320

ADAM_LR = 0.001
ADAM_B1 = 0.9
ADAM_B2 = 0.999
ADAM_EPS = 1e-08
ADAM_WD = 0.01
ADAM_STEP = 10

N_CHIPS = 4
LANES = 128
FLAT_TILE_ROWS = 2048
VMEM_LIMIT = 48 * 1024 * 1024

IN_NAMES = ['x', 'mem', 'norm_mix', 'w_in', 'fox_fbias', 'fox_qnorm', 'fox_knorm', 'gdn_conv', 'gdn_a_log',
            'gdn_dt_bias', 'gdn_onorm', 'gate_bias', 'w_oa', 'w_ob', 'w_oc', 'w_out', 'norm_xq', 'norm_mem',
            'w_mq', 'w_mkv', 'mq_norm', 'mk_norm', 'w_mo', 'norm_ffn', 'w_up', 'ffn_conv', 'ffn_conv_b', 'w_down']
WEIGHTS = IN_NAMES[2:]
SHARDED = {'w_in': 2, 'gdn_conv': 2, 'w_oa': 2, 'w_ob': 2, 'w_oc': 2, 'w_out': 1, 'w_mq': 1, 'w_mkv': 1,
           'w_mo': 2, 'w_up': 2, 'ffn_conv': 2, 'w_down': 1}
SMALL = [n for n in WEIGHTS if n not in SHARDED]


def _pick(n, cands):
    for c in cands:
        if n % c == 0:
            return c
    return n


_DOT_DIMS = {
    'nn': (((1,), (0,)), ((), ())),
    'nt': (((1,), (1,)), ((), ())),
    'tn': (((0,), (0,)), ((), ())),
}


def _mm(a, b, mode):
    if mode == 'nn':
        (m, c), (_, n) = a.shape, b.shape
    elif mode == 'nt':
        (m, c), (n, _) = a.shape, b.shape
    else:
        (c, m), (_, n) = a.shape, b.shape
    tm = _pick(m, (1024, 512, 256, 128))
    tn = _pick(n, (1664, 1408, 1024, 512, 256, 128))
    tc = _pick(c, (512, 256, 128)) if mode == 'tn' else _pick(c, (1024, 1408, 640, 512, 256, 128))
    if mode == 'tn':
        a_spec = pl.BlockSpec((tc, tm), lambda i, j, k: (k, i))
    else:
        a_spec = pl.BlockSpec((tm, tc), lambda i, j, k: (i, k))
    if mode == 'nt':
        b_spec = pl.BlockSpec((tn, tc), lambda i, j, k: (j, k))
    else:
        b_spec = pl.BlockSpec((tc, tn), lambda i, j, k: (k, j))
    dims = _DOT_DIMS[mode]

    def body(a_ref, b_ref, o_ref):
        @pl.when(pl.program_id(2) == 0)
        def _():
            o_ref[...] = jnp.zeros_like(o_ref)

        o_ref[...] += lax.dot_general(a_ref[...].astype(BF16), b_ref[...].astype(BF16), dims,
                                      preferred_element_type=F32)

    return pl.pallas_call(
        body,
        grid=(m // tm, n // tn, c // tc),
        in_specs=[a_spec, b_spec],
        out_specs=pl.BlockSpec((tm, tn), lambda i, j, k: (i, j)),
        out_shape=jax.ShapeDtypeStruct((m, n), F32),
        compiler_params=pltpu.CompilerParams(
            dimension_semantics=("parallel", "parallel", "arbitrary"), vmem_limit_bytes=VMEM_LIMIT),
        name=f"mm_{mode}_{m}x{c}x{n}",
    )(a, b)


@jax.custom_vjp
def matmul(a, w, w16):
    return _mm(a, w16, 'nn')


def _matmul_fwd(a, w, w16):
    return _mm(a, w16, 'nn'), (a, w16)


def _matmul_bwd(res, dy):
    a, w16 = res
    return _mm(dy, w16, 'nt'), _mm(a, dy, 'tn'), jnp.zeros_like(w16)


matmul.defvjp(_matmul_fwd, _matmul_bwd)


Q_TILE = 512
K_BLOCK = 1024
SUB = 256
_NT = (((1,), (1,)), ((), ()))
_TN = (((0,), (0,)), ((), ()))
_NN = (((1,), (0,)), ((), ()))


def _dot(a, b, dims):
    return lax.dot_general(a, b, dims, preferred_element_type=F32)


def _att_tiles(s):
    tq = min(Q_TILE, s)
    tk = min(K_BLOCK, s)
    assert s % tk == 0 and tk in (tq, 2 * tq) and tq % min(SUB, tq) == 0
    return tq, tk


def _att_specs(s, t):
    tile = pl.BlockSpec((t, HEAD_DIM), lambda h, i: (i, h))
    whole = pl.BlockSpec((s, HEAD_DIM), lambda h, i: (0, h))
    col = pl.BlockSpec((None, t, 1), lambda h, i: (h, i, 0))
    row = pl.BlockSpec((None, 1, s), lambda h, i: (h, 0, 0))
    return tile, whole, col, row


def _key_span(start, width):
    return pl.ds(pl.multiple_of(start, width), width)


def _causal(i, start, tq, width, strict):
    r = i * tq + lax.broadcasted_iota(jnp.int32, (tq, width), 0)
    c = start + lax.broadcasted_iota(jnp.int32, (tq, width), 1)
    return c < r if strict else c <= r


def _odd_half(i, tq, tk, fn, carry):
    if tk == tq:
        return carry
    return lax.cond(lax.rem(i * tq, tk) != 0, fn, lambda cr: cr, carry)


def _fox_fwd_call(q, k, v, c_col, c_row):
    s, w = q.shape
    tq, tk = _att_tiles(s)
    scale = HEAD_DIM ** -0.5

    def body(q_ref, k_ref, v_ref, cc_ref, cr_ref, o_ref, lse_ref):
        i = pl.program_id(1)
        n_full = lax.div(i * tq, tk)
        qb = q_ref[...]
        cq = cc_ref[...]

        def block(start, width, carry, diag):
            m, l, acc = carry
            sl = _key_span(start, width)
            sc = _dot(qb, k_ref[sl, :], _NT) * scale + (cq - cr_ref[:, sl])
            if diag:
                sc = jnp.where(_causal(i, start, tq, width, False), sc, -jnp.inf)
            m_new = jnp.maximum(m, jnp.max(sc, axis=-1, keepdims=True))
            p = jnp.exp(sc - m_new)
            alpha = jnp.exp(m - m_new)
            l = alpha * l + jnp.sum(p, axis=-1, keepdims=True)
            acc = alpha * acc + _dot(p.astype(BF16), v_ref[sl, :], _NN)
            return m_new, l, acc

        init = (jnp.full((tq, 1), -jnp.inf, F32), jnp.zeros((tq, 1), F32), jnp.zeros((tq, HEAD_DIM), F32))
        carry = lax.fori_loop(0, n_full, lambda jb, cr: block(jb * tk, tk, cr, False), init)
        m, l, acc = block(n_full * tk, tk, carry, True)
        o_ref[...] = acc / l
        lse_ref[...] = m + jnp.log(l)

    tile, whole, col, row = _att_specs(s, tq)
    return pl.pallas_call(
        body,
        grid=(w // HEAD_DIM, s // tq),
        in_specs=[tile, whole, whole, col, row],
        out_specs=[tile, col],
        out_shape=[jax.ShapeDtypeStruct((s, w), F32), jax.ShapeDtypeStruct((w // HEAD_DIM, s, 1), F32)],
        compiler_params=pltpu.CompilerParams(dimension_semantics=("parallel", "parallel"),
                                             vmem_limit_bytes=VMEM_LIMIT),
        name="fox_fwd",
    )(q, k, v, c_col, c_row)


def _fox_bwd_call(q, k, v, c_col, c_row, lse, do):
    s, w = q.shape
    tq, tk = _att_tiles(s)
    scale = HEAD_DIM ** -0.5

    def body(q_ref, k_ref, v_ref, cc_ref, cr_ref, lse_ref, do_ref, dq_ref, dk_ref, dv_ref, dcr_ref):
        i = pl.program_id(1)
        n_full = lax.div(i * tq, tk)

        @pl.when(i == 0)
        def _():
            dk_ref[...] = jnp.zeros_like(dk_ref)
            dv_ref[...] = jnp.zeros_like(dv_ref)
            dcr_ref[...] = jnp.zeros_like(dcr_ref)

        qb = q_ref[...]
        do16 = do_ref[...].astype(BF16)
        lse_q = lse_ref[...]
        cq = cc_ref[...]

        def probs(start, width, diag):
            sl = _key_span(start, width)
            ks = k_ref[sl, :]
            sc = _dot(qb, ks, _NT) * scale + (cq - cr_ref[:, sl])
            p = jnp.exp(sc - lse_q)
            if diag:
                p = jnp.where(_causal(i, start, tq, width, False), p, 0.0)
            return sl, ks, p, _dot(do16, v_ref[sl, :], _NT)

        def row_dot(start, width, acc, diag):
            _, _, p, dp = probs(start, width, diag)
            return acc + jnp.sum(p * dp, axis=-1, keepdims=True)

        delta = lax.fori_loop(0, n_full, lambda jb, a: row_dot(jb * tk, tk, a, False), jnp.zeros((tq, 1), F32))
        delta = _odd_half(i, tq, tk, lambda a: row_dot(n_full * tk, tq, a, False), delta)
        delta = row_dot(i * tq, tq, delta, True)

        def block(start, width, dq, diag):
            sl, ks, p, dp = probs(start, width, diag)
            ds = p * (dp - delta)
            ds16 = ds.astype(BF16)
            dv_ref[sl, :] += _dot(p.astype(BF16), do16, _TN)
            dk_ref[sl, :] += _dot(ds16, qb, _TN) * scale
            dcr_ref[:, sl] += -jnp.sum(ds, axis=0, keepdims=True)
            return dq + _dot(ds16, ks, _NN) * scale

        dq = lax.fori_loop(0, n_full, lambda jb, a: block(jb * tk, tk, a, False), jnp.zeros((tq, HEAD_DIM), F32))
        dq = _odd_half(i, tq, tk, lambda a: block(n_full * tk, tq, a, False), dq)
        dq_ref[...] = block(i * tq, tq, dq, True)

    tile, whole, col, row = _att_specs(s, tq)
    full = jax.ShapeDtypeStruct((s, w), F32)
    return pl.pallas_call(
        body,
        grid=(w // HEAD_DIM, s // tq),
        in_specs=[tile, whole, whole, col, row, col, tile],
        out_specs=[tile, whole, whole, row],
        out_shape=[full, full, full, jax.ShapeDtypeStruct((w // HEAD_DIM, 1, s), F32)],
        compiler_params=pltpu.CompilerParams(dimension_semantics=("parallel", "arbitrary"),
                                             vmem_limit_bytes=VMEM_LIMIT),
        name="fox_bwd",
    )(q, k, v, c_col, c_row, lse, do)


@jax.custom_vjp
def fox_core(q, k, v, c):
    return _fox_fwd(q, k, v, c)[0]


def _fox_fwd(q, k, v, c):
    q16, k16, v16 = q.astype(BF16), k.astype(BF16), v.astype(BF16)
    c_col, c_row = c.T[:, :, None], c.T[:, None, :]
    o, lse = _fox_fwd_call(q16, k16, v16, c_col, c_row)
    return o, (q16, k16, v16, c_col, c_row, lse)


def _fox_bwd(res, do):
    dq, dk, dv, dcr = _fox_bwd_call(*res, do)
    return dq, dk, dv, dcr[:, 0, :].T


fox_core.defvjp(_fox_fwd, _fox_bwd)


def _neg_softplus(z):
    e = jnp.exp(-jnp.abs(z))
    return -(jnp.maximum(z, 0.0) + jnp.log(1.0 + e)), e


def _split_dot(x, tri):
    hi = x.astype(BF16)
    lo = (x - hi.astype(F32)).astype(BF16)
    return _dot(hi, tri, _NN) + _dot(lo, tri, _NN)


def _tri(n, fn):
    r = lax.broadcasted_iota(jnp.int32, (n, n), 0)
    c = lax.broadcasted_iota(jnp.int32, (n, n), 1)
    return fn(r, c).astype(BF16)


def _sb_fwd_call(q, k, v):
    s, w = q.shape
    tq, tk = _att_tiles(s)
    sub = min(SUB, tq)
    scale = HEAD_DIM ** -0.5

    def body(q_ref, k_ref, v_ref, o_ref, tot_ref):
        i = pl.program_id(1)
        n_full = lax.div(i * tq, tk)
        qb = q_ref[...]
        tri = _tri(sub, lambda r, c: r >= c)

        def block(start, width, carry, diag):
            later, acc = carry
            sl = _key_span(start, width)
            n_sub = width // sub
            z = _dot(qb, k_ref[sl, :], _NT) * scale
            lk, _ = _neg_softplus(z)
            if diag:
                mask = _causal(i, start, tq, width, True)
                lk = jnp.where(mask, lk, 0.0)
            pieces = [None] * n_sub
            for u in reversed(range(n_sub)):
                part = lk[:, u * sub:(u + 1) * sub]
                pieces[u] = _split_dot(part, tri) + later
                later = later + jnp.sum(part, axis=-1, keepdims=True)
            a = jnp.exp(z + jnp.concatenate(pieces, axis=1))
            if diag:
                a = jnp.where(mask, a, 0.0)
            return later, acc + _dot(a.astype(BF16), v_ref[sl, :], _NN)

        carry = block(i * tq, tq, (jnp.zeros((tq, 1), F32), jnp.zeros((tq, HEAD_DIM), F32)), True)
        carry = _odd_half(i, tq, tk, lambda cr: block(n_full * tk, tq, cr, False), carry)
        later, acc = lax.fori_loop(0, n_full, lambda jj, cr: block((n_full - 1 - jj) * tk, tk, cr, False), carry)
        o_ref[...] = acc
        tot_ref[...] = later

    tile, whole, col, _ = _att_specs(s, tq)
    return pl.pallas_call(
        body,
        grid=(w // HEAD_DIM, s // tq),
        in_specs=[tile, whole, whole],
        out_specs=[tile, col],
        out_shape=[jax.ShapeDtypeStruct((s, w), F32), jax.ShapeDtypeStruct((w // HEAD_DIM, s, 1), F32)],
        compiler_params=pltpu.CompilerParams(dimension_semantics=("parallel", "parallel"),
                                             vmem_limit_bytes=VMEM_LIMIT),
        name="sb_fwd",
    )(q, k, v)


def _sb_bwd_call(q, k, v, tot, do):
    s, w = q.shape
    tq, tk = _att_tiles(s)
    sub = min(SUB, tq)
    scale = HEAD_DIM ** -0.5

    def body(q_ref, k_ref, v_ref, tot_ref, do_ref, dq_ref, dk_ref, dv_ref):
        i = pl.program_id(1)
        n_full = lax.div(i * tq, tk)

        @pl.when(i == 0)
        def _():
            dk_ref[...] = jnp.zeros_like(dk_ref)
            dv_ref[...] = jnp.zeros_like(dv_ref)

        qb = q_ref[...]
        do16 = do_ref[...].astype(BF16)
        tot_q = tot_ref[...]
        tri_before = _tri(sub, lambda r, c: r < c)
        tri_upto = _tri(sub, lambda r, c: r <= c)

        def block(start, width, carry, diag):
            before, dl_before, dq = carry
            sl = _key_span(start, width)
            n_sub = width // sub
            ks = k_ref[sl, :]
            z = _dot(qb, ks, _NT) * scale
            lk, e = _neg_softplus(z)
            if diag:
                mask = _causal(i, start, tq, width, True)
                lk = jnp.where(mask, lk, 0.0)
            sig = 1.0 - jnp.exp(lk)
            pieces = []
            for u in range(n_sub):
                part = lk[:, u * sub:(u + 1) * sub]
                pieces.append(_split_dot(part, tri_before) + before)
                before = before + jnp.sum(part, axis=-1, keepdims=True)
            a = jnp.exp(z + (tot_q - jnp.concatenate(pieces, axis=1)))
            if diag:
                a = jnp.where(mask, a, 0.0)
            dl = a * _dot(do16, v_ref[sl, :], _NT)
            dl16 = dl.astype(BF16)
            pieces = []
            for u in range(n_sub):
                pieces.append(_dot(dl16[:, u * sub:(u + 1) * sub], tri_upto, _NN) + dl_before)
                dl_before = dl_before + jnp.sum(dl[:, u * sub:(u + 1) * sub], axis=-1, keepdims=True)
            dz = dl - sig * jnp.concatenate(pieces, axis=1)
            if diag:
                dz = jnp.where(mask, dz, 0.0)
            dz16 = dz.astype(BF16)
            dv_ref[sl, :] += _dot(a.astype(BF16), do16, _TN)
            dk_ref[sl, :] += _dot(dz16, qb, _TN) * scale
            return before, dl_before, dq + _dot(dz16, ks, _NN) * scale

        init = (jnp.zeros((tq, 1), F32), jnp.zeros((tq, 1), F32), jnp.zeros((tq, HEAD_DIM), F32))
        carry = lax.fori_loop(0, n_full, lambda jb, cr: block(jb * tk, tk, cr, False), init)
        carry = _odd_half(i, tq, tk, lambda cr: block(n_full * tk, tq, cr, False), carry)
        dq_ref[...] = block(i * tq, tq, carry, True)[2]

    tile, whole, col, _ = _att_specs(s, tq)
    full = jax.ShapeDtypeStruct((s, w), F32)
    return pl.pallas_call(
        body,
        grid=(w // HEAD_DIM, s // tq),
        in_specs=[tile, whole, whole, col, tile],
        out_specs=[tile, whole, whole],
        out_shape=[full, full, full],
        compiler_params=pltpu.CompilerParams(dimension_semantics=("parallel", "arbitrary"),
                                             vmem_limit_bytes=VMEM_LIMIT),
        name="sb_bwd",
    )(q, k, v, tot, do)


@jax.custom_vjp
def sb_core(q, k, v):
    return _sb_fwd(q, k, v)[0]


def _sb_fwd(q, k, v):
    q16, k16, v16 = q.astype(BF16), k.astype(BF16), v.astype(BF16)
    o, tot = _sb_fwd_call(q16, k16, v16)
    return o, (q16, k16, v16, tot)


def _sb_bwd(res, do):
    return tuple(_sb_bwd_call(*res, do))


sb_core.defvjp(_sb_fwd, _sb_bwd)


def _gdn_specs(h, c, d):
    vec = pl.BlockSpec((None, h, c, d), lambda n: (n, 0, 0, 0))
    sq = pl.BlockSpec((None, h, c, c), lambda n: (n, 0, 0, 0))
    dec = pl.BlockSpec((None, h, 1, d), lambda n: (n, 0, 0, 0))
    st = pl.BlockSpec((None, h, d, d), lambda n: (n, 0, 0, 0))
    return vec, sq, dec, st


def _b16(x):
    return x.astype(BF16)


def _gdn_scan_fwd_call(qg, u, w, attn, kt, egl):
    n, h, c, d = qg.shape

    def body(qg_ref, u_ref, w_ref, attn_ref, kt_ref, egl_ref, o_ref, st_ref, state):
        @pl.when(pl.program_id(0) == 0)
        def _():
            state[...] = jnp.zeros_like(state)

        for hh in range(h):
            s0 = state[hh]
            st_ref[hh] = s0
            s16 = _b16(s0)
            vn = u_ref[hh] - _dot(_b16(w_ref[hh]), s16, _NN)
            vn16 = _b16(vn)
            o_ref[hh] = _dot(_b16(qg_ref[hh]), s16, _NN) + _dot(_b16(attn_ref[hh]), vn16, _NN)
            state[hh] = s0 * egl_ref[hh] + _dot(_b16(kt_ref[hh]), vn16, _TN)

    vec, sq, dec, st = _gdn_specs(h, c, d)
    return pl.pallas_call(
        body,
        grid=(n,),
        in_specs=[vec, vec, vec, sq, vec, dec],
        out_specs=[vec, st],
        out_shape=[jax.ShapeDtypeStruct((n, h, c, d), F32), jax.ShapeDtypeStruct((n, h, d, d), F32)],
        scratch_shapes=[pltpu.VMEM((h, d, d), F32)],
        compiler_params=pltpu.CompilerParams(dimension_semantics=("arbitrary",)),
        name="gdn_scan_fwd",
    )(qg, u, w, attn, kt, egl)


def _gdn_scan_bwd_call(qg, u, w, attn, kt, egl, states, do):
    n, h, c, d = qg.shape

    def body(qg_ref, u_ref, w_ref, attn_ref, kt_ref, egl_ref, st_ref, do_ref,
             dqg_ref, du_ref, dw_ref, dattn_ref, dkt_ref, degl_ref, dstate):
        @pl.when(pl.program_id(0) == 0)
        def _():
            dstate[...] = jnp.zeros_like(dstate)

        for hh in range(h):
            s0 = st_ref[hh]
            s16 = _b16(s0)
            big_d = dstate[hh]
            d16 = _b16(big_d)
            w16, kt16, qg16, attn16 = _b16(w_ref[hh]), _b16(kt_ref[hh]), _b16(qg_ref[hh]), _b16(attn_ref[hh])
            do16 = _b16(do_ref[hh])
            vn16 = _b16(u_ref[hh] - _dot(w16, s16, _NN))
            dvn = _dot(attn16, do16, _TN) + _dot(kt16, d16, _NN)
            dvn16 = _b16(dvn)
            du_ref[hh] = dvn
            dattn_ref[hh] = _dot(do16, vn16, _NT)
            dqg_ref[hh] = _dot(do16, s16, _NT)
            dkt_ref[hh] = _dot(vn16, d16, _NT)
            dw_ref[hh] = -_dot(dvn16, s16, _NT)
            degl_ref[hh] = jnp.sum(big_d * s0, axis=0, keepdims=True)
            dstate[hh] = big_d * egl_ref[hh] + _dot(qg16, do16, _TN) - _dot(w16, dvn16, _TN)

    vec, sq, dec, st = _gdn_specs(h, c, d)
    rev = lambda spec: pl.BlockSpec(spec.block_shape, lambda i: (n - 1 - i, 0, 0, 0))
    vec, sq, dec, st = rev(vec), rev(sq), rev(dec), rev(st)
    vshape = jax.ShapeDtypeStruct((n, h, c, d), F32)
    return pl.pallas_call(
        body,
        grid=(n,),
        in_specs=[vec, vec, vec, sq, vec, dec, st, vec],
        out_specs=[vec, vec, vec, sq, vec, dec],
        out_shape=[vshape, vshape, vshape, jax.ShapeDtypeStruct((n, h, c, c), F32), vshape,
                   jax.ShapeDtypeStruct((n, h, 1, d), F32)],
        scratch_shapes=[pltpu.VMEM((h, d, d), F32)],
        compiler_params=pltpu.CompilerParams(dimension_semantics=("arbitrary",)),
        name="gdn_scan_bwd",
    )(qg, u, w, attn, kt, egl, states, do)


@jax.custom_vjp
def gdn_scan(qg, u, w, attn, kt, egl):
    return _gdn_scan_fwd_call(qg, u, w, attn, kt, egl)[0]


def _gdn_scan_fwd(qg, u, w, attn, kt, egl):
    o, states = _gdn_scan_fwd_call(qg, u, w, attn, kt, egl)
    return o, (qg, u, w, attn, kt, egl, states)


def _gdn_scan_bwd(res, do):
    return tuple(_gdn_scan_bwd_call(*res, do))


gdn_scan.defvjp(_gdn_scan_fwd, _gdn_scan_bwd)


CONV_ROWS = 512
HALO = 8


def _sigmoid(x):
    return 1.0 / (1.0 + jnp.exp(-x))


def _shifted(ext, k, rows):
    if k == 0:
        return ext[HALO:HALO + rows]
    return pltpu.roll(ext, k % ext.shape[0], 0)[HALO:HALO + rows]


def _conv_specs(s, ch, tr, tc, off):
    per, last = tr // HALO, s // HALO - 1
    blk = pl.BlockSpec((tr, tc), lambda j, i: (i, j + off))
    prev = pl.BlockSpec((HALO, tc), lambda j, i: (jnp.maximum(i * per - 1, 0), j + off))
    nxt = pl.BlockSpec((HALO, tc), lambda j, i: (jnp.minimum((i + 1) * per, last), j + off))
    return blk, prev, nxt


def _dwconv_fwd_call(x, w, b, gated):
    s, ch = x.shape
    taps = w.shape[0]
    out_ch = ch // 2 if gated else ch
    tr = _pick(s, (CONV_ROWS, 256, 128, 64, 32, 16, 8))
    tc = _pick(out_ch, (1408, 512, 384, 256, 128))
    n_j = out_ch // tc
    parts = (0, n_j) if gated else (0,)

    def conv(x_ref, p_ref, w_ref, first):
        xb = x_ref[...]
        ext = jnp.concatenate([jnp.where(first, 0.0, p_ref[...]), xb], axis=0)
        y = w_ref[taps - 1:taps, :] * xb
        for k in range(1, taps):
            y = y + w_ref[taps - 1 - k:taps - k, :] * _shifted(ext, k, tr)
        return y

    def body(*refs):
        first = pl.program_id(1) == 0
        if gated:
            xa, pa, wa, ba, xb_, pb, wb, bb, o_ref = refs
            a = conv(xa, pa, wa, first) + ba[...]
            g = conv(xb_, pb, wb, first) + bb[...]
            o_ref[...] = a * _sigmoid(a) * g
        else:
            xa, pa, wa, o_ref = refs
            a = conv(xa, pa, wa, first)
            o_ref[...] = a * _sigmoid(a)

    in_specs, args = [], []
    for off in parts:
        blk, prev, _ = _conv_specs(s, ch, tr, tc, off)
        in_specs += [blk, prev, pl.BlockSpec((taps, tc), lambda j, i, off=off: (0, j + off))]
        args += [x, x, w]
        if gated:
            in_specs.append(pl.BlockSpec((1, tc), lambda j, i, off=off: (0, j + off)))
            args.append(b)
    return pl.pallas_call(
        body,
        grid=(n_j, s // tr),
        in_specs=in_specs,
        out_specs=pl.BlockSpec((tr, tc), lambda j, i: (i, j)),
        out_shape=jax.ShapeDtypeStruct((s, out_ch), F32),
        compiler_params=pltpu.CompilerParams(dimension_semantics=("parallel", "parallel"),
                                             vmem_limit_bytes=VMEM_LIMIT),
        name="dwconv_gate_fwd" if gated else "dwconv_silu_fwd",
    )(*args)


def _dwconv_bwd_call(x, w, b, do, gated):
    s, ch = x.shape
    taps = w.shape[0]
    out_ch = ch // 2 if gated else ch
    tr = _pick(s, (CONV_ROWS, 256, 128, 64, 32, 16, 8))
    tc = _pick(out_ch, (1408, 512, 384, 256, 128))
    n_j, n_i = out_ch // tc, s // tr
    parts = (0, n_j) if gated else (0,)
    ext_rows = tr + 2 * HALO

    def pre_act(x_ref, p_ref, n_ref, w_ref, first, last):
        ext = jnp.concatenate([jnp.where(first, 0.0, p_ref[...]), x_ref[...], n_ref[...]], axis=0)
        y = w_ref[taps - 1:taps, :] * ext
        for k in range(1, taps):
            y = y + w_ref[taps - 1 - k:taps - k, :] * pltpu.roll(ext, k, 0)
        return ext, y

    def grads(ext, dy, w_ref, dx_ref, dw_ref):
        dx = w_ref[taps - 1:taps, :] * dy[HALO:HALO + tr]
        for k in range(1, taps):
            dx = dx + w_ref[taps - 1 - k:taps - k, :] * _shifted(dy, -k, tr)
        dx_ref[...] = dx
        dyb = dy[HALO:HALO + tr]
        for k in range(taps):
            dw_ref[taps - 1 - k:taps - k, :] += jnp.sum(dyb * _shifted(ext, k, tr), axis=0, keepdims=True)
        return dyb

    def body(*refs):
        i = pl.program_id(1)
        first, last = i == 0, i == n_i - 1
        rows = lax.broadcasted_iota(jnp.int32, (ext_rows, 1), 0)
        inside = jnp.logical_and(rows >= HALO, jnp.logical_or(rows < HALO + tr, jnp.logical_not(last)))
        if gated:
            (xa, pa, na, wa, ba, xb_, pb, nb, wb, bb, do_ref, don_ref,
             dxa_ref, dxb_ref, dwa_ref, dwb_ref, dba_ref, dbb_ref) = refs
        else:
            xa, pa, na, wa, do_ref, don_ref, dxa_ref, dwa_ref = refs

        @pl.when(first)
        def _():
            dwa_ref[...] = jnp.zeros_like(dwa_ref)
            if gated:
                dwb_ref[...] = jnp.zeros_like(dwb_ref)
                dba_ref[...] = jnp.zeros_like(dba_ref)
                dbb_ref[...] = jnp.zeros_like(dbb_ref)

        d_out = jnp.concatenate([jnp.zeros((HALO, tc), F32), do_ref[...], don_ref[...]], axis=0)
        d_out = jnp.where(inside, d_out, 0.0)
        ext_a, a = pre_act(xa, pa, na, wa, first, last)
        if gated:
            a = a + ba[...]
            ext_b, g = pre_act(xb_, pb, nb, wb, first, last)
            g = g + bb[...]
            sg = _sigmoid(a)
            silu = a * sg
            dya = jnp.where(inside, d_out * g * (sg + silu * (1.0 - sg)), 0.0)
            dyg = jnp.where(inside, d_out * silu, 0.0)
            dba_ref[...] += jnp.sum(grads(ext_a, dya, wa, dxa_ref, dwa_ref), axis=0, keepdims=True)
            dbb_ref[...] += jnp.sum(grads(ext_b, dyg, wb, dxb_ref, dwb_ref), axis=0, keepdims=True)
        else:
            sg = _sigmoid(a)
            dya = jnp.where(inside, d_out * (sg + a * sg * (1.0 - sg)), 0.0)
            grads(ext_a, dya, wa, dxa_ref, dwa_ref)

    in_specs, args = [], []
    for off in parts:
        blk, prev, nxt = _conv_specs(s, ch, tr, tc, off)
        in_specs += [blk, prev, nxt, pl.BlockSpec((taps, tc), lambda j, i, off=off: (0, j + off))]
        args += [x, x, x, w]
        if gated:
            in_specs.append(pl.BlockSpec((1, tc), lambda j, i, off=off: (0, j + off)))
            args.append(b)
    blk, _, nxt = _conv_specs(s, out_ch, tr, tc, 0)
    in_specs += [blk, nxt]
    args += [do, do]
    n_half = len(parts)
    out_specs = ([blk] * n_half + [pl.BlockSpec((taps, tc), lambda j, i: (0, j))] * n_half
                 + ([pl.BlockSpec((1, tc), lambda j, i: (0, j))] * n_half if gated else []))
    out_shape = ([jax.ShapeDtypeStruct((s, out_ch), F32)] * n_half + [jax.ShapeDtypeStruct((taps, out_ch), F32)] * n_half
                 + ([jax.ShapeDtypeStruct((1, out_ch), F32)] * n_half if gated else []))
    return pl.pallas_call(
        body,
        grid=(n_j, n_i),
        in_specs=in_specs,
        out_specs=out_specs,
        out_shape=out_shape,
        compiler_params=pltpu.CompilerParams(dimension_semantics=("parallel", "arbitrary"),
                                             vmem_limit_bytes=VMEM_LIMIT),
        name="dwconv_gate_bwd" if gated else "dwconv_silu_bwd",
    )(*args)


@jax.custom_vjp
def conv_gate(u, w, b):
    return _dwconv_fwd_call(u, w, b[None], True)


def _conv_gate_fwd(u, w, b):
    return _dwconv_fwd_call(u, w, b[None], True), (u, w, b)


def _conv_gate_bwd(res, do):
    u, w, b = res
    dxa, dxb, dwa, dwb, dba, dbb = _dwconv_bwd_call(u, w, b[None], do, True)
    return (jnp.concatenate([dxa, dxb], axis=1), jnp.concatenate([dwa, dwb], axis=1),
            jnp.concatenate([dba, dbb], axis=1)[0])


conv_gate.defvjp(_conv_gate_fwd, _conv_gate_bwd)


@jax.custom_vjp
def conv_silu(x, w):
    return _dwconv_fwd_call(x, w, None, False)


def _conv_silu_fwd(x, w):
    return _dwconv_fwd_call(x, w, None, False), (x, w)


def _conv_silu_bwd(res, do):
    x, w = res
    dx, dw = _dwconv_bwd_call(x, w, None, do, False)
    return dx, dw


conv_silu.defvjp(_conv_silu_fwd, _conv_silu_bwd)


ROW_TILE = 256


def _rmsnorm_fwd_call(x, g):
    s, d = x.shape
    tr = _pick(s, (ROW_TILE, 128, 64, 32, 16, 8))

    def body(x_ref, g_ref, o_ref):
        xb = x_ref[...]
        r = lax.rsqrt(jnp.mean(xb * xb, axis=-1, keepdims=True) + EPS)
        o_ref[...] = (xb * r) * g_ref[...]

    return pl.pallas_call(
        body,
        grid=(s // tr,),
        in_specs=[pl.BlockSpec((tr, d), lambda i: (i, 0)), pl.BlockSpec((1, d), lambda i: (0, 0))],
        out_specs=pl.BlockSpec((tr, d), lambda i: (i, 0)),
        out_shape=jax.ShapeDtypeStruct((s, d), F32),
        compiler_params=pltpu.CompilerParams(dimension_semantics=("parallel",)),
        name="rmsnorm_fwd",
    )(x, g)


def _rmsnorm_bwd_call(x, g, dy):
    s, d = x.shape
    tr = _pick(s, (ROW_TILE, 128, 64, 32, 16, 8))

    def body(x_ref, g_ref, dy_ref, dx_ref, dg_ref):
        @pl.when(pl.program_id(0) == 0)
        def _():
            dg_ref[...] = jnp.zeros_like(dg_ref)

        xb = x_ref[...]
        r = lax.rsqrt(jnp.mean(xb * xb, axis=-1, keepdims=True) + EPS)
        y = xb * r
        dyb = dy_ref[...]
        dg_ref[...] += jnp.sum(dyb * y, axis=0, keepdims=True)
        dn = dyb * g_ref[...]
        dx_ref[...] = r * (dn - y * jnp.mean(dn * y, axis=-1, keepdims=True))

    row = pl.BlockSpec((tr, d), lambda i: (i, 0))
    vec = pl.BlockSpec((1, d), lambda i: (0, 0))
    return pl.pallas_call(
        body,
        grid=(s // tr,),
        in_specs=[row, vec, row],
        out_specs=[row, vec],
        out_shape=[jax.ShapeDtypeStruct((s, d), F32), jax.ShapeDtypeStruct((1, d), F32)],
        compiler_params=pltpu.CompilerParams(dimension_semantics=("arbitrary",)),
        name="rmsnorm_bwd",
    )(x, g, dy)


@jax.custom_vjp
def rmsnorm_rows(x, g):
    return _rmsnorm_fwd_call(x, g[None])


def _rmsnorm_rows_fwd(x, g):
    return _rmsnorm_fwd_call(x, g[None]), (x, g)


def _rmsnorm_rows_bwd(res, dy):
    x, g = res
    dx, dg = _rmsnorm_bwd_call(x, g[None], dy)
    return dx, dg[0]


rmsnorm_rows.defvjp(_rmsnorm_rows_fwd, _rmsnorm_rows_bwd)


def _merge_specs(s, d, tr):
    gate = pl.BlockSpec((tr, N_BRANCH * d), lambda i: (i, 0))
    bias = pl.BlockSpec((1, N_BRANCH * d), lambda i: (0, 0))
    row = pl.BlockSpec((tr, d), lambda i: (i, 0))
    return gate, bias, row


def _merge_fwd_call(gates, bias, ta, tb, tc):
    s, d = ta.shape
    tr = _pick(s, (ROW_TILE, 128, 64, 32, 16, 8))

    def body(g_ref, b_ref, ta_ref, tb_ref, tc_ref, o_ref):
        acc = None
        for b, t_ref in enumerate((ta_ref, tb_ref, tc_ref)):
            gate = _sigmoid(g_ref[:, b * d:(b + 1) * d] + b_ref[:, b * d:(b + 1) * d])
            acc = gate * t_ref[...] if acc is None else acc + gate * t_ref[...]
        o_ref[...] = acc

    gate, bias_s, row = _merge_specs(s, d, tr)
    return pl.pallas_call(
        body,
        grid=(s // tr,),
        in_specs=[gate, bias_s, row, row, row],
        out_specs=row,
        out_shape=jax.ShapeDtypeStruct((s, d), F32),
        compiler_params=pltpu.CompilerParams(dimension_semantics=("parallel",), vmem_limit_bytes=VMEM_LIMIT),
        name="merge_fwd",
    )(gates, bias, ta, tb, tc)


def _merge_bwd_call(gates, bias, ta, tb, tc, dm):
    s, d = ta.shape
    tr = _pick(s, (ROW_TILE, 128, 64, 32, 16, 8))

    def body(g_ref, b_ref, ta_ref, tb_ref, tc_ref, dm_ref, dg_ref, db_ref, dta_ref, dtb_ref, dtc_ref):
        @pl.when(pl.program_id(0) == 0)
        def _():
            db_ref[...] = jnp.zeros_like(db_ref)

        dmb = dm_ref[...]
        for b, (t_ref, dt_ref) in enumerate(((ta_ref, dta_ref), (tb_ref, dtb_ref), (tc_ref, dtc_ref))):
            cols = slice(b * d, (b + 1) * d)
            gate = _sigmoid(g_ref[:, cols] + b_ref[:, cols])
            dt_ref[...] = gate * dmb
            dpre = dmb * t_ref[...] * (gate * (1.0 - gate))
            dg_ref[:, cols] = dpre
            db_ref[:, cols] += jnp.sum(dpre, axis=0, keepdims=True)

    gate, bias_s, row = _merge_specs(s, d, tr)
    rows = jax.ShapeDtypeStruct((s, d), F32)
    return pl.pallas_call(
        body,
        grid=(s // tr,),
        in_specs=[gate, bias_s, row, row, row, row],
        out_specs=[gate, bias_s, row, row, row],
        out_shape=[jax.ShapeDtypeStruct((s, N_BRANCH * d), F32), jax.ShapeDtypeStruct((1, N_BRANCH * d), F32), rows, rows, rows],
        compiler_params=pltpu.CompilerParams(dimension_semantics=("arbitrary",), vmem_limit_bytes=VMEM_LIMIT),
        name="merge_bwd",
    )(gates, bias, ta, tb, tc, dm)


@jax.custom_vjp
def gated_merge(gates, bias, ta, tb, tc):
    return _merge_fwd_call(gates, bias[None], ta, tb, tc)


def _gated_merge_fwd(gates, bias, ta, tb, tc):
    return _merge_fwd_call(gates, bias[None], ta, tb, tc), (gates, bias, ta, tb, tc)


def _gated_merge_bwd(res, dm):
    gates, bias, ta, tb, tc = res
    dg, db, dta, dtb, dtc = _merge_bwd_call(gates, bias[None], ta, tb, tc, dm)
    return dg, db[0], dta, dtb, dtc


gated_merge.defvjp(_gated_merge_fwd, _gated_merge_bwd)


HEAD_ROWS = 512


def _head_stats(xs, kind):
    sq = xs * xs
    ms = jnp.sum(sq, axis=-1, keepdims=True) if kind == 'l2' else jnp.mean(sq, axis=-1, keepdims=True)
    return lax.rsqrt(ms + EPS)


def _headnorm_fwd_call(x, g, z, kind):
    s, w = x.shape
    tr = _pick(s, (HEAD_ROWS, 256, 128, 64, 32, 16, 8))

    def body(*refs):
        x_ref, o_ref = refs[0], refs[-1]
        for h in range(w // HEAD_DIM):
            cols = slice(h * HEAD_DIM, (h + 1) * HEAD_DIM)
            xs = x_ref[:, cols]
            y = xs * _head_stats(xs, kind)
            if kind != 'l2':
                y = y * refs[1][...]
            if kind == 'rms_gate':
                zs = refs[2][:, cols]
                y = y * (zs * _sigmoid(zs))
            o_ref[:, cols] = y

    row = pl.BlockSpec((tr, w), lambda i: (i, 0))
    vec = pl.BlockSpec((1, HEAD_DIM), lambda i: (0, 0))
    in_specs, args = [row], [x]
    if kind != 'l2':
        in_specs.append(vec); args.append(g)
    if kind == 'rms_gate':
        in_specs.append(row); args.append(z)
    return pl.pallas_call(
        body,
        grid=(s // tr,),
        in_specs=in_specs,
        out_specs=row,
        out_shape=jax.ShapeDtypeStruct((s, w), F32),
        compiler_params=pltpu.CompilerParams(dimension_semantics=("parallel",)),
        name=f"headnorm_{kind}_fwd",
    )(*args)


def _headnorm_bwd_call(x, g, z, dy, kind):
    s, w = x.shape
    tr = _pick(s, (HEAD_ROWS, 256, 128, 64, 32, 16, 8))
    gain, gated = kind != 'l2', kind == 'rms_gate'

    def body(*refs):
        n_in = 2 + gain + gated
        x_ref, dy_ref = refs[0], refs[n_in - 1]
        outs = refs[n_in:]
        dx_ref = outs[0]
        if gain:
            g_ref, dg_ref = refs[1], outs[1]

            @pl.when(pl.program_id(0) == 0)
            def _():
                dg_ref[...] = jnp.zeros_like(dg_ref)

        for h in range(w // HEAD_DIM):
            cols = slice(h * HEAD_DIM, (h + 1) * HEAD_DIM)
            xs = x_ref[:, cols]
            r = _head_stats(xs, kind)
            y = xs * r
            dn = dy_ref[:, cols]
            if gated:
                zs = refs[2][:, cols]
                sg = _sigmoid(zs)
                silu = zs * sg
                outs[2][:, cols] = dn * (y * g_ref[...]) * (sg + silu * (1.0 - sg))
                dn = dn * silu
            if gain:
                dg_ref[...] += jnp.sum(dn * y, axis=0, keepdims=True)
                dn = dn * g_ref[...]
            proj = jnp.sum(dn * y, axis=-1, keepdims=True)
            if kind != 'l2':
                proj = proj / HEAD_DIM
            dx_ref[:, cols] = r * (dn - y * proj)

    row = pl.BlockSpec((tr, w), lambda i: (i, 0))
    vec = pl.BlockSpec((1, HEAD_DIM), lambda i: (0, 0))
    rows, vecs = jax.ShapeDtypeStruct((s, w), F32), jax.ShapeDtypeStruct((1, HEAD_DIM), F32)
    in_specs, args = [row], [x]
    if gain:
        in_specs.append(vec); args.append(g)
    if gated:
        in_specs.append(row); args.append(z)
    in_specs.append(row); args.append(dy)
    out_specs, out_shape = [row], [rows]
    if gain:
        out_specs.append(vec); out_shape.append(vecs)
    if gated:
        out_specs.append(row); out_shape.append(rows)
    return pl.pallas_call(
        body,
        grid=(s // tr,),
        in_specs=in_specs,
        out_specs=out_specs,
        out_shape=out_shape,
        compiler_params=pltpu.CompilerParams(dimension_semantics=("arbitrary",) if gain else ("parallel",)),
        name=f"headnorm_{kind}_bwd",
    )(*args)


@jax.custom_vjp
def head_rms(x, g):
    return _headnorm_fwd_call(x, g[None], None, 'rms')


def _head_rms_fwd(x, g):
    return _headnorm_fwd_call(x, g[None], None, 'rms'), (x, g)


def _head_rms_bwd(res, dy):
    x, g = res
    dx, dg = _headnorm_bwd_call(x, g[None], None, dy, 'rms')
    return dx, dg[0]


head_rms.defvjp(_head_rms_fwd, _head_rms_bwd)


@jax.custom_vjp
def head_l2(x):
    return _headnorm_fwd_call(x, None, None, 'l2')


def _head_l2_fwd(x):
    return _headnorm_fwd_call(x, None, None, 'l2'), (x,)


def _head_l2_bwd(res, dy):
    return tuple(_headnorm_bwd_call(res[0], None, None, dy, 'l2'))


head_l2.defvjp(_head_l2_fwd, _head_l2_bwd)


@jax.custom_vjp
def head_rms_gate(x, g, z):
    return _headnorm_fwd_call(x, g[None], z, 'rms_gate')


def _head_rms_gate_fwd(x, g, z):
    return _headnorm_fwd_call(x, g[None], z, 'rms_gate'), (x, g, z)


def _head_rms_gate_bwd(res, dy):
    x, g, z = res
    dx, dg, dz = _headnorm_bwd_call(x, g[None], z, dy, 'rms_gate')
    return dx, dg[0], dz


head_rms_gate.defvjp(_head_rms_gate_fwd, _head_rms_gate_bwd)


MEM_Q_TILE = 512


def _mem_specs(s, m, t):
    tile = pl.BlockSpec((t, HEAD_DIM), lambda h, i: (i, h))
    bank = pl.BlockSpec((m, HEAD_DIM), lambda h, i: (0, h))
    return tile, bank


def _mem_probs(q_ref, k_ref):
    sc = _dot(q_ref[...], k_ref[...], _NT) * HEAD_DIM ** -0.5
    e = jnp.exp(sc - jnp.max(sc, axis=-1, keepdims=True))
    return e / jnp.sum(e, axis=-1, keepdims=True)


def _mem_attn_fwd_call(q, k, v):
    s, w = q.shape
    t = _pick(s, (MEM_Q_TILE, 256, 128, 64, 32, 16, 8))

    def body(q_ref, k_ref, v_ref, o_ref):
        o_ref[...] = _dot(_mem_probs(q_ref, k_ref).astype(BF16), v_ref[...], _NN)

    tile, bank = _mem_specs(s, k.shape[0], t)
    return pl.pallas_call(
        body,
        grid=(w // HEAD_DIM, s // t),
        in_specs=[tile, bank, bank],
        out_specs=tile,
        out_shape=jax.ShapeDtypeStruct((s, w), F32),
        compiler_params=pltpu.CompilerParams(dimension_semantics=("parallel", "parallel")),
        name="mem_attn_fwd",
    )(q, k, v)


def _mem_attn_bwd_call(q, k, v, do):
    s, w = q.shape
    m = k.shape[0]
    t = _pick(s, (MEM_Q_TILE, 256, 128, 64, 32, 16, 8))
    scale = HEAD_DIM ** -0.5

    def body(q_ref, k_ref, v_ref, do_ref, dq_ref, dk_ref, dv_ref):
        @pl.when(pl.program_id(1) == 0)
        def _():
            dk_ref[...] = jnp.zeros_like(dk_ref)
            dv_ref[...] = jnp.zeros_like(dv_ref)

        p = _mem_probs(q_ref, k_ref)
        do16 = do_ref[...].astype(BF16)
        dp = _dot(do16, v_ref[...], _NT)
        ds16 = (p * (dp - jnp.sum(p * dp, axis=-1, keepdims=True))).astype(BF16)
        dq_ref[...] = _dot(ds16, k_ref[...], _NN) * scale
        dk_ref[...] += _dot(ds16, q_ref[...], _TN) * scale
        dv_ref[...] += _dot(p.astype(BF16), do16, _TN)

    tile, bank = _mem_specs(s, m, t)
    return pl.pallas_call(
        body,
        grid=(w // HEAD_DIM, s // t),
        in_specs=[tile, bank, bank, tile],
        out_specs=[tile, bank, bank],
        out_shape=[jax.ShapeDtypeStruct((s, w), F32), jax.ShapeDtypeStruct((m, w), F32), jax.ShapeDtypeStruct((m, w), F32)],
        compiler_params=pltpu.CompilerParams(dimension_semantics=("parallel", "arbitrary")),
        name="mem_attn_bwd",
    )(q, k, v, do)


@jax.custom_vjp
def mem_attn(q, k, v):
    return _mem_attn_fwd_call(q.astype(BF16), k.astype(BF16), v.astype(BF16))


def _mem_attn_fwd(q, k, v):
    q16, k16, v16 = q.astype(BF16), k.astype(BF16), v.astype(BF16)
    return _mem_attn_fwd_call(q16, k16, v16), (q16, k16, v16)


def _mem_attn_bwd(res, do):
    return tuple(_mem_attn_bwd_call(*res, do))


mem_attn.defvjp(_mem_attn_fwd, _mem_attn_bwd)


def _flat_rows(rows):
    return _pick(rows, (FLAT_TILE_ROWS, 1024, 512, 256, 128, 64, 32, 16, 8))


def _add_own_half(g, a, core):
    _, nb, rows, _ = g.shape
    tr = _flat_rows(rows)

    def body(c_ref, g_ref, a_ref, o_ref):
        o_ref[...] = g_ref[...] + a_ref[...]

    return pl.pallas_call(
        body,
        grid_spec=pltpu.PrefetchScalarGridSpec(
            num_scalar_prefetch=1,
            grid=(nb, rows // tr),
            in_specs=[pl.BlockSpec((None, None, tr, LANES), lambda j, r, c_ref: (c_ref[0], j, r, 0)),
                      pl.BlockSpec((None, tr, LANES), lambda j, r, c_ref: (j, r, 0))],
            out_specs=pl.BlockSpec((None, tr, LANES), lambda j, r, c_ref: (j, r, 0)),
        ),
        out_shape=jax.ShapeDtypeStruct((nb, rows, LANES), F32),
        compiler_params=pltpu.CompilerParams(dimension_semantics=("parallel", "parallel")),
        name="rs_add_own_half",
    )(core, g, a)


def _sum_chips(b, core):
    nb, rows, _ = b.shape
    tr = _flat_rows(rows)

    def body(c_ref, b_ref, o_ref):
        acc = b_ref[0] + b_ref[1]
        for j in range(2, nb):
            acc = acc + b_ref[j]
        o_ref[...] = acc

    return pl.pallas_call(
        body,
        grid_spec=pltpu.PrefetchScalarGridSpec(
            num_scalar_prefetch=1,
            grid=(rows // tr,),
            in_specs=[pl.BlockSpec((nb, tr, LANES), lambda r, c_ref: (0, r, 0))],
            out_specs=pl.BlockSpec((None, tr, LANES), lambda r, c_ref: (c_ref[0], r, 0)),
        ),
        out_shape=jax.ShapeDtypeStruct((2, rows, LANES), F32),
        compiler_params=pltpu.CompilerParams(dimension_semantics=("parallel",)),
        name="rs_sum_chips",
    )(core, b)


ADAM_BLOCK_BYTES = 2 * 1024 * 1024


def _adamw(w, g, m, v, tag):
    rows, cols = w.shape
    tr = rows
    for cand in range(8, rows, 8):
        if rows % cand == 0 and cand * cols * 4 <= ADAM_BLOCK_BYTES:
            tr = cand

    def body(w_ref, g_ref, m_ref, v_ref, d_ref, nm_ref, nv_ref):
        gg = g_ref[...]
        nm = ADAM_B1 * m_ref[...] + (1.0 - ADAM_B1) * gg
        nv = ADAM_B2 * v_ref[...] + (1.0 - ADAM_B2) * jnp.square(gg)
        m_hat = nm / (1.0 - ADAM_B1 ** ADAM_STEP)
        v_hat = nv / (1.0 - ADAM_B2 ** ADAM_STEP)
        d_ref[...] = -ADAM_LR * (m_hat / (jnp.sqrt(v_hat) + ADAM_EPS) + ADAM_WD * w_ref[...])
        nm_ref[...] = nm
        nv_ref[...] = nv

    spec = pl.BlockSpec((tr, cols), lambda r: (r, 0))
    shape = jax.ShapeDtypeStruct((rows, cols), F32)
    return pl.pallas_call(
        body,
        grid=(rows // tr,),
        in_specs=[spec] * 4,
        out_specs=[spec] * 3,
        out_shape=[shape] * 3,
        compiler_params=pltpu.CompilerParams(dimension_semantics=("parallel",), vmem_limit_bytes=VMEM_LIMIT),
        name=f"adamw_{tag}",
    )(w, g, m, v)


def _loss_head(y, t):
    s, d = y.shape
    tr = _pick(s, (512, 256, 128, 64, 32, 16, 8))

    def body(y_ref, t_ref, dy_ref, l_ref):
        @pl.when(pl.program_id(0) == 0)
        def _():
            l_ref[...] = jnp.zeros_like(l_ref)

        diff = y_ref[...] - t_ref[...]
        dy_ref[...] = diff / d
        row = jnp.mean(jnp.square(diff), axis=-1, keepdims=True)
        l_ref[...] += 0.5 * jnp.sum(row, axis=0, keepdims=True)

    return pl.pallas_call(
        body,
        grid=(s // tr,),
        in_specs=[pl.BlockSpec((tr, d), lambda r: (r, 0))] * 2,
        out_specs=[pl.BlockSpec((tr, d), lambda r: (r, 0)), pl.BlockSpec((1, LANES), lambda r: (0, 0))],
        out_shape=[jax.ShapeDtypeStruct((s, d), F32), jax.ShapeDtypeStruct((1, LANES), F32)],
        compiler_params=pltpu.CompilerParams(dimension_semantics=("arbitrary",)),
        name="loss_head",
    )(y, t)


def _place():
    x, y, c = lax.axis_index("x"), lax.axis_index("y"), lax.axis_index("c")
    chips = [(1 - x, y), (x, 1 - y), (1 - x, 1 - y)]
    return x, y, c, chips


def _all_gather_chips(flat):
    _, rows, _ = flat.shape

    def body(x_ref, o_ref, send_sems, recv_sems):
        x, y, c, chips = _place()
        me = 2 * x + y
        sib = (x, y, 1 - c)

        def remote(k, src, dst, to):
            return pltpu.make_async_remote_copy(src_ref=src, dst_ref=dst, send_sem=send_sems.at[k],
                                                recv_sem=recv_sems.at[k], device_id=to, device_id_type=MESH)

        first = [remote(k, x_ref.at[c], o_ref.at[me, c], (px, py, c)) for k, (px, py) in enumerate(chips)]
        for cp in first:
            cp.start()
        passed = []
        for k, (px, py) in enumerate(chips):
            blk = o_ref.at[2 * px + py, c]
            remote(k, x_ref.at[c], blk, (px, py, c)).wait_recv()
            cp = remote(3 + k, blk, blk, sib)
            cp.start()
            passed.append(cp)
        for k, (px, py) in enumerate(chips):
            blk = o_ref.at[2 * px + py, 1 - c]
            remote(3 + k, blk, blk, sib).wait_recv()
        for cp in first + passed:
            cp.wait_send()

    return pl.pallas_call(
        body,
        in_specs=[ANY],
        out_specs=ANY,
        out_shape=jax.ShapeDtypeStruct((N_CHIPS, 2, rows, LANES), flat.dtype),
        scratch_shapes=[pltpu.SemaphoreType.DMA((6,)), pltpu.SemaphoreType.DMA((6,))],
        name=f"all_gather_chips_{jnp.dtype(flat.dtype).name}",
    )(flat)


def _rs_sibling_exchange(g):
    _, nb, rows, _ = g.shape

    def body(g_ref, a_ref, send_sem, recv_sem):
        x, y, c, _ = _place()
        cp = pltpu.make_async_remote_copy(src_ref=g_ref.at[1 - c], dst_ref=a_ref, send_sem=send_sem,
                                          recv_sem=recv_sem, device_id=(x, y, 1 - c), device_id_type=MESH)
        cp.start()
        cp.wait()

    return pl.pallas_call(
        body,
        in_specs=[ANY],
        out_specs=ANY,
        out_shape=jax.ShapeDtypeStruct((nb, rows, LANES), F32),
        scratch_shapes=[pltpu.SemaphoreType.DMA, pltpu.SemaphoreType.DMA],
        name="rs_sibling_exchange",
    )(g)


def _rs_chip_exchange(p):
    nb, rows, _ = p.shape

    def body(p_ref, b_ref, send_sems, recv_sems, local_sem):
        x, y, c, chips = _place()
        me = 2 * x + y
        mine = pltpu.make_async_copy(p_ref.at[me], b_ref.at[me], local_sem)
        mine.start()
        copies = [pltpu.make_async_remote_copy(src_ref=p_ref.at[2 * px + py], dst_ref=b_ref.at[me],
                                               send_sem=send_sems.at[k], recv_sem=recv_sems.at[k],
                                               device_id=(px, py, c), device_id_type=MESH)
                  for k, (px, py) in enumerate(chips)]
        for cp in copies:
            cp.start()
        for cp in copies:
            cp.wait()
        mine.wait()

    return pl.pallas_call(
        body,
        in_specs=[ANY],
        out_specs=ANY,
        out_shape=jax.ShapeDtypeStruct((nb, rows, LANES), F32),
        scratch_shapes=[pltpu.SemaphoreType.DMA((3,)), pltpu.SemaphoreType.DMA((3,)), pltpu.SemaphoreType.DMA],
        name="rs_chip_exchange",
    )(p)


def _rs_sibling_gather(r):
    _, rows, _ = r.shape

    def body(r_ref, o_ref, send_sem, recv_sem):
        x, y, c, _ = _place()
        cp = pltpu.make_async_remote_copy(src_ref=o_ref.at[c], dst_ref=o_ref.at[c], send_sem=send_sem,
                                          recv_sem=recv_sem, device_id=(x, y, 1 - c), device_id_type=MESH)
        cp.start()
        cp.wait()

    return pl.pallas_call(
        body,
        in_specs=[ANY],
        out_specs=ANY,
        out_shape=jax.ShapeDtypeStruct((2, rows, LANES), F32),
        input_output_aliases={0: 0},
        scratch_shapes=[pltpu.SemaphoreType.DMA, pltpu.SemaphoreType.DMA],
        name="rs_sibling_gather",
    )(r)


def _reduce_scatter(g, core):
    a = _rs_sibling_exchange(g)
    p = _add_own_half(g, a, core)
    b = _rs_chip_exchange(p)
    return _rs_sibling_gather(_sum_chips(b, core))


def _all_reduce_small(v):
    rows, _ = v.shape
    n_dev = 8

    def body(v_ref, o_ref, gath, send_sems, recv_sems):
        x, y, c, _ = _place()
        me = 4 * x + 2 * y + c
        gath[me] = v_ref[...]
        copies = []
        for mask in range(1, n_dev):
            px = 1 - x if mask & 4 else x
            py = 1 - y if mask & 2 else y
            pc = 1 - c if mask & 1 else c
            copies.append(pltpu.make_async_remote_copy(
                src_ref=v_ref, dst_ref=gath.at[me], send_sem=send_sems.at[mask - 1],
                recv_sem=recv_sems.at[mask - 1], device_id=(px, py, pc), device_id_type=MESH))
        for cp in copies:
            cp.start()
        for cp in copies:
            cp.wait()
        acc = gath[0]
        for k in range(1, n_dev):
            acc = acc + gath[k]
        o_ref[...] = acc

    return pl.pallas_call(
        body,
        in_specs=[VMEM],
        out_specs=VMEM,
        out_shape=jax.ShapeDtypeStruct((rows, LANES), F32),
        scratch_shapes=[pltpu.VMEM((n_dev, rows, LANES), F32), pltpu.SemaphoreType.DMA((n_dev - 1,)),
                        pltpu.SemaphoreType.DMA((n_dev - 1,))],
        name="all_reduce_small",
    )(v)


def _flat_len(shapes, unit_rows=FLAT_TILE_ROWS):
    n = sum(int(np.prod(s)) for s in shapes)
    unit = 2 * unit_rows * LANES
    return -(-n // unit) * unit


def _pack(arrays, total):
    flat = jnp.concatenate([a.reshape(-1) for a in arrays])
    return jnp.pad(flat, (0, total - flat.shape[0]))


def _unpack(flat, shapes):
    out, off = [], 0
    for s in shapes:
        n = int(np.prod(s))
        out.append(flat[off:off + n].reshape(s))
        off += n
    return out


def _permute_w_in(w):
    pad = jnp.zeros(w.shape[:-1] + (N_IN_PAD - N_IN,), w.dtype)
    return jnp.concatenate([w[..., 0:1536], w[..., 1540:3076], w[..., 3084:3596], w[..., 3596:5132],
                            w[..., 5132:8204], w[..., 1536:1540], w[..., 3076:3080], w[..., 3080:3084], pad],
                           axis=-1)


def _unpermute_w_in(w):
    return jnp.concatenate([w[..., 0:1536], w[..., 8192:8196], w[..., 1536:3072], w[..., 8196:8200],
                            w[..., 8200:8204], w[..., 3072:3584], w[..., 3584:5120], w[..., 5120:8192]], axis=-1)


PROJ_WIDTHS = (512, 512, 512, 1536, 512, 512, 512, 512, 3072, 4, 4, 4)


@jax.custom_vjp
def split_proj(proj):
    offs = np.cumsum((0,) + PROJ_WIDTHS)
    return tuple(proj[:, o:o + wd] for o, wd in zip(offs, PROJ_WIDTHS))


def _split_proj_fwd(proj):
    return split_proj(proj), None


def _split_proj_bwd(_, cts):
    pad = jnp.zeros((cts[0].shape[0], N_IN_PAD - sum(PROJ_WIDTHS)), F32)
    return (jnp.concatenate(list(cts) + [pad], axis=1),)


split_proj.defvjp(_split_proj_fwd, _split_proj_bwd)


def heads(x, n):
    return x.reshape(x.shape[:-1] + (n, -1))


def gated_delta_rule(q, k, v, g, beta):
    B, T, H, dk = q.shape
    dv = v.shape[-1]
    N = T // CHUNK

    def chunks(a):
        a = a.astype(F32).reshape((B, N, CHUNK, H) + a.shape[3:])
        return jnp.moveaxis(a, (1, 3), (0, 2))

    qc = chunks(q) * dk ** -0.5
    kc = chunks(k)
    vc = chunks(v)
    bc = chunks(beta)
    gc = jnp.cumsum(chunks(g), axis=-1)
    idx = jnp.arange(CHUNK)
    causal = idx[:, None] >= idx[None, :]
    strict = idx[:, None] > idx[None, :]
    decay = jnp.exp(jnp.where(causal, gc[..., :, None] - gc[..., None, :], -jnp.inf))
    kk = jnp.einsum("nbhcd,nbhed->nbhce", kc, kc)
    a_mat = jnp.where(strict, bc[..., :, None] * kk * decay, 0.0) + jnp.eye(CHUNK, dtype=F32)
    rhs = jnp.concatenate([vc * bc[..., None], kc * (bc * jnp.exp(gc))[..., None]], axis=-1)
    sol = lax.linalg.triangular_solve(a_mat, rhs, left_side=True, lower=True)
    u, w = sol[..., :dv], sol[..., dv:]
    attn = jnp.where(causal, jnp.einsum("nbhcd,nbhed->nbhce", qc, kc) * decay, 0.0)
    g_last = gc[..., -1]
    k_tail = kc * jnp.exp(g_last[..., None] - gc)[..., None]

    egl = jnp.broadcast_to(jnp.exp(g_last)[:, 0, :, None, None], (N, H, 1, dv))
    o = gdn_scan((qc * jnp.exp(gc)[..., None])[:, 0], u[:, 0], w[:, 0], attn[:, 0], k_tail[:, 0], egl)
    return o.transpose(0, 2, 1, 3).reshape(B, T, H, dv)


def _layer(x, mem, p, p16):
    S, D = x.shape
    h = rmsnorm_rows(x, p['norm_mix'])
    def mm(a, name):
        return matmul(a, p[name], p16[name])

    fq, fk, fv, gqkv, gz, sq, sk, sv, gates, ff, gb, ga = split_proj(mm(h, 'w_in'))

    logf = jax.nn.log_sigmoid((ff + p['fox_fbias']).astype(F32))
    ya = fox_core(head_rms(fq, p['fox_qnorm']), head_rms(fk, p['fox_knorm']), fv, jnp.cumsum(logf, axis=0))

    qkv = conv_silu(gqkv, p['gdn_conv'])
    cq, ck, cv = qkv[:, :512], qkv[:, 512:1024], qkv[:, 1024:]
    beta = jax.nn.sigmoid(gb.astype(F32))
    g_log = -jnp.exp(p['gdn_a_log'].astype(F32)) * jax.nn.softplus((ga + p['gdn_dt_bias']).astype(F32))
    o = gated_delta_rule(heads(head_l2(cq), HEADS)[None], heads(head_l2(ck), HEADS)[None], heads(cv, HEADS)[None],
                         g_log[None], beta[None])
    yb = head_rms_gate(o.reshape(S, 512), p['gdn_onorm'], gz)

    yc = sb_core(sq, sk, sv)

    mixed = gated_merge(gates, p['gate_bias'], mm(ya, 'w_oa'), mm(yb, 'w_ob'), mm(yc, 'w_oc'))
    x = x + mm(mixed, 'w_out')

    hq = rmsnorm_rows(x, p['norm_xq'])
    hm = rmsnorm_rows(mem, p['norm_mem'])
    kv = mm(hm, 'w_mkv')
    om = mem_attn(head_rms(mm(hq, 'w_mq'), p['mq_norm']), head_rms(kv[:, :512], p['mk_norm']), kv[:, 512:])
    x = x + mm(om, 'w_mo')

    hf = rmsnorm_rows(x, p['norm_ffn'])
    act = conv_gate(mm(hf, 'w_up'), p['ffn_conv'], p['ffn_conv_b'])
    return x + mm(act, 'w_down')


def kernel(x, mem, norm_mix, w_in, fox_fbias, fox_qnorm, fox_knorm, gdn_conv, gdn_a_log, gdn_dt_bias, gdn_onorm, gate_bias, w_oa, w_ob, w_oc, w_out, norm_xq, norm_mem, w_mq, w_mkv, mq_norm, mk_norm, w_mo, norm_ffn, w_up, ffn_conv, ffn_conv_b, w_down, loss_target, m_norm_mix, m_w_in, m_fox_fbias, m_fox_qnorm, m_fox_knorm, m_gdn_conv, m_gdn_a_log, m_gdn_dt_bias, m_gdn_onorm, m_gate_bias, m_w_oa, m_w_ob, m_w_oc, m_w_out, m_norm_xq, m_norm_mem, m_w_mq, m_w_mkv, m_mq_norm, m_mk_norm, m_w_mo, m_norm_ffn, m_w_up, m_ffn_conv, m_ffn_conv_b, m_w_down, v_norm_mix, v_w_in, v_fox_fbias, v_fox_qnorm, v_fox_knorm, v_gdn_conv, v_gdn_a_log, v_gdn_dt_bias, v_gdn_onorm, v_gate_bias, v_w_oa, v_w_ob, v_w_oc, v_w_out, v_norm_xq, v_norm_mem, v_w_mq, v_w_mkv, v_mq_norm, v_mk_norm, v_w_mo, v_norm_ffn, v_w_up, v_ffn_conv, v_ffn_conv_b, v_w_down):
    args = (x, mem, norm_mix, w_in, fox_fbias, fox_qnorm, fox_knorm, gdn_conv, gdn_a_log, gdn_dt_bias, gdn_onorm, gate_bias, w_oa, w_ob, w_oc, w_out, norm_xq, norm_mem, w_mq, w_mkv, mq_norm, mk_norm, w_mo, norm_ffn, w_up, ffn_conv, ffn_conv_b, w_down)
    moments_m = (m_norm_mix, m_w_in, m_fox_fbias, m_fox_qnorm, m_fox_knorm, m_gdn_conv, m_gdn_a_log, m_gdn_dt_bias, m_gdn_onorm, m_gate_bias, m_w_oa, m_w_ob, m_w_oc, m_w_out, m_norm_xq, m_norm_mem, m_w_mq, m_w_mkv, m_mq_norm, m_mk_norm, m_w_mo, m_norm_ffn, m_w_up, m_ffn_conv, m_ffn_conv_b, m_w_down)
    moments_v = (v_norm_mix, v_w_in, v_fox_fbias, v_fox_qnorm, v_fox_knorm, v_gdn_conv, v_gdn_a_log, v_gdn_dt_bias, v_gdn_onorm, v_gate_bias, v_w_oa, v_w_ob, v_w_oc, v_w_out, v_norm_xq, v_norm_mem, v_w_mq, v_w_mkv, v_mq_norm, v_mk_norm, v_w_mo, v_norm_ffn, v_w_up, v_ffn_conv, v_ffn_conv_b, v_w_down)
    w = dict(zip(IN_NAMES, args))
    m = dict(zip(WEIGHTS, moments_m))
    v = dict(zip(WEIGHTS, moments_v))
    xs, mems, tgt = x[0], mem[0], loss_target[0]
    core = lax.axis_index("c").astype(jnp.int32).reshape(1)

    big = list(SHARDED)
    shard_shapes = [w[n].shape for n in big]
    total = _flat_len(shard_shapes)
    half_rows = total // (2 * LANES)
    small_shapes = [w[n].shape for n in SMALL]
    n_small = sum(int(np.prod(s)) for s in small_shapes) + 1
    small_total = -(-n_small // (8 * LANES)) * (8 * LANES)

    my_chip = 2 * lax.axis_index("x") + lax.axis_index("y")

    def gather(names, dtype, unit_rows):
        shapes = [w[n].shape for n in names]
        tot = _flat_len(shapes, unit_rows)
        flat = _pack([w[n].astype(dtype) for n in names], tot)
        got = _all_gather_chips(flat.reshape(2, tot // (2 * LANES), LANES)).reshape(N_CHIPS, tot)
        out = {}
        for n, blocks in zip(names, zip(*[_unpack(got[j], shapes) for j in range(N_CHIPS)])):
            own = w[n].astype(dtype)
            out[n] = jnp.concatenate([jnp.where(my_chip == j, own, b) for j, b in enumerate(blocks)], axis=SHARDED[n])
        return out

    conv_names = ['gdn_conv', 'ffn_conv']
    params16 = gather([n for n in big if n not in conv_names], BF16, FLAT_TILE_ROWS)
    params16['w_in'] = _permute_w_in(params16['w_in'])
    params = {n: a.astype(F32) for n, a in params16.items()}
    params.update(gather(conv_names, F32, 8))
    for n in SMALL:
        params[n] = w[n]

    def model(x0, pp):
        for layer in range(DEPTH):
            x0 = _layer(x0, mems, {n: a[layer] for n, a in pp.items()}, {n: a[layer] for n, a in params16.items()})
        return x0

    y, model_vjp = jax.vjp(model, xs, params)
    dy, loss_part = _loss_head(y, tgt)
    dx0, grads = model_vjp(dy)
    grads['w_in'] = _unpermute_w_in(grads['w_in'])

    def chip_blocks(g, axis):
        return jnp.stack(jnp.split(g, N_CHIPS, axis=axis)).reshape(N_CHIPS, -1)

    g_blocks = jnp.concatenate([chip_blocks(grads[n], SHARDED[n]) for n in big], axis=1)
    g_blocks = jnp.pad(g_blocks, ((0, 0), (0, total - g_blocks.shape[1])))
    g_halves = g_blocks.reshape(N_CHIPS, 2, half_rows, LANES).transpose(1, 0, 2, 3)
    g_flat = _reduce_scatter(g_halves, core).reshape(total // LANES, LANES)

    s_part = _pack([grads[n] for n in SMALL] + [loss_part[0, :1]], small_total)
    s_sum = _all_reduce_small(s_part.reshape(small_total // LANES, LANES))
    small_grads = _unpack(s_sum.reshape(-1), small_shapes + [(1,)])
    loss = small_grads.pop()[0]

    out = {}
    for n, g_shard in zip(big, _unpack(g_flat.reshape(-1), shard_shapes)):
        shape = w[n].shape
        rows_of = lambda a: a.reshape(-1, shape[-1])
        out['grad', n] = g_shard
        for kind, a in zip(('delta', 'new_m', 'new_v'),
                           _adamw(rows_of(w[n]), rows_of(g_shard), rows_of(m[n]), rows_of(v[n]), n)):
            out[kind, n] = a.reshape(shape)
    srows = small_total // LANES
    sd, snm, snv = _adamw(_pack([w[n] for n in SMALL], small_total).reshape(srows, LANES), s_sum,
                          _pack([m[n] for n in SMALL], small_total).reshape(srows, LANES),
                          _pack([v[n] for n in SMALL], small_total).reshape(srows, LANES), "replicated")
    for kind, flat_small in (('grad', s_sum), ('delta', sd), ('new_m', snm), ('new_v', snv)):
        for n, a in zip(SMALL, _unpack(flat_small.reshape(-1), small_shapes)):
            out[kind, n] = a
    return (loss, dx0[None], *[out[kind, n] for kind in ('grad', 'delta', 'new_m', 'new_v') for n in WEIGHTS])
```
